```python
import jax
import jax.numpy as jnp
from jax import lax
import numpy as np

D_MODEL = 1024
BATCH = 4
SEQ = 4096
DEPTH = 4
DEC_BATCH = 32
DEC_SEQ = 1
PAST_LEN = 8192
PAGE_SIZE = 128

N_A_LAYERS = DEPTH // 2
N_B_LAYERS = DEPTH - N_A_LAYERS
D_PLE = 256
EPS = 1e-6
D_RNN = (5 * D_MODEL) // 4
N_RG_BLOCKS = 16
RG_BLOCK = D_RNN // N_RG_BLOCKS
RG_CONV_W = 4
RG_C = 8.0
D_FF = 3 * D_MODEL
FFN_CONV_W = 3
HEAD_DIM = 64
N_HEADS = D_MODEL // HEAD_DIM
N_KV_HEADS = 2
HPG = N_HEADS // N_KV_HEADS
L_CMP = 32
D_CMP = 16
CMP_HIDDEN = 2 * HEAD_DIM
L_SLC = 64
N_SEL = 16
WINDOW = 512
Q_BLOCK = 128
ROPE_THETA = 10000.0
NEG = -1e30

kernel_name = 'hawk_nsa_yoco_decoder_step'


def rms_norm(x, g):
    xf = x.astype(jnp.float32)
    y = xf * lax.rsqrt(jnp.mean(xf * xf, axis=-1, keepdims=True) + EPS)
    return (y * g.astype(jnp.float32)).astype(x.dtype)


def rope(x, pos):
    half = HEAD_DIM // 2
    inv_freq = ROPE_THETA ** (-jnp.arange(half, dtype=jnp.float32) / half)
    ang = pos.astype(jnp.float32)[:, None] * inv_freq[None, :]
    cos = jnp.cos(ang)[None, :, None, :].astype(x.dtype)
    sin = jnp.sin(ang)[None, :, None, :].astype(x.dtype)
    x1, x2 = x[..., :half], x[..., half:]
    return jnp.concatenate([x1 * cos - x2 * sin, x2 * cos + x1 * sin], axis=-1)


def causal_dwconv(x_hist, w, b):
    width = w.shape[0]
    t = x_hist.shape[1] - width + 1
    out = b + x_hist[:, 0:t] * w[0]
    for k in range(1, width):
        out = out + x_hist[:, k:k + t] * w[k]
    return out


def masked_softmax(s, mask):
    s = jnp.where(mask, s.astype(jnp.float32), NEG)
    return jnp.where(mask, jax.nn.softmax(s, axis=-1), 0.0)


def linear_combine(left, right):
    a1, b1 = left
    a2, b2 = right
    return a1 * a2, a2 * b1 + b2


def rg_lru_block(h, conv_hist, h0, pos, w_in, conv_w, conv_b, w_a, b_a, w_x, b_x, lam, w_out):
    b, t, _ = h.shape
    proj = h @ w_in
    y = jax.nn.gelu(proj[..., :D_RNN])
    xr = proj[..., D_RNN:]
    xh = jnp.concatenate([conv_hist.astype(xr.dtype), xr], axis=1)
    new_hist = xh[:, xh.shape[1] - (RG_CONV_W - 1):]
    xc = causal_dwconv(xh, conv_w, conv_b)
    xb = xc.reshape(b, t, N_RG_BLOCKS, RG_BLOCK)
    r = jax.nn.sigmoid(jnp.einsum('btnk,nkj->btnj', xb, w_a) + b_a).reshape(b, t, D_RNN)
    i = jax.nn.sigmoid(jnp.einsum('btnk,nkj->btnj', xb, w_x) + b_x).reshape(b, t, D_RNN)
    log_a = -RG_C * r.astype(jnp.float32) * jax.nn.softplus(-lam.astype(jnp.float32))
    a = jnp.exp(log_a)
    mult = jnp.where((pos == 0)[None, :, None], 1.0, jnp.sqrt(-jnp.expm1(2.0 * log_a)))
    u = mult * (i * xc).astype(jnp.float32)
    u = u.at[:, 0].add(a[:, 0] * h0.astype(jnp.float32))
    _, hs = lax.associative_scan(linear_combine, (a, u), axis=1)
    out = (y * hs.astype(h.dtype)) @ w_out
    return out, new_hist, hs[:, -1].astype(h.dtype)


def conv_ffn(h, conv_hist, w_up, conv_w, conv_b, w_down):
    up = h @ w_up
    uh = jnp.concatenate([conv_hist.astype(up.dtype), up], axis=1)
    new_hist = uh[:, uh.shape[1] - (FFN_CONV_W - 1):]
    uc = causal_dwconv(uh, conv_w, conv_b)
    return (jax.nn.gelu(uc[..., :D_FF]) * uc[..., D_FF:]) @ w_down, new_hist


def per_layer_embed(x, p, g, w_in, w_gate):
    return (p @ w_in) * jax.nn.sigmoid(rms_norm(x, g) @ w_gate)


def shared_kv(u, pos, w_kv):
    b, t, _ = u.shape
    kv = (u @ w_kv).reshape(b, t, 6, N_KV_HEADS, HEAD_DIM)
    cmp_kv = kv[:, :, 0:2]
    slc_kv = jnp.stack([rope(kv[:, :, 2], pos), kv[:, :, 3]], axis=2)
    win_kv = jnp.stack([rope(kv[:, :, 4], pos), kv[:, :, 5]], axis=2)
    return cmp_kv, slc_kv, win_kv


def compress(raw_kv, pos_emb, w1, b1, w2):
    b, t = raw_kv.shape[0], raw_kv.shape[1]
    n_cb = (t - L_CMP) // D_CMP + 1
    idx = jnp.arange(n_cb, dtype=jnp.int32)[:, None] * D_CMP + jnp.arange(L_CMP, dtype=jnp.int32)[None, :]
    blk = raw_kv[:, idx] + pos_emb[None, None, :, :, None, :]
    flat = jnp.transpose(blk, (0, 1, 4, 3, 2, 5)).reshape(b, n_cb, N_KV_HEADS, 2, L_CMP * HEAD_DIM)
    hid = jax.nn.gelu(jnp.einsum('bcgjf,jfe->bcgje', flat, w1) + b1)
    out = jnp.einsum('bcgje,jed->bcgjd', hid, w2)
    ends = jnp.arange(n_cb, dtype=jnp.int32) * D_CMP + (L_CMP - 1)
    return out[..., 0, :], out[..., 1, :], ends


def cmp_to_slc(n_cb, n_sb):
    c0 = jnp.arange(n_cb, dtype=jnp.int32)[:, None] * D_CMP
    s0 = jnp.arange(n_sb, dtype=jnp.int32)[None, :] * L_SLC
    return ((c0 < s0 + L_SLC) & (c0 + L_CMP > s0)).astype(jnp.float32)


def nsa_query(h, pos, w_qg):
    b, t, _ = h.shape
    proj = h @ w_qg
    q = proj[..., :N_HEADS * HEAD_DIM].reshape(b, t, N_HEADS, HEAD_DIM)
    gates = jax.nn.sigmoid(proj[..., N_HEADS * HEAD_DIM:].reshape(b, t, N_HEADS, 3))
    return q, rope(q, pos), gates


def nsa_core(q, q_rot, gates, q_pos, kc, vc, c_end, slc_blk, win_kv, w_pos):
    b, t = q.shape[0], q.shape[1]
    scale = HEAD_DIM ** -0.5
    qg = q.reshape(b, t, N_KV_HEADS, HPG, HEAD_DIM)
    qr = q_rot.reshape(b, t, N_KV_HEADS, HPG, HEAD_DIM)
    s_c = jnp.einsum('btghd,bcgd->bgthc', qg, kc, preferred_element_type=jnp.float32) * scale
    m_c = c_end[None, :] <= q_pos[:, None]
    p_c = masked_softmax(s_c, m_c[None, None, :, None, :])
    o_c = jnp.einsum('bgthc,bcgd->btghd', p_c.astype(vc.dtype), vc)
    n_sb = slc_blk.shape[1]
    imp = jnp.einsum('bgthc,cs->bgts', p_c, cmp_to_slc(kc.shape[1], n_sb))
    blk = jnp.arange(n_sb, dtype=jnp.int32)[None, :]
    cur = (q_pos // L_SLC)[:, None]
    forced = (blk == 0) | (blk == cur) | (blk == cur - 1)
    imp = jnp.where(forced, jnp.inf, imp)
    imp = jnp.where(blk * L_SLC <= q_pos[:, None], imp, -jnp.inf)
    n_top = min(N_SEL, n_sb)
    _, sel = lax.top_k(imp, n_top)
    blk_g = jnp.transpose(slc_blk, (0, 4, 1, 2, 3, 5))
    gath = jax.vmap(jax.vmap(lambda kb, ib: kb[ib]))(blk_g, sel)
    n_rows = n_top * L_SLC
    k_s = gath[..., 0, :]
    v_s = gath[..., 1, :].reshape(b, N_KV_HEADS, t, n_rows, HEAD_DIM)
    kpos = sel[..., None] * L_SLC + jnp.arange(L_SLC, dtype=jnp.int32)
    m_s = (kpos <= q_pos[None, None, :, None, None]).reshape(b, N_KV_HEADS, t, 1, n_rows)
    s_s = jnp.einsum('btghd,bgtkld->bgthkl', qr, k_s, preferred_element_type=jnp.float32) * scale
    p_s = masked_softmax(s_s.reshape(b, N_KV_HEADS, t, HPG, n_rows), m_s)
    o_s = jnp.einsum('bgthn,bgtnd->btghd', p_s.astype(v_s.dtype), v_s)
    k_w = win_kv[:, :, 0]
    v_w = win_kv[:, :, 1]
    dist = q_pos[:, None] - w_pos[None, :]
    m_w = (dist >= 0) & (dist < WINDOW) & (w_pos[None, :] >= 0)
    s_w = jnp.einsum('btghd,bwgd->bgthw', qr, k_w, preferred_element_type=jnp.float32) * scale
    p_w = masked_softmax(s_w, m_w[None, None, :, None, :])
    o_w = jnp.einsum('bgthw,bwgd->btghd', p_w.astype(v_w.dtype), v_w)
    g = gates.reshape(b, t, N_KV_HEADS, HPG, 3)
    o = o_c * g[..., 0:1] + o_s * g[..., 1:2] + o_w * g[..., 2:3]
    return o.reshape(b, t, N_HEADS * HEAD_DIM)


def gather_pages(cache, page_table):
    rows = cache[page_table]
    return rows.reshape(page_table.shape[0], page_table.shape[1] * PAGE_SIZE, *cache.shape[2:])


def setup_inputs(seed: int = 0) -> dict:
    key = jax.random.key(seed)
    ks = iter(jax.random.split(key, 48))

    def nrm(shape, scale):
        return jax.random.normal(next(ks), shape, jnp.float32) * scale

    def gain(shape):
        return 1.0 + nrm(shape, 0.01)

    n_pages = PAST_LEN // PAGE_SIZE
    n_used = DEC_BATCH * n_pages
    n_phys = (5 * n_used + 3) // 4
    page_table = jax.random.permutation(next(ks), n_phys)[:n_used].reshape(DEC_BATCH, n_pages).astype(jnp.int32)
    w_buf = min(WINDOW, PAST_LEN)
    a0 = jax.random.uniform(next(ks), (N_A_LAYERS, D_RNN), jnp.float32, 0.9, 0.999)
    a_base = a0 ** (1.0 / RG_C)
    rg_lambda = jnp.log(a_base) - jnp.log1p(-a_base)
    return {
        'x_prompt': nrm((BATCH, SEQ, D_MODEL), 1.0),
        'x_sample': nrm((DEC_BATCH, DEC_SEQ, D_MODEL), 1.0),
        'p_prompt': nrm((DEPTH, BATCH, SEQ, D_PLE), 1.0),
        'p_sample': nrm((DEPTH, DEC_BATCH, DEC_SEQ, D_PLE), 1.0),
        'cache_cmp_kv': nrm((n_phys, PAGE_SIZE, 2, N_KV_HEADS, HEAD_DIM), 1.0),
        'cache_slc_kv': nrm((n_phys, PAGE_SIZE, 2, N_KV_HEADS, HEAD_DIM), 1.0),
        'cache_win_kv': nrm((DEC_BATCH, w_buf, 2, N_KV_HEADS, HEAD_DIM), 1.0),
        'state_rg_conv': nrm((N_A_LAYERS, DEC_BATCH, RG_CONV_W - 1, D_RNN), 1.0),
        'state_rg_h': nrm((N_A_LAYERS, DEC_BATCH, D_RNN), 0.5),
        'state_ffn_conv': nrm((DEPTH, DEC_BATCH, FFN_CONV_W - 1, 2 * D_FF), 1.0),
        'page_table': page_table,
        'g_mix': gain((DEPTH, D_MODEL)),
        'g_ffn': gain((DEPTH, D_MODEL)),
        'g_ple': gain((DEPTH, D_MODEL)),
        'g_final': gain((D_MODEL,)),
        'rg_w_in': nrm((N_A_LAYERS, D_MODEL, 2 * D_RNN), D_MODEL ** -0.5),
        'rg_conv_w': nrm((N_A_LAYERS, RG_CONV_W, D_RNN), RG_CONV_W ** -0.5),
        'rg_conv_b': nrm((N_A_LAYERS, D_RNN), 0.01),
        'rg_w_a': nrm((N_A_LAYERS, N_RG_BLOCKS, RG_BLOCK, RG_BLOCK), RG_BLOCK ** -0.5),
        'rg_b_a': nrm((N_A_LAYERS, N_RG_BLOCKS, RG_BLOCK), 0.01),
        'rg_w_x': nrm((N_A_LAYERS, N_RG_BLOCKS, RG_BLOCK, RG_BLOCK), RG_BLOCK ** -0.5),
        'rg_b_x': nrm((N_A_LAYERS, N_RG_BLOCKS, RG_BLOCK), 0.01),
        'rg_lambda': rg_lambda,
        'rg_w_out': nrm((N_A_LAYERS, D_RNN, D_MODEL), D_RNN ** -0.5),
        'g_kv': gain((D_MODEL,)),
        'w_kv': nrm((D_MODEL, 6 * N_KV_HEADS * HEAD_DIM), D_MODEL ** -0.5),
        'cmp_pos': nrm((L_CMP, 2, HEAD_DIM), 0.1),
        'cmp_w1': nrm((2, L_CMP * HEAD_DIM, CMP_HIDDEN), (L_CMP * HEAD_DIM) ** -0.5),
        'cmp_b1': nrm((2, CMP_HIDDEN), 0.01),
        'cmp_w2': nrm((2, CMP_HIDDEN, HEAD_DIM), CMP_HIDDEN ** -0.5),
        'attn_w_qg': nrm((N_B_LAYERS, D_MODEL, N_HEADS * HEAD_DIM + 3 * N_HEADS), D_MODEL ** -0.5),
        'attn_w_o': nrm((N_B_LAYERS, N_HEADS * HEAD_DIM, D_MODEL), (N_HEADS * HEAD_DIM) ** -0.5),
        'ffn_w_up': nrm((DEPTH, D_MODEL, 2 * D_FF), D_MODEL ** -0.5),
        'ffn_conv_w': nrm((DEPTH, FFN_CONV_W, 2 * D_FF), FFN_CONV_W ** -0.5),
        'ffn_conv_b': nrm((DEPTH, 2 * D_FF), 0.01),
        'ffn_w_down': nrm((DEPTH, D_FF, D_MODEL), D_FF ** -0.5),
        'ple_w_in': nrm((DEPTH, D_PLE, D_MODEL), D_PLE ** -0.5),
        'ple_w_gate': nrm((DEPTH, D_MODEL, D_MODEL), D_MODEL ** -0.5),
    }


def reference(x_prompt, x_sample, p_prompt, p_sample,
              cache_cmp_kv, cache_slc_kv, cache_win_kv,
              state_rg_conv, state_rg_h, state_ffn_conv, page_table,
              g_mix, g_ffn, g_ple, g_final,
              rg_w_in, rg_conv_w, rg_conv_b, rg_w_a, rg_b_a, rg_w_x, rg_b_x, rg_lambda, rg_w_out,
              g_kv, w_kv, cmp_pos, cmp_w1, cmp_b1, cmp_w2,
              attn_w_qg, attn_w_o,
              ffn_w_up, ffn_conv_w, ffn_conv_b, ffn_w_down,
              ple_w_in, ple_w_gate):

    def prompt_kv_side(stream):
        b, t = stream.shape[0], stream.shape[1]
        pos = jnp.arange(t, dtype=jnp.int32)
        cmp_kv, slc_kv, win_kv = shared_kv(rms_norm(stream, g_kv), pos, w_kv)
        kc, vc, c_end = compress(cmp_kv, cmp_pos, cmp_w1, cmp_b1, cmp_w2)
        slc_blk = slc_kv.reshape(b, t // L_SLC, L_SLC, 2, N_KV_HEADS, HEAD_DIM)
        win_pad = jnp.pad(win_kv, ((0, 0), (WINDOW, 0), (0, 0), (0, 0), (0, 0)))
        nqb = t // Q_BLOCK

        def blockify(a):
            return a.reshape(b, nqb, Q_BLOCK, *a.shape[2:]).swapaxes(0, 1)

        def one_block(args):
            qb, qrb, gb, start = args
            qpos = start + jnp.arange(Q_BLOCK, dtype=jnp.int32)
            kw = lax.dynamic_slice_in_dim(win_pad, start, WINDOW + Q_BLOCK, axis=1)
            wpos = start - WINDOW + jnp.arange(WINDOW + Q_BLOCK, dtype=jnp.int32)
            return nsa_core(qb, qrb, gb, qpos, kc, vc, c_end, slc_blk, kw, wpos)

        def attend(h, w_qg, w_o):
            q, qr, gates = nsa_query(h, pos, w_qg)
            starts = jnp.arange(nqb, dtype=jnp.int32) * Q_BLOCK
            o = lax.map(one_block, (blockify(q), blockify(qr), blockify(gates), starts))
            return o.swapaxes(0, 1).reshape(b, t, N_HEADS * HEAD_DIM) @ w_o

        return attend, (cmp_kv, slc_kv, win_kv[:, t - min(WINDOW, t):])

    def sample_kv_side(stream):
        db, tn = stream.shape[0], stream.shape[1]
        pos = PAST_LEN + jnp.arange(tn, dtype=jnp.int32)
        cmp_new, slc_new, win_new = shared_kv(rms_norm(stream, g_kv), pos, w_kv)
        t = PAST_LEN + tn
        cmp_full = jnp.concatenate([gather_pages(cache_cmp_kv, page_table).astype(cmp_new.dtype), cmp_new], axis=1)
        slc_full = jnp.concatenate([gather_pages(cache_slc_kv, page_table).astype(slc_new.dtype), slc_new], axis=1)
        kc, vc, c_end = compress(cmp_full, cmp_pos, cmp_w1, cmp_b1, cmp_w2)
        n_sb = -(-t // L_SLC)
        slc_blk = jnp.pad(slc_full, ((0, 0), (0, n_sb * L_SLC - t), (0, 0), (0, 0), (0, 0)))
        slc_blk = slc_blk.reshape(db, n_sb, L_SLC, 2, N_KV_HEADS, HEAD_DIM)
        w_buf = cache_win_kv.shape[1]
        win_full = jnp.concatenate([cache_win_kv.astype(win_new.dtype), win_new], axis=1)
        w_pos = PAST_LEN - w_buf + jnp.arange(w_buf + tn, dtype=jnp.int32)

        def attend(h, w_qg, w_o):
            q, qr, gates = nsa_query(h, pos, w_qg)
            return nsa_core(q, qr, gates, pos, kc, vc, c_end, slc_blk, win_full, w_pos) @ w_o

        keep = min(WINDOW, t)
        return attend, (cmp_new, slc_new, win_full[:, win_full.shape[1] - keep:])

    def run_group(x, p, pos, rg_conv0, rg_h0, ffn_conv0, kv_side):
        rg_conv_new, rg_h_new, ffn_conv_new = [], [], []
        attend, kv_state = None, None
        for i in range(DEPTH):
            if i == N_A_LAYERS:
                attend, kv_state = kv_side(x)
            h = rms_norm(x, g_mix[i])
            if i < N_A_LAYERS:
                o, c_hist, h_last = rg_lru_block(h, rg_conv0[i], rg_h0[i], pos, rg_w_in[i], rg_conv_w[i], rg_conv_b[i],
                                                 rg_w_a[i], rg_b_a[i], rg_w_x[i], rg_b_x[i], rg_lambda[i], rg_w_out[i])
                rg_conv_new.append(c_hist)
                rg_h_new.append(h_last)
            else:
                j = i - N_A_LAYERS
                o = attend(h, attn_w_qg[j], attn_w_o[j])
            x = x + o
            f, f_hist = conv_ffn(rms_norm(x, g_ffn[i]), ffn_conv0[i], ffn_w_up[i], ffn_conv_w[i], ffn_conv_b[i], ffn_w_down[i])
            x = x + f
            ffn_conv_new.append(f_hist)
            x = x + per_layer_embed(x, p[i], g_ple[i], ple_w_in[i], ple_w_gate[i])
        return rms_norm(x, g_final), jnp.stack(rg_conv_new), jnp.stack(rg_h_new), jnp.stack(ffn_conv_new), kv_state

    bp, sp = x_prompt.shape[0], x_prompt.shape[1]
    dt = x_prompt.dtype
    y_prompt, rgc_p, rgh_p, ffc_p, kv_p = run_group(
        x_prompt, p_prompt, jnp.arange(sp, dtype=jnp.int32),
        jnp.zeros((N_A_LAYERS, bp, RG_CONV_W - 1, D_RNN), dt),
        jnp.zeros((N_A_LAYERS, bp, D_RNN), dt),
        jnp.zeros((DEPTH, bp, FFN_CONV_W - 1, 2 * D_FF), dt),
        prompt_kv_side)
    y_sample, rgc_s, rgh_s, ffc_s, kv_s = run_group(
        x_sample, p_sample, PAST_LEN + jnp.arange(x_sample.shape[1], dtype=jnp.int32),
        state_rg_conv, state_rg_h, state_ffn_conv, sample_kv_side)
    cmp_p, slc_p, win_p = kv_p
    cmp_s, slc_s, win_s = kv_s
    return (y_prompt, y_sample, cmp_p, cmp_s, slc_p, slc_s, win_p, win_s,
            rgc_p, rgc_s, rgh_p, rgh_s, ffc_p, ffc_s)
```

```python
import functools

import jax
import jax.numpy as jnp
import numpy as np
from jax import lax
from jax.experimental import pallas as pl
from jax.experimental.pallas import tpu as pltpu

BF = jnp.bfloat16
F32 = jnp.float32

EPS = 1e-6
RG_C = 8.0
N_RG_BLOCKS = 16
HEAD_DIM = 64
N_KV_HEADS = 2
HPG = 8
L_CMP = 32
D_CMP = 16
L_SLC = 64
N_SEL = 16
WINDOW = 512
Q_BLOCK = 128
ROPE_THETA = 10000.0
PAGE_SIZE = 128
NEG = -1e30
LANES = 128
SUBLANES = 8
SLC_TILE = 512
WIN_TILE = 128
VMEM_LIMIT = 56 * 1024 * 1024


def _cparams(n_axes):
    return pltpu.CompilerParams(dimension_semantics=("arbitrary",) * n_axes,
                                vmem_limit_bytes=VMEM_LIMIT)


def _const_spec(shape):
    zeros = (0,) * len(shape)
    return pl.BlockSpec(shape, lambda *_: zeros, pipeline_mode=pl.Buffered(1))


def _dot(a, b):
    return jnp.dot(a, b, preferred_element_type=F32)


def _dot_nt(a, b):
    return lax.dot_general(a, b, (((1,), (1,)), ((), ())), preferred_element_type=F32)


def _rms(x, g):
    return x * lax.rsqrt(jnp.mean(x * x, axis=-1, keepdims=True) + EPS) * g


def _gelu(x):
    return jax.nn.gelu(x, approximate=True)


def _sigmoid(x):
    return jax.nn.sigmoid(x)


def _shift_rows(x, prev8, j):
    if j == 0:
        return x
    r = pltpu.roll(x, j, axis=0)
    p = pltpu.roll(prev8, j, axis=0)
    row = lax.broadcasted_iota(jnp.int32, (SUBLANES, x.shape[1]), 0)
    head = jnp.where(row < j, p, r[:SUBLANES])
    if x.shape[0] == SUBLANES:
        return head
    return jnp.concatenate([head, r[SUBLANES:]], axis=0)


def _rope(z, cos, sin_signed):
    lane = lax.broadcasted_iota(jnp.int32, z.shape, 1)
    first = (lane & (HEAD_DIM - 1)) < (HEAD_DIM // 2)
    rot = jnp.where(first, pltpu.roll(z, LANES - HEAD_DIM // 2, axis=1), pltpu.roll(z, HEAD_DIM // 2, axis=1))
    return z * cos + rot * sin_signed


def _rg_in_kernel(x_ref, g_ref, w_ref, yg_ref, xr_ref, *, chunk):
    h = _rms(x_ref[...], g_ref[...]).astype(BF)
    n = yg_ref.shape[1]
    for c0 in range(0, n, chunk):
        yg_ref[:, c0:c0 + chunk] = _gelu(_dot(h, w_ref[:, c0:c0 + chunk]))
        xr_ref[:, c0:c0 + chunk] = _dot(h, w_ref[:, n + c0:n + c0 + chunk])


def _rg_in(x, g, w_bf):
    n, d = x.shape
    dr = w_bf.shape[1] // 2
    tm = min(512, n)
    return pl.pallas_call(
        functools.partial(_rg_in_kernel, chunk=256),
        grid=(n // tm,),
        in_specs=[pl.BlockSpec((tm, d), lambda i: (i, 0)), _const_spec((1, d)), _const_spec(w_bf.shape)],
        out_specs=[pl.BlockSpec((tm, dr), lambda i: (i, 0)), pl.BlockSpec((tm, dr), lambda i: (i, 0))],
        out_shape=[jax.ShapeDtypeStruct((n, dr), F32), jax.ShapeDtypeStruct((n, dr), F32)],
        compiler_params=_cparams(1), name="rg_in",
    )(x, g, w_bf)


def _rg_gates(xc, wg_ref, bg_ref, c_ref):
    d = xc.shape[1]
    gates = _dot(xc.astype(BF), wg_ref[...]) + bg_ref[...]
    r = _sigmoid(gates[:, :d])
    i = _sigmoid(gates[:, d:])
    log_a = c_ref[...] * r
    a = jnp.exp(log_a)
    mult = jnp.sqrt(-jnp.tanh(log_a) * (a * a + 1.0))
    return a, mult, i * xc


def _rg_core_kernel(xr_ref, yg_ref, hist_ref, h0_ref, cw_ref, cb_ref, wg_ref, bg_ref, c_ref,
                    out_ref, hl_ref, xprev, hc):
    t = pl.program_id(1)

    @pl.when(t == 0)
    def _():
        xprev[...] = hist_ref[...]
        hc[...] = h0_ref[...]

    x = xr_ref[...]
    tt, d = x.shape
    prev = xprev[...]
    xc = cb_ref[...] + _shift_rows(x, prev, 3) * cw_ref[0:1, :]
    xc = xc + _shift_rows(x, prev, 2) * cw_ref[1:2, :]
    xc = xc + _shift_rows(x, prev, 1) * cw_ref[2:3, :]
    xc = xc + x * cw_ref[3:4, :]
    xprev[...] = x[tt - SUBLANES:, :]

    a, mult, ixc = _rg_gates(xc, wg_ref, bg_ref, c_ref)
    row = lax.broadcasted_iota(jnp.int32, (tt, d), 0)
    mult = jnp.where((row == 0) & (t == 0), 1.0, mult)
    u = mult * ixc

    av, bv = a, u
    dist = 1
    while dist < tt:
        a_s = pltpu.roll(av, dist, axis=0)
        b_s = pltpu.roll(bv, dist, axis=0)
        ok = row >= dist
        bv = jnp.where(ok, av * b_s + bv, bv)
        av = jnp.where(ok, av * a_s, av)
        dist *= 2
    h = av * hc[0:1, :] + bv
    hc[...] = jnp.broadcast_to(h[tt - 1:tt, :], (SUBLANES, d))
    hl_ref[...] = hc[...]
    out_ref[...] = (yg_ref[...] * h).astype(BF)


def _rg_core(xr, yg, hist8, h08, cw8, cb, wg_bf, bg, c, b, t):
    n, d = xr.shape
    tt = min(256, t)
    nt = t // tt
    row_spec = pl.BlockSpec((tt, d), lambda bi, ti: (bi * nt + ti, 0))
    st_spec = pl.BlockSpec((None, SUBLANES, d), lambda bi, ti: (bi, 0, 0))
    return pl.pallas_call(
        _rg_core_kernel,
        grid=(b, nt),
        in_specs=[row_spec, row_spec, st_spec, st_spec, _const_spec(cw8.shape), _const_spec(cb.shape),
                  _const_spec(wg_bf.shape), _const_spec(bg.shape), _const_spec(c.shape)],
        out_specs=[row_spec, st_spec],
        out_shape=[jax.ShapeDtypeStruct((n, d), BF), jax.ShapeDtypeStruct((b, SUBLANES, d), F32)],
        scratch_shapes=[pltpu.VMEM((SUBLANES, d), F32), pltpu.VMEM((SUBLANES, d), F32)],
        compiler_params=_cparams(2), name="rg_core",
    )(xr, yg, hist8, h08, cw8, cb, wg_bf, bg, c)


def _rg_step_kernel(xr_ref, yg_ref, h0_ref, h1_ref, h2_ref, hs_ref, cw_ref, cb_ref, wg_ref, bg_ref, c_ref,
                    out_ref, hn_ref):
    x = xr_ref[...]
    xc = cb_ref[...] + h0_ref[...] * cw_ref[0:1, :]
    xc = xc + h1_ref[...] * cw_ref[1:2, :]
    xc = xc + h2_ref[...] * cw_ref[2:3, :]
    xc = xc + x * cw_ref[3:4, :]
    a, mult, ixc = _rg_gates(xc, wg_ref, bg_ref, c_ref)
    h = mult * ixc + a * hs_ref[...]
    hn_ref[...] = h
    out_ref[...] = (yg_ref[...] * h).astype(BF)


def _rg_step(xr, yg, h0, h1, h2, hs, cw8, cb, wg_bf, bg, c):
    n, d = xr.shape
    args = (xr, yg, h0, h1, h2, hs, cw8, cb, wg_bf, bg, c)
    return pl.pallas_call(
        _rg_step_kernel,
        grid=(1,),
        in_specs=[_const_spec(a.shape) for a in args],
        out_specs=[pl.BlockSpec((n, d), lambda i: (0, 0)), pl.BlockSpec((n, d), lambda i: (0, 0))],
        out_shape=[jax.ShapeDtypeStruct((n, d), BF), jax.ShapeDtypeStruct((n, d), F32)],
        compiler_params=_cparams(1), name="rg_step",
    )(*args)


def _matmul_res_kernel(a_ref, w_ref, x_ref, o_ref):
    o_ref[...] = x_ref[...] + _dot(a_ref[...], w_ref[...])


def _matmul_res(a_bf, w_bf, x):
    n, k = a_bf.shape
    d = w_bf.shape[1]
    tm = min(512, n)
    return pl.pallas_call(
        _matmul_res_kernel,
        grid=(n // tm,),
        in_specs=[pl.BlockSpec((tm, k), lambda i: (i, 0)), _const_spec(w_bf.shape),
                  pl.BlockSpec((tm, d), lambda i: (i, 0))],
        out_specs=pl.BlockSpec((tm, d), lambda i: (i, 0)),
        out_shape=jax.ShapeDtypeStruct((n, d), F32),
        compiler_params=_cparams(1), name="matmul_res",
    )(a_bf, w_bf, x)


def _ffn_tail(x, acc, p_ref, gp_ref, wgate_ref, wpin_ref, gfin_ref, final_norm):
    x1 = x + acc
    hp = _rms(x1, gp_ref[...]).astype(BF)
    sig = _sigmoid(_dot(hp, wgate_ref[...]))
    pe = _dot(p_ref[...].astype(BF), wpin_ref[...])
    x2 = x1 + pe * sig
    if final_norm:
        x2 = _rms(x2, gfin_ref[...])
    return x2


def _ffn_kernel(x_ref, p_ref, hist_ref, gf_ref, wup_ref, cw_ref, cb_ref, wdn_ref, gp_ref, wgate_ref, wpin_ref,
                gfin_ref, o_ref, ho_ref, uprev, *, chunk, final_norm):
    t = pl.program_id(1)

    @pl.when(t == 0)
    def _():
        uprev[...] = hist_ref[...]

    x = x_ref[...]
    tm = x.shape[0]
    dff = wdn_ref.shape[0]
    h = _rms(x, gf_ref[...]).astype(BF)
    acc = jnp.zeros(x.shape, F32)
    for c0 in range(0, dff, chunk):
        halves = []
        for off in (c0, dff + c0):
            up = _dot(h, wup_ref[:, off:off + chunk])
            prev = uprev[:, off:off + chunk]
            uc = cb_ref[:, off:off + chunk] + _shift_rows(up, prev, 2) * cw_ref[0:1, off:off + chunk]
            uc = uc + _shift_rows(up, prev, 1) * cw_ref[1:2, off:off + chunk]
            uc = uc + up * cw_ref[2:3, off:off + chunk]
            uprev[:, off:off + chunk] = up[tm - SUBLANES:, :]
            halves.append(uc)
        act = (_gelu(halves[0]) * halves[1]).astype(BF)
        acc = acc + _dot(act, wdn_ref[c0:c0 + chunk, :])
    ho_ref[...] = uprev[...]
    o_ref[...] = _ffn_tail(x, acc, p_ref, gp_ref, wgate_ref, wpin_ref, gfin_ref, final_norm)


def _ffn(x, p, hist8, gf, wup_bf, cw8, cb, wdn_bf, gp, wgate_bf, wpin_bf, gfin, b, t, final_norm):
    n, d = x.shape
    dp = p.shape[1]
    d2 = wup_bf.shape[1]
    tm = min(512, t)
    nt = t // tm
    row = lambda w: pl.BlockSpec((tm, w), lambda bi, ti: (bi * nt + ti, 0))
    st_spec = pl.BlockSpec((None, SUBLANES, d2), lambda bi, ti: (bi, 0, 0))
    consts = (gf, wup_bf, cw8, cb, wdn_bf, gp, wgate_bf, wpin_bf, gfin)
    return pl.pallas_call(
        functools.partial(_ffn_kernel, chunk=512, final_norm=final_norm),
        grid=(b, nt),
        in_specs=[row(d), row(dp), st_spec] + [_const_spec(a.shape) for a in consts],
        out_specs=[row(d), st_spec],
        out_shape=[jax.ShapeDtypeStruct((n, d), F32), jax.ShapeDtypeStruct((b, SUBLANES, d2), F32)],
        scratch_shapes=[pltpu.VMEM((SUBLANES, d2), F32)],
        compiler_params=_cparams(2), name="ffn",
    )(x, p, hist8, *consts)


def _ffn_step_kernel(x_ref, p_ref, h0_ref, h1_ref, gf_ref, wup_ref, cw_ref, cb_ref, wdn_ref, gp_ref, wgate_ref,
                     wpin_ref, gfin_ref, o_ref, up_ref, *, chunk, final_norm):
    x = x_ref[...]
    dff = wdn_ref.shape[0]
    h = _rms(x, gf_ref[...]).astype(BF)
    acc = jnp.zeros(x.shape, F32)
    for c0 in range(0, dff, chunk):
        halves = []
        for off in (c0, dff + c0):
            up = _dot(h, wup_ref[:, off:off + chunk])
            up_ref[:, off:off + chunk] = up
            uc = cb_ref[:, off:off + chunk] + h0_ref[:, off:off + chunk] * cw_ref[0:1, off:off + chunk]
            uc = uc + h1_ref[:, off:off + chunk] * cw_ref[1:2, off:off + chunk]
            uc = uc + up * cw_ref[2:3, off:off + chunk]
            halves.append(uc)
        act = (_gelu(halves[0]) * halves[1]).astype(BF)
        acc = acc + _dot(act, wdn_ref[c0:c0 + chunk, :])
    o_ref[...] = _ffn_tail(x, acc, p_ref, gp_ref, wgate_ref, wpin_ref, gfin_ref, final_norm)


def _ffn_step(x, p, h0, h1, gf, wup_bf, cw8, cb, wdn_bf, gp, wgate_bf, wpin_bf, gfin, final_norm):
    n, d = x.shape
    d2 = wup_bf.shape[1]
    args = (x, p, h0, h1, gf, wup_bf, cw8, cb, wdn_bf, gp, wgate_bf, wpin_bf, gfin)
    return pl.pallas_call(
        functools.partial(_ffn_step_kernel, chunk=512, final_norm=final_norm),
        grid=(1,),
        in_specs=[_const_spec(a.shape) for a in args],
        out_specs=[pl.BlockSpec((n, d), lambda i: (0, 0)), pl.BlockSpec((n, d2), lambda i: (0, 0))],
        out_shape=[jax.ShapeDtypeStruct((n, d), F32), jax.ShapeDtypeStruct((n, d2), F32)],
        compiler_params=_cparams(1), name="ffn_step",
    )(*args)


def _kv_kernel(x_ref, g_ref, w_ref, cos_ref, sin_ref, kv_ref, slc_ref, win_ref):
    h = _rms(x_ref[...], g_ref[...]).astype(BF)
    kv = _dot(h, w_ref[...])
    cos, sin = cos_ref[...], sin_ref[...]
    w2 = 2 * LANES
    k_slc = _rope(kv[:, w2:w2 + LANES], cos, sin)
    k_win = _rope(kv[:, 2 * w2:2 * w2 + LANES], cos, sin)
    kv_ref[:, 0:w2] = kv[:, 0:w2]
    kv_ref[:, w2:w2 + LANES] = k_slc
    kv_ref[:, w2 + LANES:2 * w2] = kv[:, w2 + LANES:2 * w2]
    kv_ref[:, 2 * w2:2 * w2 + LANES] = k_win
    kv_ref[:, 2 * w2 + LANES:3 * w2] = kv[:, 2 * w2 + LANES:3 * w2]
    slc_ref[:, 0:LANES] = k_slc.astype(BF)
    slc_ref[:, LANES:w2] = kv[:, w2 + LANES:2 * w2].astype(BF)
    win_ref[:, 0:LANES] = k_win.astype(BF)
    win_ref[:, LANES:w2] = kv[:, 2 * w2 + LANES:3 * w2].astype(BF)


def _kv_proj(x, g, w_bf, cos, sin):
    n, d = x.shape
    nk = w_bf.shape[1]
    tm = min(512, n)
    row = lambda w: pl.BlockSpec((tm, w), lambda i: (i, 0))
    return pl.pallas_call(
        _kv_kernel,
        grid=(n // tm,),
        in_specs=[row(d), _const_spec((1, d)), _const_spec(w_bf.shape), row(LANES), row(LANES)],
        out_specs=[row(nk), row(2 * LANES), row(2 * LANES)],
        out_shape=[jax.ShapeDtypeStruct((n, nk), F32), jax.ShapeDtypeStruct((n, 2 * LANES), BF),
                   jax.ShapeDtypeStruct((n, 2 * LANES), BF)],
        compiler_params=_cparams(1), name="kv_proj",
    )(x, g, w_bf, cos, sin)


def _q_kernel(x_ref, g_ref, wq_ref, wg_ref, cos_ref, sin_ref, qs_ref, qrs_ref, gt_ref):
    h = _rms(x_ref[...], g_ref[...]).astype(BF)
    cos, sin = cos_ref[...], sin_ref[...]
    scale = HEAD_DIM ** -0.5
    for c0 in range(0, wq_ref.shape[1], LANES):
        q = _dot(h, wq_ref[:, c0:c0 + LANES])
        qs_ref[:, c0:c0 + LANES] = (q * scale).astype(BF)
        qrs_ref[:, c0:c0 + LANES] = (_rope(q, cos, sin) * scale).astype(BF)
    gt_ref[...] = _sigmoid(_dot(h, wg_ref[...]))


def _q_proj(x, g, wq_bf, wg_bf, cos, sin):
    n, d = x.shape
    dq = wq_bf.shape[1]
    tm = min(512, n)
    row = lambda w: pl.BlockSpec((tm, w), lambda i: (i, 0))
    return pl.pallas_call(
        _q_kernel,
        grid=(n // tm,),
        in_specs=[row(d), _const_spec((1, d)), _const_spec(wq_bf.shape), _const_spec(wg_bf.shape),
                  row(LANES), row(LANES)],
        out_specs=[row(dq), row(dq), row(LANES)],
        out_shape=[jax.ShapeDtypeStruct((n, dq), BF), jax.ShapeDtypeStruct((n, dq), BF),
                   jax.ShapeDtypeStruct((n, LANES), F32)],
        compiler_params=_cparams(1), name="q_proj",
    )(x, g, wq_bf, wg_bf, cos, sin)


def _compress_kernel(xk_ref, xv_ref, pa_ref, pb_ref, w1a_ref, w1b_ref, b1_ref, w2_ref, kc_ref, vc_ref):
    nhb = kc_ref.shape[0]
    for j, x_ref, o_ref in ((0, xk_ref, kc_ref), (1, xv_ref, vc_ref)):
        acc_a = jnp.zeros((nhb, 2 * LANES), F32)
        acc_b = jnp.zeros((nhb, 2 * LANES), F32)
        for l in range(D_CMP):
            xl = x_ref[pl.ds(l, nhb, stride=D_CMP), :]
            acc_a = acc_a + _dot((xl + pa_ref[j, l:l + 1, :]).astype(BF), w1a_ref[j, l])
            acc_b = acc_b + _dot((xl + pb_ref[j, l:l + 1, :]).astype(BF), w1b_ref[j, l])
        hid = _gelu(acc_a + pltpu.roll(acc_b, nhb - 1, axis=0) + b1_ref[j])
        o_ref[...] = _dot(hid.astype(BF), w2_ref[j]).astype(BF)


def _compress(rows, params, b, t):
    nhb = t // D_CMP
    out = pl.BlockSpec((None, nhb, LANES), lambda bi: (bi, 0, 0))
    return pl.pallas_call(
        _compress_kernel,
        grid=(b,),
        in_specs=[pl.BlockSpec((None, t, LANES), lambda bi: (bi, 0, 0)),
                  pl.BlockSpec((None, t, LANES), lambda bi: (bi, 0, 1))] + [_const_spec(a.shape) for a in params],
        out_specs=[out, out],
        out_shape=[jax.ShapeDtypeStruct((b, nhb, LANES), BF)] * 2,
        compiler_params=_cparams(1), name="compress",
    )(rows, rows, *params)


def _gather_kernel(pt_ref, *refs):
    del pt_ref
    o_ref = refs[-1]
    for r, in_ref in enumerate(refs[:-1]):
        o_ref[r * PAGE_SIZE:(r + 1) * PAGE_SIZE, :] = in_ref[...].astype(o_ref.dtype)


def _gather_pages(cache, page_table, dtype):
    db, n_pages = page_table.shape
    c = cache.shape[2]
    per_step = min(16, n_pages)
    in_specs = [pl.BlockSpec((None, PAGE_SIZE, c), functools.partial(
        lambda bi, i, pt, r: (pt[bi, i * per_step + r], 0, 0), r=r)) for r in range(per_step)]
    return pl.pallas_call(
        _gather_kernel,
        grid_spec=pltpu.PrefetchScalarGridSpec(
            num_scalar_prefetch=1, grid=(db, n_pages // per_step), in_specs=in_specs,
            out_specs=pl.BlockSpec((None, per_step * PAGE_SIZE, c), lambda bi, i, pt: (bi, i, 0))),
        out_shape=jax.ShapeDtypeStruct((db, n_pages * PAGE_SIZE, c), dtype),
        compiler_params=_cparams(2), name="gather_pages",
    )(page_table, *([cache] * per_step))


def _stack_heads(q_ref, g, tq):
    lane = lax.broadcasted_iota(jnp.int32, (tq, LANES), 1)
    keep = (lane >= HEAD_DIM * g) & (lane < HEAD_DIM * (g + 1))
    parts = []
    for h in range(HPG):
        hh = g * HPG + h
        ch = hh // 2
        c = q_ref[:, ch * LANES:(ch + 1) * LANES].astype(F32)
        if hh % 2 != g:
            c = pltpu.roll(c, HEAD_DIM, axis=1)
        parts.append(jnp.where(keep, c, 0.0))
    return jnp.concatenate(parts, axis=0).astype(BF)


def _cmp_branch(q, kc, vc, tpos_col, n_cb, tq):
    ncp = kc.shape[0]
    s = _dot_nt(q, kc).reshape(HPG, tq, ncp)
    cidx = lax.broadcasted_iota(jnp.int32, (tq, ncp), 1)
    mask = ((cidx * D_CMP + (L_CMP - 1)) <= tpos_col) & (cidx < n_cb)
    sm = jnp.where(mask[None], s, NEG)
    m = jnp.max(sm, axis=-1, keepdims=True)
    e = jnp.exp(sm - m)
    p = jnp.where(mask[None], e / jnp.sum(e, axis=-1, keepdims=True), 0.0)
    o = _dot(p.reshape(HPG * tq, ncp).astype(BF), vc)
    return o, jnp.sum(p, axis=0)


def _select_blocks(psum, mt_ref, imp_ref, tpos_row, n_sb, tq):
    sp = mt_ref.shape[0]
    if tq < LANES:
        psum = jnp.concatenate([psum, jnp.zeros((LANES - tq, psum.shape[1]), F32)], axis=0)
    mt = mt_ref[...]
    p_hi = psum.astype(BF)
    r1 = psum - p_hi.astype(F32)
    p_mid = r1.astype(BF)
    p_lo = (r1 - p_mid.astype(F32)).astype(BF)
    imp = _dot_nt(mt, p_hi) + _dot_nt(mt, p_mid) + _dot_nt(mt, p_lo)
    s_idx = lax.broadcasted_iota(jnp.int32, (sp, LANES), 0)
    cur = jnp.right_shift(tpos_row, L_SLC.bit_length() - 1)
    forced = (s_idx == 0) | (s_idx == cur) | (s_idx == cur - 1)
    imp = jnp.where(forced, jnp.inf, imp)
    imp = jnp.where(s_idx * L_SLC <= tpos_row, imp, -jnp.inf)
    imp_ref[...] = imp

    def body(i, cnt):
        row = imp_ref[pl.ds(i, 1), :]
        beats = (row > imp) | ((row == imp) & (i < s_idx))
        return cnt + jnp.where(beats, 1.0, 0.0)

    cnt = lax.fori_loop(0, n_sb, body, jnp.zeros((sp, LANES), F32))
    sel_t = jnp.where(cnt < float(N_SEL), 1.0, 0.0)
    return sel_t.T[:tq].astype(BF)


def _flash_init(tq):
    return (jnp.full((HPG, tq, 1), NEG, F32), jnp.zeros((HPG, tq, 1), F32), jnp.zeros((HPG * tq, LANES), F32))


def _flash_tile(carry, q, k, v, mask, tq):
    m, l, acc = carry
    kt = k.shape[0]
    s = _dot_nt(q, k).reshape(HPG, tq, kt)
    sm = jnp.where(mask[None], s, NEG)
    m_new = jnp.maximum(m, jnp.max(sm, axis=-1, keepdims=True))
    alpha = jnp.exp(m - m_new)
    p = jnp.where(mask[None], jnp.exp(sm - m_new), 0.0)
    l = alpha * l + jnp.sum(p, axis=-1, keepdims=True)
    acc = alpha.reshape(HPG * tq, 1) * acc + _dot(p.reshape(HPG * tq, kt).astype(BF), v)
    return m_new, l, acc


def _flash_out(carry, tq):
    _, l, acc = carry
    return acc / l.reshape(HPG * tq, 1)


def _slc_mask(sel, e_tile, kpos_row, tpos_col):
    return (_dot(sel, e_tile) > 0.5) & (kpos_row <= tpos_col)


def _win_mask(kpos_row, tpos_col):
    dist = tpos_col - kpos_row
    return (dist >= 0) & (dist < WINDOW)


def _merge_heads(o_c, o_s, o_w, gt_ref, g, o_ref, tq):
    lane = lax.broadcasted_iota(jnp.int32, (tq, LANES), 1)
    for pair in range(HPG // 2):
        both = []
        for h in (2 * pair, 2 * pair + 1):
            col = (g * HPG + h) * 3
            rows = slice(h * tq, (h + 1) * tq)
            both.append(o_c[rows] * gt_ref[:, col:col + 1] + o_s[rows] * gt_ref[:, col + 1:col + 2]
                        + o_w[rows] * gt_ref[:, col + 2:col + 3])
        even, odd = both
        if g == 0:
            odd = pltpu.roll(odd, HEAD_DIM, axis=1)
        else:
            even = pltpu.roll(even, HEAD_DIM, axis=1)
        ch = g * (HPG // 2) + pair
        o_ref[:, ch * LANES:(ch + 1) * LANES] = jnp.where(lane < HEAD_DIM, even, odd).astype(o_ref.dtype)


def _attn_prompt_kernel(qs_ref, qrs_ref, gt_ref, kc_ref, vc_ref, slc_ref, win_ref, mt_ref, e3_ref, o_ref, imp_ref,
                        *, n_cb, n_sb):
    tq = qs_ref.shape[0]
    qb = pl.program_id(1)
    t0 = qb * tq
    tpos_col = t0 + lax.broadcasted_iota(jnp.int32, (tq, 1), 0)
    tpos_row = t0 + lax.broadcasted_iota(jnp.int32, (1, LANES), 1)
    n_slc = (t0 + tq + SLC_TILE - 1) // SLC_TILE
    win_lo = jnp.maximum(qb - WINDOW // WIN_TILE, 0)
    kv_lanes = (slice(0, LANES), slice(LANES, 2 * LANES))
    for g in range(N_KV_HEADS):
        qc = _stack_heads(qs_ref, g, tq)
        qr = _stack_heads(qrs_ref, g, tq)
        o_c, psum = _cmp_branch(qc, kc_ref[...], vc_ref[...], tpos_col, n_cb, tq)
        sel = _select_blocks(psum, mt_ref, imp_ref, tpos_row, n_sb, tq)

        def slc_body(j, carry):
            rows = pl.ds(pl.multiple_of(j * SLC_TILE, SLC_TILE), SLC_TILE)
            kpos = j * SLC_TILE + lax.broadcasted_iota(jnp.int32, (1, SLC_TILE), 1)
            mask = _slc_mask(sel, e3_ref[j], kpos, tpos_col)
            return _flash_tile(carry, qr, slc_ref[rows, kv_lanes[0]], slc_ref[rows, kv_lanes[1]], mask, tq)

        o_s = _flash_out(lax.fori_loop(0, n_slc, slc_body, _flash_init(tq)), tq)

        def win_body(j, carry):
            rows = pl.ds(pl.multiple_of(j * WIN_TILE, WIN_TILE), WIN_TILE)
            kpos = j * WIN_TILE + lax.broadcasted_iota(jnp.int32, (1, WIN_TILE), 1)
            mask = _win_mask(kpos, tpos_col)
            return _flash_tile(carry, qr, win_ref[rows, kv_lanes[0]], win_ref[rows, kv_lanes[1]], mask, tq)

        o_w = _flash_out(lax.fori_loop(win_lo, qb + 1, win_body, _flash_init(tq)), tq)
        _merge_heads(o_c, o_s, o_w, gt_ref, g, o_ref, tq)


def _attn_prompt(qs, qrs, gates, kc, vc, slc_bf, win_bf, mt, e3, b, t, n_cb, n_sb):
    tq = Q_BLOCK
    nqb = t // tq
    d = qs.shape[1]
    row = lambda w: pl.BlockSpec((tq, w), lambda bi, qi: (bi * nqb + qi, 0))
    per_b = lambda a: pl.BlockSpec((None,) + a.shape[1:], lambda bi, qi: (bi, 0, 0))
    return pl.pallas_call(
        functools.partial(_attn_prompt_kernel, n_cb=n_cb, n_sb=n_sb),
        grid=(b, nqb),
        in_specs=[row(d), row(d), row(LANES), per_b(kc), per_b(vc), per_b(slc_bf), per_b(win_bf),
                  _const_spec(mt.shape), _const_spec(e3.shape)],
        out_specs=row(d),
        out_shape=jax.ShapeDtypeStruct((b * t, d), BF),
        scratch_shapes=[pltpu.VMEM((mt.shape[0], LANES), F32)],
        compiler_params=_cparams(2), name="attn_prompt",
    )(qs, qrs, gates, kc, vc, slc_bf, win_bf, mt, e3)


def _attn_sample_kernel(qs_ref, qrs_ref, gt_ref, kc_ref, vc_ref, slc_ref, slct_ref, win_ref, wint_ref, mt_ref, e3_ref,
                        et_ref, o_ref, imp_ref, *, n_cb, n_sb, past):
    tq = qs_ref.shape[0]
    tpos_col = jnp.full((tq, 1), past, jnp.int32)
    tpos_row = jnp.full((1, LANES), past, jnp.int32)
    n_slc = slc_ref.shape[0] // SLC_TILE
    w_buf = win_ref.shape[0]
    tail = slct_ref.shape[0]
    kv_lanes = (slice(0, LANES), slice(LANES, 2 * LANES))
    tail_pos = past + lax.broadcasted_iota(jnp.int32, (1, tail), 1)
    for g in range(N_KV_HEADS):
        qc = _stack_heads(qs_ref, g, tq)
        qr = _stack_heads(qrs_ref, g, tq)
        o_c, psum = _cmp_branch(qc, kc_ref[...], vc_ref[...], tpos_col, n_cb, tq)
        sel = _select_blocks(psum, mt_ref, imp_ref, tpos_row, n_sb, tq)

        def slc_body(j, carry):
            rows = pl.ds(pl.multiple_of(j * SLC_TILE, SLC_TILE), SLC_TILE)
            kpos = j * SLC_TILE + lax.broadcasted_iota(jnp.int32, (1, SLC_TILE), 1)
            mask = _slc_mask(sel, e3_ref[j], kpos, tpos_col)
            return _flash_tile(carry, qr, slc_ref[rows, kv_lanes[0]], slc_ref[rows, kv_lanes[1]], mask, tq)

        carry = lax.fori_loop(0, n_slc, slc_body, _flash_init(tq))
        carry = _flash_tile(carry, qr, slct_ref[:, kv_lanes[0]], slct_ref[:, kv_lanes[1]],
                            _slc_mask(sel, et_ref[...], tail_pos, tpos_col), tq)
        o_s = _flash_out(carry, tq)

        carry = _flash_init(tq)
        for j in range(w_buf // WIN_TILE):
            rows = slice(j * WIN_TILE, (j + 1) * WIN_TILE)
            kpos = past - w_buf + j * WIN_TILE + lax.broadcasted_iota(jnp.int32, (1, WIN_TILE), 1)
            carry = _flash_tile(carry, qr, win_ref[rows, kv_lanes[0]].astype(BF), win_ref[rows, kv_lanes[1]].astype(BF),
                                _win_mask(kpos, tpos_col), tq)
        carry = _flash_tile(carry, qr, wint_ref[:, kv_lanes[0]], wint_ref[:, kv_lanes[1]],
                            _win_mask(tail_pos, tpos_col), tq)
        o_w = _flash_out(carry, tq)
        _merge_heads(o_c, o_s, o_w, gt_ref, g, o_ref, tq)


def _attn_sample(qs, qrs, gates, kc, vc, slc_g, slc_tail, win_c, win_tail, mt, e3, et, n_cb, n_sb, past):
    db = qs.shape[0]
    per_b = lambda a: pl.BlockSpec((None,) + a.shape[1:], lambda bi: (bi, 0, 0))
    args = (qs, qrs, gates, kc, vc, slc_g, slc_tail, win_c, win_tail)
    return pl.pallas_call(
        functools.partial(_attn_sample_kernel, n_cb=n_cb, n_sb=n_sb, past=past),
        grid=(db,),
        in_specs=[per_b(a) for a in args] + [_const_spec(mt.shape), _const_spec(e3.shape), _const_spec(et.shape)],
        out_specs=per_b(qs),
        out_shape=jax.ShapeDtypeStruct(qs.shape, BF),
        scratch_shapes=[pltpu.VMEM((mt.shape[0], LANES), F32)],
        compiler_params=_cparams(1), name="attn_sample",
    )(*args, mt, e3, et)


def _block_diag(blocks):
    n, k, j = blocks.shape
    eye = jnp.eye(n, dtype=blocks.dtype)
    return (eye[:, None, :, None] * blocks[:, :, None, :]).reshape(n * k, n * j)


def _pad_rows8(rows):
    return jnp.pad(rows, ((0, 0), (SUBLANES - rows.shape[1], 0), (0, 0)))


def _rope_tables(pos):
    half = HEAD_DIM // 2
    inv_freq = ROPE_THETA ** (-jnp.arange(half, dtype=F32) / half)
    ang = pos.astype(F32)[:, None] * inv_freq[None, :]
    cos, sin = jnp.cos(ang), jnp.sin(ang)
    reps = LANES // HEAD_DIM
    return (jnp.tile(jnp.concatenate([cos, cos], axis=1), (1, reps)),
            jnp.tile(jnp.concatenate([-sin, sin], axis=1), (1, reps)))


def _overlap_t(n_cb, n_sb, ncp, sp):
    c0 = np.arange(ncp)[None, :] * D_CMP
    s0 = np.arange(sp)[:, None] * L_SLC
    m = (c0 < s0 + L_SLC) & (c0 + L_CMP > s0) & (np.arange(ncp)[None, :] < n_cb) & (np.arange(sp)[:, None] < n_sb)
    return jnp.asarray(m, dtype=BF)


def _expand_tiles(n_tiles, sp, tile, first_key=0):
    key = first_key + np.arange(n_tiles)[:, None, None] * tile + np.arange(tile)[None, None, :]
    return jnp.asarray(np.arange(sp)[None, :, None] == key // L_SLC, dtype=BF)


def _round_up(x, m):
    return (x + m - 1) // m * m


def kernel(x_prompt, x_sample, p_prompt, p_sample, cache_cmp_kv, cache_slc_kv, cache_win_kv, state_rg_conv,
           state_rg_h, state_ffn_conv, page_table, g_mix, g_ffn, g_ple, g_final, rg_w_in, rg_conv_w, rg_conv_b,
           rg_w_a, rg_b_a, rg_w_x, rg_b_x, rg_lambda, rg_w_out, g_kv, w_kv, cmp_pos, cmp_w1, cmp_b1, cmp_w2,
           attn_w_qg, attn_w_o, ffn_w_up, ffn_conv_w, ffn_conv_b, ffn_w_down, ple_w_in, ple_w_gate):
    b, t, d = x_prompt.shape
    db = x_sample.shape[0]
    assert x_sample.shape[1] == 1, "the sample path handles one new token per sequence"
    depth = g_mix.shape[0]
    n_a = rg_w_in.shape[0]
    d_rnn = rg_w_out.shape[1]
    n_pages = page_table.shape[1]
    past = n_pages * PAGE_SIZE
    kvc = 2 * N_KV_HEADS * HEAD_DIM
    n_q = N_KV_HEADS * HPG * HEAD_DIM

    row1 = lambda v: v.reshape(1, -1)
    pad8 = lambda w: jnp.pad(w, ((0, SUBLANES - w.shape[0]), (0, 0)))

    rg_in_bf = rg_w_in.astype(BF)
    rg_out_bf = rg_w_out.astype(BF)
    rg_gate_bf = [jnp.concatenate([_block_diag(rg_w_a[i]), _block_diag(rg_w_x[i])], axis=1).astype(BF)
                  for i in range(n_a)]
    rg_gate_b = [row1(jnp.concatenate([rg_b_a[i].reshape(-1), rg_b_x[i].reshape(-1)])) for i in range(n_a)]
    rg_c = [row1(-RG_C * jax.nn.softplus(-rg_lambda[i])) for i in range(n_a)]
    up_bf = ffn_w_up.astype(BF)
    down_bf = ffn_w_down.astype(BF)
    gate_bf = ple_w_gate.astype(BF)
    pin_bf = ple_w_in.astype(BF)
    wkv_bf = w_kv.astype(BF)
    wq_bf = attn_w_qg[:, :, :n_q].astype(BF)
    wgl_bf = jnp.pad(attn_w_qg[:, :, n_q:], ((0, 0), (0, 0), (0, LANES - 3 * N_KV_HEADS * HPG))).astype(BF)
    wo_bf = attn_w_o.astype(BF)
    gfin = row1(g_final)

    w1 = cmp_w1.reshape(2, 2, D_CMP, HEAD_DIM, -1)
    eye_g = jnp.eye(N_KV_HEADS, dtype=F32)
    w1_bd = (eye_g[None, None, None, :, None, :, None] * w1[:, :, :, None, :, None, :]).reshape(
        2, 2, D_CMP, N_KV_HEADS * HEAD_DIM, -1).astype(BF)
    pos_t = jnp.tile(jnp.transpose(cmp_pos, (1, 0, 2)), (1, 1, N_KV_HEADS)).reshape(2, 2, D_CMP, LANES)
    cmp_params = (pos_t[:, 0], pos_t[:, 1], w1_bd[:, 0], w1_bd[:, 1],
                  jnp.tile(cmp_b1, (1, N_KV_HEADS)).reshape(2, 1, -1),
                  jnp.stack([_block_diag(jnp.stack([cmp_w2[j]] * N_KV_HEADS)) for j in range(2)]).astype(BF))

    def layer_consts(i, final):
        return (row1(g_ffn[i]), up_bf[i], pad8(ffn_conv_w[i]), row1(ffn_conv_b[i]), down_bf[i], row1(g_ple[i]),
                gate_bf[i], pin_bf[i], gfin), final

    n = b * t
    x = x_prompt.reshape(n, d)
    p = p_prompt.reshape(depth, n, -1)
    cos_p, sin_p = _rope_tables(jnp.tile(jnp.arange(t, dtype=jnp.int32), b))
    rg_conv_p, rg_h_p, ffn_conv_p = [], [], []
    zeros_rg = jnp.zeros((b, SUBLANES, d_rnn), F32)
    zeros_ffn = jnp.zeros((b, SUBLANES, ffn_w_up.shape[2]), F32)
    for i in range(depth):
        if i == n_a:
            kv_p, slc_bf, win_bf = _kv_proj(x, row1(g_kv), wkv_bf, cos_p, sin_p)
            kc_p, vc_p = _compress(kv_p.reshape(b, t, -1), cmp_params, b, t)
            n_cb_p = (t - L_CMP) // D_CMP + 1
            n_sb_p = t // L_SLC
            mt_p = _overlap_t(n_cb_p, n_sb_p, t // D_CMP, _round_up(n_sb_p, LANES))
            e3_p = _expand_tiles(t // SLC_TILE, _round_up(n_sb_p, LANES), SLC_TILE)
            slc_bf = slc_bf.reshape(b, t, -1)
            win_bf = win_bf.reshape(b, t, -1)
        if i < n_a:
            yg, xr = _rg_in(x, row1(g_mix[i]), rg_in_bf[i])
            gated, h_last = _rg_core(xr, yg, zeros_rg, zeros_rg, pad8(rg_conv_w[i]), row1(rg_conv_b[i]),
                                     rg_gate_bf[i], rg_gate_b[i], rg_c[i], b, t)
            x = _matmul_res(gated, rg_out_bf[i], x)
            rg_conv_p.append(xr.reshape(b, t, -1)[:, t - 3:])
            rg_h_p.append(h_last[:, 0])
        else:
            j = i - n_a
            qs, qrs, gates = _q_proj(x, row1(g_mix[i]), wq_bf[j], wgl_bf[j], cos_p, sin_p)
            o = _attn_prompt(qs, qrs, gates, kc_p, vc_p, slc_bf, win_bf, mt_p, e3_p, b, t, n_cb_p, n_sb_p)
            x = _matmul_res(o, wo_bf[j], x)
        consts, final = layer_consts(i, i == depth - 1)
        x, f_hist = _ffn(x, p[i], zeros_ffn, *consts, b, t, final)
        ffn_conv_p.append(f_hist[:, SUBLANES - 2:])
    y_prompt = x.reshape(b, t, d)
    kv5 = kv_p.reshape(b, t, 6, N_KV_HEADS, HEAD_DIM)
    keep = min(WINDOW, t)
    cmp_p, slc_p, win_p = kv5[:, :, 0:2], kv5[:, :, 2:4], kv5[:, t - keep:, 4:6]

    x = x_sample.reshape(db, d)
    p = p_sample.reshape(depth, db, -1)
    cos_s, sin_s = _rope_tables(jnp.full((db,), past, jnp.int32))
    rg_conv_s, rg_h_s, ffn_conv_s = [], [], []
    for i in range(depth):
        if i == n_a:
            kv_s, slc_new_bf, win_new_bf = _kv_proj(x, row1(g_kv), wkv_bf, cos_s, sin_s)
            cmp_rows = _gather_pages(cache_cmp_kv.reshape(-1, PAGE_SIZE, kvc), page_table, F32)
            slc_rows = _gather_pages(cache_slc_kv.reshape(-1, PAGE_SIZE, kvc), page_table, BF)
            kc_s, vc_s = _compress(cmp_rows, cmp_params, db, past)
            n_cb_s = (past + 1 - L_CMP) // D_CMP + 1
            n_sb_s = -(-(past + 1) // L_SLC)
            sp_s = _round_up(n_sb_s, LANES)
            mt_s = _overlap_t(n_cb_s, n_sb_s, past // D_CMP, sp_s)
            e3_s = _expand_tiles(past // SLC_TILE, sp_s, SLC_TILE)
            et_s = _expand_tiles(1, sp_s, LANES, first_key=past)[0]
            tail = lambda a: jnp.pad(a.reshape(db, 1, -1), ((0, 0), (0, LANES - 1), (0, 0)))
            slc_tail, win_tail = tail(slc_new_bf), tail(win_new_bf)
            win_c = cache_win_kv.reshape(db, cache_win_kv.shape[1], kvc)
        if i < n_a:
            yg, xr = _rg_in(x, row1(g_mix[i]), rg_in_bf[i])
            hist = state_rg_conv[i]
            gated, h_new = _rg_step(xr, yg, hist[:, 0], hist[:, 1], hist[:, 2], state_rg_h[i], pad8(rg_conv_w[i]),
                                    row1(rg_conv_b[i]), rg_gate_bf[i], rg_gate_b[i], rg_c[i])
            x = _matmul_res(gated, rg_out_bf[i], x)
            rg_conv_s.append(jnp.concatenate([hist[:, 1:], xr[:, None, :]], axis=1))
            rg_h_s.append(h_new)
        else:
            j = i - n_a
            qs, qrs, gates = _q_proj(x, row1(g_mix[i]), wq_bf[j], wgl_bf[j], cos_s, sin_s)
            rep = lambda a: jnp.broadcast_to(a[:, None, :], (db, SUBLANES, a.shape[1]))
            o = _attn_sample(rep(qs), rep(qrs), rep(gates), kc_s, vc_s, slc_rows, slc_tail, win_c, win_tail,
                             mt_s, e3_s, et_s, n_cb_s, n_sb_s, past)
            x = _matmul_res(o[:, 0], wo_bf[j], x)
        consts, final = layer_consts(i, i == depth - 1)
        fh = state_ffn_conv[i]
        x, up = _ffn_step(x, p[i], fh[:, 0], fh[:, 1], *consts, final)
        ffn_conv_s.append(jnp.stack([fh[:, 1], up], axis=1))
    y_sample = x.reshape(db, 1, d)
    kv5 = kv_s.reshape(db, 1, 6, N_KV_HEADS, HEAD_DIM)
    cmp_s, slc_s, win_new = kv5[:, :, 0:2], kv5[:, :, 2:4], kv5[:, :, 4:6]
    win_full = jnp.concatenate([cache_win_kv, win_new], axis=1)
    keep = min(WINDOW, past + 1)
    win_s = win_full[:, win_full.shape[1] - keep:]

    return (y_prompt, y_sample, cmp_p, cmp_s, slc_p, slc_s, win_p, win_s,
            jnp.stack(rg_conv_p), jnp.stack(rg_conv_s), jnp.stack(rg_h_p), jnp.stack(rg_h_s),
            jnp.stack(ffn_conv_p), jnp.stack(ffn_conv_s))
```

```python
import functools

import jax
import jax.numpy as jnp
import numpy as np
from jax import lax
from jax.experimental import pallas as pl
from jax.experimental.pallas import tpu as pltpu

BF = jnp.bfloat16
F32 = jnp.float32

EPS = 1e-6
RG_C = 8.0
N_RG_BLOCKS = 16
HEAD_DIM = 64
N_KV_HEADS = 2
HPG = 8
L_CMP = 32
D_CMP = 16
L_SLC = 64
N_SEL = 16
WINDOW = 512
Q_BLOCK = 128
ROPE_THETA = 10000.0
PAGE_SIZE = 128
NEG = -1e30
LANES = 128
SUBLANES = 8
SLC_TILE = 512
WIN_TILE = 128
VMEM_LIMIT = 56 * 1024 * 1024


def _cparams(n_axes):
    return pltpu.CompilerParams(dimension_semantics=("arbitrary",) * n_axes,
                                vmem_limit_bytes=VMEM_LIMIT)


def _const_spec(shape):
    zeros = (0,) * len(shape)
    return pl.BlockSpec(shape, lambda *_: zeros, pipeline_mode=pl.Buffered(1))


def _dot(a, b):
    return jnp.dot(a, b, preferred_element_type=F32)


def _dot_nt(a, b):
    return lax.dot_general(a, b, (((1,), (1,)), ((), ())), preferred_element_type=F32)


def _rms(x, g):
    return x * lax.rsqrt(jnp.mean(x * x, axis=-1, keepdims=True) + EPS) * g


def _gelu(x):
    return jax.nn.gelu(x, approximate=True)


def _sigmoid(x):
    return jax.nn.sigmoid(x)


def _shift_rows(x, prev8, j):
    if j == 0:
        return x
    r = pltpu.roll(x, j, axis=0)
    p = pltpu.roll(prev8, j, axis=0)
    row = lax.broadcasted_iota(jnp.int32, (SUBLANES, x.shape[1]), 0)
    head = jnp.where(row < j, p, r[:SUBLANES])
    if x.shape[0] == SUBLANES:
        return head
    return jnp.concatenate([head, r[SUBLANES:]], axis=0)


def _rope(z, cos, sin_signed):
    lane = lax.broadcasted_iota(jnp.int32, z.shape, 1)
    first = (lane & (HEAD_DIM - 1)) < (HEAD_DIM // 2)
    rot = jnp.where(first, pltpu.roll(z, LANES - HEAD_DIM // 2, axis=1), pltpu.roll(z, HEAD_DIM // 2, axis=1))
    return z * cos + rot * sin_signed


def _rg_in_kernel(x_ref, g_ref, w_ref, yg_ref, xr_ref, *, chunk):
    h = _rms(x_ref[...], g_ref[...]).astype(BF)
    n = yg_ref.shape[1]
    for c0 in range(0, n, chunk):
        yg_ref[:, c0:c0 + chunk] = _gelu(_dot(h, w_ref[:, c0:c0 + chunk]))
        xr_ref[:, c0:c0 + chunk] = _dot(h, w_ref[:, n + c0:n + c0 + chunk])


def _rg_in(x, g, w_bf):
    n, d = x.shape
    dr = w_bf.shape[1] // 2
    tm = min(512, n)
    return pl.pallas_call(
        functools.partial(_rg_in_kernel, chunk=256),
        grid=(n // tm,),
        in_specs=[pl.BlockSpec((tm, d), lambda i: (i, 0)), _const_spec((1, d)), _const_spec(w_bf.shape)],
        out_specs=[pl.BlockSpec((tm, dr), lambda i: (i, 0)), pl.BlockSpec((tm, dr), lambda i: (i, 0))],
        out_shape=[jax.ShapeDtypeStruct((n, dr), F32), jax.ShapeDtypeStruct((n, dr), F32)],
        compiler_params=_cparams(1), name="rg_in",
    )(x, g, w_bf)


def _rg_gates(xc, wg_ref, bg_ref, c_ref):
    d = xc.shape[1]
    gates = _dot(xc.astype(BF), wg_ref[...]) + bg_ref[...]
    r = _sigmoid(gates[:, :d])
    i = _sigmoid(gates[:, d:])
    log_a = c_ref[...] * r
    a = jnp.exp(log_a)
    mult = jnp.sqrt(-jnp.tanh(log_a) * (a * a + 1.0))
    return a, mult, i * xc


def _rg_core_kernel(xr_ref, yg_ref, hist_ref, h0_ref, cw_ref, cb_ref, wg_ref, bg_ref, c_ref,
                    out_ref, hl_ref, xprev, hc):
    t = pl.program_id(1)

    @pl.when(t == 0)
    def _():
        xprev[...] = hist_ref[...]
        hc[...] = h0_ref[...]

    x = xr_ref[...]
    tt, d = x.shape
    prev = xprev[...]
    xc = cb_ref[...] + _shift_rows(x, prev, 3) * cw_ref[0:1, :]
    xc = xc + _shift_rows(x, prev, 2) * cw_ref[1:2, :]
    xc = xc + _shift_rows(x, prev, 1) * cw_ref[2:3, :]
    xc = xc + x * cw_ref[3:4, :]
    xprev[...] = x[tt - SUBLANES:, :]

    a, mult, ixc = _rg_gates(xc, wg_ref, bg_ref, c_ref)
    row = lax.broadcasted_iota(jnp.int32, (tt, d), 0)
    mult = jnp.where((row == 0) & (t == 0), 1.0, mult)
    u = mult * ixc

    av, bv = a, u
    dist = 1
    while dist < tt:
        if dist < SUBLANES:
            ok = row >= dist
            bv = jnp.where(ok, av * pltpu.roll(bv, dist, axis=0) + bv, bv)
            av = jnp.where(ok, av * pltpu.roll(av, dist, axis=0), av)
        else:
            bv = jnp.concatenate([bv[:dist], av[dist:] * bv[:tt - dist] + bv[dist:]], axis=0)
            av = jnp.concatenate([av[:dist], av[dist:] * av[:tt - dist]], axis=0)
        dist *= 2
    h = av * hc[0:1, :] + bv
    hc[...] = jnp.broadcast_to(h[tt - 1:tt, :], (SUBLANES, d))
    hl_ref[...] = hc[...]
    out_ref[...] = (yg_ref[...] * h).astype(BF)


def _rg_core(xr, yg, hist8, h08, cw8, cb, wg_bf, bg, c, b, t):
    n, d = xr.shape
    tt = min(256, t)
    nt = t // tt
    row_spec = pl.BlockSpec((tt, d), lambda bi, ti: (bi * nt + ti, 0))
    st_spec = pl.BlockSpec((None, SUBLANES, d), lambda bi, ti: (bi, 0, 0))
    return pl.pallas_call(
        _rg_core_kernel,
        grid=(b, nt),
        in_specs=[row_spec, row_spec, st_spec, st_spec, _const_spec(cw8.shape), _const_spec(cb.shape),
                  _const_spec(wg_bf.shape), _const_spec(bg.shape), _const_spec(c.shape)],
        out_specs=[row_spec, st_spec],
        out_shape=[jax.ShapeDtypeStruct((n, d), BF), jax.ShapeDtypeStruct((b, SUBLANES, d), F32)],
        scratch_shapes=[pltpu.VMEM((SUBLANES, d), F32), pltpu.VMEM((SUBLANES, d), F32)],
        compiler_params=_cparams(2), name="rg_core",
    )(xr, yg, hist8, h08, cw8, cb, wg_bf, bg, c)


def _rg_step_kernel(xr_ref, yg_ref, h0_ref, h1_ref, h2_ref, hs_ref, cw_ref, cb_ref, wg_ref, bg_ref, c_ref,
                    out_ref, hn_ref):
    x = xr_ref[...]
    xc = cb_ref[...] + h0_ref[...] * cw_ref[0:1, :]
    xc = xc + h1_ref[...] * cw_ref[1:2, :]
    xc = xc + h2_ref[...] * cw_ref[2:3, :]
    xc = xc + x * cw_ref[3:4, :]
    a, mult, ixc = _rg_gates(xc, wg_ref, bg_ref, c_ref)
    h = mult * ixc + a * hs_ref[...]
    hn_ref[...] = h
    out_ref[...] = (yg_ref[...] * h).astype(BF)


def _rg_step(xr, yg, h0, h1, h2, hs, cw8, cb, wg_bf, bg, c):
    n, d = xr.shape
    args = (xr, yg, h0, h1, h2, hs, cw8, cb, wg_bf, bg, c)
    return pl.pallas_call(
        _rg_step_kernel,
        grid=(1,),
        in_specs=[_const_spec(a.shape) for a in args],
        out_specs=[pl.BlockSpec((n, d), lambda i: (0, 0)), pl.BlockSpec((n, d), lambda i: (0, 0))],
        out_shape=[jax.ShapeDtypeStruct((n, d), BF), jax.ShapeDtypeStruct((n, d), F32)],
        compiler_params=_cparams(1), name="rg_step",
    )(*args)


def _matmul_res_kernel(a_ref, w_ref, x_ref, o_ref):
    o_ref[...] = x_ref[...] + _dot(a_ref[...], w_ref[...])


def _matmul_res(a_bf, w_bf, x):
    n, k = a_bf.shape
    d = w_bf.shape[1]
    tm = min(512, n)
    return pl.pallas_call(
        _matmul_res_kernel,
        grid=(n // tm,),
        in_specs=[pl.BlockSpec((tm, k), lambda i: (i, 0)), _const_spec(w_bf.shape),
                  pl.BlockSpec((tm, d), lambda i: (i, 0))],
        out_specs=pl.BlockSpec((tm, d), lambda i: (i, 0)),
        out_shape=jax.ShapeDtypeStruct((n, d), F32),
        compiler_params=_cparams(1), name="matmul_res",
    )(a_bf, w_bf, x)


def _ffn_tail(x, acc, p_ref, gp_ref, wgate_ref, wpin_ref, gfin_ref, final_norm):
    x1 = x + acc
    hp = _rms(x1, gp_ref[...]).astype(BF)
    sig = _sigmoid(_dot(hp, wgate_ref[...]))
    pe = _dot(p_ref[...].astype(BF), wpin_ref[...])
    x2 = x1 + pe * sig
    if final_norm:
        x2 = _rms(x2, gfin_ref[...])
    return x2


def _ffn_kernel(x_ref, p_ref, hist_ref, gf_ref, wup_ref, cw_ref, cb_ref, wdn_ref, gp_ref, wgate_ref, wpin_ref,
                gfin_ref, o_ref, ho_ref, uprev, *, chunk, final_norm):
    t = pl.program_id(1)

    @pl.when(t == 0)
    def _():
        uprev[...] = hist_ref[...]

    x = x_ref[...]
    tm = x.shape[0]
    dff = wdn_ref.shape[0]
    h = _rms(x, gf_ref[...]).astype(BF)
    acc = jnp.zeros(x.shape, F32)
    for c0 in range(0, dff, chunk):
        halves = []
        for off in (c0, dff + c0):
            up = _dot(h, wup_ref[:, off:off + chunk])
            prev = uprev[:, off:off + chunk]
            uc = cb_ref[:, off:off + chunk] + _shift_rows(up, prev, 2) * cw_ref[0:1, off:off + chunk]
            uc = uc + _shift_rows(up, prev, 1) * cw_ref[1:2, off:off + chunk]
            uc = uc + up * cw_ref[2:3, off:off + chunk]
            uprev[:, off:off + chunk] = up[tm - SUBLANES:, :]
            halves.append(uc)
        act = (_gelu(halves[0]) * halves[1]).astype(BF)
        acc = acc + _dot(act, wdn_ref[c0:c0 + chunk, :])
    ho_ref[...] = uprev[...]
    o_ref[...] = _ffn_tail(x, acc, p_ref, gp_ref, wgate_ref, wpin_ref, gfin_ref, final_norm)


def _ffn(x, p, hist8, gf, wup_bf, cw8, cb, wdn_bf, gp, wgate_bf, wpin_bf, gfin, b, t, final_norm):
    n, d = x.shape
    dp = p.shape[1]
    d2 = wup_bf.shape[1]
    tm = min(512, t)
    nt = t // tm
    row = lambda w: pl.BlockSpec((tm, w), lambda bi, ti: (bi * nt + ti, 0))
    st_spec = pl.BlockSpec((None, SUBLANES, d2), lambda bi, ti: (bi, 0, 0))
    consts = (gf, wup_bf, cw8, cb, wdn_bf, gp, wgate_bf, wpin_bf, gfin)
    return pl.pallas_call(
        functools.partial(_ffn_kernel, chunk=512, final_norm=final_norm),
        grid=(b, nt),
        in_specs=[row(d), row(dp), st_spec] + [_const_spec(a.shape) for a in consts],
        out_specs=[row(d), st_spec],
        out_shape=[jax.ShapeDtypeStruct((n, d), F32), jax.ShapeDtypeStruct((b, SUBLANES, d2), F32)],
        scratch_shapes=[pltpu.VMEM((SUBLANES, d2), F32)],
        compiler_params=_cparams(2), name="ffn",
    )(x, p, hist8, *consts)


def _ffn_step_kernel(x_ref, p_ref, h0_ref, h1_ref, gf_ref, wup_ref, cw_ref, cb_ref, wdn_ref, gp_ref, wgate_ref,
                     wpin_ref, gfin_ref, o_ref, up_ref, *, chunk, final_norm):
    x = x_ref[...]
    dff = wdn_ref.shape[0]
    h = _rms(x, gf_ref[...]).astype(BF)
    acc = jnp.zeros(x.shape, F32)
    for c0 in range(0, dff, chunk):
        halves = []
        for off in (c0, dff + c0):
            up = _dot(h, wup_ref[:, off:off + chunk])
            up_ref[:, off:off + chunk] = up
            uc = cb_ref[:, off:off + chunk] + h0_ref[:, off:off + chunk] * cw_ref[0:1, off:off + chunk]
            uc = uc + h1_ref[:, off:off + chunk] * cw_ref[1:2, off:off + chunk]
            uc = uc + up * cw_ref[2:3, off:off + chunk]
            halves.append(uc)
        act = (_gelu(halves[0]) * halves[1]).astype(BF)
        acc = acc + _dot(act, wdn_ref[c0:c0 + chunk, :])
    o_ref[...] = _ffn_tail(x, acc, p_ref, gp_ref, wgate_ref, wpin_ref, gfin_ref, final_norm)


def _ffn_step(x, p, h0, h1, gf, wup_bf, cw8, cb, wdn_bf, gp, wgate_bf, wpin_bf, gfin, final_norm):
    n, d = x.shape
    d2 = wup_bf.shape[1]
    args = (x, p, h0, h1, gf, wup_bf, cw8, cb, wdn_bf, gp, wgate_bf, wpin_bf, gfin)
    return pl.pallas_call(
        functools.partial(_ffn_step_kernel, chunk=512, final_norm=final_norm),
        grid=(1,),
        in_specs=[_const_spec(a.shape) for a in args],
        out_specs=[pl.BlockSpec((n, d), lambda i: (0, 0)), pl.BlockSpec((n, d2), lambda i: (0, 0))],
        out_shape=[jax.ShapeDtypeStruct((n, d), F32), jax.ShapeDtypeStruct((n, d2), F32)],
        compiler_params=_cparams(1), name="ffn_step",
    )(*args)


def _kv_kernel(x_ref, g_ref, w_ref, cos_ref, sin_ref, kv_ref, slc_ref, win_ref):
    h = _rms(x_ref[...], g_ref[...]).astype(BF)
    kv = _dot(h, w_ref[...])
    cos, sin = cos_ref[...], sin_ref[...]
    w2 = 2 * LANES
    k_slc = _rope(kv[:, w2:w2 + LANES], cos, sin)
    k_win = _rope(kv[:, 2 * w2:2 * w2 + LANES], cos, sin)
    kv_ref[:, 0:w2] = kv[:, 0:w2]
    kv_ref[:, w2:w2 + LANES] = k_slc
    kv_ref[:, w2 + LANES:2 * w2] = kv[:, w2 + LANES:2 * w2]
    kv_ref[:, 2 * w2:2 * w2 + LANES] = k_win
    kv_ref[:, 2 * w2 + LANES:3 * w2] = kv[:, 2 * w2 + LANES:3 * w2]
    slc_ref[:, 0:LANES] = k_slc.astype(BF)
    slc_ref[:, LANES:w2] = kv[:, w2 + LANES:2 * w2].astype(BF)
    win_ref[:, 0:LANES] = k_win.astype(BF)
    win_ref[:, LANES:w2] = kv[:, 2 * w2 + LANES:3 * w2].astype(BF)


def _kv_proj(x, g, w_bf, cos, sin):
    n, d = x.shape
    nk = w_bf.shape[1]
    tm = min(512, n)
    row = lambda w: pl.BlockSpec((tm, w), lambda i: (i, 0))
    return pl.pallas_call(
        _kv_kernel,
        grid=(n // tm,),
        in_specs=[row(d), _const_spec((1, d)), _const_spec(w_bf.shape), row(LANES), row(LANES)],
        out_specs=[row(nk), row(2 * LANES), row(2 * LANES)],
        out_shape=[jax.ShapeDtypeStruct((n, nk), F32), jax.ShapeDtypeStruct((n, 2 * LANES), BF),
                   jax.ShapeDtypeStruct((n, 2 * LANES), BF)],
        compiler_params=_cparams(1), name="kv_proj",
    )(x, g, w_bf, cos, sin)


def _q_kernel(x_ref, g_ref, wq_ref, wg_ref, cos_ref, sin_ref, qs_ref, qrs_ref, gt_ref):
    h = _rms(x_ref[...], g_ref[...]).astype(BF)
    cos, sin = cos_ref[...], sin_ref[...]
    scale = HEAD_DIM ** -0.5
    for c0 in range(0, wq_ref.shape[1], LANES):
        q = _dot(h, wq_ref[:, c0:c0 + LANES])
        qs_ref[:, c0:c0 + LANES] = (q * scale).astype(BF)
        qrs_ref[:, c0:c0 + LANES] = (_rope(q, cos, sin) * scale).astype(BF)
    gt_ref[...] = _sigmoid(_dot(h, wg_ref[...]))


def _q_proj(x, g, wq_bf, wg_bf, cos, sin):
    n, d = x.shape
    dq = wq_bf.shape[1]
    tm = min(512, n)
    row = lambda w: pl.BlockSpec((tm, w), lambda i: (i, 0))
    return pl.pallas_call(
        _q_kernel,
        grid=(n // tm,),
        in_specs=[row(d), _const_spec((1, d)), _const_spec(wq_bf.shape), _const_spec(wg_bf.shape),
                  row(LANES), row(LANES)],
        out_specs=[row(dq), row(dq), row(LANES)],
        out_shape=[jax.ShapeDtypeStruct((n, dq), BF), jax.ShapeDtypeStruct((n, dq), BF),
                   jax.ShapeDtypeStruct((n, LANES), F32)],
        compiler_params=_cparams(1), name="q_proj",
    )(x, g, wq_bf, wg_bf, cos, sin)


def _compress_kernel(xk_ref, xv_ref, pa_ref, pb_ref, w1a_ref, w1b_ref, b1_ref, w2_ref, kc_ref, vc_ref):
    nhb = kc_ref.shape[0]
    for j, x_ref, o_ref in ((0, xk_ref, kc_ref), (1, xv_ref, vc_ref)):
        acc_a = jnp.zeros((nhb, 2 * LANES), F32)
        acc_b = jnp.zeros((nhb, 2 * LANES), F32)
        for l in range(D_CMP):
            xl = x_ref[pl.ds(l, nhb, stride=D_CMP), :]
            acc_a = acc_a + _dot((xl + pa_ref[j, l:l + 1, :]).astype(BF), w1a_ref[j, l])
            acc_b = acc_b + _dot((xl + pb_ref[j, l:l + 1, :]).astype(BF), w1b_ref[j, l])
        hid = _gelu(acc_a + pltpu.roll(acc_b, nhb - 1, axis=0) + b1_ref[j])
        o_ref[...] = _dot(hid.astype(BF), w2_ref[j]).astype(BF)


def _compress(rows, params, b, t):
    nhb = t // D_CMP
    out = pl.BlockSpec((None, nhb, LANES), lambda bi: (bi, 0, 0))
    return pl.pallas_call(
        _compress_kernel,
        grid=(b,),
        in_specs=[pl.BlockSpec((None, t, LANES), lambda bi: (bi, 0, 0)),
                  pl.BlockSpec((None, t, LANES), lambda bi: (bi, 0, 1))] + [_const_spec(a.shape) for a in params],
        out_specs=[out, out],
        out_shape=[jax.ShapeDtypeStruct((b, nhb, LANES), BF)] * 2,
        compiler_params=_cparams(1), name="compress",
    )(rows, rows, *params)


def _gather_kernel(pt_ref, *refs):
    del pt_ref
    o_ref = refs[-1]
    for r, in_ref in enumerate(refs[:-1]):
        o_ref[r * PAGE_SIZE:(r + 1) * PAGE_SIZE, :] = in_ref[...].astype(o_ref.dtype)


def _gather_pages(cache, page_table, dtype):
    db, n_pages = page_table.shape
    c = cache.shape[2]
    per_step = min(16, n_pages)
    in_specs = [pl.BlockSpec((None, PAGE_SIZE, c), functools.partial(
        lambda bi, i, pt, r: (pt[bi, i * per_step + r], 0, 0), r=r)) for r in range(per_step)]
    return pl.pallas_call(
        _gather_kernel,
        grid_spec=pltpu.PrefetchScalarGridSpec(
            num_scalar_prefetch=1, grid=(db, n_pages // per_step), in_specs=in_specs,
            out_specs=pl.BlockSpec((None, per_step * PAGE_SIZE, c), lambda bi, i, pt: (bi, i, 0))),
        out_shape=jax.ShapeDtypeStruct((db, n_pages * PAGE_SIZE, c), dtype),
        compiler_params=_cparams(2), name="gather_pages",
    )(page_table, *([cache] * per_step))


def _stack_heads(q_ref, g, tq):
    lane = lax.broadcasted_iota(jnp.int32, (tq, LANES), 1)
    keep = (lane >= HEAD_DIM * g) & (lane < HEAD_DIM * (g + 1))
    parts = []
    for h in range(HPG):
        hh = g * HPG + h
        ch = hh // 2
        c = q_ref[:, ch * LANES:(ch + 1) * LANES].astype(F32)
        if hh % 2 != g:
            c = pltpu.roll(c, HEAD_DIM, axis=1)
        parts.append(jnp.where(keep, c, 0.0))
    return jnp.concatenate(parts, axis=0).astype(BF)


def _cmp_branch(q, kc, vc, tpos_col, n_cb, tq):
    ncp = kc.shape[0]
    cidx = lax.broadcasted_iota(jnp.int32, (tq, ncp), 1)
    valid = ((cidx * D_CMP + (L_CMP - 1)) <= tpos_col) & (cidx < n_cb)
    s = _dot_nt(q, kc).reshape(HPG, tq, ncp) + jnp.where(valid, 0.0, NEG)[None]
    e = jnp.exp(s - jnp.max(s, axis=-1, keepdims=True))
    any_valid = (tpos_col >= L_CMP - 1) & (n_cb > 0)
    inv = jnp.where(any_valid[None], 1.0 / jnp.sum(e, axis=-1, keepdims=True), 0.0)
    p = e * inv
    o = _dot(p.reshape(HPG * tq, ncp).astype(BF), vc)
    return o, jnp.sum(p, axis=0)


def _importance(psum, mt_ref, tpos_row):
    sp = mt_ref.shape[0]
    mt = mt_ref[...]
    p_hi = psum.astype(BF)
    r1 = psum - p_hi.astype(F32)
    p_mid = r1.astype(BF)
    p_lo = (r1 - p_mid.astype(F32)).astype(BF)
    imp = _dot_nt(mt, p_hi) + _dot_nt(mt, p_mid) + _dot_nt(mt, p_lo)
    s_idx = lax.broadcasted_iota(jnp.int32, (sp, LANES), 0)
    cur = jnp.right_shift(tpos_row, L_SLC.bit_length() - 1)
    forced = (s_idx == 0) | (s_idx == cur) | (s_idx == cur - 1)
    imp = jnp.where(forced, jnp.inf, imp)
    return jnp.where(s_idx * L_SLC <= tpos_row, imp, -jnp.inf)


def _select_blocks(psums, mt_ref, imp_ref, tpos_row, n_sb, tq):
    sp = mt_ref.shape[0]
    nr = _round_up(n_sb, SUBLANES)
    imps = []
    for g, psum in enumerate(psums):
        imp = _importance(psum, mt_ref, tpos_row)
        imp_ref[g] = imp
        imps.append(imp[:nr])
    s_idx = lax.broadcasted_iota(jnp.int32, (nr, LANES), 0)

    def body(i, cnts):
        tie = jnp.where(s_idx > i, 1.0, 0.0)
        out = []
        for g, imp in enumerate(imps):
            row = imp_ref[g, pl.ds(i, 1), :]
            out.append(cnts[g] + jnp.where(row > imp, 1.0, jnp.where(row == imp, tie, 0.0)))
        return tuple(out)

    cnts = lax.fori_loop(0, n_sb, body, tuple(jnp.zeros((nr, LANES), F32) for _ in psums))
    sels = []
    for cnt in cnts:
        sel_t = jnp.where(cnt < float(N_SEL), 1.0, 0.0)
        if nr < sp:
            sel_t = jnp.concatenate([sel_t, jnp.zeros((sp - nr, LANES), F32)], axis=0)
        sels.append(sel_t.T[:tq].astype(BF))
    return sels


def _select_single(psum, mt_ref, tpos, tq):
    sp = mt_ref.shape[0]
    prow = jnp.broadcast_to(psum[0:1], (LANES, psum.shape[1]))
    imp = _importance(prow, mt_ref, jnp.full((1, LANES), tpos, jnp.int32))
    a = jnp.broadcast_to(imp[:, 0:1], (sp, sp))
    b = a.T
    r_i = lax.broadcasted_iota(jnp.int32, (sp, sp), 0)
    c_i = lax.broadcasted_iota(jnp.int32, (sp, sp), 1)
    beats = jnp.where(a > b, 1.0, jnp.where(a == b, jnp.where(r_i < c_i, 1.0, 0.0), 0.0))
    cnt = jnp.sum(beats, axis=0, keepdims=True)
    sel = jnp.where(cnt < float(N_SEL), 1.0, 0.0)
    return jnp.broadcast_to(sel, (tq, sp)).astype(BF)


def _flash_init(tq):
    return (jnp.full((HPG, tq, 1), NEG, F32), jnp.zeros((HPG, tq, 1), F32), jnp.zeros((HPG * tq, LANES), F32))


def _flash_tile(carry, q, k, v, bias, tq):
    m, l, acc = carry
    kt = k.shape[0]
    s = _dot_nt(q, k).reshape(HPG, tq, kt) + bias[None]
    m_new = jnp.maximum(m, jnp.max(s, axis=-1, keepdims=True))
    alpha = jnp.exp(m - m_new)
    p = jnp.exp(s - m_new)
    l = alpha * l + jnp.sum(p, axis=-1, keepdims=True)
    acc = alpha.reshape(HPG * tq, 1) * acc + _dot(p.reshape(HPG * tq, kt).astype(BF), v)
    return m_new, l, acc


def _flash_out(carry, tq):
    _, l, acc = carry
    return acc / l.reshape(HPG * tq, 1)


def _attend_once(q, k, v, bias, tq):
    kt = k.shape[0]
    s = _dot_nt(q, k).reshape(HPG, tq, kt) + bias[None]
    p = jnp.exp(s - jnp.max(s, axis=-1, keepdims=True))
    l = jnp.sum(p, axis=-1, keepdims=True)
    return _dot(p.reshape(HPG * tq, kt).astype(BF), v) / l.reshape(HPG * tq, 1)


def _mask_bias(valid):
    return jnp.where(valid, 0.0, NEG)


def _win_valid(kpos_row, tpos_col):
    dist = tpos_col - kpos_row
    return (dist >= 0) & (dist < WINDOW)


def _merge_heads(o_c, o_s, o_w, gt_ref, g, o_ref, tq):
    lane = lax.broadcasted_iota(jnp.int32, (tq, LANES), 1)
    for pair in range(HPG // 2):
        both = []
        for h in (2 * pair, 2 * pair + 1):
            col = (g * HPG + h) * 3
            rows = slice(h * tq, (h + 1) * tq)
            both.append(o_c[rows] * gt_ref[:, col:col + 1] + o_s[rows] * gt_ref[:, col + 1:col + 2]
                        + o_w[rows] * gt_ref[:, col + 2:col + 3])
        even, odd = both
        if g == 0:
            odd = pltpu.roll(odd, HEAD_DIM, axis=1)
        else:
            even = pltpu.roll(even, HEAD_DIM, axis=1)
        ch = g * (HPG // 2) + pair
        o_ref[:, ch * LANES:(ch + 1) * LANES] = jnp.where(lane < HEAD_DIM, even, odd).astype(o_ref.dtype)


def _attn_prompt_kernel(qs_ref, qrs_ref, gt_ref, kc_ref, vc_ref, slc_ref, win_ref, mt_ref, e3_ref, o_ref, imp_ref,
                        *, n_cb, n_sb):
    tq = qs_ref.shape[0]
    t_all = slc_ref.shape[0]
    qb = pl.program_id(1)
    t0 = qb * tq
    tpos_col = t0 + lax.broadcasted_iota(jnp.int32, (tq, 1), 0)
    tpos_row = t0 + lax.broadcasted_iota(jnp.int32, (1, LANES), 1)
    groups = range(N_KV_HEADS)
    k_lanes, v_lanes = slice(0, LANES), slice(LANES, 2 * LANES)

    qr = [_stack_heads(qrs_ref, g, tq) for g in groups]
    cmp_out = [_cmp_branch(_stack_heads(qs_ref, g, tq), kc_ref[...], vc_ref[...], tpos_col, n_cb, tq) for g in groups]
    sel = _select_blocks([ps for _, ps in cmp_out], mt_ref, imp_ref, tpos_row, n_sb, tq)

    def slc_body(j, carries):
        rows = pl.ds(pl.multiple_of(j * SLC_TILE, SLC_TILE), SLC_TILE)
        k, v, e = slc_ref[rows, k_lanes], slc_ref[rows, v_lanes], e3_ref[j]
        causal = (j * SLC_TILE + lax.broadcasted_iota(jnp.int32, (1, SLC_TILE), 1)) <= tpos_col
        return tuple(_flash_tile(carries[g], qr[g], k, v, _mask_bias((_dot(sel[g], e) > 0.5) & causal), tq)
                     for g in groups)

    n_slc = (t0 + tq + SLC_TILE - 1) // SLC_TILE
    carries = lax.fori_loop(0, n_slc, slc_body, tuple(_flash_init(tq) for _ in groups))

    wk = min(WINDOW + tq, t_all)
    start = pl.multiple_of(jnp.clip(t0 - WINDOW, 0, t_all - wk), tq)
    wbias = _mask_bias(_win_valid(start + lax.broadcasted_iota(jnp.int32, (1, wk), 1), tpos_col))
    kw, vw = win_ref[pl.ds(start, wk), k_lanes], win_ref[pl.ds(start, wk), v_lanes]
    for g in groups:
        o_w = _attend_once(qr[g], kw, vw, wbias, tq)
        _merge_heads(cmp_out[g][0], _flash_out(carries[g], tq), o_w, gt_ref, g, o_ref, tq)


def _attn_prompt(qs, qrs, gates, kc, vc, slc_bf, win_bf, mt, e3, b, t, n_cb, n_sb):
    tq = Q_BLOCK
    nqb = t // tq
    d = qs.shape[1]
    row = lambda w: pl.BlockSpec((tq, w), lambda bi, qi: (bi * nqb + qi, 0))
    per_b = lambda a: pl.BlockSpec((None,) + a.shape[1:], lambda bi, qi: (bi, 0, 0))
    return pl.pallas_call(
        functools.partial(_attn_prompt_kernel, n_cb=n_cb, n_sb=n_sb),
        grid=(b, nqb),
        in_specs=[row(d), row(d), row(LANES), per_b(kc), per_b(vc), per_b(slc_bf), per_b(win_bf),
                  _const_spec(mt.shape), _const_spec(e3.shape)],
        out_specs=row(d),
        out_shape=jax.ShapeDtypeStruct((b * t, d), BF),
        scratch_shapes=[pltpu.VMEM((N_KV_HEADS, mt.shape[0], LANES), F32)],
        compiler_params=_cparams(2), name="attn_prompt",
    )(qs, qrs, gates, kc, vc, slc_bf, win_bf, mt, e3)


def _attn_sample_kernel(qs_ref, qrs_ref, gt_ref, kc_ref, vc_ref, slc_ref, slct_ref, win_ref, wint_ref, mt_ref, e_ref,
                        et_ref, o_ref, *, n_cb, past):
    tq = qs_ref.shape[0]
    tpos_col = jnp.full((tq, 1), past, jnp.int32)
    w_buf = win_ref.shape[0]
    tail = slct_ref.shape[0]
    k_lanes, v_lanes = slice(0, LANES), slice(LANES, 2 * LANES)
    tail_pos = past + lax.broadcasted_iota(jnp.int32, (1, tail), 1)
    win_pos = past - w_buf + lax.broadcasted_iota(jnp.int32, (1, w_buf), 1)
    wbias = jnp.broadcast_to(_mask_bias(_win_valid(win_pos, tpos_col)), (tq, w_buf))
    wtbias = jnp.broadcast_to(_mask_bias(_win_valid(tail_pos, tpos_col)), (tq, tail))
    kw, vw = win_ref[:, k_lanes].astype(BF), win_ref[:, v_lanes].astype(BF)
    for g in range(N_KV_HEADS):
        qr = _stack_heads(qrs_ref, g, tq)
        o_c, psum = _cmp_branch(_stack_heads(qs_ref, g, tq), kc_ref[...], vc_ref[...], tpos_col, n_cb, tq)
        sel = _select_single(psum, mt_ref, past, tq)
        carry = _flash_tile(_flash_init(tq), qr, slc_ref[:, k_lanes], slc_ref[:, v_lanes],
                            _mask_bias(_dot(sel, e_ref[...]) > 0.5), tq)
        carry = _flash_tile(carry, qr, slct_ref[:, k_lanes], slct_ref[:, v_lanes],
                            _mask_bias((_dot(sel, et_ref[...]) > 0.5) & (tail_pos <= tpos_col)), tq)
        o_s = _flash_out(carry, tq)
        carry = _flash_tile(_flash_init(tq), qr, kw, vw, wbias, tq)
        carry = _flash_tile(carry, qr, wint_ref[:, k_lanes], wint_ref[:, v_lanes], wtbias, tq)
        _merge_heads(o_c, o_s, _flash_out(carry, tq), gt_ref, g, o_ref, tq)


def _attn_sample(qs, qrs, gates, kc, vc, slc_g, slc_tail, win_c, win_tail, mt, e_all, et, n_cb, past):
    db = qs.shape[0]
    per_b = lambda a: pl.BlockSpec((None,) + a.shape[1:], lambda bi: (bi, 0, 0))
    args = (qs, qrs, gates, kc, vc, slc_g, slc_tail, win_c, win_tail)
    return pl.pallas_call(
        functools.partial(_attn_sample_kernel, n_cb=n_cb, past=past),
        grid=(db,),
        in_specs=[per_b(a) for a in args] + [_const_spec(mt.shape), _const_spec(e_all.shape), _const_spec(et.shape)],
        out_specs=per_b(qs),
        out_shape=jax.ShapeDtypeStruct(qs.shape, BF),
        compiler_params=_cparams(1), name="attn_sample",
    )(*args, mt, e_all, et)


def _block_diag(blocks):
    n, k, j = blocks.shape
    eye = jnp.eye(n, dtype=blocks.dtype)
    return (eye[:, None, :, None] * blocks[:, :, None, :]).reshape(n * k, n * j)


def _pad_rows8(rows):
    return jnp.pad(rows, ((0, 0), (SUBLANES - rows.shape[1], 0), (0, 0)))


def _rope_tables(pos):
    half = HEAD_DIM // 2
    inv_freq = ROPE_THETA ** (-jnp.arange(half, dtype=F32) / half)
    ang = pos.astype(F32)[:, None] * inv_freq[None, :]
    cos, sin = jnp.cos(ang), jnp.sin(ang)
    reps = LANES // HEAD_DIM
    return (jnp.tile(jnp.concatenate([cos, cos], axis=1), (1, reps)),
            jnp.tile(jnp.concatenate([-sin, sin], axis=1), (1, reps)))


def _overlap_t(n_cb, n_sb, ncp, sp):
    c0 = np.arange(ncp)[None, :] * D_CMP
    s0 = np.arange(sp)[:, None] * L_SLC
    m = (c0 < s0 + L_SLC) & (c0 + L_CMP > s0) & (np.arange(ncp)[None, :] < n_cb) & (np.arange(sp)[:, None] < n_sb)
    return jnp.asarray(m, dtype=BF)


def _expand_tiles(n_tiles, sp, tile, first_key=0):
    key = first_key + np.arange(n_tiles)[:, None, None] * tile + np.arange(tile)[None, None, :]
    return jnp.asarray(np.arange(sp)[None, :, None] == key // L_SLC, dtype=BF)


def _round_up(x, m):
    return (x + m - 1) // m * m


def kernel(x_prompt, x_sample, p_prompt, p_sample, cache_cmp_kv, cache_slc_kv, cache_win_kv, state_rg_conv,
           state_rg_h, state_ffn_conv, page_table, g_mix, g_ffn, g_ple, g_final, rg_w_in, rg_conv_w, rg_conv_b,
           rg_w_a, rg_b_a, rg_w_x, rg_b_x, rg_lambda, rg_w_out, g_kv, w_kv, cmp_pos, cmp_w1, cmp_b1, cmp_w2,
           attn_w_qg, attn_w_o, ffn_w_up, ffn_conv_w, ffn_conv_b, ffn_w_down, ple_w_in, ple_w_gate):
    b, t, d = x_prompt.shape
    db = x_sample.shape[0]
    assert x_sample.shape[1] == 1, "the sample path handles one new token per sequence"
    depth = g_mix.shape[0]
    n_a = rg_w_in.shape[0]
    d_rnn = rg_w_out.shape[1]
    n_pages = page_table.shape[1]
    past = n_pages * PAGE_SIZE
    kvc = 2 * N_KV_HEADS * HEAD_DIM
    n_q = N_KV_HEADS * HPG * HEAD_DIM

    row1 = lambda v: v.reshape(1, -1)
    pad8 = lambda w: jnp.pad(w, ((0, SUBLANES - w.shape[0]), (0, 0)))

    rg_in_bf = rg_w_in.astype(BF)
    rg_out_bf = rg_w_out.astype(BF)
    rg_gate_bf = [jnp.concatenate([_block_diag(rg_w_a[i]), _block_diag(rg_w_x[i])], axis=1).astype(BF)
                  for i in range(n_a)]
    rg_gate_b = [row1(jnp.concatenate([rg_b_a[i].reshape(-1), rg_b_x[i].reshape(-1)])) for i in range(n_a)]
    rg_c = [row1(-RG_C * jax.nn.softplus(-rg_lambda[i])) for i in range(n_a)]
    up_bf = ffn_w_up.astype(BF)
    down_bf = ffn_w_down.astype(BF)
    gate_bf = ple_w_gate.astype(BF)
    pin_bf = ple_w_in.astype(BF)
    wkv_bf = w_kv.astype(BF)
    wq_bf = attn_w_qg[:, :, :n_q].astype(BF)
    wgl_bf = jnp.pad(attn_w_qg[:, :, n_q:], ((0, 0), (0, 0), (0, LANES - 3 * N_KV_HEADS * HPG))).astype(BF)
    wo_bf = attn_w_o.astype(BF)
    gfin = row1(g_final)

    w1 = cmp_w1.reshape(2, 2, D_CMP, HEAD_DIM, -1)
    eye_g = jnp.eye(N_KV_HEADS, dtype=F32)
    w1_bd = (eye_g[None, None, None, :, None, :, None] * w1[:, :, :, None, :, None, :]).reshape(
        2, 2, D_CMP, N_KV_HEADS * HEAD_DIM, -1).astype(BF)
    pos_t = jnp.tile(jnp.transpose(cmp_pos, (1, 0, 2)), (1, 1, N_KV_HEADS)).reshape(2, 2, D_CMP, LANES)
    cmp_params = (pos_t[:, 0], pos_t[:, 1], w1_bd[:, 0], w1_bd[:, 1],
                  jnp.tile(cmp_b1, (1, N_KV_HEADS)).reshape(2, 1, -1),
                  jnp.stack([_block_diag(jnp.stack([cmp_w2[j]] * N_KV_HEADS)) for j in range(2)]).astype(BF))

    def layer_consts(i, final):
        return (row1(g_ffn[i]), up_bf[i], pad8(ffn_conv_w[i]), row1(ffn_conv_b[i]), down_bf[i], row1(g_ple[i]),
                gate_bf[i], pin_bf[i], gfin), final

    n = b * t
    x = x_prompt.reshape(n, d)
    p = p_prompt.reshape(depth, n, -1)
    cos_p, sin_p = _rope_tables(jnp.tile(jnp.arange(t, dtype=jnp.int32), b))
    rg_conv_p, rg_h_p, ffn_conv_p = [], [], []
    zeros_rg = jnp.zeros((b, SUBLANES, d_rnn), F32)
    zeros_ffn = jnp.zeros((b, SUBLANES, ffn_w_up.shape[2]), F32)
    for i in range(depth):
        if i == n_a:
            kv_p, slc_bf, win_bf = _kv_proj(x, row1(g_kv), wkv_bf, cos_p, sin_p)
            kc_p, vc_p = _compress(kv_p.reshape(b, t, -1), cmp_params, b, t)
            n_cb_p = (t - L_CMP) // D_CMP + 1
            n_sb_p = t // L_SLC
            mt_p = _overlap_t(n_cb_p, n_sb_p, t // D_CMP, _round_up(n_sb_p, LANES))
            e3_p = _expand_tiles(t // SLC_TILE, _round_up(n_sb_p, LANES), SLC_TILE)
            slc_bf = slc_bf.reshape(b, t, -1)
            win_bf = win_bf.reshape(b, t, -1)
        if i < n_a:
            yg, xr = _rg_in(x, row1(g_mix[i]), rg_in_bf[i])
            gated, h_last = _rg_core(xr, yg, zeros_rg, zeros_rg, pad8(rg_conv_w[i]), row1(rg_conv_b[i]),
                                     rg_gate_bf[i], rg_gate_b[i], rg_c[i], b, t)
            x = _matmul_res(gated, rg_out_bf[i], x)
            rg_conv_p.append(xr.reshape(b, t, -1)[:, t - 3:])
            rg_h_p.append(h_last[:, 0])
        else:
            j = i - n_a
            qs, qrs, gates = _q_proj(x, row1(g_mix[i]), wq_bf[j], wgl_bf[j], cos_p, sin_p)
            o = _attn_prompt(qs, qrs, gates, kc_p, vc_p, slc_bf, win_bf, mt_p, e3_p, b, t, n_cb_p, n_sb_p)
            x = _matmul_res(o, wo_bf[j], x)
        consts, final = layer_consts(i, i == depth - 1)
        x, f_hist = _ffn(x, p[i], zeros_ffn, *consts, b, t, final)
        ffn_conv_p.append(f_hist[:, SUBLANES - 2:])
    y_prompt = x.reshape(b, t, d)
    kv5 = kv_p.reshape(b, t, 6, N_KV_HEADS, HEAD_DIM)
    keep = min(WINDOW, t)
    cmp_p, slc_p, win_p = kv5[:, :, 0:2], kv5[:, :, 2:4], kv5[:, t - keep:, 4:6]

    x = x_sample.reshape(db, d)
    p = p_sample.reshape(depth, db, -1)
    cos_s, sin_s = _rope_tables(jnp.full((db,), past, jnp.int32))
    rg_conv_s, rg_h_s, ffn_conv_s = [], [], []
    for i in range(depth):
        if i == n_a:
            kv_s, slc_new_bf, win_new_bf = _kv_proj(x, row1(g_kv), wkv_bf, cos_s, sin_s)
            cmp_rows = _gather_pages(cache_cmp_kv.reshape(-1, PAGE_SIZE, kvc), page_table, F32)
            slc_rows = _gather_pages(cache_slc_kv.reshape(-1, PAGE_SIZE, kvc), page_table, BF)
            kc_s, vc_s = _compress(cmp_rows, cmp_params, db, past)
            n_cb_s = (past + 1 - L_CMP) // D_CMP + 1
            n_sb_s = -(-(past + 1) // L_SLC)
            sp_s = _round_up(n_sb_s, LANES)
            mt_s = _overlap_t(n_cb_s, n_sb_s, past // D_CMP, sp_s)
            e_s = _expand_tiles(1, sp_s, past)[0]
            et_s = _expand_tiles(1, sp_s, LANES, first_key=past)[0]
            tail = lambda a: jnp.pad(a.reshape(db, 1, -1), ((0, 0), (0, LANES - 1), (0, 0)))
            slc_tail, win_tail = tail(slc_new_bf), tail(win_new_bf)
            win_c = cache_win_kv.reshape(db, cache_win_kv.shape[1], kvc)
        if i < n_a:
            yg, xr = _rg_in(x, row1(g_mix[i]), rg_in_bf[i])
            hist = state_rg_conv[i]
            gated, h_new = _rg_step(xr, yg, hist[:, 0], hist[:, 1], hist[:, 2], state_rg_h[i], pad8(rg_conv_w[i]),
                                    row1(rg_conv_b[i]), rg_gate_bf[i], rg_gate_b[i], rg_c[i])
            x = _matmul_res(gated, rg_out_bf[i], x)
            rg_conv_s.append(jnp.concatenate([hist[:, 1:], xr[:, None, :]], axis=1))
            rg_h_s.append(h_new)
        else:
            j = i - n_a
            qs, qrs, gates = _q_proj(x, row1(g_mix[i]), wq_bf[j], wgl_bf[j], cos_s, sin_s)
            rep = lambda a: jnp.broadcast_to(a[:, None, :], (db, SUBLANES, a.shape[1]))
            o = _attn_sample(rep(qs), rep(qrs), rep(gates), kc_s, vc_s, slc_rows, slc_tail, win_c, win_tail,
                             mt_s, e_s, et_s, n_cb_s, past)
            x = _matmul_res(o[:, 0], wo_bf[j], x)
        consts, final = layer_consts(i, i == depth - 1)
        fh = state_ffn_conv[i]
        x, up = _ffn_step(x, p[i], fh[:, 0], fh[:, 1], *consts, final)
        ffn_conv_s.append(jnp.stack([fh[:, 1], up], axis=1))
    y_sample = x.reshape(db, 1, d)
    kv5 = kv_s.reshape(db, 1, 6, N_KV_HEADS, HEAD_DIM)
    cmp_s, slc_s, win_new = kv5[:, :, 0:2], kv5[:, :, 2:4], kv5[:, :, 4:6]
    win_full = jnp.concatenate([cache_win_kv, win_new], axis=1)
    keep = min(WINDOW, past + 1)
    win_s = win_full[:, win_full.shape[1] - keep:]

    return (y_prompt, y_sample, cmp_p, cmp_s, slc_p, slc_s, win_p, win_s,
            jnp.stack(rg_conv_p), jnp.stack(rg_conv_s), jnp.stack(rg_h_p), jnp.stack(rg_h_s),
            jnp.stack(ffn_conv_p), jnp.stack(ffn_conv_s))
```

```python
import functools

import jax
import jax.numpy as jnp
import numpy as np
from jax import lax
from jax.experimental import pallas as pl
from jax.experimental.pallas import tpu as pltpu

BF = jnp.bfloat16
F32 = jnp.float32

EPS = 1e-6
RG_C = 8.0
N_RG_BLOCKS = 16
HEAD_DIM = 64
N_KV_HEADS = 2
HPG = 8
L_CMP = 32
D_CMP = 16
L_SLC = 64
N_SEL = 16
WINDOW = 512
Q_BLOCK = 128
ROPE_THETA = 10000.0
PAGE_SIZE = 128
NEG = -1e30
LANES = 128
SUBLANES = 8
SLC_TILE = 512
WIN_TILE = 128
VMEM_LIMIT = 56 * 1024 * 1024


def _cparams(n_axes):
    return pltpu.CompilerParams(dimension_semantics=("arbitrary",) * n_axes,
                                vmem_limit_bytes=VMEM_LIMIT)


def _const_spec(shape):
    zeros = (0,) * len(shape)
    return pl.BlockSpec(shape, lambda *_: zeros, pipeline_mode=pl.Buffered(1))


def _dot(a, b):
    return jnp.dot(a, b, preferred_element_type=F32)


def _dot_nt(a, b):
    return lax.dot_general(a, b, (((1,), (1,)), ((), ())), preferred_element_type=F32)


def _rms(x, g):
    return x * lax.rsqrt(jnp.mean(x * x, axis=-1, keepdims=True) + EPS) * g


def _gelu(x):
    return jax.nn.gelu(x, approximate=True)


def _sigmoid(x):
    return jax.nn.sigmoid(x)


def _shift_rows(x, prev8, j):
    if j == 0:
        return x
    r = pltpu.roll(x, j, axis=0)
    p = pltpu.roll(prev8, j, axis=0)
    row = lax.broadcasted_iota(jnp.int32, (SUBLANES, x.shape[1]), 0)
    head = jnp.where(row < j, p, r[:SUBLANES])
    if x.shape[0] == SUBLANES:
        return head
    return jnp.concatenate([head, r[SUBLANES:]], axis=0)


def _rope(z, cos, sin_signed):
    lane = lax.broadcasted_iota(jnp.int32, z.shape, 1)
    first = (lane & (HEAD_DIM - 1)) < (HEAD_DIM // 2)
    rot = jnp.where(first, pltpu.roll(z, LANES - HEAD_DIM // 2, axis=1), pltpu.roll(z, HEAD_DIM // 2, axis=1))
    return z * cos + rot * sin_signed


def _rg_in_kernel(x_ref, g_ref, w_ref, yg_ref, xr_ref, *, chunk):
    h = _rms(x_ref[...], g_ref[...]).astype(BF)
    n = yg_ref.shape[1]
    for c0 in range(0, n, chunk):
        yg_ref[:, c0:c0 + chunk] = _gelu(_dot(h, w_ref[:, c0:c0 + chunk]))
        xr_ref[:, c0:c0 + chunk] = _dot(h, w_ref[:, n + c0:n + c0 + chunk])


def _rg_in(x, g, w_bf):
    n, d = x.shape
    dr = w_bf.shape[1] // 2
    tm = min(512, n)
    return pl.pallas_call(
        functools.partial(_rg_in_kernel, chunk=256),
        grid=(n // tm,),
        in_specs=[pl.BlockSpec((tm, d), lambda i: (i, 0)), _const_spec((1, d)), _const_spec(w_bf.shape)],
        out_specs=[pl.BlockSpec((tm, dr), lambda i: (i, 0)), pl.BlockSpec((tm, dr), lambda i: (i, 0))],
        out_shape=[jax.ShapeDtypeStruct((n, dr), F32), jax.ShapeDtypeStruct((n, dr), F32)],
        compiler_params=_cparams(1), name="rg_in",
    )(x, g, w_bf)


def _rg_gates(xc, wg_ref, bg_ref, c_ref):
    d = xc.shape[1]
    gates = _dot(xc.astype(BF), wg_ref[...]) + bg_ref[...]
    r = _sigmoid(gates[:, :d])
    i = _sigmoid(gates[:, d:])
    log_a = c_ref[...] * r
    a = jnp.exp(log_a)
    z = -jnp.tanh(log_a) * (a * a + 1.0)
    mult = jnp.where(z > 0.0, z * lax.rsqrt(z), 0.0)
    return a, mult, i * xc


def _rg_core_kernel(xr_ref, yg_ref, hist_ref, h0_ref, cw_ref, cb_ref, wg_ref, bg_ref, c_ref,
                    out_ref, hl_ref, xprev, hc):
    t = pl.program_id(1)

    @pl.when(t == 0)
    def _():
        xprev[...] = hist_ref[...]
        hc[...] = h0_ref[...]

    x = xr_ref[...]
    tt, d = x.shape
    prev = xprev[...]
    xc = cb_ref[...] + _shift_rows(x, prev, 3) * cw_ref[0:1, :]
    xc = xc + _shift_rows(x, prev, 2) * cw_ref[1:2, :]
    xc = xc + _shift_rows(x, prev, 1) * cw_ref[2:3, :]
    xc = xc + x * cw_ref[3:4, :]
    xprev[...] = x[tt - SUBLANES:, :]

    a, mult, ixc = _rg_gates(xc, wg_ref, bg_ref, c_ref)
    row = lax.broadcasted_iota(jnp.int32, (tt, d), 0)
    mult = jnp.where((row == 0) & (t == 0), 1.0, mult)
    u = mult * ixc

    av, bv = a, u
    dist = 1
    while dist < tt:
        if dist < SUBLANES:
            ok = row >= dist
            bv = jnp.where(ok, av * pltpu.roll(bv, dist, axis=0) + bv, bv)
            av = jnp.where(ok, av * pltpu.roll(av, dist, axis=0), av)
        else:
            bv = jnp.concatenate([bv[:dist], av[dist:] * bv[:tt - dist] + bv[dist:]], axis=0)
            av = jnp.concatenate([av[:dist], av[dist:] * av[:tt - dist]], axis=0)
        dist *= 2
    h = av * hc[0:1, :] + bv
    hc[...] = jnp.broadcast_to(h[tt - 1:tt, :], (SUBLANES, d))
    hl_ref[...] = hc[...]
    out_ref[...] = (yg_ref[...] * h).astype(BF)


def _rg_core(xr, yg, hist8, h08, cw8, cb, wg_bf, bg, c, b, t):
    n, d = xr.shape
    tt = min(256, t)
    nt = t // tt
    row_spec = pl.BlockSpec((tt, d), lambda bi, ti: (bi * nt + ti, 0))
    st_spec = pl.BlockSpec((None, SUBLANES, d), lambda bi, ti: (bi, 0, 0))
    return pl.pallas_call(
        _rg_core_kernel,
        grid=(b, nt),
        in_specs=[row_spec, row_spec, st_spec, st_spec, _const_spec(cw8.shape), _const_spec(cb.shape),
                  _const_spec(wg_bf.shape), _const_spec(bg.shape), _const_spec(c.shape)],
        out_specs=[row_spec, st_spec],
        out_shape=[jax.ShapeDtypeStruct((n, d), BF), jax.ShapeDtypeStruct((b, SUBLANES, d), F32)],
        scratch_shapes=[pltpu.VMEM((SUBLANES, d), F32), pltpu.VMEM((SUBLANES, d), F32)],
        compiler_params=_cparams(2), name="rg_core",
    )(xr, yg, hist8, h08, cw8, cb, wg_bf, bg, c)


def _rg_step_kernel(xr_ref, yg_ref, h0_ref, h1_ref, h2_ref, hs_ref, cw_ref, cb_ref, wg_ref, bg_ref, c_ref,
                    out_ref, hn_ref):
    x = xr_ref[...]
    xc = cb_ref[...] + h0_ref[...] * cw_ref[0:1, :]
    xc = xc + h1_ref[...] * cw_ref[1:2, :]
    xc = xc + h2_ref[...] * cw_ref[2:3, :]
    xc = xc + x * cw_ref[3:4, :]
    a, mult, ixc = _rg_gates(xc, wg_ref, bg_ref, c_ref)
    h = mult * ixc + a * hs_ref[...]
    hn_ref[...] = h
    out_ref[...] = (yg_ref[...] * h).astype(BF)


def _rg_step(xr, yg, h0, h1, h2, hs, cw8, cb, wg_bf, bg, c):
    n, d = xr.shape
    args = (xr, yg, h0, h1, h2, hs, cw8, cb, wg_bf, bg, c)
    return pl.pallas_call(
        _rg_step_kernel,
        grid=(1,),
        in_specs=[_const_spec(a.shape) for a in args],
        out_specs=[pl.BlockSpec((n, d), lambda i: (0, 0)), pl.BlockSpec((n, d), lambda i: (0, 0))],
        out_shape=[jax.ShapeDtypeStruct((n, d), BF), jax.ShapeDtypeStruct((n, d), F32)],
        compiler_params=_cparams(1), name="rg_step",
    )(*args)


def _matmul_res_kernel(a_ref, w_ref, x_ref, o_ref):
    o_ref[...] = x_ref[...] + _dot(a_ref[...], w_ref[...])


def _matmul_res(a_bf, w_bf, x):
    n, k = a_bf.shape
    d = w_bf.shape[1]
    tm = min(512, n)
    return pl.pallas_call(
        _matmul_res_kernel,
        grid=(n // tm,),
        in_specs=[pl.BlockSpec((tm, k), lambda i: (i, 0)), _const_spec(w_bf.shape),
                  pl.BlockSpec((tm, d), lambda i: (i, 0))],
        out_specs=pl.BlockSpec((tm, d), lambda i: (i, 0)),
        out_shape=jax.ShapeDtypeStruct((n, d), F32),
        compiler_params=_cparams(1), name="matmul_res",
    )(a_bf, w_bf, x)


def _ffn_tail(x, acc, p_ref, gp_ref, wgate_ref, wpin_ref, gfin_ref, final_norm):
    x1 = x + acc
    hp = _rms(x1, gp_ref[...]).astype(BF)
    sig = _sigmoid(_dot(hp, wgate_ref[...]))
    pe = _dot(p_ref[...].astype(BF), wpin_ref[...])
    x2 = x1 + pe * sig
    if final_norm:
        x2 = _rms(x2, gfin_ref[...])
    return x2


def _ffn_kernel(x_ref, p_ref, hist_ref, gf_ref, wup_ref, cw_ref, cb_ref, wdn_ref, gp_ref, wgate_ref, wpin_ref,
                gfin_ref, o_ref, ho_ref, uprev, *, chunk, final_norm):
    t = pl.program_id(1)

    @pl.when(t == 0)
    def _():
        uprev[...] = hist_ref[...]

    x = x_ref[...]
    tm = x.shape[0]
    dff = wdn_ref.shape[0]
    h = _rms(x, gf_ref[...]).astype(BF)
    acc = jnp.zeros(x.shape, F32)
    for c0 in range(0, dff, chunk):
        halves = []
        for off in (c0, dff + c0):
            up = _dot(h, wup_ref[:, off:off + chunk])
            prev = uprev[:, off:off + chunk]
            uc = cb_ref[:, off:off + chunk] + _shift_rows(up, prev, 2) * cw_ref[0:1, off:off + chunk]
            uc = uc + _shift_rows(up, prev, 1) * cw_ref[1:2, off:off + chunk]
            uc = uc + up * cw_ref[2:3, off:off + chunk]
            uprev[:, off:off + chunk] = up[tm - SUBLANES:, :]
            halves.append(uc)
        act = (_gelu(halves[0]) * halves[1]).astype(BF)
        acc = acc + _dot(act, wdn_ref[c0:c0 + chunk, :])
    ho_ref[...] = uprev[...]
    o_ref[...] = _ffn_tail(x, acc, p_ref, gp_ref, wgate_ref, wpin_ref, gfin_ref, final_norm)


def _ffn(x, p, hist8, gf, wup_bf, cw8, cb, wdn_bf, gp, wgate_bf, wpin_bf, gfin, b, t, final_norm):
    n, d = x.shape
    dp = p.shape[1]
    d2 = wup_bf.shape[1]
    tm = min(512, t)
    nt = t // tm
    row = lambda w: pl.BlockSpec((tm, w), lambda bi, ti: (bi * nt + ti, 0))
    st_spec = pl.BlockSpec((None, SUBLANES, d2), lambda bi, ti: (bi, 0, 0))
    consts = (gf, wup_bf, cw8, cb, wdn_bf, gp, wgate_bf, wpin_bf, gfin)
    return pl.pallas_call(
        functools.partial(_ffn_kernel, chunk=512, final_norm=final_norm),
        grid=(b, nt),
        in_specs=[row(d), row(dp), st_spec] + [_const_spec(a.shape) for a in consts],
        out_specs=[row(d), st_spec],
        out_shape=[jax.ShapeDtypeStruct((n, d), F32), jax.ShapeDtypeStruct((b, SUBLANES, d2), F32)],
        scratch_shapes=[pltpu.VMEM((SUBLANES, d2), F32)],
        compiler_params=_cparams(2), name="ffn",
    )(x, p, hist8, *consts)


def _ffn_step_kernel(x_ref, p_ref, h0_ref, h1_ref, gf_ref, wup_ref, cw_ref, cb_ref, wdn_ref, gp_ref, wgate_ref,
                     wpin_ref, gfin_ref, o_ref, up_ref, *, chunk, final_norm):
    x = x_ref[...]
    dff = wdn_ref.shape[0]
    h = _rms(x, gf_ref[...]).astype(BF)
    acc = jnp.zeros(x.shape, F32)
    for c0 in range(0, dff, chunk):
        halves = []
        for off in (c0, dff + c0):
            up = _dot(h, wup_ref[:, off:off + chunk])
            up_ref[:, off:off + chunk] = up
            uc = cb_ref[:, off:off + chunk] + h0_ref[:, off:off + chunk] * cw_ref[0:1, off:off + chunk]
            uc = uc + h1_ref[:, off:off + chunk] * cw_ref[1:2, off:off + chunk]
            uc = uc + up * cw_ref[2:3, off:off + chunk]
            halves.append(uc)
        act = (_gelu(halves[0]) * halves[1]).astype(BF)
        acc = acc + _dot(act, wdn_ref[c0:c0 + chunk, :])
    o_ref[...] = _ffn_tail(x, acc, p_ref, gp_ref, wgate_ref, wpin_ref, gfin_ref, final_norm)


def _ffn_step(x, p, h0, h1, gf, wup_bf, cw8, cb, wdn_bf, gp, wgate_bf, wpin_bf, gfin, final_norm):
    n, d = x.shape
    d2 = wup_bf.shape[1]
    args = (x, p, h0, h1, gf, wup_bf, cw8, cb, wdn_bf, gp, wgate_bf, wpin_bf, gfin)
    return pl.pallas_call(
        functools.partial(_ffn_step_kernel, chunk=512, final_norm=final_norm),
        grid=(1,),
        in_specs=[_const_spec(a.shape) for a in args],
        out_specs=[pl.BlockSpec((n, d), lambda i: (0, 0)), pl.BlockSpec((n, d2), lambda i: (0, 0))],
        out_shape=[jax.ShapeDtypeStruct((n, d), F32), jax.ShapeDtypeStruct((n, d2), F32)],
        compiler_params=_cparams(1), name="ffn_step",
    )(*args)


def _kv_kernel(x_ref, g_ref, w_ref, cos_ref, sin_ref, oh_ref, kv_ref, slc_ref, win_ref, ka_ref):
    h = _rms(x_ref[...], g_ref[...]).astype(BF)
    kv = _dot(h, w_ref[...])
    cos, sin = cos_ref[...], sin_ref[...]
    w2 = 2 * LANES
    k_slc = _rope(kv[:, w2:w2 + LANES], cos, sin)
    k_win = _rope(kv[:, 2 * w2:2 * w2 + LANES], cos, sin)
    kv_ref[:, 0:w2] = kv[:, 0:w2]
    kv_ref[:, w2:w2 + LANES] = k_slc
    kv_ref[:, w2 + LANES:2 * w2] = kv[:, w2 + LANES:2 * w2]
    kv_ref[:, 2 * w2:2 * w2 + LANES] = k_win
    kv_ref[:, 2 * w2 + LANES:3 * w2] = kv[:, 2 * w2 + LANES:3 * w2]
    slc_ref[:, 0:LANES] = k_slc.astype(BF)
    slc_ref[:, LANES:w2] = kv[:, w2 + LANES:2 * w2].astype(BF)
    win_ref[:, 0:LANES] = k_win.astype(BF)
    win_ref[:, LANES:w2] = kv[:, 2 * w2 + LANES:3 * w2].astype(BF)
    low = lax.broadcasted_iota(jnp.int32, k_slc.shape, 1) < HEAD_DIM
    oh = oh_ref[...]
    ka_ref[:, 0:LANES] = jnp.where(low, k_slc, oh).astype(BF)
    ka_ref[:, LANES:w2] = jnp.where(low, oh, k_slc).astype(BF)


def _kv_proj(x, g, w_bf, cos, sin, onehot):
    n, d = x.shape
    nk = w_bf.shape[1]
    tm = min(512, n)
    row = lambda w: pl.BlockSpec((tm, w), lambda i: (i, 0))
    return pl.pallas_call(
        _kv_kernel,
        grid=(n // tm,),
        in_specs=[row(d), _const_spec((1, d)), _const_spec(w_bf.shape), row(LANES), row(LANES), row(LANES)],
        out_specs=[row(nk), row(2 * LANES), row(2 * LANES), row(2 * LANES)],
        out_shape=[jax.ShapeDtypeStruct((n, nk), F32), jax.ShapeDtypeStruct((n, 2 * LANES), BF),
                   jax.ShapeDtypeStruct((n, 2 * LANES), BF), jax.ShapeDtypeStruct((n, 2 * LANES), BF)],
        compiler_params=_cparams(1), name="kv_proj",
    )(x, g, w_bf, cos, sin, onehot)


def _q_kernel(x_ref, g_ref, wq_ref, wg_ref, cos_ref, sin_ref, qs_ref, qrs_ref, gt_ref):
    h = _rms(x_ref[...], g_ref[...]).astype(BF)
    cos, sin = cos_ref[...], sin_ref[...]
    scale = HEAD_DIM ** -0.5
    for c0 in range(0, wq_ref.shape[1], LANES):
        q = _dot(h, wq_ref[:, c0:c0 + LANES])
        qs_ref[:, c0:c0 + LANES] = (q * scale).astype(BF)
        qrs_ref[:, c0:c0 + LANES] = (_rope(q, cos, sin) * scale).astype(BF)
    gt_ref[...] = _sigmoid(_dot(h, wg_ref[...]))


def _q_proj(x, g, wq_bf, wg_bf, cos, sin):
    n, d = x.shape
    dq = wq_bf.shape[1]
    tm = min(512, n)
    row = lambda w: pl.BlockSpec((tm, w), lambda i: (i, 0))
    return pl.pallas_call(
        _q_kernel,
        grid=(n // tm,),
        in_specs=[row(d), _const_spec((1, d)), _const_spec(wq_bf.shape), _const_spec(wg_bf.shape),
                  row(LANES), row(LANES)],
        out_specs=[row(dq), row(dq), row(LANES)],
        out_shape=[jax.ShapeDtypeStruct((n, dq), BF), jax.ShapeDtypeStruct((n, dq), BF),
                   jax.ShapeDtypeStruct((n, LANES), F32)],
        compiler_params=_cparams(1), name="q_proj",
    )(x, g, wq_bf, wg_bf, cos, sin)


def _compress_kernel(xk_ref, xv_ref, pa_ref, pb_ref, w1a_ref, w1b_ref, b1_ref, w2_ref, kc_ref, vc_ref):
    nhb = kc_ref.shape[0]
    for j, x_ref, o_ref in ((0, xk_ref, kc_ref), (1, xv_ref, vc_ref)):
        acc_a = jnp.zeros((nhb, 2 * LANES), F32)
        acc_b = jnp.zeros((nhb, 2 * LANES), F32)
        for l in range(D_CMP):
            xl = x_ref[pl.ds(l, nhb, stride=D_CMP), :]
            acc_a = acc_a + _dot((xl + pa_ref[j, l:l + 1, :]).astype(BF), w1a_ref[j, l])
            acc_b = acc_b + _dot((xl + pb_ref[j, l:l + 1, :]).astype(BF), w1b_ref[j, l])
        hid = _gelu(acc_a + pltpu.roll(acc_b, nhb - 1, axis=0) + b1_ref[j])
        o_ref[...] = _dot(hid.astype(BF), w2_ref[j]).astype(BF)


def _compress(rows, params, b, t):
    nhb = t // D_CMP
    out = pl.BlockSpec((None, nhb, LANES), lambda bi: (bi, 0, 0))
    return pl.pallas_call(
        _compress_kernel,
        grid=(b,),
        in_specs=[pl.BlockSpec((None, t, LANES), lambda bi: (bi, 0, 0)),
                  pl.BlockSpec((None, t, LANES), lambda bi: (bi, 0, 1))] + [_const_spec(a.shape) for a in params],
        out_specs=[out, out],
        out_shape=[jax.ShapeDtypeStruct((b, nhb, LANES), BF)] * 2,
        compiler_params=_cparams(1), name="compress",
    )(rows, rows, *params)


def _gather_kernel(pt_ref, *refs):
    del pt_ref
    o_ref = refs[-1]
    for r, in_ref in enumerate(refs[:-1]):
        o_ref[r * PAGE_SIZE:(r + 1) * PAGE_SIZE, :] = in_ref[...].astype(o_ref.dtype)


def _gather_pages(cache, page_table, dtype):
    db, n_pages = page_table.shape
    c = cache.shape[2]
    per_step = min(16, n_pages)
    in_specs = [pl.BlockSpec((None, PAGE_SIZE, c), functools.partial(
        lambda bi, i, pt, r: (pt[bi, i * per_step + r], 0, 0), r=r)) for r in range(per_step)]
    return pl.pallas_call(
        _gather_kernel,
        grid_spec=pltpu.PrefetchScalarGridSpec(
            num_scalar_prefetch=1, grid=(db, n_pages // per_step), in_specs=in_specs,
            out_specs=pl.BlockSpec((None, per_step * PAGE_SIZE, c), lambda bi, i, pt: (bi, i, 0))),
        out_shape=jax.ShapeDtypeStruct((db, n_pages * PAGE_SIZE, c), dtype),
        compiler_params=_cparams(2), name="gather_pages",
    )(page_table, *([cache] * per_step))


def _stack_heads(q_ref, g, tq, fill=0.0):
    lane = lax.broadcasted_iota(jnp.int32, (tq, LANES), 1)
    keep = (lane >= HEAD_DIM * g) & (lane < HEAD_DIM * (g + 1))
    parts = []
    for h in range(HPG):
        hh = g * HPG + h
        ch = hh // 2
        c = q_ref[:, ch * LANES:(ch + 1) * LANES].astype(F32)
        if hh % 2 != g:
            c = pltpu.roll(c, HEAD_DIM, axis=1)
        parts.append(jnp.where(keep, c, fill))
    return jnp.concatenate(parts, axis=0).astype(BF)


def _cmp_branch(q, kc, vc, tpos_col, n_cb, tq):
    ncp = kc.shape[0]
    cidx = lax.broadcasted_iota(jnp.int32, (tq, ncp), 1)
    valid = ((cidx * D_CMP + (L_CMP - 1)) <= tpos_col) & (cidx < n_cb)
    s = _dot_nt(q, kc).reshape(HPG, tq, ncp) + jnp.where(valid, 0.0, NEG)[None]
    e = jnp.exp(s - jnp.max(s, axis=-1, keepdims=True))
    any_valid = (tpos_col >= L_CMP - 1) & (n_cb > 0)
    inv = jnp.where(any_valid[None], 1.0 / jnp.sum(e, axis=-1, keepdims=True), 0.0)
    p = e * inv
    o = _dot(p.reshape(HPG * tq, ncp).astype(BF), vc)
    return o, jnp.sum(p, axis=0)


def _cmp_branch_staged(g, q, kc, vc, tpos_col, n_cb, s_ref, p_ref, oc_ref, tq):
    ncp = kc.shape[0]
    cidx = lax.broadcasted_iota(jnp.int32, (tq, ncp), 1)
    bias = _mask_bias(((cidx * D_CMP + (L_CMP - 1)) <= tpos_col) & (cidx < n_cb))
    any_valid = (tpos_col >= L_CMP - 1) & (n_cb > 0)
    s_ref[g, :, 0:ncp] = _dot_nt(q, kc)
    psum = jnp.zeros((tq, ncp), F32)
    for h in range(HPG):
        rows = slice(h * tq, (h + 1) * tq)
        m = jnp.max(s_ref[g, rows, 0:ncp] + bias, axis=-1, keepdims=True)
        e = jnp.exp(s_ref[g, rows, 0:ncp] + bias - m)
        p = e * jnp.where(any_valid, 1.0 / jnp.sum(e, axis=-1, keepdims=True), 0.0)
        psum = psum + p
        p_ref[g, rows, 0:ncp] = p.astype(BF)
    oc_ref[g] = _dot(p_ref[g, :, 0:ncp], vc)
    return psum


def _importance(psum, mt_ref, tpos_row):
    sp = mt_ref.shape[0]
    mt = mt_ref[...]
    p_hi = psum.astype(BF)
    r1 = psum - p_hi.astype(F32)
    p_mid = r1.astype(BF)
    p_lo = (r1 - p_mid.astype(F32)).astype(BF)
    imp = _dot_nt(mt, p_hi) + _dot_nt(mt, p_mid) + _dot_nt(mt, p_lo)
    s_idx = lax.broadcasted_iota(jnp.int32, (sp, LANES), 0)
    cur = jnp.right_shift(tpos_row, L_SLC.bit_length() - 1)
    forced = (s_idx == 0) | (s_idx == cur) | (s_idx == cur - 1)
    imp = jnp.where(forced, jnp.inf, imp)
    return jnp.where(s_idx * L_SLC <= tpos_row, imp, -jnp.inf)


def _select_blocks(psums, mt_ref, imp_ref, tpos_row, n_sb, tq):
    sp = mt_ref.shape[0]
    assert n_sb <= HEAD_DIM and sp == LANES, "one-hot block lanes hold at most 64 selection blocks"
    nr = _round_up(n_sb, SUBLANES)
    imps = []
    for g, psum in enumerate(psums):
        imp = _importance(psum, mt_ref, tpos_row)
        imp_ref[g] = imp
        imps.append(imp[:nr])
    s_idx = lax.broadcasted_iota(jnp.int32, (nr, LANES), 0)

    def body(i, cnts):
        tie = jnp.where(s_idx > i, 1.0, 0.0)
        out = []
        for g, imp in enumerate(imps):
            row = imp_ref[g, pl.ds(i, 1), :]
            out.append(cnts[g] + jnp.where(row > imp, 1.0, jnp.where(row == imp, tie, 0.0)))
        return tuple(out)

    cnts = lax.fori_loop(0, n_sb, body, tuple(jnp.zeros((nr, LANES), F32) for _ in psums))
    starts_before = s_idx * L_SLC <= tpos_row
    biases = []
    for g, cnt in enumerate(cnts):
        bias_t = jnp.where((cnt < float(N_SEL)) & starts_before, 0.0, NEG)
        if nr < sp:
            bias_t = jnp.concatenate([bias_t, jnp.full((sp - nr, LANES), NEG, F32)], axis=0)
        bias = bias_t.T[:tq]
        biases.append(pltpu.roll(bias, HEAD_DIM, axis=1) if g == 0 else bias)
    return biases


def _select_single(psum, mt_ref, tpos, tq):
    sp = mt_ref.shape[0]
    prow = jnp.broadcast_to(psum[0:1], (LANES, psum.shape[1]))
    imp = _importance(prow, mt_ref, jnp.full((1, LANES), tpos, jnp.int32))
    a = jnp.broadcast_to(imp[:, 0:1], (sp, sp))
    b = a.T
    r_i = lax.broadcasted_iota(jnp.int32, (sp, sp), 0)
    c_i = lax.broadcasted_iota(jnp.int32, (sp, sp), 1)
    beats = jnp.where(a > b, 1.0, jnp.where(a == b, jnp.where(r_i < c_i, 1.0, 0.0), 0.0))
    cnt = jnp.sum(beats, axis=0, keepdims=True)
    sel = jnp.where(cnt < float(N_SEL), 1.0, 0.0)
    return jnp.broadcast_to(sel, (tq, sp)).astype(BF)


def _flash_init(tq):
    return (jnp.full((HPG, tq, 1), NEG, F32), jnp.zeros((HPG, tq, 1), F32), jnp.zeros((HPG * tq, LANES), F32))


def _flash_tile(carry, q, k, v, bias, tq):
    m, l, acc = carry
    kt = k.shape[0]
    s = _dot_nt(q, k).reshape(HPG, tq, kt)
    if bias is not None:
        s = s + bias[None]
    m_new = jnp.maximum(m, jnp.max(s, axis=-1, keepdims=True))
    alpha = jnp.exp(m - m_new)
    p = jnp.exp(s - m_new)
    l = alpha * l + jnp.sum(p, axis=-1, keepdims=True)
    acc = alpha.reshape(HPG * tq, 1) * acc + _dot(p.reshape(HPG * tq, kt).astype(BF), v)
    return m_new, l, acc


def _flash_out(carry, tq):
    _, l, acc = carry
    return acc / l.reshape(HPG * tq, 1)


def _attend_once(q, k, v, bias, tq):
    kt = k.shape[0]
    s = _dot_nt(q, k).reshape(HPG, tq, kt) + bias[None]
    p = jnp.exp(s - jnp.max(s, axis=-1, keepdims=True))
    l = jnp.sum(p, axis=-1, keepdims=True)
    return _dot(p.reshape(HPG * tq, kt).astype(BF), v) / l.reshape(HPG * tq, 1)


def _mask_bias(valid):
    return jnp.where(valid, 0.0, NEG)


def _win_valid(kpos_row, tpos_col):
    dist = tpos_col - kpos_row
    return (dist >= 0) & (dist < WINDOW)


def _merge_heads(o_c, o_s, o_w, gt_ref, g, o_ref, tq):
    lane = lax.broadcasted_iota(jnp.int32, (tq, LANES), 1)
    for pair in range(HPG // 2):
        both = []
        for h in (2 * pair, 2 * pair + 1):
            col = (g * HPG + h) * 3
            rows = slice(h * tq, (h + 1) * tq)
            both.append(o_c(rows) * gt_ref[:, col:col + 1] + o_s(rows) * gt_ref[:, col + 1:col + 2]
                        + o_w(rows) * gt_ref[:, col + 2:col + 3])
        even, odd = both
        if g == 0:
            odd = pltpu.roll(odd, HEAD_DIM, axis=1)
        else:
            even = pltpu.roll(even, HEAD_DIM, axis=1)
        ch = g * (HPG // 2) + pair
        o_ref[:, ch * LANES:(ch + 1) * LANES] = jnp.where(lane < HEAD_DIM, even, odd).astype(o_ref.dtype)


def _staged_tile(state, stage, q, k, v, bias, refs, tq, first):
    s_ref, p_ref, m_ref, l_ref, acc_ref = refs
    kt = k.shape[0]
    reps = kt // LANES
    s_ref[stage, :, 0:kt] = _dot_nt(q, k)
    for h in range(HPG):
        rows = slice(h * tq, (h + 1) * tq)

        def scores():
            s = s_ref[stage, rows, 0:kt]
            return s if bias is None else s + bias

        m_tile = jnp.max(scores(), axis=-1, keepdims=True)
        if first:
            m_new = jnp.broadcast_to(m_tile, (tq, LANES))
        else:
            m_old = m_ref[state, rows, :]
            m_new = jnp.maximum(m_old, m_tile)
            alpha = jnp.exp(m_old - m_new)
        p = jnp.exp(scores() - jnp.concatenate([m_new] * reps, axis=1))
        l_tile = jnp.sum(p, axis=-1, keepdims=True)
        if first:
            l_ref[state, rows, :] = jnp.broadcast_to(l_tile, (tq, LANES))
        else:
            l_ref[state, rows, :] = alpha * l_ref[state, rows, :] + l_tile
            acc_ref[state, rows, :] = alpha * acc_ref[state, rows, :]
        m_ref[state, rows, :] = m_new
        p_ref[stage, rows, 0:kt] = p.astype(BF)
    pv = _dot(p_ref[stage, :, 0:kt], v)
    if first:
        acc_ref[state] = pv
    else:
        acc_ref[state] = acc_ref[state] + pv


def _attn_prompt_kernel(qs_ref, qrs_ref, gt_ref, kc_ref, vc_ref, ka_ref, slc_ref, win_ref, mt_ref, o_ref, imp_ref,
                        s_ref, p_ref, m_ref, l_ref, acc_ref, oc_ref, *, n_cb, n_sb):
    tq = qs_ref.shape[0]
    t_all = slc_ref.shape[0]
    qb = pl.program_id(1)
    t0 = qb * tq
    tpos_col = t0 + lax.broadcasted_iota(jnp.int32, (tq, 1), 0)
    tpos_row = t0 + lax.broadcasted_iota(jnp.int32, (1, LANES), 1)
    groups = range(N_KV_HEADS)
    k_lanes, v_lanes = slice(0, LANES), slice(LANES, 2 * LANES)

    psums = [_cmp_branch_staged(g, _stack_heads(qs_ref, g, tq), kc_ref[...], vc_ref[...], tpos_col, n_cb,
                                s_ref, p_ref, oc_ref, tq) for g in groups]
    blk_bias = _select_blocks(psums, mt_ref, imp_ref, tpos_row, n_sb, tq)
    qr_blk = [_stack_heads(qrs_ref, g, tq, fill=blk_bias[g]) for g in groups]

    refs = (s_ref, p_ref, m_ref, l_ref, acc_ref)

    def slc_tile(j, bias, first):
        rows = pl.ds(pl.multiple_of(j * SLC_TILE, SLC_TILE), SLC_TILE)
        v = slc_ref[rows, v_lanes]
        for g in groups:
            _staged_tile(g, g, qr_blk[g], ka_ref[rows, g * LANES:(g + 1) * LANES], v, bias, refs, tq, first)

    last = (t0 + tq - 1) // SLC_TILE
    causal = (last * SLC_TILE + lax.broadcasted_iota(jnp.int32, (1, SLC_TILE), 1)) <= tpos_col
    slc_tile(last, _mask_bias(causal), True)

    def slc_body(j, c):
        slc_tile(j, None, False)
        return c

    lax.fori_loop(0, last, slc_body, 0)

    wk = min(WINDOW + tq, t_all)
    start = pl.multiple_of(jnp.clip(t0 - WINDOW, 0, t_all - wk), tq)
    wbias = _mask_bias(_win_valid(start + lax.broadcasted_iota(jnp.int32, (1, wk), 1), tpos_col))
    kw, vw = win_ref[pl.ds(start, wk), k_lanes], win_ref[pl.ds(start, wk), v_lanes]
    for g in groups:
        _staged_tile(N_KV_HEADS + g, g, _stack_heads(qrs_ref, g, tq), kw, vw, wbias, refs, tq, True)
    for g in groups:
        w = N_KV_HEADS + g
        _merge_heads(lambda rows: oc_ref[g, rows, :],
                     lambda rows: acc_ref[g, rows, :] / l_ref[g, rows, :],
                     lambda rows: acc_ref[w, rows, :] / l_ref[w, rows, :], gt_ref, g, o_ref, tq)


def _attn_prompt(qs, qrs, gates, kc, vc, ka_bf, slc_bf, win_bf, mt, b, t, n_cb, n_sb):
    tq = Q_BLOCK
    nqb = t // tq
    d = qs.shape[1]
    stage_w = max(SLC_TILE, min(WINDOW + tq, t))
    row = lambda w: pl.BlockSpec((tq, w), lambda bi, qi: (bi * nqb + qi, 0))
    per_b = lambda a: pl.BlockSpec((None,) + a.shape[1:], lambda bi, qi: (bi, 0, 0))
    return pl.pallas_call(
        functools.partial(_attn_prompt_kernel, n_cb=n_cb, n_sb=n_sb),
        grid=(b, nqb),
        in_specs=[row(d), row(d), row(LANES), per_b(kc), per_b(vc), per_b(ka_bf), per_b(slc_bf), per_b(win_bf),
                  _const_spec(mt.shape)],
        out_specs=row(d),
        out_shape=jax.ShapeDtypeStruct((b * t, d), BF),
        scratch_shapes=[pltpu.VMEM((N_KV_HEADS, mt.shape[0], LANES), F32),
                        pltpu.VMEM((N_KV_HEADS, HPG * tq, stage_w), F32),
                        pltpu.VMEM((N_KV_HEADS, HPG * tq, stage_w), BF),
                        pltpu.VMEM((2 * N_KV_HEADS, HPG * tq, LANES), F32),
                        pltpu.VMEM((2 * N_KV_HEADS, HPG * tq, LANES), F32),
                        pltpu.VMEM((2 * N_KV_HEADS, HPG * tq, LANES), F32),
                        pltpu.VMEM((N_KV_HEADS, HPG * tq, LANES), F32)],
        compiler_params=_cparams(2), name="attn_prompt",
    )(qs, qrs, gates, kc, vc, ka_bf, slc_bf, win_bf, mt)


def _attn_sample_kernel(qs_ref, qrs_ref, gt_ref, kc_ref, vc_ref, slc_ref, slct_ref, win_ref, wint_ref, mt_ref, e_ref,
                        et_ref, o_ref, *, n_cb, past):
    tq = qs_ref.shape[0]
    tpos_col = jnp.full((tq, 1), past, jnp.int32)
    w_buf = win_ref.shape[0]
    tail = slct_ref.shape[0]
    k_lanes, v_lanes = slice(0, LANES), slice(LANES, 2 * LANES)
    tail_pos = past + lax.broadcasted_iota(jnp.int32, (1, tail), 1)
    win_pos = past - w_buf + lax.broadcasted_iota(jnp.int32, (1, w_buf), 1)
    wbias = jnp.broadcast_to(_mask_bias(_win_valid(win_pos, tpos_col)), (tq, w_buf))
    wtbias = jnp.broadcast_to(_mask_bias(_win_valid(tail_pos, tpos_col)), (tq, tail))
    kw, vw = win_ref[:, k_lanes].astype(BF), win_ref[:, v_lanes].astype(BF)
    for g in range(N_KV_HEADS):
        qr = _stack_heads(qrs_ref, g, tq)
        o_c, psum = _cmp_branch(_stack_heads(qs_ref, g, tq), kc_ref[...], vc_ref[...], tpos_col, n_cb, tq)
        sel = _select_single(psum, mt_ref, past, tq)
        carry = _flash_tile(_flash_init(tq), qr, slc_ref[:, k_lanes], slc_ref[:, v_lanes],
                            _mask_bias(_dot(sel, e_ref[...]) > 0.5), tq)
        carry = _flash_tile(carry, qr, slct_ref[:, k_lanes], slct_ref[:, v_lanes],
                            _mask_bias((_dot(sel, et_ref[...]) > 0.5) & (tail_pos <= tpos_col)), tq)
        o_s = _flash_out(carry, tq)
        carry = _flash_tile(_flash_init(tq), qr, kw, vw, wbias, tq)
        carry = _flash_tile(carry, qr, wint_ref[:, k_lanes], wint_ref[:, v_lanes], wtbias, tq)
        o_w = _flash_out(carry, tq)
        _merge_heads(lambda rows: o_c[rows], lambda rows: o_s[rows], lambda rows: o_w[rows], gt_ref, g, o_ref, tq)


def _attn_sample(qs, qrs, gates, kc, vc, slc_g, slc_tail, win_c, win_tail, mt, e_all, et, n_cb, past):
    db = qs.shape[0]
    per_b = lambda a: pl.BlockSpec((None,) + a.shape[1:], lambda bi: (bi, 0, 0))
    args = (qs, qrs, gates, kc, vc, slc_g, slc_tail, win_c, win_tail)
    return pl.pallas_call(
        functools.partial(_attn_sample_kernel, n_cb=n_cb, past=past),
        grid=(db,),
        in_specs=[per_b(a) for a in args] + [_const_spec(mt.shape), _const_spec(e_all.shape), _const_spec(et.shape)],
        out_specs=per_b(qs),
        out_shape=jax.ShapeDtypeStruct(qs.shape, BF),
        compiler_params=_cparams(1), name="attn_sample",
    )(*args, mt, e_all, et)


def _block_diag(blocks):
    n, k, j = blocks.shape
    eye = jnp.eye(n, dtype=blocks.dtype)
    return (eye[:, None, :, None] * blocks[:, :, None, :]).reshape(n * k, n * j)


def _pad_rows8(rows):
    return jnp.pad(rows, ((0, 0), (SUBLANES - rows.shape[1], 0), (0, 0)))


def _rope_tables(pos):
    half = HEAD_DIM // 2
    inv_freq = ROPE_THETA ** (-jnp.arange(half, dtype=F32) / half)
    ang = pos.astype(F32)[:, None] * inv_freq[None, :]
    cos, sin = jnp.cos(ang), jnp.sin(ang)
    reps = LANES // HEAD_DIM
    return (jnp.tile(jnp.concatenate([cos, cos], axis=1), (1, reps)),
            jnp.tile(jnp.concatenate([-sin, sin], axis=1), (1, reps)))


def _block_onehot(pos):
    lane = jnp.arange(LANES, dtype=jnp.int32)[None, :] % HEAD_DIM
    return (lane == (pos[:, None] // L_SLC)).astype(F32)


def _overlap_t(n_cb, n_sb, ncp, sp):
    c0 = np.arange(ncp)[None, :] * D_CMP
    s0 = np.arange(sp)[:, None] * L_SLC
    m = (c0 < s0 + L_SLC) & (c0 + L_CMP > s0) & (np.arange(ncp)[None, :] < n_cb) & (np.arange(sp)[:, None] < n_sb)
    return jnp.asarray(m, dtype=BF)


def _expand_tiles(n_tiles, sp, tile, first_key=0):
    key = first_key + np.arange(n_tiles)[:, None, None] * tile + np.arange(tile)[None, None, :]
    return jnp.asarray(np.arange(sp)[None, :, None] == key // L_SLC, dtype=BF)


def _round_up(x, m):
    return (x + m - 1) // m * m


def kernel(x_prompt, x_sample, p_prompt, p_sample, cache_cmp_kv, cache_slc_kv, cache_win_kv, state_rg_conv,
           state_rg_h, state_ffn_conv, page_table, g_mix, g_ffn, g_ple, g_final, rg_w_in, rg_conv_w, rg_conv_b,
           rg_w_a, rg_b_a, rg_w_x, rg_b_x, rg_lambda, rg_w_out, g_kv, w_kv, cmp_pos, cmp_w1, cmp_b1, cmp_w2,
           attn_w_qg, attn_w_o, ffn_w_up, ffn_conv_w, ffn_conv_b, ffn_w_down, ple_w_in, ple_w_gate):
    b, t, d = x_prompt.shape
    db = x_sample.shape[0]
    assert x_sample.shape[1] == 1, "the sample path handles one new token per sequence"
    depth = g_mix.shape[0]
    n_a = rg_w_in.shape[0]
    d_rnn = rg_w_out.shape[1]
    n_pages = page_table.shape[1]
    past = n_pages * PAGE_SIZE
    kvc = 2 * N_KV_HEADS * HEAD_DIM
    n_q = N_KV_HEADS * HPG * HEAD_DIM

    row1 = lambda v: v.reshape(1, -1)
    pad8 = lambda w: jnp.pad(w, ((0, SUBLANES - w.shape[0]), (0, 0)))

    rg_in_bf = rg_w_in.astype(BF)
    rg_out_bf = rg_w_out.astype(BF)
    rg_gate_bf = [jnp.concatenate([_block_diag(rg_w_a[i]), _block_diag(rg_w_x[i])], axis=1).astype(BF)
                  for i in range(n_a)]
    rg_gate_b = [row1(jnp.concatenate([rg_b_a[i].reshape(-1), rg_b_x[i].reshape(-1)])) for i in range(n_a)]
    rg_c = [row1(-RG_C * jax.nn.softplus(-rg_lambda[i])) for i in range(n_a)]
    up_bf = ffn_w_up.astype(BF)
    down_bf = ffn_w_down.astype(BF)
    gate_bf = ple_w_gate.astype(BF)
    pin_bf = ple_w_in.astype(BF)
    wkv_bf = w_kv.astype(BF)
    wq_bf = attn_w_qg[:, :, :n_q].astype(BF)
    wgl_bf = jnp.pad(attn_w_qg[:, :, n_q:], ((0, 0), (0, 0), (0, LANES - 3 * N_KV_HEADS * HPG))).astype(BF)
    wo_bf = attn_w_o.astype(BF)
    gfin = row1(g_final)

    w1 = cmp_w1.reshape(2, 2, D_CMP, HEAD_DIM, -1)
    eye_g = jnp.eye(N_KV_HEADS, dtype=F32)
    w1_bd = (eye_g[None, None, None, :, None, :, None] * w1[:, :, :, None, :, None, :]).reshape(
        2, 2, D_CMP, N_KV_HEADS * HEAD_DIM, -1).astype(BF)
    pos_t = jnp.tile(jnp.transpose(cmp_pos, (1, 0, 2)), (1, 1, N_KV_HEADS)).reshape(2, 2, D_CMP, LANES)
    cmp_params = (pos_t[:, 0], pos_t[:, 1], w1_bd[:, 0], w1_bd[:, 1],
                  jnp.tile(cmp_b1, (1, N_KV_HEADS)).reshape(2, 1, -1),
                  jnp.stack([_block_diag(jnp.stack([cmp_w2[j]] * N_KV_HEADS)) for j in range(2)]).astype(BF))

    def layer_consts(i, final):
        return (row1(g_ffn[i]), up_bf[i], pad8(ffn_conv_w[i]), row1(ffn_conv_b[i]), down_bf[i], row1(g_ple[i]),
                gate_bf[i], pin_bf[i], gfin), final

    n = b * t
    x = x_prompt.reshape(n, d)
    p = p_prompt.reshape(depth, n, -1)
    pos_p = jnp.tile(jnp.arange(t, dtype=jnp.int32), b)
    cos_p, sin_p = _rope_tables(pos_p)
    rg_conv_p, rg_h_p, ffn_conv_p = [], [], []
    zeros_rg = jnp.zeros((b, SUBLANES, d_rnn), F32)
    zeros_ffn = jnp.zeros((b, SUBLANES, ffn_w_up.shape[2]), F32)
    for i in range(depth):
        if i == n_a:
            kv_p, slc_bf, win_bf, ka_bf = _kv_proj(x, row1(g_kv), wkv_bf, cos_p, sin_p, _block_onehot(pos_p))
            kc_p, vc_p = _compress(kv_p.reshape(b, t, -1), cmp_params, b, t)
            n_cb_p = (t - L_CMP) // D_CMP + 1
            n_sb_p = t // L_SLC
            mt_p = _overlap_t(n_cb_p, n_sb_p, t // D_CMP, _round_up(n_sb_p, LANES))
            ka_bf = ka_bf.reshape(b, t, -1)
            slc_bf = slc_bf.reshape(b, t, -1)
            win_bf = win_bf.reshape(b, t, -1)
        if i < n_a:
            yg, xr = _rg_in(x, row1(g_mix[i]), rg_in_bf[i])
            gated, h_last = _rg_core(xr, yg, zeros_rg, zeros_rg, pad8(rg_conv_w[i]), row1(rg_conv_b[i]),
                                     rg_gate_bf[i], rg_gate_b[i], rg_c[i], b, t)
            x = _matmul_res(gated, rg_out_bf[i], x)
            rg_conv_p.append(xr.reshape(b, t, -1)[:, t - 3:])
            rg_h_p.append(h_last[:, 0])
        else:
            j = i - n_a
            qs, qrs, gates = _q_proj(x, row1(g_mix[i]), wq_bf[j], wgl_bf[j], cos_p, sin_p)
            o = _attn_prompt(qs, qrs, gates, kc_p, vc_p, ka_bf, slc_bf, win_bf, mt_p, b, t, n_cb_p, n_sb_p)
            x = _matmul_res(o, wo_bf[j], x)
        consts, final = layer_consts(i, i == depth - 1)
        x, f_hist = _ffn(x, p[i], zeros_ffn, *consts, b, t, final)
        ffn_conv_p.append(f_hist[:, SUBLANES - 2:])
    y_prompt = x.reshape(b, t, d)
    kv5 = kv_p.reshape(b, t, 6, N_KV_HEADS, HEAD_DIM)
    keep = min(WINDOW, t)
    cmp_p, slc_p, win_p = kv5[:, :, 0:2], kv5[:, :, 2:4], kv5[:, t - keep:, 4:6]

    x = x_sample.reshape(db, d)
    p = p_sample.reshape(depth, db, -1)
    cos_s, sin_s = _rope_tables(jnp.full((db,), past, jnp.int32))
    rg_conv_s, rg_h_s, ffn_conv_s = [], [], []
    for i in range(depth):
        if i == n_a:
            pos_s = jnp.full((db,), past, jnp.int32)
            kv_s, slc_new_bf, win_new_bf, _ = _kv_proj(x, row1(g_kv), wkv_bf, cos_s, sin_s, _block_onehot(pos_s))
            cmp_rows = _gather_pages(cache_cmp_kv.reshape(-1, PAGE_SIZE, kvc), page_table, F32)
            slc_rows = _gather_pages(cache_slc_kv.reshape(-1, PAGE_SIZE, kvc), page_table, BF)
            kc_s, vc_s = _compress(cmp_rows, cmp_params, db, past)
            n_cb_s = (past + 1 - L_CMP) // D_CMP + 1
            n_sb_s = -(-(past + 1) // L_SLC)
            sp_s = _round_up(n_sb_s, LANES)
            mt_s = _overlap_t(n_cb_s, n_sb_s, past // D_CMP, sp_s)
            e_s = _expand_tiles(1, sp_s, past)[0]
            et_s = _expand_tiles(1, sp_s, LANES, first_key=past)[0]
            tail = lambda a: jnp.pad(a.reshape(db, 1, -1), ((0, 0), (0, LANES - 1), (0, 0)))
            slc_tail, win_tail = tail(slc_new_bf), tail(win_new_bf)
            win_c = cache_win_kv.reshape(db, cache_win_kv.shape[1], kvc)
        if i < n_a:
            yg, xr = _rg_in(x, row1(g_mix[i]), rg_in_bf[i])
            hist = state_rg_conv[i]
            gated, h_new = _rg_step(xr, yg, hist[:, 0], hist[:, 1], hist[:, 2], state_rg_h[i], pad8(rg_conv_w[i]),
                                    row1(rg_conv_b[i]), rg_gate_bf[i], rg_gate_b[i], rg_c[i])
            x = _matmul_res(gated, rg_out_bf[i], x)
            rg_conv_s.append(jnp.concatenate([hist[:, 1:], xr[:, None, :]], axis=1))
            rg_h_s.append(h_new)
        else:
            j = i - n_a
            qs, qrs, gates = _q_proj(x, row1(g_mix[i]), wq_bf[j], wgl_bf[j], cos_s, sin_s)
            rep = lambda a: jnp.broadcast_to(a[:, None, :], (db, SUBLANES, a.shape[1]))
            o = _attn_sample(rep(qs), rep(qrs), rep(gates), kc_s, vc_s, slc_rows, slc_tail, win_c, win_tail,
                             mt_s, e_s, et_s, n_cb_s, past)
            x = _matmul_res(o[:, 0], wo_bf[j], x)
        consts, final = layer_consts(i, i == depth - 1)
        fh = state_ffn_conv[i]
        x, up = _ffn_step(x, p[i], fh[:, 0], fh[:, 1], *consts, final)
        ffn_conv_s.append(jnp.stack([fh[:, 1], up], axis=1))
    y_sample = x.reshape(db, 1, d)
    kv5 = kv_s.reshape(db, 1, 6, N_KV_HEADS, HEAD_DIM)
    cmp_s, slc_s, win_new = kv5[:, :, 0:2], kv5[:, :, 2:4], kv5[:, :, 4:6]
    win_full = jnp.concatenate([cache_win_kv, win_new], axis=1)
    keep = min(WINDOW, past + 1)
    win_s = win_full[:, win_full.shape[1] - keep:]

    return (y_prompt, y_sample, cmp_p, cmp_s, slc_p, slc_s, win_p, win_s,
            jnp.stack(rg_conv_p), jnp.stack(rg_conv_s), jnp.stack(rg_h_p), jnp.stack(rg_h_s),
            jnp.stack(ffn_conv_p), jnp.stack(ffn_conv_s))
```

```python
import functools

import jax
import jax.numpy as jnp
import numpy as np
from jax import lax
from jax.experimental import pallas as pl
from jax.experimental.pallas import tpu as pltpu

BF = jnp.bfloat16
F32 = jnp.float32

EPS = 1e-6
RG_C = 8.0
N_RG_BLOCKS = 16
HEAD_DIM = 64
N_KV_HEADS = 2
HPG = 8
L_CMP = 32
D_CMP = 16
L_SLC = 64
N_SEL = 16
WINDOW = 512
Q_BLOCK = 128
ROPE_THETA = 10000.0
PAGE_SIZE = 128
NEG = -1e30
LANES = 128
SUBLANES = 8
SLC_TILE = 512
WIN_TILE = 128
VMEM_LIMIT = 56 * 1024 * 1024


def _cparams(n_axes):
    return pltpu.CompilerParams(dimension_semantics=("arbitrary",) * n_axes,
                                vmem_limit_bytes=VMEM_LIMIT)


def _const_spec(shape):
    zeros = (0,) * len(shape)
    return pl.BlockSpec(shape, lambda *_: zeros, pipeline_mode=pl.Buffered(1))


def _dot(a, b):
    return jnp.dot(a, b, preferred_element_type=F32)


def _dot_nt(a, b):
    return lax.dot_general(a, b, (((1,), (1,)), ((), ())), preferred_element_type=F32)


def _rms(x, g):
    return x * lax.rsqrt(jnp.mean(x * x, axis=-1, keepdims=True) + EPS) * g


def _gelu(x):
    return jax.nn.gelu(x, approximate=True)


def _sigmoid(x):
    return jax.nn.sigmoid(x)


def _shift_rows(x, prev8, j):
    if j == 0:
        return x
    r = pltpu.roll(x, j, axis=0)
    p = pltpu.roll(prev8, j, axis=0)
    row = lax.broadcasted_iota(jnp.int32, (SUBLANES, x.shape[1]), 0)
    head = jnp.where(row < j, p, r[:SUBLANES])
    if x.shape[0] == SUBLANES:
        return head
    return jnp.concatenate([head, r[SUBLANES:]], axis=0)


def _rope(z, cos, sin_signed):
    lane = lax.broadcasted_iota(jnp.int32, z.shape, 1)
    first = (lane & (HEAD_DIM - 1)) < (HEAD_DIM // 2)
    rot = jnp.where(first, pltpu.roll(z, LANES - HEAD_DIM // 2, axis=1), pltpu.roll(z, HEAD_DIM // 2, axis=1))
    return z * cos + rot * sin_signed


def _rg_in_kernel(x_ref, g_ref, w_ref, yg_ref, xr_ref, *, chunk):
    h = _rms(x_ref[...], g_ref[...]).astype(BF)
    n = yg_ref.shape[1]
    for c0 in range(0, n, chunk):
        yg_ref[:, c0:c0 + chunk] = _gelu(_dot(h, w_ref[:, c0:c0 + chunk]))
        xr_ref[:, c0:c0 + chunk] = _dot(h, w_ref[:, n + c0:n + c0 + chunk])


def _rg_in(x, g, w_bf):
    n, d = x.shape
    dr = w_bf.shape[1] // 2
    tm = min(512, n)
    return pl.pallas_call(
        functools.partial(_rg_in_kernel, chunk=256),
        grid=(n // tm,),
        in_specs=[pl.BlockSpec((tm, d), lambda i: (i, 0)), _const_spec((1, d)), _const_spec(w_bf.shape)],
        out_specs=[pl.BlockSpec((tm, dr), lambda i: (i, 0)), pl.BlockSpec((tm, dr), lambda i: (i, 0))],
        out_shape=[jax.ShapeDtypeStruct((n, dr), F32), jax.ShapeDtypeStruct((n, dr), F32)],
        compiler_params=_cparams(1), name="rg_in",
    )(x, g, w_bf)


def _rg_gates(xc, wg_ref, bg_ref, c_ref):
    d = xc.shape[1]
    gates = _dot(xc.astype(BF), wg_ref[...]) + bg_ref[...]
    r = _sigmoid(gates[:, :d])
    i = _sigmoid(gates[:, d:])
    log_a = c_ref[...] * r
    a = jnp.exp(log_a)
    z = -jnp.tanh(log_a) * (a * a + 1.0)
    mult = jnp.where(z > 0.0, z * lax.rsqrt(z), 0.0)
    return a, mult, i * xc


def _rg_core_kernel(xr_ref, yg_ref, hist_ref, h0_ref, cw_ref, cb_ref, wg_ref, bg_ref, c_ref,
                    out_ref, hl_ref, xprev, hc):
    t = pl.program_id(1)

    @pl.when(t == 0)
    def _():
        xprev[...] = hist_ref[...]
        hc[...] = h0_ref[...]

    x = xr_ref[...]
    tt, d = x.shape
    prev = xprev[...]
    xc = cb_ref[...] + _shift_rows(x, prev, 3) * cw_ref[0:1, :]
    xc = xc + _shift_rows(x, prev, 2) * cw_ref[1:2, :]
    xc = xc + _shift_rows(x, prev, 1) * cw_ref[2:3, :]
    xc = xc + x * cw_ref[3:4, :]
    xprev[...] = x[tt - SUBLANES:, :]

    a, mult, ixc = _rg_gates(xc, wg_ref, bg_ref, c_ref)
    row = lax.broadcasted_iota(jnp.int32, (tt, d), 0)
    mult = jnp.where((row == 0) & (t == 0), 1.0, mult)
    u = mult * ixc

    av, bv = a, u
    dist = 1
    while dist < tt:
        if dist < SUBLANES:
            ok = row >= dist
            bv = jnp.where(ok, av * pltpu.roll(bv, dist, axis=0) + bv, bv)
            av = jnp.where(ok, av * pltpu.roll(av, dist, axis=0), av)
        else:
            bv = jnp.concatenate([bv[:dist], av[dist:] * bv[:tt - dist] + bv[dist:]], axis=0)
            av = jnp.concatenate([av[:dist], av[dist:] * av[:tt - dist]], axis=0)
        dist *= 2
    h = av * hc[0:1, :] + bv
    hc[...] = jnp.broadcast_to(h[tt - 1:tt, :], (SUBLANES, d))
    hl_ref[...] = hc[...]
    out_ref[...] = (yg_ref[...] * h).astype(BF)


def _rg_core(xr, yg, hist8, h08, cw8, cb, wg_bf, bg, c, b, t):
    n, d = xr.shape
    tt = min(256, t)
    nt = t // tt
    row_spec = pl.BlockSpec((tt, d), lambda bi, ti: (bi * nt + ti, 0))
    st_spec = pl.BlockSpec((None, SUBLANES, d), lambda bi, ti: (bi, 0, 0))
    return pl.pallas_call(
        _rg_core_kernel,
        grid=(b, nt),
        in_specs=[row_spec, row_spec, st_spec, st_spec, _const_spec(cw8.shape), _const_spec(cb.shape),
                  _const_spec(wg_bf.shape), _const_spec(bg.shape), _const_spec(c.shape)],
        out_specs=[row_spec, st_spec],
        out_shape=[jax.ShapeDtypeStruct((n, d), BF), jax.ShapeDtypeStruct((b, SUBLANES, d), F32)],
        scratch_shapes=[pltpu.VMEM((SUBLANES, d), F32), pltpu.VMEM((SUBLANES, d), F32)],
        compiler_params=_cparams(2), name="rg_core",
    )(xr, yg, hist8, h08, cw8, cb, wg_bf, bg, c)


def _rg_step_kernel(xr_ref, yg_ref, h0_ref, h1_ref, h2_ref, hs_ref, cw_ref, cb_ref, wg_ref, bg_ref, c_ref,
                    out_ref, hn_ref):
    x = xr_ref[...]
    xc = cb_ref[...] + h0_ref[...] * cw_ref[0:1, :]
    xc = xc + h1_ref[...] * cw_ref[1:2, :]
    xc = xc + h2_ref[...] * cw_ref[2:3, :]
    xc = xc + x * cw_ref[3:4, :]
    a, mult, ixc = _rg_gates(xc, wg_ref, bg_ref, c_ref)
    h = mult * ixc + a * hs_ref[...]
    hn_ref[...] = h
    out_ref[...] = (yg_ref[...] * h).astype(BF)


def _rg_step(xr, yg, h0, h1, h2, hs, cw8, cb, wg_bf, bg, c):
    n, d = xr.shape
    args = (xr, yg, h0, h1, h2, hs, cw8, cb, wg_bf, bg, c)
    return pl.pallas_call(
        _rg_step_kernel,
        grid=(1,),
        in_specs=[_const_spec(a.shape) for a in args],
        out_specs=[pl.BlockSpec((n, d), lambda i: (0, 0)), pl.BlockSpec((n, d), lambda i: (0, 0))],
        out_shape=[jax.ShapeDtypeStruct((n, d), BF), jax.ShapeDtypeStruct((n, d), F32)],
        compiler_params=_cparams(1), name="rg_step",
    )(*args)


def _matmul_res_kernel(a_ref, w_ref, x_ref, o_ref):
    o_ref[...] = x_ref[...] + _dot(a_ref[...], w_ref[...])


def _matmul_res(a_bf, w_bf, x):
    n, k = a_bf.shape
    d = w_bf.shape[1]
    tm = min(512, n)
    return pl.pallas_call(
        _matmul_res_kernel,
        grid=(n // tm,),
        in_specs=[pl.BlockSpec((tm, k), lambda i: (i, 0)), _const_spec(w_bf.shape),
                  pl.BlockSpec((tm, d), lambda i: (i, 0))],
        out_specs=pl.BlockSpec((tm, d), lambda i: (i, 0)),
        out_shape=jax.ShapeDtypeStruct((n, d), F32),
        compiler_params=_cparams(1), name="matmul_res",
    )(a_bf, w_bf, x)


def _ffn_tail(x, acc, p_ref, gp_ref, wgate_ref, wpin_ref, gfin_ref, final_norm):
    x1 = x + acc
    hp = _rms(x1, gp_ref[...]).astype(BF)
    sig = _sigmoid(_dot(hp, wgate_ref[...]))
    pe = _dot(p_ref[...].astype(BF), wpin_ref[...])
    x2 = x1 + pe * sig
    if final_norm:
        x2 = _rms(x2, gfin_ref[...])
    return x2


def _ffn_kernel(x_ref, p_ref, hist_ref, gf_ref, wup_ref, cw_ref, cb_ref, wdn_ref, gp_ref, wgate_ref, wpin_ref,
                gfin_ref, o_ref, ho_ref, uprev, *, chunk, final_norm):
    t = pl.program_id(1)

    @pl.when(t == 0)
    def _():
        uprev[...] = hist_ref[...]

    x = x_ref[...]
    tm = x.shape[0]
    dff = wdn_ref.shape[0]
    h = _rms(x, gf_ref[...]).astype(BF)
    acc = jnp.zeros(x.shape, F32)
    for c0 in range(0, dff, chunk):
        halves = []
        for off in (c0, dff + c0):
            up = _dot(h, wup_ref[:, off:off + chunk])
            prev = uprev[:, off:off + chunk]
            uc = cb_ref[:, off:off + chunk] + _shift_rows(up, prev, 2) * cw_ref[0:1, off:off + chunk]
            uc = uc + _shift_rows(up, prev, 1) * cw_ref[1:2, off:off + chunk]
            uc = uc + up * cw_ref[2:3, off:off + chunk]
            uprev[:, off:off + chunk] = up[tm - SUBLANES:, :]
            halves.append(uc)
        act = (_gelu(halves[0]) * halves[1]).astype(BF)
        acc = acc + _dot(act, wdn_ref[c0:c0 + chunk, :])
    ho_ref[...] = uprev[...]
    o_ref[...] = _ffn_tail(x, acc, p_ref, gp_ref, wgate_ref, wpin_ref, gfin_ref, final_norm)


def _ffn(x, p_all, layer, hist8, gf, wup_bf, cw8, cb, wdn_bf, gp, wgate_bf, wpin_bf, gfin, b, t, final_norm):
    n, d = x.shape
    dp = p_all.shape[2]
    d2 = wup_bf.shape[1]
    tm = min(1024, t)
    nt = t // tm
    row = lambda w: pl.BlockSpec((tm, w), lambda bi, ti: (bi * nt + ti, 0))
    p_spec = pl.BlockSpec((None, tm, dp), lambda bi, ti: (layer, bi * nt + ti, 0))
    st_spec = pl.BlockSpec((None, SUBLANES, d2), lambda bi, ti: (bi, 0, 0))
    consts = (gf, wup_bf, cw8, cb, wdn_bf, gp, wgate_bf, wpin_bf, gfin)
    return pl.pallas_call(
        functools.partial(_ffn_kernel, chunk=512, final_norm=final_norm),
        grid=(b, nt),
        in_specs=[row(d), p_spec, st_spec] + [_const_spec(a.shape) for a in consts],
        out_specs=[row(d), st_spec],
        out_shape=[jax.ShapeDtypeStruct((n, d), F32), jax.ShapeDtypeStruct((b, SUBLANES, d2), F32)],
        scratch_shapes=[pltpu.VMEM((SUBLANES, d2), F32)],
        compiler_params=_cparams(2), name="ffn",
    )(x, p_all, hist8, *consts)


def _ffn_step_kernel(x_ref, p_ref, h0_ref, h1_ref, gf_ref, wup_ref, cw_ref, cb_ref, wdn_ref, gp_ref, wgate_ref,
                     wpin_ref, gfin_ref, o_ref, up_ref, *, chunk, final_norm):
    x = x_ref[...]
    dff = wdn_ref.shape[0]
    h = _rms(x, gf_ref[...]).astype(BF)
    acc = jnp.zeros(x.shape, F32)
    for c0 in range(0, dff, chunk):
        halves = []
        for off in (c0, dff + c0):
            up = _dot(h, wup_ref[:, off:off + chunk])
            up_ref[:, off:off + chunk] = up
            uc = cb_ref[:, off:off + chunk] + h0_ref[:, off:off + chunk] * cw_ref[0:1, off:off + chunk]
            uc = uc + h1_ref[:, off:off + chunk] * cw_ref[1:2, off:off + chunk]
            uc = uc + up * cw_ref[2:3, off:off + chunk]
            halves.append(uc)
        act = (_gelu(halves[0]) * halves[1]).astype(BF)
        acc = acc + _dot(act, wdn_ref[c0:c0 + chunk, :])
    o_ref[...] = _ffn_tail(x, acc, p_ref, gp_ref, wgate_ref, wpin_ref, gfin_ref, final_norm)


def _ffn_step(x, p, h0, h1, gf, wup_bf, cw8, cb, wdn_bf, gp, wgate_bf, wpin_bf, gfin, final_norm):
    n, d = x.shape
    d2 = wup_bf.shape[1]
    args = (x, p, h0, h1, gf, wup_bf, cw8, cb, wdn_bf, gp, wgate_bf, wpin_bf, gfin)
    return pl.pallas_call(
        functools.partial(_ffn_step_kernel, chunk=512, final_norm=final_norm),
        grid=(1,),
        in_specs=[_const_spec(a.shape) for a in args],
        out_specs=[pl.BlockSpec((n, d), lambda i: (0, 0)), pl.BlockSpec((n, d2), lambda i: (0, 0))],
        out_shape=[jax.ShapeDtypeStruct((n, d), F32), jax.ShapeDtypeStruct((n, d2), F32)],
        compiler_params=_cparams(1), name="ffn_step",
    )(*args)


def _kv_kernel(x_ref, g_ref, w_ref, cos_ref, sin_ref, oh_ref, kv_ref, slc_ref, win_ref, ka_ref):
    h = _rms(x_ref[...], g_ref[...]).astype(BF)
    kv = _dot(h, w_ref[...])
    cos, sin = cos_ref[...], sin_ref[...]
    w2 = 2 * LANES
    k_slc = _rope(kv[:, w2:w2 + LANES], cos, sin)
    k_win = _rope(kv[:, 2 * w2:2 * w2 + LANES], cos, sin)
    kv_ref[:, 0:w2] = kv[:, 0:w2]
    kv_ref[:, w2:w2 + LANES] = k_slc
    kv_ref[:, w2 + LANES:2 * w2] = kv[:, w2 + LANES:2 * w2]
    kv_ref[:, 2 * w2:2 * w2 + LANES] = k_win
    kv_ref[:, 2 * w2 + LANES:3 * w2] = kv[:, 2 * w2 + LANES:3 * w2]
    slc_ref[:, 0:LANES] = k_slc.astype(BF)
    slc_ref[:, LANES:w2] = kv[:, w2 + LANES:2 * w2].astype(BF)
    win_ref[:, 0:LANES] = k_win.astype(BF)
    win_ref[:, LANES:w2] = kv[:, 2 * w2 + LANES:3 * w2].astype(BF)
    low = lax.broadcasted_iota(jnp.int32, k_slc.shape, 1) < HEAD_DIM
    oh = oh_ref[...]
    ka_ref[:, 0:LANES] = jnp.where(low, k_slc, oh).astype(BF)
    ka_ref[:, LANES:w2] = jnp.where(low, oh, k_slc).astype(BF)


def _kv_proj(x, g, w_bf, cos, sin, onehot):
    n, d = x.shape
    nk = w_bf.shape[1]
    tm = min(512, n)
    row = lambda w: pl.BlockSpec((tm, w), lambda i: (i, 0))
    return pl.pallas_call(
        _kv_kernel,
        grid=(n // tm,),
        in_specs=[row(d), _const_spec((1, d)), _const_spec(w_bf.shape), row(LANES), row(LANES), row(LANES)],
        out_specs=[row(nk), row(2 * LANES), row(2 * LANES), row(2 * LANES)],
        out_shape=[jax.ShapeDtypeStruct((n, nk), F32), jax.ShapeDtypeStruct((n, 2 * LANES), BF),
                   jax.ShapeDtypeStruct((n, 2 * LANES), BF), jax.ShapeDtypeStruct((n, 2 * LANES), BF)],
        compiler_params=_cparams(1), name="kv_proj",
    )(x, g, w_bf, cos, sin, onehot)


def _q_kernel(x_ref, g_ref, wq_ref, wg_ref, cos_ref, sin_ref, qs_ref, qrs_ref, gt_ref):
    h = _rms(x_ref[...], g_ref[...]).astype(BF)
    cos, sin = cos_ref[...], sin_ref[...]
    scale = HEAD_DIM ** -0.5
    for c0 in range(0, wq_ref.shape[1], LANES):
        q = _dot(h, wq_ref[:, c0:c0 + LANES])
        qs_ref[:, c0:c0 + LANES] = (q * scale).astype(BF)
        qrs_ref[:, c0:c0 + LANES] = (_rope(q, cos, sin) * scale).astype(BF)
    gt_ref[...] = _sigmoid(_dot(h, wg_ref[...]))


def _q_proj(x, g, wq_bf, wg_bf, cos, sin):
    n, d = x.shape
    dq = wq_bf.shape[1]
    tm = min(512, n)
    row = lambda w: pl.BlockSpec((tm, w), lambda i: (i, 0))
    return pl.pallas_call(
        _q_kernel,
        grid=(n // tm,),
        in_specs=[row(d), _const_spec((1, d)), _const_spec(wq_bf.shape), _const_spec(wg_bf.shape),
                  row(LANES), row(LANES)],
        out_specs=[row(dq), row(dq), row(LANES)],
        out_shape=[jax.ShapeDtypeStruct((n, dq), BF), jax.ShapeDtypeStruct((n, dq), BF),
                   jax.ShapeDtypeStruct((n, LANES), F32)],
        compiler_params=_cparams(1), name="q_proj",
    )(x, g, wq_bf, wg_bf, cos, sin)


def _compress_kernel(xk_ref, xv_ref, pa_ref, pb_ref, w1a_ref, w1b_ref, b1_ref, w2_ref, kc_ref, vc_ref):
    nhb = kc_ref.shape[0]
    for j, x_ref, o_ref in ((0, xk_ref, kc_ref), (1, xv_ref, vc_ref)):
        acc_a = jnp.zeros((nhb, 2 * LANES), F32)
        acc_b = jnp.zeros((nhb, 2 * LANES), F32)
        for l in range(D_CMP):
            xl = x_ref[pl.ds(l, nhb, stride=D_CMP), :]
            acc_a = acc_a + _dot((xl + pa_ref[j, l:l + 1, :]).astype(BF), w1a_ref[j, l])
            acc_b = acc_b + _dot((xl + pb_ref[j, l:l + 1, :]).astype(BF), w1b_ref[j, l])
        hid = _gelu(acc_a + pltpu.roll(acc_b, nhb - 1, axis=0) + b1_ref[j])
        o_ref[...] = _dot(hid.astype(BF), w2_ref[j]).astype(BF)


def _compress(rows, params, b, t):
    nhb = t // D_CMP
    out = pl.BlockSpec((None, nhb, LANES), lambda bi: (bi, 0, 0))
    return pl.pallas_call(
        _compress_kernel,
        grid=(b,),
        in_specs=[pl.BlockSpec((None, t, LANES), lambda bi: (bi, 0, 0)),
                  pl.BlockSpec((None, t, LANES), lambda bi: (bi, 0, 1))] + [_const_spec(a.shape) for a in params],
        out_specs=[out, out],
        out_shape=[jax.ShapeDtypeStruct((b, nhb, LANES), BF)] * 2,
        compiler_params=_cparams(1), name="compress",
    )(rows, rows, *params)


def _compress_paged_kernel(pt_ref, *refs, per_step):
    del pt_ref
    page_refs = refs[:2 * per_step]
    params = refs[2 * per_step:-4]
    kc_ref, vc_ref, xk_scr, xv_scr = refs[-4:]
    step = pl.program_id(1)
    for r in range(per_step):
        rows = pl.ds(pl.multiple_of((step * per_step + r) * PAGE_SIZE, PAGE_SIZE), PAGE_SIZE)
        xk_scr[rows, :] = page_refs[2 * r][...]
        xv_scr[rows, :] = page_refs[2 * r + 1][...]

    @pl.when(step == pl.num_programs(1) - 1)
    def _():
        _compress_kernel(xk_scr, xv_scr, *params, kc_ref, vc_ref)


def _compress_paged(cache, page_table, params):
    db, n_pages = page_table.shape
    t = n_pages * PAGE_SIZE
    nhb = t // D_CMP
    per_step = min(16, n_pages)
    in_specs = []
    for r in range(per_step):
        for half in range(2):
            in_specs.append(pl.BlockSpec((None, PAGE_SIZE, LANES), functools.partial(
                lambda bi, i, pt, r, half: (pt[bi, i * per_step + r], 0, half), r=r, half=half)))
    in_specs += [_const_spec(a.shape) for a in params]
    out = pl.BlockSpec((None, nhb, LANES), lambda bi, i, pt: (bi, 0, 0))
    return pl.pallas_call(
        functools.partial(_compress_paged_kernel, per_step=per_step),
        grid_spec=pltpu.PrefetchScalarGridSpec(
            num_scalar_prefetch=1, grid=(db, n_pages // per_step), in_specs=in_specs, out_specs=[out, out],
            scratch_shapes=[pltpu.VMEM((t, LANES), F32), pltpu.VMEM((t, LANES), F32)]),
        out_shape=[jax.ShapeDtypeStruct((db, nhb, LANES), BF)] * 2,
        compiler_params=_cparams(2), name="compress_paged",
    )(page_table, *([cache] * (2 * per_step)), *params)


def _stack_heads(q_ref, g, tq, fill=0.0):
    lane = lax.broadcasted_iota(jnp.int32, (tq, LANES), 1)
    keep = (lane >= HEAD_DIM * g) & (lane < HEAD_DIM * (g + 1))
    parts = []
    for h in range(HPG):
        hh = g * HPG + h
        ch = hh // 2
        c = q_ref[:, ch * LANES:(ch + 1) * LANES].astype(F32)
        if hh % 2 != g:
            c = pltpu.roll(c, HEAD_DIM, axis=1)
        parts.append(jnp.where(keep, c, fill))
    return jnp.concatenate(parts, axis=0).astype(BF)


def _cmp_branch(q, kc, vc, tpos_col, n_cb, tq):
    ncp = kc.shape[0]
    cidx = lax.broadcasted_iota(jnp.int32, (tq, ncp), 1)
    valid = ((cidx * D_CMP + (L_CMP - 1)) <= tpos_col) & (cidx < n_cb)
    s = _dot_nt(q, kc).reshape(HPG, tq, ncp) + jnp.where(valid, 0.0, NEG)[None]
    e = jnp.exp(s - jnp.max(s, axis=-1, keepdims=True))
    any_valid = (tpos_col >= L_CMP - 1) & (n_cb > 0)
    inv = jnp.where(any_valid[None], 1.0 / jnp.sum(e, axis=-1, keepdims=True), 0.0)
    p = e * inv
    o = _dot(p.reshape(HPG * tq, ncp).astype(BF), vc)
    return o, jnp.sum(p, axis=0)


def _cmp_branch_staged(g, q, kc, vc, tpos_col, n_cb, s_ref, p_ref, oc_ref, tq):
    ncp = kc.shape[0]
    cidx = lax.broadcasted_iota(jnp.int32, (tq, ncp), 1)
    bias = _mask_bias(((cidx * D_CMP + (L_CMP - 1)) <= tpos_col) & (cidx < n_cb))
    any_valid = (tpos_col >= L_CMP - 1) & (n_cb > 0)
    s_ref[g, :, 0:ncp] = _dot_nt(q, kc)
    cw = min(LANES, ncp)
    chunks = [slice(c0, c0 + cw) for c0 in range(0, ncp, cw)]
    psum = [jnp.zeros((tq, cw), F32) for _ in chunks]
    for h in range(HPG):
        rows = slice(h * tq, (h + 1) * tq)
        m_lanes = s_ref[g, rows, chunks[0]] + bias[:, chunks[0]]
        for c in chunks[1:]:
            m_lanes = jnp.maximum(m_lanes, s_ref[g, rows, c] + bias[:, c])
        m = jnp.max(m_lanes, axis=-1, keepdims=True)
        es = [jnp.exp(s_ref[g, rows, c] + bias[:, c] - m) for c in chunks]
        l_lanes = es[0]
        for e in es[1:]:
            l_lanes = l_lanes + e
        inv = jnp.where(any_valid, 1.0 / jnp.sum(l_lanes, axis=-1, keepdims=True), 0.0)
        for i, c in enumerate(chunks):
            p = es[i] * inv
            psum[i] = psum[i] + p
            p_ref[g, rows, c] = p.astype(BF)
    oc_ref[g] = _dot(p_ref[g, :, 0:ncp], vc)
    return psum[0] if len(chunks) == 1 else jnp.concatenate(psum, axis=1)


def _importance(psum, mt_ref, tpos_row):
    sp = mt_ref.shape[0]
    mt = mt_ref[...]
    p_hi = psum.astype(BF)
    r1 = psum - p_hi.astype(F32)
    p_mid = r1.astype(BF)
    p_lo = (r1 - p_mid.astype(F32)).astype(BF)
    imp = _dot_nt(mt, p_hi) + _dot_nt(mt, p_mid) + _dot_nt(mt, p_lo)
    s_idx = lax.broadcasted_iota(jnp.int32, (sp, LANES), 0)
    cur = jnp.right_shift(tpos_row, L_SLC.bit_length() - 1)
    forced = (s_idx == 0) | (s_idx == cur) | (s_idx == cur - 1)
    imp = jnp.where(forced, jnp.inf, imp)
    return jnp.where(s_idx * L_SLC <= tpos_row, imp, -jnp.inf)


def _select_blocks(psums, mt_ref, imp_ref, tpos_row, n_sb, tq):
    sp = mt_ref.shape[0]
    assert n_sb <= HEAD_DIM and sp == LANES, "one-hot block lanes hold at most 64 selection blocks"
    nr = _round_up(n_sb, SUBLANES)
    imps = []
    for g, psum in enumerate(psums):
        imp = _importance(psum, mt_ref, tpos_row)
        imp_ref[g] = imp
        imps.append(imp[:nr])
    s_idx = lax.broadcasted_iota(jnp.int32, (nr, LANES), 0)

    def body(i, cnts):
        tie = jnp.where(s_idx > i, 1.0, 0.0)
        out = []
        for g, imp in enumerate(imps):
            row = imp_ref[g, pl.ds(i, 1), :]
            out.append(cnts[g] + jnp.where(row > imp, 1.0, jnp.where(row == imp, tie, 0.0)))
        return tuple(out)

    cnts = lax.fori_loop(0, n_sb, body, tuple(jnp.zeros((nr, LANES), F32) for _ in psums))
    starts_before = s_idx * L_SLC <= tpos_row
    biases = []
    for g, cnt in enumerate(cnts):
        bias_t = jnp.where((cnt < float(N_SEL)) & starts_before, 0.0, NEG)
        if nr < sp:
            bias_t = jnp.concatenate([bias_t, jnp.full((sp - nr, LANES), NEG, F32)], axis=0)
        bias = bias_t.T[:tq]
        biases.append(pltpu.roll(bias, HEAD_DIM, axis=1) if g == 0 else bias)
    return biases


def _select_single(psum, mt_ref, tpos, tq):
    sp = mt_ref.shape[0]
    prow = jnp.broadcast_to(psum[0:1], (LANES, psum.shape[1]))
    imp = _importance(prow, mt_ref, jnp.full((1, LANES), tpos, jnp.int32))
    a = jnp.broadcast_to(imp[:, 0:1], (sp, sp))
    b = a.T
    r_i = lax.broadcasted_iota(jnp.int32, (sp, sp), 0)
    c_i = lax.broadcasted_iota(jnp.int32, (sp, sp), 1)
    beats = jnp.where(a > b, 1.0, jnp.where(a == b, jnp.where(r_i < c_i, 1.0, 0.0), 0.0))
    cnt = jnp.sum(beats, axis=0, keepdims=True)
    sel = jnp.where(cnt < float(N_SEL), 1.0, 0.0)
    return jnp.broadcast_to(sel, (tq, sp)).astype(BF)


def _flash_init(tq):
    return (jnp.full((HPG, tq, 1), NEG, F32), jnp.zeros((HPG, tq, 1), F32), jnp.zeros((HPG * tq, LANES), F32))


def _flash_tile(carry, q, k, v, bias, tq):
    m, l, acc = carry
    kt = k.shape[0]
    s = _dot_nt(q, k).reshape(HPG, tq, kt)
    if bias is not None:
        s = s + bias[None]
    m_new = jnp.maximum(m, jnp.max(s, axis=-1, keepdims=True))
    alpha = jnp.exp(m - m_new)
    p = jnp.exp(s - m_new)
    l = alpha * l + jnp.sum(p, axis=-1, keepdims=True)
    acc = alpha.reshape(HPG * tq, 1) * acc + _dot(p.reshape(HPG * tq, kt).astype(BF), v)
    return m_new, l, acc


def _flash_out(carry, tq):
    _, l, acc = carry
    return acc / l.reshape(HPG * tq, 1)


def _attend_once(q, k, v, bias, tq):
    kt = k.shape[0]
    s = _dot_nt(q, k).reshape(HPG, tq, kt) + bias[None]
    p = jnp.exp(s - jnp.max(s, axis=-1, keepdims=True))
    l = jnp.sum(p, axis=-1, keepdims=True)
    return _dot(p.reshape(HPG * tq, kt).astype(BF), v) / l.reshape(HPG * tq, 1)


def _mask_bias(valid):
    return jnp.where(valid, 0.0, NEG)


def _win_valid(kpos_row, tpos_col):
    dist = tpos_col - kpos_row
    return (dist >= 0) & (dist < WINDOW)


def _merge_heads(o_c, o_s, o_w, gt_ref, g, o_ref, tq):
    lane = lax.broadcasted_iota(jnp.int32, (tq, LANES), 1)
    for pair in range(HPG // 2):
        both = []
        for h in (2 * pair, 2 * pair + 1):
            col = (g * HPG + h) * 3
            rows = slice(h * tq, (h + 1) * tq)
            both.append(o_c(rows) * gt_ref[:, col:col + 1] + o_s(rows) * gt_ref[:, col + 1:col + 2]
                        + o_w(rows) * gt_ref[:, col + 2:col + 3])
        even, odd = both
        if g == 0:
            odd = pltpu.roll(odd, HEAD_DIM, axis=1)
        else:
            even = pltpu.roll(even, HEAD_DIM, axis=1)
        ch = g * (HPG // 2) + pair
        o_ref[:, ch * LANES:(ch + 1) * LANES] = jnp.where(lane < HEAD_DIM, even, odd).astype(o_ref.dtype)


def _staged_tile(state, stage, q, k, v, bias, refs, tq, first):
    s_ref, p_ref, m_ref, l_ref, acc_ref = refs
    kt = k.shape[0]
    chunks = [slice(c0, c0 + LANES) for c0 in range(0, kt, LANES)]
    s_ref[stage, :, 0:kt] = _dot_nt(q, k)
    for h in range(HPG):
        rows = slice(h * tq, (h + 1) * tq)

        def scores(c):
            s = s_ref[stage, rows, c]
            return s if bias is None else s + bias[:, c]

        m_lanes = scores(chunks[0])
        for c in chunks[1:]:
            m_lanes = jnp.maximum(m_lanes, scores(c))
        m_tile = jnp.max(m_lanes, axis=-1, keepdims=True)
        if first:
            m_new = jnp.broadcast_to(m_tile, (tq, LANES))
        else:
            m_old = m_ref[state, rows, :]
            m_new = jnp.maximum(m_old, m_tile)
            alpha = jnp.exp(m_old - m_new)
        l_lanes = jnp.zeros((tq, LANES), F32)
        for c in chunks:
            p = jnp.exp(scores(c) - m_new)
            l_lanes = l_lanes + p
            p_ref[stage, rows, c] = p.astype(BF)
        l_tile = jnp.sum(l_lanes, axis=-1, keepdims=True)
        if first:
            l_ref[state, rows, :] = jnp.broadcast_to(l_tile, (tq, LANES))
        else:
            l_ref[state, rows, :] = alpha * l_ref[state, rows, :] + l_tile
            acc_ref[state, rows, :] = alpha * acc_ref[state, rows, :]
        m_ref[state, rows, :] = m_new
    pv = _dot(p_ref[stage, :, 0:kt], v)
    if first:
        acc_ref[state] = pv
    else:
        acc_ref[state] = acc_ref[state] + pv


def _attn_prompt_kernel(qs_ref, qrs_ref, gt_ref, kc_ref, vc_ref, ka_ref, slc_ref, win_ref, mt_ref, o_ref, imp_ref,
                        s_ref, p_ref, m_ref, l_ref, acc_ref, oc_ref, *, n_cb, n_sb):
    tq = qs_ref.shape[0]
    t_all = slc_ref.shape[0]
    qb = pl.program_id(1)
    t0 = qb * tq
    tpos_col = t0 + lax.broadcasted_iota(jnp.int32, (tq, 1), 0)
    tpos_row = t0 + lax.broadcasted_iota(jnp.int32, (1, LANES), 1)
    groups = range(N_KV_HEADS)
    k_lanes, v_lanes = slice(0, LANES), slice(LANES, 2 * LANES)

    psums = [_cmp_branch_staged(g, _stack_heads(qs_ref, g, tq), kc_ref[...], vc_ref[...], tpos_col, n_cb,
                                s_ref, p_ref, oc_ref, tq) for g in groups]
    blk_bias = _select_blocks(psums, mt_ref, imp_ref, tpos_row, n_sb, tq)
    qr_blk = [_stack_heads(qrs_ref, g, tq, fill=blk_bias[g]) for g in groups]

    refs = (s_ref, p_ref, m_ref, l_ref, acc_ref)

    def slc_tile(j, bias, first):
        rows = pl.ds(pl.multiple_of(j * SLC_TILE, SLC_TILE), SLC_TILE)
        v = slc_ref[rows, v_lanes]
        for g in groups:
            _staged_tile(g, g, qr_blk[g], ka_ref[rows, g * LANES:(g + 1) * LANES], v, bias, refs, tq, first)

    last = (t0 + tq - 1) // SLC_TILE
    causal = (last * SLC_TILE + lax.broadcasted_iota(jnp.int32, (1, SLC_TILE), 1)) <= tpos_col
    slc_tile(last, _mask_bias(causal), True)

    def slc_body(j, c):
        slc_tile(j, None, False)
        return c

    lax.fori_loop(0, last, slc_body, 0)

    wk = min(WINDOW + tq, t_all)
    start = pl.multiple_of(jnp.clip(t0 - WINDOW, 0, t_all - wk), tq)
    wbias = _mask_bias(_win_valid(start + lax.broadcasted_iota(jnp.int32, (1, wk), 1), tpos_col))
    kw, vw = win_ref[pl.ds(start, wk), k_lanes], win_ref[pl.ds(start, wk), v_lanes]
    for g in groups:
        _staged_tile(N_KV_HEADS + g, g, _stack_heads(qrs_ref, g, tq), kw, vw, wbias, refs, tq, True)
    for g in groups:
        w = N_KV_HEADS + g
        _merge_heads(lambda rows: oc_ref[g, rows, :],
                     lambda rows: acc_ref[g, rows, :] / l_ref[g, rows, :],
                     lambda rows: acc_ref[w, rows, :] / l_ref[w, rows, :], gt_ref, g, o_ref, tq)


def _attn_prompt(qs, qrs, gates, kc, vc, ka_bf, slc_bf, win_bf, mt, b, t, n_cb, n_sb):
    tq = Q_BLOCK
    nqb = t // tq
    d = qs.shape[1]
    stage_w = max(SLC_TILE, min(WINDOW + tq, t))
    row = lambda w: pl.BlockSpec((tq, w), lambda bi, qi: (bi * nqb + qi, 0))
    per_b = lambda a: pl.BlockSpec((None,) + a.shape[1:], lambda bi, qi: (bi, 0, 0))
    return pl.pallas_call(
        functools.partial(_attn_prompt_kernel, n_cb=n_cb, n_sb=n_sb),
        grid=(b, nqb),
        in_specs=[row(d), row(d), row(LANES), per_b(kc), per_b(vc), per_b(ka_bf), per_b(slc_bf), per_b(win_bf),
                  _const_spec(mt.shape)],
        out_specs=row(d),
        out_shape=jax.ShapeDtypeStruct((b * t, d), BF),
        scratch_shapes=[pltpu.VMEM((N_KV_HEADS, mt.shape[0], LANES), F32),
                        pltpu.VMEM((N_KV_HEADS, HPG * tq, stage_w), F32),
                        pltpu.VMEM((N_KV_HEADS, HPG * tq, stage_w), BF),
                        pltpu.VMEM((2 * N_KV_HEADS, HPG * tq, LANES), F32),
                        pltpu.VMEM((2 * N_KV_HEADS, HPG * tq, LANES), F32),
                        pltpu.VMEM((2 * N_KV_HEADS, HPG * tq, LANES), F32),
                        pltpu.VMEM((N_KV_HEADS, HPG * tq, LANES), F32)],
        compiler_params=_cparams(2), name="attn_prompt",
    )(qs, qrs, gates, kc, vc, ka_bf, slc_bf, win_bf, mt)


def _attn_sample_kernel(pt_ref, qs_ref, qrs_ref, gt_ref, kc_ref, vc_ref, slct_ref, win_ref, wint_ref, mt_ref, e_ref,
                        et_ref, *rest, n_cb, past):
    del pt_ref
    page_refs, o_ref, slc_ref = rest[:-2], rest[-2], rest[-1]
    for r, page in enumerate(page_refs):
        slc_ref[r * PAGE_SIZE:(r + 1) * PAGE_SIZE, :] = page[...].astype(BF)
    tq = qs_ref.shape[0]
    tpos_col = jnp.full((tq, 1), past, jnp.int32)
    w_buf = win_ref.shape[0]
    tail = slct_ref.shape[0]
    k_lanes, v_lanes = slice(0, LANES), slice(LANES, 2 * LANES)
    tail_pos = past + lax.broadcasted_iota(jnp.int32, (1, tail), 1)
    win_pos = past - w_buf + lax.broadcasted_iota(jnp.int32, (1, w_buf), 1)
    wbias = jnp.broadcast_to(_mask_bias(_win_valid(win_pos, tpos_col)), (tq, w_buf))
    wtbias = jnp.broadcast_to(_mask_bias(_win_valid(tail_pos, tpos_col)), (tq, tail))
    kw, vw = win_ref[:, k_lanes].astype(BF), win_ref[:, v_lanes].astype(BF)
    for g in range(N_KV_HEADS):
        qr = _stack_heads(qrs_ref, g, tq)
        o_c, psum = _cmp_branch(_stack_heads(qs_ref, g, tq), kc_ref[...], vc_ref[...], tpos_col, n_cb, tq)
        sel = _select_single(psum, mt_ref, past, tq)
        carry = _flash_tile(_flash_init(tq), qr, slc_ref[:, k_lanes], slc_ref[:, v_lanes],
                            _mask_bias(_dot(sel, e_ref[...]) > 0.5), tq)
        carry = _flash_tile(carry, qr, slct_ref[:, k_lanes], slct_ref[:, v_lanes],
                            _mask_bias((_dot(sel, et_ref[...]) > 0.5) & (tail_pos <= tpos_col)), tq)
        o_s = _flash_out(carry, tq)
        carry = _flash_tile(_flash_init(tq), qr, kw, vw, wbias, tq)
        carry = _flash_tile(carry, qr, wint_ref[:, k_lanes], wint_ref[:, v_lanes], wtbias, tq)
        o_w = _flash_out(carry, tq)
        _merge_heads(lambda rows: o_c[rows], lambda rows: o_s[rows], lambda rows: o_w[rows], gt_ref, g, o_ref, tq)


def _attn_sample(qs, qrs, gates, kc, vc, slc_cache, page_table, slc_tail, win_c, win_tail, mt, e_all, et, n_cb, past):
    db, n_pages = page_table.shape
    per_b = lambda a: pl.BlockSpec((None,) + a.shape[1:], lambda bi, pt: (bi, 0, 0))
    args = (qs, qrs, gates, kc, vc, slc_tail, win_c, win_tail)
    pages = [pl.BlockSpec((None, PAGE_SIZE, slc_cache.shape[2]), functools.partial(
        lambda bi, pt, r: (pt[bi, r], 0, 0), r=r)) for r in range(n_pages)]
    return pl.pallas_call(
        functools.partial(_attn_sample_kernel, n_cb=n_cb, past=past),
        grid_spec=pltpu.PrefetchScalarGridSpec(
            num_scalar_prefetch=1, grid=(db,),
            in_specs=[per_b(a) for a in args] + [_const_spec(mt.shape), _const_spec(e_all.shape),
                                                 _const_spec(et.shape)] + pages,
            out_specs=per_b(qs),
            scratch_shapes=[pltpu.VMEM((n_pages * PAGE_SIZE, slc_cache.shape[2]), BF)]),
        out_shape=jax.ShapeDtypeStruct(qs.shape, BF),
        compiler_params=_cparams(1), name="attn_sample",
    )(page_table, *args, mt, e_all, et, *([slc_cache] * n_pages))


def _block_diag(blocks):
    n, k, j = blocks.shape
    eye = jnp.eye(n, dtype=blocks.dtype)
    return (eye[:, None, :, None] * blocks[:, :, None, :]).reshape(n * k, n * j)


def _pad_rows8(rows):
    return jnp.pad(rows, ((0, 0), (SUBLANES - rows.shape[1], 0), (0, 0)))


def _rope_tables(pos):
    half = HEAD_DIM // 2
    inv_freq = ROPE_THETA ** (-jnp.arange(half, dtype=F32) / half)
    ang = pos.astype(F32)[:, None] * inv_freq[None, :]
    cos, sin = jnp.cos(ang), jnp.sin(ang)
    reps = LANES // HEAD_DIM
    return (jnp.tile(jnp.concatenate([cos, cos], axis=1), (1, reps)),
            jnp.tile(jnp.concatenate([-sin, sin], axis=1), (1, reps)))


def _block_onehot(pos):
    lane = jnp.arange(LANES, dtype=jnp.int32)[None, :] % HEAD_DIM
    return (lane == (pos[:, None] // L_SLC)).astype(F32)


def _overlap_t(n_cb, n_sb, ncp, sp):
    c0 = np.arange(ncp)[None, :] * D_CMP
    s0 = np.arange(sp)[:, None] * L_SLC
    m = (c0 < s0 + L_SLC) & (c0 + L_CMP > s0) & (np.arange(ncp)[None, :] < n_cb) & (np.arange(sp)[:, None] < n_sb)
    return jnp.asarray(m, dtype=BF)


def _expand_tiles(n_tiles, sp, tile, first_key=0):
    key = first_key + np.arange(n_tiles)[:, None, None] * tile + np.arange(tile)[None, None, :]
    return jnp.asarray(np.arange(sp)[None, :, None] == key // L_SLC, dtype=BF)


def _round_up(x, m):
    return (x + m - 1) // m * m


def kernel(x_prompt, x_sample, p_prompt, p_sample, cache_cmp_kv, cache_slc_kv, cache_win_kv, state_rg_conv,
           state_rg_h, state_ffn_conv, page_table, g_mix, g_ffn, g_ple, g_final, rg_w_in, rg_conv_w, rg_conv_b,
           rg_w_a, rg_b_a, rg_w_x, rg_b_x, rg_lambda, rg_w_out, g_kv, w_kv, cmp_pos, cmp_w1, cmp_b1, cmp_w2,
           attn_w_qg, attn_w_o, ffn_w_up, ffn_conv_w, ffn_conv_b, ffn_w_down, ple_w_in, ple_w_gate):
    b, t, d = x_prompt.shape
    db = x_sample.shape[0]
    assert x_sample.shape[1] == 1, "the sample path handles one new token per sequence"
    depth = g_mix.shape[0]
    n_a = rg_w_in.shape[0]
    d_rnn = rg_w_out.shape[1]
    n_pages = page_table.shape[1]
    past = n_pages * PAGE_SIZE
    kvc = 2 * N_KV_HEADS * HEAD_DIM
    n_q = N_KV_HEADS * HPG * HEAD_DIM

    row1 = lambda v: v.reshape(1, -1)
    pad8 = lambda w: jnp.pad(w, ((0, SUBLANES - w.shape[0]), (0, 0)))

    rg_in_bf = rg_w_in.astype(BF)
    rg_out_bf = rg_w_out.astype(BF)
    rg_gate_bf = [jnp.concatenate([_block_diag(rg_w_a[i]), _block_diag(rg_w_x[i])], axis=1).astype(BF)
                  for i in range(n_a)]
    rg_gate_b = [row1(jnp.concatenate([rg_b_a[i].reshape(-1), rg_b_x[i].reshape(-1)])) for i in range(n_a)]
    rg_c = [row1(-RG_C * jax.nn.softplus(-rg_lambda[i])) for i in range(n_a)]
    up_bf = ffn_w_up.astype(BF)
    down_bf = ffn_w_down.astype(BF)
    gate_bf = ple_w_gate.astype(BF)
    pin_bf = ple_w_in.astype(BF)
    wkv_bf = w_kv.astype(BF)
    wq_bf = attn_w_qg[:, :, :n_q].astype(BF)
    wgl_bf = jnp.pad(attn_w_qg[:, :, n_q:], ((0, 0), (0, 0), (0, LANES - 3 * N_KV_HEADS * HPG))).astype(BF)
    wo_bf = attn_w_o.astype(BF)
    gfin = row1(g_final)

    w1 = cmp_w1.reshape(2, 2, D_CMP, HEAD_DIM, -1)
    eye_g = jnp.eye(N_KV_HEADS, dtype=F32)
    w1_bd = (eye_g[None, None, None, :, None, :, None] * w1[:, :, :, None, :, None, :]).reshape(
        2, 2, D_CMP, N_KV_HEADS * HEAD_DIM, -1).astype(BF)
    pos_t = jnp.tile(jnp.transpose(cmp_pos, (1, 0, 2)), (1, 1, N_KV_HEADS)).reshape(2, 2, D_CMP, LANES)
    cmp_params = (pos_t[:, 0], pos_t[:, 1], w1_bd[:, 0], w1_bd[:, 1],
                  jnp.tile(cmp_b1, (1, N_KV_HEADS)).reshape(2, 1, -1),
                  jnp.stack([_block_diag(jnp.stack([cmp_w2[j]] * N_KV_HEADS)) for j in range(2)]).astype(BF))

    def layer_consts(i, final):
        return (row1(g_ffn[i]), up_bf[i], pad8(ffn_conv_w[i]), row1(ffn_conv_b[i]), down_bf[i], row1(g_ple[i]),
                gate_bf[i], pin_bf[i], gfin), final

    n = b * t
    x = x_prompt.reshape(n, d)
    p = p_prompt.reshape(depth, n, -1)
    pos_p = jnp.tile(jnp.arange(t, dtype=jnp.int32), b)
    cos_p, sin_p = _rope_tables(pos_p)
    rg_conv_p, rg_h_p, ffn_conv_p = [], [], []
    zeros_rg = jnp.zeros((b, SUBLANES, d_rnn), F32)
    zeros_ffn = jnp.zeros((b, SUBLANES, ffn_w_up.shape[2]), F32)
    for i in range(depth):
        if i == n_a:
            kv_p, slc_bf, win_bf, ka_bf = _kv_proj(x, row1(g_kv), wkv_bf, cos_p, sin_p, _block_onehot(pos_p))
            kc_p, vc_p = _compress(kv_p.reshape(b, t, -1), cmp_params, b, t)
            n_cb_p = (t - L_CMP) // D_CMP + 1
            n_sb_p = t // L_SLC
            mt_p = _overlap_t(n_cb_p, n_sb_p, t // D_CMP, _round_up(n_sb_p, LANES))
            ka_bf = ka_bf.reshape(b, t, -1)
            slc_bf = slc_bf.reshape(b, t, -1)
            win_bf = win_bf.reshape(b, t, -1)
        if i < n_a:
            yg, xr = _rg_in(x, row1(g_mix[i]), rg_in_bf[i])
            gated, h_last = _rg_core(xr, yg, zeros_rg, zeros_rg, pad8(rg_conv_w[i]), row1(rg_conv_b[i]),
                                     rg_gate_bf[i], rg_gate_b[i], rg_c[i], b, t)
            x = _matmul_res(gated, rg_out_bf[i], x)
            rg_conv_p.append(xr.reshape(b, t, -1)[:, t - 3:])
            rg_h_p.append(h_last[:, 0])
        else:
            j = i - n_a
            qs, qrs, gates = _q_proj(x, row1(g_mix[i]), wq_bf[j], wgl_bf[j], cos_p, sin_p)
            o = _attn_prompt(qs, qrs, gates, kc_p, vc_p, ka_bf, slc_bf, win_bf, mt_p, b, t, n_cb_p, n_sb_p)
            x = _matmul_res(o, wo_bf[j], x)
        consts, final = layer_consts(i, i == depth - 1)
        x, f_hist = _ffn(x, p, i, zeros_ffn, *consts, b, t, final)
        ffn_conv_p.append(f_hist[:, SUBLANES - 2:])
    y_prompt = x.reshape(b, t, d)
    kv5 = kv_p.reshape(b, t, 6, N_KV_HEADS, HEAD_DIM)
    keep = min(WINDOW, t)
    cmp_p, slc_p, win_p = kv5[:, :, 0:2], kv5[:, :, 2:4], kv5[:, t - keep:, 4:6]

    x = x_sample.reshape(db, d)
    p = p_sample.reshape(depth, db, -1)
    cos_s, sin_s = _rope_tables(jnp.full((db,), past, jnp.int32))
    rg_conv_s, rg_h_s, ffn_conv_s = [], [], []
    for i in range(depth):
        if i == n_a:
            pos_s = jnp.full((db,), past, jnp.int32)
            kv_s, slc_new_bf, win_new_bf, _ = _kv_proj(x, row1(g_kv), wkv_bf, cos_s, sin_s, _block_onehot(pos_s))
            kc_s, vc_s = _compress_paged(cache_cmp_kv.reshape(-1, PAGE_SIZE, kvc), page_table, cmp_params)
            slc_cache = cache_slc_kv.reshape(-1, PAGE_SIZE, kvc)
            n_cb_s = (past + 1 - L_CMP) // D_CMP + 1
            n_sb_s = -(-(past + 1) // L_SLC)
            sp_s = _round_up(n_sb_s, LANES)
            mt_s = _overlap_t(n_cb_s, n_sb_s, past // D_CMP, sp_s)
            e_s = _expand_tiles(1, sp_s, past)[0]
            et_s = _expand_tiles(1, sp_s, LANES, first_key=past)[0]
            tail = lambda a: jnp.pad(a.reshape(db, 1, -1), ((0, 0), (0, LANES - 1), (0, 0)))
            slc_tail, win_tail = tail(slc_new_bf), tail(win_new_bf)
            win_c = cache_win_kv.reshape(db, cache_win_kv.shape[1], kvc)
        if i < n_a:
            yg, xr = _rg_in(x, row1(g_mix[i]), rg_in_bf[i])
            hist = state_rg_conv[i]
            gated, h_new = _rg_step(xr, yg, hist[:, 0], hist[:, 1], hist[:, 2], state_rg_h[i], pad8(rg_conv_w[i]),
                                    row1(rg_conv_b[i]), rg_gate_bf[i], rg_gate_b[i], rg_c[i])
            x = _matmul_res(gated, rg_out_bf[i], x)
            rg_conv_s.append(jnp.concatenate([hist[:, 1:], xr[:, None, :]], axis=1))
            rg_h_s.append(h_new)
        else:
            j = i - n_a
            qs, qrs, gates = _q_proj(x, row1(g_mix[i]), wq_bf[j], wgl_bf[j], cos_s, sin_s)
            rep = lambda a: jnp.broadcast_to(a[:, None, :], (db, SUBLANES, a.shape[1]))
            o = _attn_sample(rep(qs), rep(qrs), rep(gates), kc_s, vc_s, slc_cache, page_table, slc_tail, win_c,
                             win_tail, mt_s, e_s, et_s, n_cb_s, past)
            x = _matmul_res(o[:, 0], wo_bf[j], x)
        consts, final = layer_consts(i, i == depth - 1)
        fh = state_ffn_conv[i]
        x, up = _ffn_step(x, p[i], fh[:, 0], fh[:, 1], *consts, final)
        ffn_conv_s.append(jnp.stack([fh[:, 1], up], axis=1))
    y_sample = x.reshape(db, 1, d)
    kv5 = kv_s.reshape(db, 1, 6, N_KV_HEADS, HEAD_DIM)
    cmp_s, slc_s, win_new = kv5[:, :, 0:2], kv5[:, :, 2:4], kv5[:, :, 4:6]
    win_full = jnp.concatenate([cache_win_kv, win_new], axis=1)
    keep = min(WINDOW, past + 1)
    win_s = win_full[:, win_full.shape[1] - keep:]

    return (y_prompt, y_sample, cmp_p, cmp_s, slc_p, slc_s, win_p, win_s,
            jnp.stack(rg_conv_p), jnp.stack(rg_conv_s), jnp.stack(rg_h_p), jnp.stack(rg_h_s),
            jnp.stack(ffn_conv_p), jnp.stack(ffn_conv_s))
```

```python
import functools

import jax
import jax.numpy as jnp
import numpy as np
from jax import lax
from jax.experimental import pallas as pl
from jax.experimental.pallas import tpu as pltpu

BF = jnp.bfloat16
F32 = jnp.float32

EPS = 1e-6
RG_C = 8.0
N_RG_BLOCKS = 16
HEAD_DIM = 64
N_KV_HEADS = 2
HPG = 8
L_CMP = 32
D_CMP = 16
L_SLC = 64
N_SEL = 16
WINDOW = 512
Q_BLOCK = 128
ROPE_THETA = 10000.0
PAGE_SIZE = 128
NEG = -1e30
LANES = 128
SUBLANES = 8
SLC_TILE = 512
WIN_TILE = 128
VMEM_LIMIT = 56 * 1024 * 1024


def _cparams(n_axes):
    return pltpu.CompilerParams(dimension_semantics=("arbitrary",) * n_axes,
                                vmem_limit_bytes=VMEM_LIMIT)


def _const_spec(shape):
    zeros = (0,) * len(shape)
    return pl.BlockSpec(shape, lambda *_: zeros, pipeline_mode=pl.Buffered(1))


def _dot(a, b):
    return jnp.dot(a, b, preferred_element_type=F32)


def _dot_nt(a, b):
    return lax.dot_general(a, b, (((1,), (1,)), ((), ())), preferred_element_type=F32)


def _rms(x, g):
    return x * lax.rsqrt(jnp.mean(x * x, axis=-1, keepdims=True) + EPS) * g


def _gelu(x):
    return jax.nn.gelu(x, approximate=True)


def _sigmoid(x):
    return jax.nn.sigmoid(x)


def _shift_rows(x, prev8, j):
    if j == 0:
        return x
    r = pltpu.roll(x, j, axis=0)
    p = pltpu.roll(prev8, j, axis=0)
    row = lax.broadcasted_iota(jnp.int32, (SUBLANES, x.shape[1]), 0)
    head = jnp.where(row < j, p, r[:SUBLANES])
    if x.shape[0] == SUBLANES:
        return head
    return jnp.concatenate([head, r[SUBLANES:]], axis=0)


def _rope(z, cos, sin_signed):
    lane = lax.broadcasted_iota(jnp.int32, z.shape, 1)
    first = (lane & (HEAD_DIM - 1)) < (HEAD_DIM // 2)
    rot = jnp.where(first, pltpu.roll(z, LANES - HEAD_DIM // 2, axis=1), pltpu.roll(z, HEAD_DIM // 2, axis=1))
    return z * cos + rot * sin_signed


def _rg_in_kernel(x_ref, g_ref, w_ref, yg_ref, xr_ref, *, chunk):
    h = _rms(x_ref[...], g_ref[...]).astype(BF)
    n = yg_ref.shape[1]
    for c0 in range(0, n, chunk):
        yg_ref[:, c0:c0 + chunk] = _gelu(_dot(h, w_ref[:, c0:c0 + chunk]))
        xr_ref[:, c0:c0 + chunk] = _dot(h, w_ref[:, n + c0:n + c0 + chunk])


def _rg_in(x, g, w_bf):
    n, d = x.shape
    dr = w_bf.shape[1] // 2
    tm = min(512, n)
    return pl.pallas_call(
        functools.partial(_rg_in_kernel, chunk=256),
        grid=(n // tm,),
        in_specs=[pl.BlockSpec((tm, d), lambda i: (i, 0)), _const_spec((1, d)), _const_spec(w_bf.shape)],
        out_specs=[pl.BlockSpec((tm, dr), lambda i: (i, 0)), pl.BlockSpec((tm, dr), lambda i: (i, 0))],
        out_shape=[jax.ShapeDtypeStruct((n, dr), F32), jax.ShapeDtypeStruct((n, dr), F32)],
        compiler_params=_cparams(1), name="rg_in",
    )(x, g, w_bf)


def _rg_gates(xc, wg_ref, bg_ref, c_ref):
    d = xc.shape[1]
    gates = _dot(xc.astype(BF), wg_ref[...]) + bg_ref[...]
    r = _sigmoid(gates[:, :d])
    i = _sigmoid(gates[:, d:])
    log_a = c_ref[...] * r
    a = jnp.exp(log_a)
    z = -jnp.tanh(log_a) * (a * a + 1.0)
    mult = jnp.where(z > 0.0, z * lax.rsqrt(z), 0.0)
    return a, mult, i * xc


def _rg_core_kernel(xr_ref, yg_ref, hist_ref, h0_ref, cw_ref, cb_ref, wg_ref, bg_ref, c_ref,
                    out_ref, hl_ref, xprev, hc):
    t = pl.program_id(1)

    @pl.when(t == 0)
    def _():
        xprev[...] = hist_ref[...]
        hc[...] = h0_ref[...]

    x = xr_ref[...]
    tt, d = x.shape
    prev = xprev[...]
    xc = cb_ref[...] + _shift_rows(x, prev, 3) * cw_ref[0:1, :]
    xc = xc + _shift_rows(x, prev, 2) * cw_ref[1:2, :]
    xc = xc + _shift_rows(x, prev, 1) * cw_ref[2:3, :]
    xc = xc + x * cw_ref[3:4, :]
    xprev[...] = x[tt - SUBLANES:, :]

    a, mult, ixc = _rg_gates(xc, wg_ref, bg_ref, c_ref)
    row = lax.broadcasted_iota(jnp.int32, (tt, d), 0)
    mult = jnp.where((row == 0) & (t == 0), 1.0, mult)
    u = mult * ixc

    av, bv = a, u
    dist = 1
    while dist < tt:
        if dist < SUBLANES:
            ok = row >= dist
            bv = jnp.where(ok, av * pltpu.roll(bv, dist, axis=0) + bv, bv)
            av = jnp.where(ok, av * pltpu.roll(av, dist, axis=0), av)
        else:
            bv = jnp.concatenate([bv[:dist], av[dist:] * bv[:tt - dist] + bv[dist:]], axis=0)
            av = jnp.concatenate([av[:dist], av[dist:] * av[:tt - dist]], axis=0)
        dist *= 2
    h = av * hc[0:1, :] + bv
    hc[...] = jnp.broadcast_to(h[tt - 1:tt, :], (SUBLANES, d))
    hl_ref[...] = hc[...]
    out_ref[...] = (yg_ref[...] * h).astype(BF)


def _rg_core(xr, yg, hist8, h08, cw8, cb, wg_bf, bg, c, b, t):
    n, d = xr.shape
    tt = min(256, t)
    nt = t // tt
    row_spec = pl.BlockSpec((tt, d), lambda bi, ti: (bi * nt + ti, 0))
    st_spec = pl.BlockSpec((None, SUBLANES, d), lambda bi, ti: (bi, 0, 0))
    return pl.pallas_call(
        _rg_core_kernel,
        grid=(b, nt),
        in_specs=[row_spec, row_spec, st_spec, st_spec, _const_spec(cw8.shape), _const_spec(cb.shape),
                  _const_spec(wg_bf.shape), _const_spec(bg.shape), _const_spec(c.shape)],
        out_specs=[row_spec, st_spec],
        out_shape=[jax.ShapeDtypeStruct((n, d), BF), jax.ShapeDtypeStruct((b, SUBLANES, d), F32)],
        scratch_shapes=[pltpu.VMEM((SUBLANES, d), F32), pltpu.VMEM((SUBLANES, d), F32)],
        compiler_params=_cparams(2), name="rg_core",
    )(xr, yg, hist8, h08, cw8, cb, wg_bf, bg, c)


def _rg_step_kernel(xr_ref, yg_ref, h0_ref, h1_ref, h2_ref, hs_ref, cw_ref, cb_ref, wg_ref, bg_ref, c_ref,
                    out_ref, hn_ref):
    x = xr_ref[...]
    xc = cb_ref[...] + h0_ref[...] * cw_ref[0:1, :]
    xc = xc + h1_ref[...] * cw_ref[1:2, :]
    xc = xc + h2_ref[...] * cw_ref[2:3, :]
    xc = xc + x * cw_ref[3:4, :]
    a, mult, ixc = _rg_gates(xc, wg_ref, bg_ref, c_ref)
    h = mult * ixc + a * hs_ref[...]
    hn_ref[...] = h
    out_ref[...] = (yg_ref[...] * h).astype(BF)


def _rg_step(xr, yg, h0, h1, h2, hs, cw8, cb, wg_bf, bg, c):
    n, d = xr.shape
    args = (xr, yg, h0, h1, h2, hs, cw8, cb, wg_bf, bg, c)
    return pl.pallas_call(
        _rg_step_kernel,
        grid=(1,),
        in_specs=[_const_spec(a.shape) for a in args],
        out_specs=[pl.BlockSpec((n, d), lambda i: (0, 0)), pl.BlockSpec((n, d), lambda i: (0, 0))],
        out_shape=[jax.ShapeDtypeStruct((n, d), BF), jax.ShapeDtypeStruct((n, d), F32)],
        compiler_params=_cparams(1), name="rg_step",
    )(*args)


def _matmul_res_kernel(a_ref, w_ref, x_ref, o_ref):
    o_ref[...] = x_ref[...] + _dot(a_ref[...], w_ref[...])


def _matmul_res(a_bf, w_bf, x):
    n, k = a_bf.shape
    d = w_bf.shape[1]
    tm = min(512, n)
    return pl.pallas_call(
        _matmul_res_kernel,
        grid=(n // tm,),
        in_specs=[pl.BlockSpec((tm, k), lambda i: (i, 0)), _const_spec(w_bf.shape),
                  pl.BlockSpec((tm, d), lambda i: (i, 0))],
        out_specs=pl.BlockSpec((tm, d), lambda i: (i, 0)),
        out_shape=jax.ShapeDtypeStruct((n, d), F32),
        compiler_params=_cparams(1), name="matmul_res",
    )(a_bf, w_bf, x)


def _ffn_tail(x, acc, p_ref, gp_ref, wgate_ref, wpin_ref, gfin_ref, final_norm):
    x1 = x + acc
    hp = _rms(x1, gp_ref[...]).astype(BF)
    sig = _sigmoid(_dot(hp, wgate_ref[...]))
    pe = _dot(p_ref[...].astype(BF), wpin_ref[...])
    x2 = x1 + pe * sig
    if final_norm:
        x2 = _rms(x2, gfin_ref[...])
    return x2


def _ffn_kernel(x_ref, p_ref, hist_ref, gf_ref, wup_ref, cw_ref, cb_ref, wdn_ref, gp_ref, wgate_ref, wpin_ref,
                gfin_ref, o_ref, ho_ref, uprev, *, chunk, final_norm):
    t = pl.program_id(1)

    @pl.when(t == 0)
    def _():
        uprev[...] = hist_ref[...]

    x = x_ref[...]
    tm = x.shape[0]
    dff = wdn_ref.shape[0]
    h = _rms(x, gf_ref[...]).astype(BF)
    acc = jnp.zeros(x.shape, F32)
    for c0 in range(0, dff, chunk):
        halves = []
        for off in (c0, dff + c0):
            up = _dot(h, wup_ref[:, off:off + chunk])
            prev = uprev[:, off:off + chunk]
            uc = cb_ref[:, off:off + chunk] + _shift_rows(up, prev, 2) * cw_ref[0:1, off:off + chunk]
            uc = uc + _shift_rows(up, prev, 1) * cw_ref[1:2, off:off + chunk]
            uc = uc + up * cw_ref[2:3, off:off + chunk]
            uprev[:, off:off + chunk] = up[tm - SUBLANES:, :]
            halves.append(uc)
        act = (_gelu(halves[0]) * halves[1]).astype(BF)
        acc = acc + _dot(act, wdn_ref[c0:c0 + chunk, :])
    ho_ref[...] = uprev[...]
    o_ref[...] = _ffn_tail(x, acc, p_ref, gp_ref, wgate_ref, wpin_ref, gfin_ref, final_norm)


def _ffn(x, p_all, layer, hist8, gf, wup_bf, cw8, cb, wdn_bf, gp, wgate_bf, wpin_bf, gfin, b, t, final_norm):
    n, d = x.shape
    dp = p_all.shape[2]
    d2 = wup_bf.shape[1]
    tm = min(1024, t)
    nt = t // tm
    row = lambda w: pl.BlockSpec((tm, w), lambda bi, ti: (bi * nt + ti, 0))
    p_spec = pl.BlockSpec((None, tm, dp), lambda bi, ti: (layer, bi * nt + ti, 0))
    st_spec = pl.BlockSpec((None, SUBLANES, d2), lambda bi, ti: (bi, 0, 0))
    consts = (gf, wup_bf, cw8, cb, wdn_bf, gp, wgate_bf, wpin_bf, gfin)
    return pl.pallas_call(
        functools.partial(_ffn_kernel, chunk=512, final_norm=final_norm),
        grid=(b, nt),
        in_specs=[row(d), p_spec, st_spec] + [_const_spec(a.shape) for a in consts],
        out_specs=[row(d), st_spec],
        out_shape=[jax.ShapeDtypeStruct((n, d), F32), jax.ShapeDtypeStruct((b, SUBLANES, d2), F32)],
        scratch_shapes=[pltpu.VMEM((SUBLANES, d2), F32)],
        compiler_params=_cparams(2), name="ffn",
    )(x, p_all, hist8, *consts)


def _ffn_step_kernel(x_ref, p_ref, h0_ref, h1_ref, gf_ref, wup_ref, cw_ref, cb_ref, wdn_ref, gp_ref, wgate_ref,
                     wpin_ref, gfin_ref, o_ref, up_ref, *, chunk, final_norm):
    x = x_ref[...]
    dff = wdn_ref.shape[0]
    h = _rms(x, gf_ref[...]).astype(BF)
    acc = jnp.zeros(x.shape, F32)
    for c0 in range(0, dff, chunk):
        halves = []
        for off in (c0, dff + c0):
            up = _dot(h, wup_ref[:, off:off + chunk])
            up_ref[:, off:off + chunk] = up
            uc = cb_ref[:, off:off + chunk] + h0_ref[:, off:off + chunk] * cw_ref[0:1, off:off + chunk]
            uc = uc + h1_ref[:, off:off + chunk] * cw_ref[1:2, off:off + chunk]
            uc = uc + up * cw_ref[2:3, off:off + chunk]
            halves.append(uc)
        act = (_gelu(halves[0]) * halves[1]).astype(BF)
        acc = acc + _dot(act, wdn_ref[c0:c0 + chunk, :])
    o_ref[...] = _ffn_tail(x, acc, p_ref, gp_ref, wgate_ref, wpin_ref, gfin_ref, final_norm)


def _ffn_step(x, p, h0, h1, gf, wup_bf, cw8, cb, wdn_bf, gp, wgate_bf, wpin_bf, gfin, final_norm):
    n, d = x.shape
    d2 = wup_bf.shape[1]
    args = (x, p, h0, h1, gf, wup_bf, cw8, cb, wdn_bf, gp, wgate_bf, wpin_bf, gfin)
    return pl.pallas_call(
        functools.partial(_ffn_step_kernel, chunk=512, final_norm=final_norm),
        grid=(1,),
        in_specs=[_const_spec(a.shape) for a in args],
        out_specs=[pl.BlockSpec((n, d), lambda i: (0, 0)), pl.BlockSpec((n, d2), lambda i: (0, 0))],
        out_shape=[jax.ShapeDtypeStruct((n, d), F32), jax.ShapeDtypeStruct((n, d2), F32)],
        compiler_params=_cparams(1), name="ffn_step",
    )(*args)


def _kv_kernel(x_ref, g_ref, w_ref, cos_ref, sin_ref, oh_ref, kv_ref, slc_ref, win_ref, ka_ref, sva_ref, wva_ref):
    h = _rms(x_ref[...], g_ref[...]).astype(BF)
    kv = _dot(h, w_ref[...])
    cos, sin = cos_ref[...], sin_ref[...]
    w2 = 2 * LANES
    k_slc = _rope(kv[:, w2:w2 + LANES], cos, sin)
    k_win = _rope(kv[:, 2 * w2:2 * w2 + LANES], cos, sin)
    kv_ref[:, 0:w2] = kv[:, 0:w2]
    kv_ref[:, w2:w2 + LANES] = k_slc
    kv_ref[:, w2 + LANES:2 * w2] = kv[:, w2 + LANES:2 * w2]
    kv_ref[:, 2 * w2:2 * w2 + LANES] = k_win
    kv_ref[:, 2 * w2 + LANES:3 * w2] = kv[:, 2 * w2 + LANES:3 * w2]
    slc_ref[:, 0:LANES] = k_slc.astype(BF)
    slc_ref[:, LANES:w2] = kv[:, w2 + LANES:2 * w2].astype(BF)
    win_ref[:, 0:LANES] = k_win.astype(BF)
    win_ref[:, LANES:w2] = kv[:, 2 * w2 + LANES:3 * w2].astype(BF)
    low = lax.broadcasted_iota(jnp.int32, k_slc.shape, 1) < HEAD_DIM
    oh = oh_ref[...]
    ka_ref[:, 0:LANES] = jnp.where(low, k_slc, oh).astype(BF)
    ka_ref[:, LANES:w2] = jnp.where(low, oh, k_slc).astype(BF)
    for va_ref, v in ((sva_ref, kv[:, w2 + LANES:2 * w2]), (wva_ref, kv[:, 2 * w2 + LANES:3 * w2])):
        va_ref[:, 0:LANES] = jnp.where(low, v, 1.0).astype(BF)
        va_ref[:, LANES:w2] = jnp.where(low, 1.0, v).astype(BF)


def _kv_proj(x, g, w_bf, cos, sin, onehot):
    n, d = x.shape
    nk = w_bf.shape[1]
    tm = min(512, n)
    row = lambda w: pl.BlockSpec((tm, w), lambda i: (i, 0))
    return pl.pallas_call(
        _kv_kernel,
        grid=(n // tm,),
        in_specs=[row(d), _const_spec((1, d)), _const_spec(w_bf.shape), row(LANES), row(LANES), row(LANES)],
        out_specs=[row(nk)] + [row(2 * LANES)] * 5,
        out_shape=[jax.ShapeDtypeStruct((n, nk), F32)] + [jax.ShapeDtypeStruct((n, 2 * LANES), BF)] * 5,
        compiler_params=_cparams(1), name="kv_proj",
    )(x, g, w_bf, cos, sin, onehot)


def _q_kernel(x_ref, g_ref, wq_ref, wg_ref, cos_ref, sin_ref, qs_ref, qrs_ref, gt_ref):
    h = _rms(x_ref[...], g_ref[...]).astype(BF)
    cos, sin = cos_ref[...], sin_ref[...]
    scale = HEAD_DIM ** -0.5
    for c0 in range(0, wq_ref.shape[1], LANES):
        q = _dot(h, wq_ref[:, c0:c0 + LANES])
        qs_ref[:, c0:c0 + LANES] = (q * scale).astype(BF)
        qrs_ref[:, c0:c0 + LANES] = (_rope(q, cos, sin) * scale).astype(BF)
    gt_ref[...] = _sigmoid(_dot(h, wg_ref[...]))


def _q_proj(x, g, wq_bf, wg_bf, cos, sin):
    n, d = x.shape
    dq = wq_bf.shape[1]
    tm = min(512, n)
    row = lambda w: pl.BlockSpec((tm, w), lambda i: (i, 0))
    return pl.pallas_call(
        _q_kernel,
        grid=(n // tm,),
        in_specs=[row(d), _const_spec((1, d)), _const_spec(wq_bf.shape), _const_spec(wg_bf.shape),
                  row(LANES), row(LANES)],
        out_specs=[row(dq), row(dq), row(LANES)],
        out_shape=[jax.ShapeDtypeStruct((n, dq), BF), jax.ShapeDtypeStruct((n, dq), BF),
                   jax.ShapeDtypeStruct((n, LANES), F32)],
        compiler_params=_cparams(1), name="q_proj",
    )(x, g, wq_bf, wg_bf, cos, sin)


def _compress_kernel(xk_ref, xv_ref, pa_ref, pb_ref, w1a_ref, w1b_ref, b1_ref, w2_ref, kc_ref, vc_ref):
    nhb = kc_ref.shape[0]
    for j, x_ref, o_ref in ((0, xk_ref, kc_ref), (1, xv_ref, vc_ref)):
        acc_a = jnp.zeros((nhb, 2 * LANES), F32)
        acc_b = jnp.zeros((nhb, 2 * LANES), F32)
        for l in range(D_CMP):
            xl = x_ref[pl.ds(l, nhb, stride=D_CMP), :]
            acc_a = acc_a + _dot((xl + pa_ref[j, l:l + 1, :]).astype(BF), w1a_ref[j, l])
            acc_b = acc_b + _dot((xl + pb_ref[j, l:l + 1, :]).astype(BF), w1b_ref[j, l])
        hid = _gelu(acc_a + pltpu.roll(acc_b, nhb - 1, axis=0) + b1_ref[j])
        o_ref[...] = _dot(hid.astype(BF), w2_ref[j]).astype(BF)


def _compress(rows, params, b, t):
    nhb = t // D_CMP
    out = pl.BlockSpec((None, nhb, LANES), lambda bi: (bi, 0, 0))
    return pl.pallas_call(
        _compress_kernel,
        grid=(b,),
        in_specs=[pl.BlockSpec((None, t, LANES), lambda bi: (bi, 0, 0)),
                  pl.BlockSpec((None, t, LANES), lambda bi: (bi, 0, 1))] + [_const_spec(a.shape) for a in params],
        out_specs=[out, out],
        out_shape=[jax.ShapeDtypeStruct((b, nhb, LANES), BF)] * 2,
        compiler_params=_cparams(1), name="compress",
    )(rows, rows, *params)


def _compress_paged_kernel(pt_ref, *refs, per_step):
    del pt_ref
    page_refs = refs[:2 * per_step]
    params = refs[2 * per_step:-4]
    kc_ref, vc_ref, xk_scr, xv_scr = refs[-4:]
    step = pl.program_id(1)
    for r in range(per_step):
        rows = pl.ds(pl.multiple_of((step * per_step + r) * PAGE_SIZE, PAGE_SIZE), PAGE_SIZE)
        xk_scr[rows, :] = page_refs[2 * r][...]
        xv_scr[rows, :] = page_refs[2 * r + 1][...]

    @pl.when(step == pl.num_programs(1) - 1)
    def _():
        _compress_kernel(xk_scr, xv_scr, *params, kc_ref, vc_ref)


def _compress_paged(cache, page_table, params):
    db, n_pages = page_table.shape
    t = n_pages * PAGE_SIZE
    nhb = t // D_CMP
    per_step = min(16, n_pages)
    in_specs = []
    for r in range(per_step):
        for half in range(2):
            in_specs.append(pl.BlockSpec((None, PAGE_SIZE, LANES), functools.partial(
                lambda bi, i, pt, r, half: (pt[bi, i * per_step + r], 0, half), r=r, half=half)))
    in_specs += [_const_spec(a.shape) for a in params]
    out = pl.BlockSpec((None, nhb, LANES), lambda bi, i, pt: (bi, 0, 0))
    return pl.pallas_call(
        functools.partial(_compress_paged_kernel, per_step=per_step),
        grid_spec=pltpu.PrefetchScalarGridSpec(
            num_scalar_prefetch=1, grid=(db, n_pages // per_step), in_specs=in_specs, out_specs=[out, out],
            scratch_shapes=[pltpu.VMEM((t, LANES), F32), pltpu.VMEM((t, LANES), F32)]),
        out_shape=[jax.ShapeDtypeStruct((db, nhb, LANES), BF)] * 2,
        compiler_params=_cparams(2), name="compress_paged",
    )(page_table, *([cache] * (2 * per_step)), *params)


def _stack_heads(q_ref, g, tq, fill=0.0):
    lane = lax.broadcasted_iota(jnp.int32, (tq, LANES), 1)
    keep = (lane >= HEAD_DIM * g) & (lane < HEAD_DIM * (g + 1))
    parts = []
    for h in range(HPG):
        hh = g * HPG + h
        ch = hh // 2
        c = q_ref[:, ch * LANES:(ch + 1) * LANES].astype(F32)
        if hh % 2 != g:
            c = pltpu.roll(c, HEAD_DIM, axis=1)
        parts.append(jnp.where(keep, c, fill))
    return jnp.concatenate(parts, axis=0).astype(BF)


def _cmp_branch(q, kc, vc, tpos_col, n_cb, tq):
    ncp = kc.shape[0]
    cidx = lax.broadcasted_iota(jnp.int32, (tq, ncp), 1)
    valid = ((cidx * D_CMP + (L_CMP - 1)) <= tpos_col) & (cidx < n_cb)
    s = _dot_nt(q, kc).reshape(HPG, tq, ncp) + jnp.where(valid, 0.0, NEG)[None]
    e = jnp.exp(s - jnp.max(s, axis=-1, keepdims=True))
    any_valid = (tpos_col >= L_CMP - 1) & (n_cb > 0)
    inv = jnp.where(any_valid[None], 1.0 / jnp.sum(e, axis=-1, keepdims=True), 0.0)
    p = e * inv
    o = _dot(p.reshape(HPG * tq, ncp).astype(BF), vc)
    return o, jnp.sum(p, axis=0)


def _cmp_branch_staged(g, q, kc, vc, tpos_col, n_cb, s_ref, p_ref, oc_ref, tq):
    ncp = kc.shape[0]
    cidx = lax.broadcasted_iota(jnp.int32, (tq, ncp), 1)
    bias = _mask_bias(((cidx * D_CMP + (L_CMP - 1)) <= tpos_col) & (cidx < n_cb))
    any_valid = (tpos_col >= L_CMP - 1) & (n_cb > 0)
    s_ref[g, :, 0:ncp] = _dot_nt(q, kc)
    psum = jnp.zeros((tq, ncp), F32)
    for h in range(HPG):
        rows = slice(h * tq, (h + 1) * tq)
        m = jnp.max(s_ref[g, rows, 0:ncp] + bias, axis=-1, keepdims=True)
        e = jnp.exp(s_ref[g, rows, 0:ncp] + bias - m)
        p = e * jnp.where(any_valid, 1.0 / jnp.sum(e, axis=-1, keepdims=True), 0.0)
        psum = psum + p
        p_ref[g, rows, 0:ncp] = p.astype(BF)
    oc_ref[g] = _dot(p_ref[g, :, 0:ncp], vc)
    return psum


def _importance(psum, mt_ref, tpos_row):
    sp = mt_ref.shape[0]
    mt = mt_ref[...]
    p_hi = psum.astype(BF)
    r1 = psum - p_hi.astype(F32)
    p_mid = r1.astype(BF)
    p_lo = (r1 - p_mid.astype(F32)).astype(BF)
    imp = _dot_nt(mt, p_hi) + _dot_nt(mt, p_mid) + _dot_nt(mt, p_lo)
    s_idx = lax.broadcasted_iota(jnp.int32, (sp, LANES), 0)
    cur = jnp.right_shift(tpos_row, L_SLC.bit_length() - 1)
    forced = (s_idx == 0) | (s_idx == cur) | (s_idx == cur - 1)
    imp = jnp.where(forced, jnp.inf, imp)
    return jnp.where(s_idx * L_SLC <= tpos_row, imp, -jnp.inf)


def _select_blocks(psums, mt_ref, imp_ref, tpos_row, n_sb, tq):
    sp = mt_ref.shape[0]
    assert n_sb <= HEAD_DIM and sp == LANES, "one-hot block lanes hold at most 64 selection blocks"
    nr = _round_up(n_sb, SUBLANES)
    imps = []
    for g, psum in enumerate(psums):
        imp = _importance(psum, mt_ref, tpos_row)
        imp_ref[g] = imp
        imps.append(imp[:nr])
    s_idx = lax.broadcasted_iota(jnp.int32, (nr, LANES), 0)

    def body(i, cnts):
        tie = jnp.where(s_idx > i, 1.0, 0.0)
        out = []
        for g, imp in enumerate(imps):
            row = imp_ref[g, pl.ds(i, 1), :]
            out.append(cnts[g] + jnp.where(row > imp, 1.0, jnp.where(row == imp, tie, 0.0)))
        return tuple(out)

    cnts = lax.fori_loop(0, n_sb, body, tuple(jnp.zeros((nr, LANES), F32) for _ in psums))
    starts_before = s_idx * L_SLC <= tpos_row
    biases = []
    for g, cnt in enumerate(cnts):
        bias_t = jnp.where((cnt < float(N_SEL)) & starts_before, 0.0, NEG)
        if nr < sp:
            bias_t = jnp.concatenate([bias_t, jnp.full((sp - nr, LANES), NEG, F32)], axis=0)
        bias = bias_t.T[:tq]
        biases.append(pltpu.roll(bias, HEAD_DIM, axis=1) if g == 0 else bias)
    return biases


def _select_single(psum, mt_ref, tpos, tq):
    sp = mt_ref.shape[0]
    prow = jnp.broadcast_to(psum[0:1], (LANES, psum.shape[1]))
    imp = _importance(prow, mt_ref, jnp.full((1, LANES), tpos, jnp.int32))
    a = jnp.broadcast_to(imp[:, 0:1], (sp, sp))
    b = a.T
    r_i = lax.broadcasted_iota(jnp.int32, (sp, sp), 0)
    c_i = lax.broadcasted_iota(jnp.int32, (sp, sp), 1)
    beats = jnp.where(a > b, 1.0, jnp.where(a == b, jnp.where(r_i < c_i, 1.0, 0.0), 0.0))
    cnt = jnp.sum(beats, axis=0, keepdims=True)
    sel = jnp.where(cnt < float(N_SEL), 1.0, 0.0)
    return jnp.broadcast_to(sel, (tq, sp)).astype(BF)


def _flash_init(tq):
    return (jnp.full((HPG, tq, 1), NEG, F32), jnp.zeros((HPG, tq, 1), F32), jnp.zeros((HPG * tq, LANES), F32))


def _flash_tile(carry, q, k, v, bias, tq):
    m, l, acc = carry
    kt = k.shape[0]
    s = _dot_nt(q, k).reshape(HPG, tq, kt)
    if bias is not None:
        s = s + bias[None]
    m_new = jnp.maximum(m, jnp.max(s, axis=-1, keepdims=True))
    alpha = jnp.exp(m - m_new)
    p = jnp.exp(s - m_new)
    l = alpha * l + jnp.sum(p, axis=-1, keepdims=True)
    acc = alpha.reshape(HPG * tq, 1) * acc + _dot(p.reshape(HPG * tq, kt).astype(BF), v)
    return m_new, l, acc


def _flash_out(carry, tq):
    _, l, acc = carry
    return acc / l.reshape(HPG * tq, 1)


def _attend_once(q, k, v, bias, tq):
    kt = k.shape[0]
    s = _dot_nt(q, k).reshape(HPG, tq, kt) + bias[None]
    p = jnp.exp(s - jnp.max(s, axis=-1, keepdims=True))
    l = jnp.sum(p, axis=-1, keepdims=True)
    return _dot(p.reshape(HPG * tq, kt).astype(BF), v) / l.reshape(HPG * tq, 1)


def _mask_bias(valid):
    return jnp.where(valid, 0.0, NEG)


def _win_valid(kpos_row, tpos_col):
    dist = tpos_col - kpos_row
    return (dist >= 0) & (dist < WINDOW)


def _merge_heads(o_c, o_s, o_w, gt_ref, g, o_ref, tq):
    lane = lax.broadcasted_iota(jnp.int32, (tq, LANES), 1)
    for pair in range(HPG // 2):
        both = []
        for h in (2 * pair, 2 * pair + 1):
            col = (g * HPG + h) * 3
            rows = slice(h * tq, (h + 1) * tq)
            both.append(o_c(rows) * gt_ref[:, col:col + 1] + o_s(rows) * gt_ref[:, col + 1:col + 2]
                        + o_w(rows) * gt_ref[:, col + 2:col + 3])
        even, odd = both
        if g == 0:
            odd = pltpu.roll(odd, HEAD_DIM, axis=1)
        else:
            even = pltpu.roll(even, HEAD_DIM, axis=1)
        ch = g * (HPG // 2) + pair
        o_ref[:, ch * LANES:(ch + 1) * LANES] = jnp.where(lane < HEAD_DIM, even, odd).astype(o_ref.dtype)


def _staged_tile(state, stage, q, k, v, bias, refs, tq, first):
    s_ref, p_ref, m_ref, acc_ref = refs
    kt = k.shape[0]
    reps = kt // LANES
    s_ref[stage, :, 0:kt] = _dot_nt(q, k)
    for h in range(HPG):
        rows = slice(h * tq, (h + 1) * tq)

        def scores():
            s = s_ref[stage, rows, 0:kt]
            return s if bias is None else s + bias

        m_tile = jnp.max(scores(), axis=-1, keepdims=True)
        if first:
            m_new = jnp.broadcast_to(m_tile, (tq, LANES))
        else:
            m_old = m_ref[state, rows, :]
            m_new = jnp.maximum(m_old, m_tile)
            acc_ref[state, rows, :] = jnp.exp(m_old - m_new) * acc_ref[state, rows, :]
        m_ref[state, rows, :] = m_new
        p_ref[stage, rows, 0:kt] = jnp.exp(scores() - jnp.concatenate([m_new] * reps, axis=1)).astype(BF)
    pv = _dot(p_ref[stage, :, 0:kt], v)
    if first:
        acc_ref[state] = pv
    else:
        acc_ref[state] = acc_ref[state] + pv


def _attn_prompt_kernel(qs_ref, qrs_ref, gt_ref, kc_ref, vc_ref, ka_ref, sva_ref, win_ref, wva_ref, mt_ref, o_ref,
                        imp_ref, s_ref, p_ref, m_ref, acc_ref, oc_ref, *, n_cb, n_sb):
    tq = qs_ref.shape[0]
    t_all = ka_ref.shape[0]
    qb = pl.program_id(1)
    t0 = qb * tq
    tpos_col = t0 + lax.broadcasted_iota(jnp.int32, (tq, 1), 0)
    tpos_row = t0 + lax.broadcasted_iota(jnp.int32, (1, LANES), 1)
    groups = range(N_KV_HEADS)
    k_lanes, v_lanes = slice(0, LANES), slice(LANES, 2 * LANES)

    psums = [_cmp_branch_staged(g, _stack_heads(qs_ref, g, tq), kc_ref[...], vc_ref[...], tpos_col, n_cb,
                                s_ref, p_ref, oc_ref, tq) for g in groups]
    blk_bias = _select_blocks(psums, mt_ref, imp_ref, tpos_row, n_sb, tq)
    qr_blk = [_stack_heads(qrs_ref, g, tq, fill=blk_bias[g]) for g in groups]

    refs = (s_ref, p_ref, m_ref, acc_ref)
    g_lanes = [slice(g * LANES, (g + 1) * LANES) for g in groups]

    def slc_tile(j, bias, first):
        rows = pl.ds(pl.multiple_of(j * SLC_TILE, SLC_TILE), SLC_TILE)
        for g in groups:
            _staged_tile(g, g, qr_blk[g], ka_ref[rows, g_lanes[g]], sva_ref[rows, g_lanes[g]], bias, refs, tq, first)

    last = (t0 + tq - 1) // SLC_TILE
    causal = (last * SLC_TILE + lax.broadcasted_iota(jnp.int32, (1, SLC_TILE), 1)) <= tpos_col
    slc_tile(last, _mask_bias(causal), True)

    def slc_body(j, c):
        slc_tile(j, None, False)
        return c

    lax.fori_loop(0, last, slc_body, 0)

    wk = min(WINDOW + tq, t_all)
    start = pl.multiple_of(jnp.clip(t0 - WINDOW, 0, t_all - wk), tq)
    wbias = _mask_bias(_win_valid(start + lax.broadcasted_iota(jnp.int32, (1, wk), 1), tpos_col))
    wrows = pl.ds(start, wk)
    kw = win_ref[wrows, k_lanes]
    for g in groups:
        _staged_tile(N_KV_HEADS + g, g, _stack_heads(qrs_ref, g, tq), kw, wva_ref[wrows, g_lanes[g]], wbias, refs,
                     tq, True)

    def normalised(slot, rows):
        acc = acc_ref[slot, rows, :]
        return acc / pltpu.roll(acc, HEAD_DIM, axis=1)

    for g in groups:
        _merge_heads(lambda rows: oc_ref[g, rows, :], functools.partial(normalised, g),
                     functools.partial(normalised, N_KV_HEADS + g), gt_ref, g, o_ref, tq)


def _attn_prompt(qs, qrs, gates, kc, vc, ka_bf, slc_va, win_bf, win_va, mt, b, t, n_cb, n_sb):
    tq = Q_BLOCK
    nqb = t // tq
    d = qs.shape[1]
    stage_w = max(SLC_TILE, min(WINDOW + tq, t))
    row = lambda w: pl.BlockSpec((tq, w), lambda bi, qi: (bi * nqb + qi, 0))
    per_b = lambda a: pl.BlockSpec((None,) + a.shape[1:], lambda bi, qi: (bi, 0, 0))
    return pl.pallas_call(
        functools.partial(_attn_prompt_kernel, n_cb=n_cb, n_sb=n_sb),
        grid=(b, nqb),
        in_specs=[row(d), row(d), row(LANES), per_b(kc), per_b(vc), per_b(ka_bf), per_b(slc_va), per_b(win_bf),
                  per_b(win_va), _const_spec(mt.shape)],
        out_specs=row(d),
        out_shape=jax.ShapeDtypeStruct((b * t, d), BF),
        scratch_shapes=[pltpu.VMEM((N_KV_HEADS, mt.shape[0], LANES), F32),
                        pltpu.VMEM((N_KV_HEADS, HPG * tq, stage_w), F32),
                        pltpu.VMEM((N_KV_HEADS, HPG * tq, stage_w), BF),
                        pltpu.VMEM((2 * N_KV_HEADS, HPG * tq, LANES), F32),
                        pltpu.VMEM((2 * N_KV_HEADS, HPG * tq, LANES), F32),
                        pltpu.VMEM((N_KV_HEADS, HPG * tq, LANES), F32)],
        compiler_params=_cparams(2), name="attn_prompt",
    )(qs, qrs, gates, kc, vc, ka_bf, slc_va, win_bf, win_va, mt)


def _attn_sample_kernel(pt_ref, qs_ref, qrs_ref, gt_ref, kc_ref, vc_ref, slct_ref, win_ref, wint_ref, mt_ref, e_ref,
                        et_ref, *rest, n_cb, past):
    del pt_ref
    page_refs, o_ref, slc_ref = rest[:-2], rest[-2], rest[-1]
    for r, page in enumerate(page_refs):
        slc_ref[r * PAGE_SIZE:(r + 1) * PAGE_SIZE, :] = page[...].astype(BF)
    tq = qs_ref.shape[0]
    tpos_col = jnp.full((tq, 1), past, jnp.int32)
    w_buf = win_ref.shape[0]
    tail = slct_ref.shape[0]
    k_lanes, v_lanes = slice(0, LANES), slice(LANES, 2 * LANES)
    tail_pos = past + lax.broadcasted_iota(jnp.int32, (1, tail), 1)
    win_pos = past - w_buf + lax.broadcasted_iota(jnp.int32, (1, w_buf), 1)
    wbias = jnp.broadcast_to(_mask_bias(_win_valid(win_pos, tpos_col)), (tq, w_buf))
    wtbias = jnp.broadcast_to(_mask_bias(_win_valid(tail_pos, tpos_col)), (tq, tail))
    kw, vw = win_ref[:, k_lanes].astype(BF), win_ref[:, v_lanes].astype(BF)
    for g in range(N_KV_HEADS):
        qr = _stack_heads(qrs_ref, g, tq)
        o_c, psum = _cmp_branch(_stack_heads(qs_ref, g, tq), kc_ref[...], vc_ref[...], tpos_col, n_cb, tq)
        sel = _select_single(psum, mt_ref, past, tq)
        carry = _flash_tile(_flash_init(tq), qr, slc_ref[:, k_lanes], slc_ref[:, v_lanes],
                            _mask_bias(_dot(sel, e_ref[...]) > 0.5), tq)
        carry = _flash_tile(carry, qr, slct_ref[:, k_lanes], slct_ref[:, v_lanes],
                            _mask_bias((_dot(sel, et_ref[...]) > 0.5) & (tail_pos <= tpos_col)), tq)
        o_s = _flash_out(carry, tq)
        carry = _flash_tile(_flash_init(tq), qr, kw, vw, wbias, tq)
        carry = _flash_tile(carry, qr, wint_ref[:, k_lanes], wint_ref[:, v_lanes], wtbias, tq)
        o_w = _flash_out(carry, tq)
        _merge_heads(lambda rows: o_c[rows], lambda rows: o_s[rows], lambda rows: o_w[rows], gt_ref, g, o_ref, tq)


def _attn_sample(qs, qrs, gates, kc, vc, slc_cache, page_table, slc_tail, win_c, win_tail, mt, e_all, et, n_cb, past):
    db, n_pages = page_table.shape
    per_b = lambda a: pl.BlockSpec((None,) + a.shape[1:], lambda bi, pt: (bi, 0, 0))
    args = (qs, qrs, gates, kc, vc, slc_tail, win_c, win_tail)
    pages = [pl.BlockSpec((None, PAGE_SIZE, slc_cache.shape[2]), functools.partial(
        lambda bi, pt, r: (pt[bi, r], 0, 0), r=r)) for r in range(n_pages)]
    return pl.pallas_call(
        functools.partial(_attn_sample_kernel, n_cb=n_cb, past=past),
        grid_spec=pltpu.PrefetchScalarGridSpec(
            num_scalar_prefetch=1, grid=(db,),
            in_specs=[per_b(a) for a in args] + [_const_spec(mt.shape), _const_spec(e_all.shape),
                                                 _const_spec(et.shape)] + pages,
            out_specs=per_b(qs),
            scratch_shapes=[pltpu.VMEM((n_pages * PAGE_SIZE, slc_cache.shape[2]), BF)]),
        out_shape=jax.ShapeDtypeStruct(qs.shape, BF),
        compiler_params=_cparams(1), name="attn_sample",
    )(page_table, *args, mt, e_all, et, *([slc_cache] * n_pages))


def _block_diag(blocks):
    n, k, j = blocks.shape
    eye = jnp.eye(n, dtype=blocks.dtype)
    return (eye[:, None, :, None] * blocks[:, :, None, :]).reshape(n * k, n * j)


def _pad_rows8(rows):
    return jnp.pad(rows, ((0, 0), (SUBLANES - rows.shape[1], 0), (0, 0)))


def _rope_tables(pos):
    half = HEAD_DIM // 2
    inv_freq = ROPE_THETA ** (-jnp.arange(half, dtype=F32) / half)
    ang = pos.astype(F32)[:, None] * inv_freq[None, :]
    cos, sin = jnp.cos(ang), jnp.sin(ang)
    reps = LANES // HEAD_DIM
    return (jnp.tile(jnp.concatenate([cos, cos], axis=1), (1, reps)),
            jnp.tile(jnp.concatenate([-sin, sin], axis=1), (1, reps)))


def _block_onehot(pos):
    lane = jnp.arange(LANES, dtype=jnp.int32)[None, :] % HEAD_DIM
    return (lane == (pos[:, None] // L_SLC)).astype(F32)


def _overlap_t(n_cb, n_sb, ncp, sp):
    c0 = np.arange(ncp)[None, :] * D_CMP
    s0 = np.arange(sp)[:, None] * L_SLC
    m = (c0 < s0 + L_SLC) & (c0 + L_CMP > s0) & (np.arange(ncp)[None, :] < n_cb) & (np.arange(sp)[:, None] < n_sb)
    return jnp.asarray(m, dtype=BF)


def _expand_tiles(n_tiles, sp, tile, first_key=0):
    key = first_key + np.arange(n_tiles)[:, None, None] * tile + np.arange(tile)[None, None, :]
    return jnp.asarray(np.arange(sp)[None, :, None] == key // L_SLC, dtype=BF)


def _round_up(x, m):
    return (x + m - 1) // m * m


def kernel(x_prompt, x_sample, p_prompt, p_sample, cache_cmp_kv, cache_slc_kv, cache_win_kv, state_rg_conv,
           state_rg_h, state_ffn_conv, page_table, g_mix, g_ffn, g_ple, g_final, rg_w_in, rg_conv_w, rg_conv_b,
           rg_w_a, rg_b_a, rg_w_x, rg_b_x, rg_lambda, rg_w_out, g_kv, w_kv, cmp_pos, cmp_w1, cmp_b1, cmp_w2,
           attn_w_qg, attn_w_o, ffn_w_up, ffn_conv_w, ffn_conv_b, ffn_w_down, ple_w_in, ple_w_gate):
    b, t, d = x_prompt.shape
    db = x_sample.shape[0]
    assert x_sample.shape[1] == 1, "the sample path handles one new token per sequence"
    depth = g_mix.shape[0]
    n_a = rg_w_in.shape[0]
    d_rnn = rg_w_out.shape[1]
    n_pages = page_table.shape[1]
    past = n_pages * PAGE_SIZE
    kvc = 2 * N_KV_HEADS * HEAD_DIM
    n_q = N_KV_HEADS * HPG * HEAD_DIM

    row1 = lambda v: v.reshape(1, -1)
    pad8 = lambda w: jnp.pad(w, ((0, SUBLANES - w.shape[0]), (0, 0)))

    rg_in_bf = rg_w_in.astype(BF)
    rg_out_bf = rg_w_out.astype(BF)
    rg_gate_bf = [jnp.concatenate([_block_diag(rg_w_a[i]), _block_diag(rg_w_x[i])], axis=1).astype(BF)
                  for i in range(n_a)]
    rg_gate_b = [row1(jnp.concatenate([rg_b_a[i].reshape(-1), rg_b_x[i].reshape(-1)])) for i in range(n_a)]
    rg_c = [row1(-RG_C * jax.nn.softplus(-rg_lambda[i])) for i in range(n_a)]
    up_bf = ffn_w_up.astype(BF)
    down_bf = ffn_w_down.astype(BF)
    gate_bf = ple_w_gate.astype(BF)
    pin_bf = ple_w_in.astype(BF)
    wkv_bf = w_kv.astype(BF)
    wq_bf = attn_w_qg[:, :, :n_q].astype(BF)
    wgl_bf = jnp.pad(attn_w_qg[:, :, n_q:], ((0, 0), (0, 0), (0, LANES - 3 * N_KV_HEADS * HPG))).astype(BF)
    wo_bf = attn_w_o.astype(BF)
    gfin = row1(g_final)

    w1 = cmp_w1.reshape(2, 2, D_CMP, HEAD_DIM, -1)
    eye_g = jnp.eye(N_KV_HEADS, dtype=F32)
    w1_bd = (eye_g[None, None, None, :, None, :, None] * w1[:, :, :, None, :, None, :]).reshape(
        2, 2, D_CMP, N_KV_HEADS * HEAD_DIM, -1).astype(BF)
    pos_t = jnp.tile(jnp.transpose(cmp_pos, (1, 0, 2)), (1, 1, N_KV_HEADS)).reshape(2, 2, D_CMP, LANES)
    cmp_params = (pos_t[:, 0], pos_t[:, 1], w1_bd[:, 0], w1_bd[:, 1],
                  jnp.tile(cmp_b1, (1, N_KV_HEADS)).reshape(2, 1, -1),
                  jnp.stack([_block_diag(jnp.stack([cmp_w2[j]] * N_KV_HEADS)) for j in range(2)]).astype(BF))

    def layer_consts(i, final):
        return (row1(g_ffn[i]), up_bf[i], pad8(ffn_conv_w[i]), row1(ffn_conv_b[i]), down_bf[i], row1(g_ple[i]),
                gate_bf[i], pin_bf[i], gfin), final

    n = b * t
    x = x_prompt.reshape(n, d)
    p = p_prompt.reshape(depth, n, -1)
    pos_p = jnp.tile(jnp.arange(t, dtype=jnp.int32), b)
    cos_p, sin_p = _rope_tables(pos_p)
    rg_conv_p, rg_h_p, ffn_conv_p = [], [], []
    zeros_rg = jnp.zeros((b, SUBLANES, d_rnn), F32)
    zeros_ffn = jnp.zeros((b, SUBLANES, ffn_w_up.shape[2]), F32)
    for i in range(depth):
        if i == n_a:
            kv_p, _, win_bf, ka_bf, slc_va, win_va = _kv_proj(x, row1(g_kv), wkv_bf, cos_p, sin_p,
                                                              _block_onehot(pos_p))
            kc_p, vc_p = _compress(kv_p.reshape(b, t, -1), cmp_params, b, t)
            n_cb_p = (t - L_CMP) // D_CMP + 1
            n_sb_p = t // L_SLC
            mt_p = _overlap_t(n_cb_p, n_sb_p, t // D_CMP, _round_up(n_sb_p, LANES))
            ka_bf, slc_va, win_bf, win_va = (a.reshape(b, t, -1) for a in (ka_bf, slc_va, win_bf, win_va))
        if i < n_a:
            yg, xr = _rg_in(x, row1(g_mix[i]), rg_in_bf[i])
            gated, h_last = _rg_core(xr, yg, zeros_rg, zeros_rg, pad8(rg_conv_w[i]), row1(rg_conv_b[i]),
                                     rg_gate_bf[i], rg_gate_b[i], rg_c[i], b, t)
            x = _matmul_res(gated, rg_out_bf[i], x)
            rg_conv_p.append(xr.reshape(b, t, -1)[:, t - 3:])
            rg_h_p.append(h_last[:, 0])
        else:
            j = i - n_a
            qs, qrs, gates = _q_proj(x, row1(g_mix[i]), wq_bf[j], wgl_bf[j], cos_p, sin_p)
            o = _attn_prompt(qs, qrs, gates, kc_p, vc_p, ka_bf, slc_va, win_bf, win_va, mt_p, b, t, n_cb_p, n_sb_p)
            x = _matmul_res(o, wo_bf[j], x)
        consts, final = layer_consts(i, i == depth - 1)
        x, f_hist = _ffn(x, p, i, zeros_ffn, *consts, b, t, final)
        ffn_conv_p.append(f_hist[:, SUBLANES - 2:])
    y_prompt = x.reshape(b, t, d)
    kv5 = kv_p.reshape(b, t, 6, N_KV_HEADS, HEAD_DIM)
    keep = min(WINDOW, t)
    cmp_p, slc_p, win_p = kv5[:, :, 0:2], kv5[:, :, 2:4], kv5[:, t - keep:, 4:6]

    x = x_sample.reshape(db, d)
    p = p_sample.reshape(depth, db, -1)
    cos_s, sin_s = _rope_tables(jnp.full((db,), past, jnp.int32))
    rg_conv_s, rg_h_s, ffn_conv_s = [], [], []
    for i in range(depth):
        if i == n_a:
            pos_s = jnp.full((db,), past, jnp.int32)
            kv_s, slc_new_bf, win_new_bf = _kv_proj(x, row1(g_kv), wkv_bf, cos_s, sin_s, _block_onehot(pos_s))[:3]
            kc_s, vc_s = _compress_paged(cache_cmp_kv.reshape(-1, PAGE_SIZE, kvc), page_table, cmp_params)
            slc_cache = cache_slc_kv.reshape(-1, PAGE_SIZE, kvc)
            n_cb_s = (past + 1 - L_CMP) // D_CMP + 1
            n_sb_s = -(-(past + 1) // L_SLC)
            sp_s = _round_up(n_sb_s, LANES)
            mt_s = _overlap_t(n_cb_s, n_sb_s, past // D_CMP, sp_s)
            e_s = _expand_tiles(1, sp_s, past)[0]
            et_s = _expand_tiles(1, sp_s, LANES, first_key=past)[0]
            tail = lambda a: jnp.pad(a.reshape(db, 1, -1), ((0, 0), (0, LANES - 1), (0, 0)))
            slc_tail, win_tail = tail(slc_new_bf), tail(win_new_bf)
            win_c = cache_win_kv.reshape(db, cache_win_kv.shape[1], kvc)
        if i < n_a:
            yg, xr = _rg_in(x, row1(g_mix[i]), rg_in_bf[i])
            hist = state_rg_conv[i]
            gated, h_new = _rg_step(xr, yg, hist[:, 0], hist[:, 1], hist[:, 2], state_rg_h[i], pad8(rg_conv_w[i]),
                                    row1(rg_conv_b[i]), rg_gate_bf[i], rg_gate_b[i], rg_c[i])
            x = _matmul_res(gated, rg_out_bf[i], x)
            rg_conv_s.append(jnp.concatenate([hist[:, 1:], xr[:, None, :]], axis=1))
            rg_h_s.append(h_new)
        else:
            j = i - n_a
            qs, qrs, gates = _q_proj(x, row1(g_mix[i]), wq_bf[j], wgl_bf[j], cos_s, sin_s)
            rep = lambda a: jnp.broadcast_to(a[:, None, :], (db, SUBLANES, a.shape[1]))
            o = _attn_sample(rep(qs), rep(qrs), rep(gates), kc_s, vc_s, slc_cache, page_table, slc_tail, win_c,
                             win_tail, mt_s, e_s, et_s, n_cb_s, past)
            x = _matmul_res(o[:, 0], wo_bf[j], x)
        consts, final = layer_consts(i, i == depth - 1)
        fh = state_ffn_conv[i]
        x, up = _ffn_step(x, p[i], fh[:, 0], fh[:, 1], *consts, final)
        ffn_conv_s.append(jnp.stack([fh[:, 1], up], axis=1))
    y_sample = x.reshape(db, 1, d)
    kv5 = kv_s.reshape(db, 1, 6, N_KV_HEADS, HEAD_DIM)
    cmp_s, slc_s, win_new = kv5[:, :, 0:2], kv5[:, :, 2:4], kv5[:, :, 4:6]
    win_full = jnp.concatenate([cache_win_kv, win_new], axis=1)
    keep = min(WINDOW, past + 1)
    win_s = win_full[:, win_full.shape[1] - keep:]

    return (y_prompt, y_sample, cmp_p, cmp_s, slc_p, slc_s, win_p, win_s,
            jnp.stack(rg_conv_p), jnp.stack(rg_conv_s), jnp.stack(rg_h_p), jnp.stack(rg_h_s),
            jnp.stack(ffn_conv_p), jnp.stack(ffn_conv_s))
```

```python
import functools

import jax
import jax.numpy as jnp
import numpy as np
from jax import lax
from jax.experimental import pallas as pl
from jax.experimental.pallas import tpu as pltpu

BF = jnp.bfloat16
F32 = jnp.float32

EPS = 1e-6
RG_C = 8.0
N_RG_BLOCKS = 16
HEAD_DIM = 64
N_KV_HEADS = 2
HPG = 8
L_CMP = 32
D_CMP = 16
L_SLC = 64
N_SEL = 16
WINDOW = 512
Q_BLOCK = 128
ROPE_THETA = 10000.0
PAGE_SIZE = 128
NEG = -1e30
LANES = 128
SUBLANES = 8
SLC_TILE = 512
WIN_TILE = 128
VMEM_LIMIT = 56 * 1024 * 1024


def _cparams(n_axes):
    return pltpu.CompilerParams(dimension_semantics=("arbitrary",) * n_axes,
                                vmem_limit_bytes=VMEM_LIMIT)


def _const_spec(shape):
    zeros = (0,) * len(shape)
    return pl.BlockSpec(shape, lambda *_: zeros, pipeline_mode=pl.Buffered(1))


def _dot(a, b):
    return jnp.dot(a, b, preferred_element_type=F32)


def _dot_nt(a, b):
    return lax.dot_general(a, b, (((1,), (1,)), ((), ())), preferred_element_type=F32)


def _rms(x, g):
    return x * lax.rsqrt(jnp.mean(x * x, axis=-1, keepdims=True) + EPS) * g


def _gelu(x):
    return jax.nn.gelu(x, approximate=True)


def _sigmoid(x):
    return jax.nn.sigmoid(x)


def _shift_rows(x, prev8, j):
    if j == 0:
        return x
    r = pltpu.roll(x, j, axis=0)
    p = pltpu.roll(prev8, j, axis=0)
    row = lax.broadcasted_iota(jnp.int32, (SUBLANES, x.shape[1]), 0)
    head = jnp.where(row < j, p, r[:SUBLANES])
    if x.shape[0] == SUBLANES:
        return head
    return jnp.concatenate([head, r[SUBLANES:]], axis=0)


def _rope(z, cos, sin_signed):
    lane = lax.broadcasted_iota(jnp.int32, z.shape, 1)
    first = (lane & (HEAD_DIM - 1)) < (HEAD_DIM // 2)
    rot = jnp.where(first, pltpu.roll(z, LANES - HEAD_DIM // 2, axis=1), pltpu.roll(z, HEAD_DIM // 2, axis=1))
    return z * cos + rot * sin_signed


def _rg_in_kernel(x_ref, g_ref, w_ref, yg_ref, xr_ref, *, chunk):
    h = _rms(x_ref[...], g_ref[...]).astype(BF)
    n = yg_ref.shape[1]
    for c0 in range(0, n, chunk):
        yg_ref[:, c0:c0 + chunk] = _gelu(_dot(h, w_ref[:, c0:c0 + chunk]))
        xr_ref[:, c0:c0 + chunk] = _dot(h, w_ref[:, n + c0:n + c0 + chunk])


def _rg_in(x, g, w_bf):
    n, d = x.shape
    dr = w_bf.shape[1] // 2
    tm = min(512, n)
    return pl.pallas_call(
        functools.partial(_rg_in_kernel, chunk=256),
        grid=(n // tm,),
        in_specs=[pl.BlockSpec((tm, d), lambda i: (i, 0)), _const_spec((1, d)), _const_spec(w_bf.shape)],
        out_specs=[pl.BlockSpec((tm, dr), lambda i: (i, 0)), pl.BlockSpec((tm, dr), lambda i: (i, 0))],
        out_shape=[jax.ShapeDtypeStruct((n, dr), F32), jax.ShapeDtypeStruct((n, dr), F32)],
        compiler_params=_cparams(1), name="rg_in",
    )(x, g, w_bf)


def _rg_gates(xc, wg_ref, bg_ref, c_ref):
    d = xc.shape[1]
    gates = _dot(xc.astype(BF), wg_ref[...]) + bg_ref[...]
    r = _sigmoid(gates[:, :d])
    i = _sigmoid(gates[:, d:])
    log_a = c_ref[...] * r
    a = jnp.exp(log_a)
    z = -jnp.tanh(log_a) * (a * a + 1.0)
    mult = jnp.where(z > 0.0, z * lax.rsqrt(z), 0.0)
    return a, mult, i * xc


def _rg_mixer_kernel(xin_ref, gm_ref, win_ref, hist_ref, h0_ref, cw_ref, cb_ref, wg_ref, bg_ref, c_ref, wout_ref,
                     out_ref, hl_ref, xl_ref, xprev, hc, *, sub):
    t = pl.program_id(1)

    @pl.when(t == 0)
    def _():
        xprev[...] = hist_ref[...]
        hc[...] = h0_ref[...]

    d = wout_ref.shape[0]
    prev = xprev[...]
    carry = hc[0:1, :]
    row = lax.broadcasted_iota(jnp.int32, (sub, d), 0)
    pieces = range(0, xin_ref.shape[0], sub)
    projs = []
    for r0 in pieces:
        hn = _rms(xin_ref[r0:r0 + sub, :], gm_ref[...]).astype(BF)
        projs.append((_dot(hn, win_ref[:, 0:d]), _dot(hn, win_ref[:, d:2 * d])))
    for (yg_pre, x), r0 in zip(projs, pieces):
        x_in = xin_ref[r0:r0 + sub, :]
        yg = _gelu(yg_pre)
        xc = cb_ref[...] + _shift_rows(x, prev, 3) * cw_ref[0:1, :]
        xc = xc + _shift_rows(x, prev, 2) * cw_ref[1:2, :]
        xc = xc + _shift_rows(x, prev, 1) * cw_ref[2:3, :]
        xc = xc + x * cw_ref[3:4, :]
        prev = x[sub - SUBLANES:, :]

        a, mult, ixc = _rg_gates(xc, wg_ref, bg_ref, c_ref)
        if r0 == 0:
            mult = jnp.where((row == 0) & (t == 0), 1.0, mult)
        u = mult * ixc

        av, bv = a, u
        dist = 1
        while dist < sub:
            if dist < SUBLANES:
                ok = row >= dist
                bv = jnp.where(ok, av * pltpu.roll(bv, dist, axis=0) + bv, bv)
                av = jnp.where(ok, av * pltpu.roll(av, dist, axis=0), av)
            else:
                bv = jnp.concatenate([bv[:dist], av[dist:] * bv[:sub - dist] + bv[dist:]], axis=0)
                av = jnp.concatenate([av[:dist], av[dist:] * av[:sub - dist]], axis=0)
            dist *= 2
        h = av * carry + bv
        carry = h[sub - 1:sub, :]
        out_ref[r0:r0 + sub, :] = x_in + _dot((yg * h).astype(BF), wout_ref[...])
    xprev[...] = prev
    hc[...] = jnp.broadcast_to(carry, (SUBLANES, d))
    hl_ref[...] = hc[...]
    xl_ref[...] = prev


def _rg_mixer(x, gm, win_bf, hist8, h08, cw8, cb, wg_bf, bg, c, wout_bf, b, t):
    n, dm = x.shape
    d = wout_bf.shape[0]
    tt = min(512, t)
    nt = t // tt
    row_spec = pl.BlockSpec((tt, dm), lambda bi, ti: (bi * nt + ti, 0))
    st_spec = pl.BlockSpec((None, SUBLANES, d), lambda bi, ti: (bi, 0, 0))
    consts = (cw8, cb, wg_bf, bg, c, wout_bf)
    return pl.pallas_call(
        functools.partial(_rg_mixer_kernel, sub=min(256, tt)),
        grid=(b, nt),
        in_specs=[row_spec, _const_spec(gm.shape), _const_spec(win_bf.shape), st_spec, st_spec]
        + [_const_spec(a.shape) for a in consts],
        out_specs=[row_spec, st_spec, st_spec],
        out_shape=[jax.ShapeDtypeStruct((n, dm), F32), jax.ShapeDtypeStruct((b, SUBLANES, d), F32),
                   jax.ShapeDtypeStruct((b, SUBLANES, d), F32)],
        scratch_shapes=[pltpu.VMEM((SUBLANES, d), F32), pltpu.VMEM((SUBLANES, d), F32)],
        compiler_params=_cparams(2), name="rg_mixer",
    )(x, gm, win_bf, hist8, h08, *consts)


def _rg_step_kernel(xr_ref, yg_ref, h0_ref, h1_ref, h2_ref, hs_ref, cw_ref, cb_ref, wg_ref, bg_ref, c_ref,
                    out_ref, hn_ref):
    x = xr_ref[...]
    xc = cb_ref[...] + h0_ref[...] * cw_ref[0:1, :]
    xc = xc + h1_ref[...] * cw_ref[1:2, :]
    xc = xc + h2_ref[...] * cw_ref[2:3, :]
    xc = xc + x * cw_ref[3:4, :]
    a, mult, ixc = _rg_gates(xc, wg_ref, bg_ref, c_ref)
    h = mult * ixc + a * hs_ref[...]
    hn_ref[...] = h
    out_ref[...] = (yg_ref[...] * h).astype(BF)


def _rg_step(xr, yg, h0, h1, h2, hs, cw8, cb, wg_bf, bg, c):
    n, d = xr.shape
    args = (xr, yg, h0, h1, h2, hs, cw8, cb, wg_bf, bg, c)
    return pl.pallas_call(
        _rg_step_kernel,
        grid=(1,),
        in_specs=[_const_spec(a.shape) for a in args],
        out_specs=[pl.BlockSpec((n, d), lambda i: (0, 0)), pl.BlockSpec((n, d), lambda i: (0, 0))],
        out_shape=[jax.ShapeDtypeStruct((n, d), BF), jax.ShapeDtypeStruct((n, d), F32)],
        compiler_params=_cparams(1), name="rg_step",
    )(*args)


def _matmul_res_kernel(a_ref, w_ref, x_ref, o_ref):
    o_ref[...] = x_ref[...] + _dot(a_ref[...], w_ref[...])


def _matmul_res(a_bf, w_bf, x):
    n, k = a_bf.shape
    d = w_bf.shape[1]
    tm = min(512, n)
    return pl.pallas_call(
        _matmul_res_kernel,
        grid=(n // tm,),
        in_specs=[pl.BlockSpec((tm, k), lambda i: (i, 0)), _const_spec(w_bf.shape),
                  pl.BlockSpec((tm, d), lambda i: (i, 0))],
        out_specs=pl.BlockSpec((tm, d), lambda i: (i, 0)),
        out_shape=jax.ShapeDtypeStruct((n, d), F32),
        compiler_params=_cparams(1), name="matmul_res",
    )(a_bf, w_bf, x)


def _ffn_tail(x, acc, p_ref, gp_ref, wgate_ref, wpin_ref, gfin_ref, final_norm):
    x1 = x + acc
    hp = _rms(x1, gp_ref[...]).astype(BF)
    sig = _sigmoid(_dot(hp, wgate_ref[...]))
    pe = _dot(p_ref[...].astype(BF), wpin_ref[...])
    x2 = x1 + pe * sig
    if final_norm:
        x2 = _rms(x2, gfin_ref[...])
    return x2


def _ffn_kernel(x_ref, p_ref, hist_ref, gf_ref, wup_ref, cw_ref, cb_ref, wdn_ref, gp_ref, wgate_ref, wpin_ref,
                gfin_ref, o_ref, ho_ref, uprev, *, chunk, final_norm):
    t = pl.program_id(1)

    @pl.when(t == 0)
    def _():
        uprev[...] = hist_ref[...]

    x = x_ref[...]
    tm = x.shape[0]
    dff = wdn_ref.shape[0]
    h = _rms(x, gf_ref[...]).astype(BF)
    acc = jnp.zeros(x.shape, F32)
    for c0 in range(0, dff, chunk):
        halves = []
        for off in (c0, dff + c0):
            up = _dot(h, wup_ref[:, off:off + chunk])
            prev = uprev[:, off:off + chunk]
            uc = cb_ref[:, off:off + chunk] + _shift_rows(up, prev, 2) * cw_ref[0:1, off:off + chunk]
            uc = uc + _shift_rows(up, prev, 1) * cw_ref[1:2, off:off + chunk]
            uc = uc + up * cw_ref[2:3, off:off + chunk]
            uprev[:, off:off + chunk] = up[tm - SUBLANES:, :]
            halves.append(uc)
        act = (_gelu(halves[0]) * halves[1]).astype(BF)
        acc = acc + _dot(act, wdn_ref[c0:c0 + chunk, :])
    ho_ref[...] = uprev[...]
    o_ref[...] = _ffn_tail(x, acc, p_ref, gp_ref, wgate_ref, wpin_ref, gfin_ref, final_norm)


def _ffn(x, p_all, layer, hist8, gf, wup_bf, cw8, cb, wdn_bf, gp, wgate_bf, wpin_bf, gfin, b, t, final_norm):
    n, d = x.shape
    dp = p_all.shape[2]
    d2 = wup_bf.shape[1]
    tm = min(1024, t)
    nt = t // tm
    row = lambda w: pl.BlockSpec((tm, w), lambda bi, ti: (bi * nt + ti, 0))
    p_spec = pl.BlockSpec((None, tm, dp), lambda bi, ti: (layer, bi * nt + ti, 0))
    st_spec = pl.BlockSpec((None, SUBLANES, d2), lambda bi, ti: (bi, 0, 0))
    consts = (gf, wup_bf, cw8, cb, wdn_bf, gp, wgate_bf, wpin_bf, gfin)
    return pl.pallas_call(
        functools.partial(_ffn_kernel, chunk=512, final_norm=final_norm),
        grid=(b, nt),
        in_specs=[row(d), p_spec, st_spec] + [_const_spec(a.shape) for a in consts],
        out_specs=[row(d), st_spec],
        out_shape=[jax.ShapeDtypeStruct((n, d), F32), jax.ShapeDtypeStruct((b, SUBLANES, d2), F32)],
        scratch_shapes=[pltpu.VMEM((SUBLANES, d2), F32)],
        compiler_params=_cparams(2), name="ffn",
    )(x, p_all, hist8, *consts)


def _ffn_step_kernel(x_ref, p_ref, h0_ref, h1_ref, gf_ref, wup_ref, cw_ref, cb_ref, wdn_ref, gp_ref, wgate_ref,
                     wpin_ref, gfin_ref, o_ref, up_ref, *, chunk, final_norm):
    x = x_ref[...]
    dff = wdn_ref.shape[0]
    h = _rms(x, gf_ref[...]).astype(BF)
    acc = jnp.zeros(x.shape, F32)
    for c0 in range(0, dff, chunk):
        halves = []
        for off in (c0, dff + c0):
            up = _dot(h, wup_ref[:, off:off + chunk])
            up_ref[:, off:off + chunk] = up
            uc = cb_ref[:, off:off + chunk] + h0_ref[:, off:off + chunk] * cw_ref[0:1, off:off + chunk]
            uc = uc + h1_ref[:, off:off + chunk] * cw_ref[1:2, off:off + chunk]
            uc = uc + up * cw_ref[2:3, off:off + chunk]
            halves.append(uc)
        act = (_gelu(halves[0]) * halves[1]).astype(BF)
        acc = acc + _dot(act, wdn_ref[c0:c0 + chunk, :])
    o_ref[...] = _ffn_tail(x, acc, p_ref, gp_ref, wgate_ref, wpin_ref, gfin_ref, final_norm)


def _ffn_step(x, p, h0, h1, gf, wup_bf, cw8, cb, wdn_bf, gp, wgate_bf, wpin_bf, gfin, final_norm):
    n, d = x.shape
    d2 = wup_bf.shape[1]
    args = (x, p, h0, h1, gf, wup_bf, cw8, cb, wdn_bf, gp, wgate_bf, wpin_bf, gfin)
    return pl.pallas_call(
        functools.partial(_ffn_step_kernel, chunk=512, final_norm=final_norm),
        grid=(1,),
        in_specs=[_const_spec(a.shape) for a in args],
        out_specs=[pl.BlockSpec((n, d), lambda i: (0, 0)), pl.BlockSpec((n, d2), lambda i: (0, 0))],
        out_shape=[jax.ShapeDtypeStruct((n, d), F32), jax.ShapeDtypeStruct((n, d2), F32)],
        compiler_params=_cparams(1), name="ffn_step",
    )(*args)


def _kv_kernel(x_ref, g_ref, w_ref, cos_ref, sin_ref, oh_ref, kv_ref, slc_ref, win_ref, ka_ref, sva_ref, wva_ref):
    h = _rms(x_ref[...], g_ref[...]).astype(BF)
    kv = _dot(h, w_ref[...])
    cos, sin = cos_ref[...], sin_ref[...]
    w2 = 2 * LANES
    k_slc = _rope(kv[:, w2:w2 + LANES], cos, sin)
    k_win = _rope(kv[:, 2 * w2:2 * w2 + LANES], cos, sin)
    kv_ref[:, 0:w2] = kv[:, 0:w2]
    kv_ref[:, w2:w2 + LANES] = k_slc
    kv_ref[:, w2 + LANES:2 * w2] = kv[:, w2 + LANES:2 * w2]
    kv_ref[:, 2 * w2:2 * w2 + LANES] = k_win
    kv_ref[:, 2 * w2 + LANES:3 * w2] = kv[:, 2 * w2 + LANES:3 * w2]
    slc_ref[:, 0:LANES] = k_slc.astype(BF)
    slc_ref[:, LANES:w2] = kv[:, w2 + LANES:2 * w2].astype(BF)
    win_ref[:, 0:LANES] = k_win.astype(BF)
    win_ref[:, LANES:w2] = kv[:, 2 * w2 + LANES:3 * w2].astype(BF)
    low = lax.broadcasted_iota(jnp.int32, k_slc.shape, 1) < HEAD_DIM
    oh = oh_ref[...]
    ka_ref[:, 0:LANES] = jnp.where(low, k_slc, oh).astype(BF)
    ka_ref[:, LANES:w2] = jnp.where(low, oh, k_slc).astype(BF)
    for va_ref, v in ((sva_ref, kv[:, w2 + LANES:2 * w2]), (wva_ref, kv[:, 2 * w2 + LANES:3 * w2])):
        va_ref[:, 0:LANES] = jnp.where(low, v, 1.0).astype(BF)
        va_ref[:, LANES:w2] = jnp.where(low, 1.0, v).astype(BF)


def _kv_proj(x, g, w_bf, cos, sin, onehot):
    n, d = x.shape
    nk = w_bf.shape[1]
    tm = min(512, n)
    row = lambda w: pl.BlockSpec((tm, w), lambda i: (i, 0))
    return pl.pallas_call(
        _kv_kernel,
        grid=(n // tm,),
        in_specs=[row(d), _const_spec((1, d)), _const_spec(w_bf.shape), row(LANES), row(LANES), row(LANES)],
        out_specs=[row(nk)] + [row(2 * LANES)] * 5,
        out_shape=[jax.ShapeDtypeStruct((n, nk), F32)] + [jax.ShapeDtypeStruct((n, 2 * LANES), BF)] * 5,
        compiler_params=_cparams(1), name="kv_proj",
    )(x, g, w_bf, cos, sin, onehot)


def _q_kernel(x_ref, g_ref, wq_ref, wg_ref, cos_ref, sin_ref, qs_ref, qrs_ref, gt_ref):
    h = _rms(x_ref[...], g_ref[...]).astype(BF)
    cos, sin = cos_ref[...], sin_ref[...]
    scale = HEAD_DIM ** -0.5
    for c0 in range(0, wq_ref.shape[1], LANES):
        q = _dot(h, wq_ref[:, c0:c0 + LANES])
        qs_ref[:, c0:c0 + LANES] = (q * scale).astype(BF)
        qrs_ref[:, c0:c0 + LANES] = (_rope(q, cos, sin) * scale).astype(BF)
    gt_ref[...] = _sigmoid(_dot(h, wg_ref[...]))


def _q_proj(x, g, wq_bf, wg_bf, cos, sin):
    n, d = x.shape
    dq = wq_bf.shape[1]
    tm = min(512, n)
    row = lambda w: pl.BlockSpec((tm, w), lambda i: (i, 0))
    return pl.pallas_call(
        _q_kernel,
        grid=(n // tm,),
        in_specs=[row(d), _const_spec((1, d)), _const_spec(wq_bf.shape), _const_spec(wg_bf.shape),
                  row(LANES), row(LANES)],
        out_specs=[row(dq), row(dq), row(LANES)],
        out_shape=[jax.ShapeDtypeStruct((n, dq), BF), jax.ShapeDtypeStruct((n, dq), BF),
                   jax.ShapeDtypeStruct((n, LANES), F32)],
        compiler_params=_cparams(1), name="q_proj",
    )(x, g, wq_bf, wg_bf, cos, sin)


def _compress_kernel(xk_ref, xv_ref, pa_ref, pb_ref, w1a_ref, w1b_ref, b1_ref, w2_ref, kc_ref, vc_ref):
    nhb = kc_ref.shape[0]
    for j, x_ref, o_ref in ((0, xk_ref, kc_ref), (1, xv_ref, vc_ref)):
        acc_a = jnp.zeros((nhb, 2 * LANES), F32)
        acc_b = jnp.zeros((nhb, 2 * LANES), F32)
        for l in range(D_CMP):
            xl = x_ref[pl.ds(l, nhb, stride=D_CMP), :]
            acc_a = acc_a + _dot((xl + pa_ref[j, l:l + 1, :]).astype(BF), w1a_ref[j, l])
            acc_b = acc_b + _dot((xl + pb_ref[j, l:l + 1, :]).astype(BF), w1b_ref[j, l])
        hid = _gelu(acc_a + pltpu.roll(acc_b, nhb - 1, axis=0) + b1_ref[j])
        o_ref[...] = _dot(hid.astype(BF), w2_ref[j]).astype(BF)


def _compress(rows, params, b, t):
    nhb = t // D_CMP
    out = pl.BlockSpec((None, nhb, LANES), lambda bi: (bi, 0, 0))
    return pl.pallas_call(
        _compress_kernel,
        grid=(b,),
        in_specs=[pl.BlockSpec((None, t, LANES), lambda bi: (bi, 0, 0)),
                  pl.BlockSpec((None, t, LANES), lambda bi: (bi, 0, 1))] + [_const_spec(a.shape) for a in params],
        out_specs=[out, out],
        out_shape=[jax.ShapeDtypeStruct((b, nhb, LANES), BF)] * 2,
        compiler_params=_cparams(1), name="compress",
    )(rows, rows, *params)


def _compress_paged_kernel(pt_ref, *refs, per_step):
    del pt_ref
    page_refs = refs[:2 * per_step]
    params = refs[2 * per_step:-4]
    kc_ref, vc_ref, xk_scr, xv_scr = refs[-4:]
    step = pl.program_id(1)
    for r in range(per_step):
        rows = pl.ds(pl.multiple_of((step * per_step + r) * PAGE_SIZE, PAGE_SIZE), PAGE_SIZE)
        xk_scr[rows, :] = page_refs[2 * r][...]
        xv_scr[rows, :] = page_refs[2 * r + 1][...]

    @pl.when(step == pl.num_programs(1) - 1)
    def _():
        _compress_kernel(xk_scr, xv_scr, *params, kc_ref, vc_ref)


def _compress_paged(cache, page_table, params):
    db, n_pages = page_table.shape
    t = n_pages * PAGE_SIZE
    nhb = t // D_CMP
    per_step = min(16, n_pages)
    in_specs = []
    for r in range(per_step):
        for half in range(2):
            in_specs.append(pl.BlockSpec((None, PAGE_SIZE, LANES), functools.partial(
                lambda bi, i, pt, r, half: (pt[bi, i * per_step + r], 0, half), r=r, half=half)))
    in_specs += [_const_spec(a.shape) for a in params]
    out = pl.BlockSpec((None, nhb, LANES), lambda bi, i, pt: (bi, 0, 0))
    return pl.pallas_call(
        functools.partial(_compress_paged_kernel, per_step=per_step),
        grid_spec=pltpu.PrefetchScalarGridSpec(
            num_scalar_prefetch=1, grid=(db, n_pages // per_step), in_specs=in_specs, out_specs=[out, out],
            scratch_shapes=[pltpu.VMEM((t, LANES), F32), pltpu.VMEM((t, LANES), F32)]),
        out_shape=[jax.ShapeDtypeStruct((db, nhb, LANES), BF)] * 2,
        compiler_params=_cparams(2), name="compress_paged",
    )(page_table, *([cache] * (2 * per_step)), *params)


def _stack_heads(q_ref, g, tq, fill=0.0):
    lane = lax.broadcasted_iota(jnp.int32, (tq, LANES), 1)
    keep = (lane >= HEAD_DIM * g) & (lane < HEAD_DIM * (g + 1))
    parts = []
    for h in range(HPG):
        hh = g * HPG + h
        ch = hh // 2
        c = q_ref[:, ch * LANES:(ch + 1) * LANES].astype(F32)
        if hh % 2 != g:
            c = pltpu.roll(c, HEAD_DIM, axis=1)
        parts.append(jnp.where(keep, c, fill))
    return jnp.concatenate(parts, axis=0).astype(BF)


def _cmp_branch(q, kc, vc, tpos_col, n_cb, tq):
    ncp = kc.shape[0]
    cidx = lax.broadcasted_iota(jnp.int32, (tq, ncp), 1)
    valid = ((cidx * D_CMP + (L_CMP - 1)) <= tpos_col) & (cidx < n_cb)
    s = _dot_nt(q, kc).reshape(HPG, tq, ncp) + jnp.where(valid, 0.0, NEG)[None]
    e = jnp.exp(s - jnp.max(s, axis=-1, keepdims=True))
    any_valid = (tpos_col >= L_CMP - 1) & (n_cb > 0)
    inv = jnp.where(any_valid[None], 1.0 / jnp.sum(e, axis=-1, keepdims=True), 0.0)
    p = e * inv
    o = _dot(p.reshape(HPG * tq, ncp).astype(BF), vc)
    return o, jnp.sum(p, axis=0)


def _cmp_branch_staged(g, q, kc, vc, tpos_col, n_cb, s_ref, p_ref, oc_ref, tq):
    ncp = kc.shape[0]
    cidx = lax.broadcasted_iota(jnp.int32, (tq, ncp), 1)
    bias = _mask_bias(((cidx * D_CMP + (L_CMP - 1)) <= tpos_col) & (cidx < n_cb))
    any_valid = (tpos_col >= L_CMP - 1) & (n_cb > 0)
    s_ref[g, :, 0:ncp] = _dot_nt(q, kc)
    psum = jnp.zeros((tq, ncp), F32)
    for h in range(HPG):
        rows = slice(h * tq, (h + 1) * tq)
        m = jnp.max(s_ref[g, rows, 0:ncp] + bias, axis=-1, keepdims=True)
        e = jnp.exp(s_ref[g, rows, 0:ncp] + bias - m)
        p = e * jnp.where(any_valid, 1.0 / jnp.sum(e, axis=-1, keepdims=True), 0.0)
        psum = psum + p
        p_ref[g, rows, 0:ncp] = p.astype(BF)
    oc_ref[g] = _dot(p_ref[g, :, 0:ncp], vc)
    return psum


def _importance(psum, mt_ref, tpos_row):
    sp = mt_ref.shape[0]
    mt = mt_ref[...]
    p_hi = psum.astype(BF)
    r1 = psum - p_hi.astype(F32)
    p_mid = r1.astype(BF)
    p_lo = (r1 - p_mid.astype(F32)).astype(BF)
    imp = _dot_nt(mt, p_hi) + _dot_nt(mt, p_mid) + _dot_nt(mt, p_lo)
    s_idx = lax.broadcasted_iota(jnp.int32, (sp, LANES), 0)
    cur = jnp.right_shift(tpos_row, L_SLC.bit_length() - 1)
    forced = (s_idx == 0) | (s_idx == cur) | (s_idx == cur - 1)
    imp = jnp.where(forced, jnp.inf, imp)
    return jnp.where(s_idx * L_SLC <= tpos_row, imp, -jnp.inf)


def _select_blocks(psums, mt_ref, imp_ref, tpos_row, n_sb, tq):
    sp = mt_ref.shape[0]
    assert n_sb <= HEAD_DIM and sp == LANES, "one-hot block lanes hold at most 64 selection blocks"
    nr = _round_up(n_sb, SUBLANES)
    imps = []
    for g, psum in enumerate(psums):
        imp = _importance(psum, mt_ref, tpos_row)
        imp_ref[g] = imp
        imps.append(imp[:nr])
    s_idx = lax.broadcasted_iota(jnp.int32, (nr, LANES), 0)

    def body(i, cnts):
        tie = jnp.where(s_idx > i, 1.0, 0.0)
        out = []
        for g, imp in enumerate(imps):
            row = imp_ref[g, pl.ds(i, 1), :]
            out.append(cnts[g] + jnp.where(row > imp, 1.0, jnp.where(row == imp, tie, 0.0)))
        return tuple(out)

    cnts = lax.fori_loop(0, n_sb, body, tuple(jnp.zeros((nr, LANES), F32) for _ in psums))
    starts_before = s_idx * L_SLC <= tpos_row
    biases = []
    for g, cnt in enumerate(cnts):
        bias_t = jnp.where((cnt < float(N_SEL)) & starts_before, 0.0, NEG)
        if nr < sp:
            bias_t = jnp.concatenate([bias_t, jnp.full((sp - nr, LANES), NEG, F32)], axis=0)
        bias = bias_t.T[:tq]
        biases.append(pltpu.roll(bias, HEAD_DIM, axis=1) if g == 0 else bias)
    return biases


def _select_single(psum, mt_ref, tpos, tq):
    sp = mt_ref.shape[0]
    prow = jnp.broadcast_to(psum[0:1], (LANES, psum.shape[1]))
    imp = _importance(prow, mt_ref, jnp.full((1, LANES), tpos, jnp.int32))
    a = jnp.broadcast_to(imp[:, 0:1], (sp, sp))
    b = a.T
    r_i = lax.broadcasted_iota(jnp.int32, (sp, sp), 0)
    c_i = lax.broadcasted_iota(jnp.int32, (sp, sp), 1)
    beats = jnp.where(a > b, 1.0, jnp.where(a == b, jnp.where(r_i < c_i, 1.0, 0.0), 0.0))
    cnt = jnp.sum(beats, axis=0, keepdims=True)
    sel = jnp.where(cnt < float(N_SEL), 1.0, 0.0)
    return jnp.broadcast_to(sel, (tq, sp)).astype(BF)


def _flash_init(tq):
    return (jnp.full((HPG, tq, 1), NEG, F32), jnp.zeros((HPG, tq, 1), F32), jnp.zeros((HPG * tq, LANES), F32))


def _flash_tile(carry, q, k, v, bias, tq):
    m, l, acc = carry
    kt = k.shape[0]
    s = _dot_nt(q, k).reshape(HPG, tq, kt)
    if bias is not None:
        s = s + bias[None]
    m_new = jnp.maximum(m, jnp.max(s, axis=-1, keepdims=True))
    alpha = jnp.exp(m - m_new)
    p = jnp.exp(s - m_new)
    l = alpha * l + jnp.sum(p, axis=-1, keepdims=True)
    acc = alpha.reshape(HPG * tq, 1) * acc + _dot(p.reshape(HPG * tq, kt).astype(BF), v)
    return m_new, l, acc


def _flash_out(carry, tq):
    _, l, acc = carry
    return acc / l.reshape(HPG * tq, 1)


def _attend_once(q, k, v, bias, tq):
    kt = k.shape[0]
    s = _dot_nt(q, k).reshape(HPG, tq, kt) + bias[None]
    p = jnp.exp(s - jnp.max(s, axis=-1, keepdims=True))
    l = jnp.sum(p, axis=-1, keepdims=True)
    return _dot(p.reshape(HPG * tq, kt).astype(BF), v) / l.reshape(HPG * tq, 1)


def _mask_bias(valid):
    return jnp.where(valid, 0.0, NEG)


def _win_valid(kpos_row, tpos_col):
    dist = tpos_col - kpos_row
    return (dist >= 0) & (dist < WINDOW)


def _merge_heads(o_c, o_s, o_w, gt_ref, g, o_ref, tq):
    lane = lax.broadcasted_iota(jnp.int32, (tq, LANES), 1)
    for pair in range(HPG // 2):
        both = []
        for h in (2 * pair, 2 * pair + 1):
            col = (g * HPG + h) * 3
            rows = slice(h * tq, (h + 1) * tq)
            both.append(o_c(rows) * gt_ref[:, col:col + 1] + o_s(rows) * gt_ref[:, col + 1:col + 2]
                        + o_w(rows) * gt_ref[:, col + 2:col + 3])
        even, odd = both
        if g == 0:
            odd = pltpu.roll(odd, HEAD_DIM, axis=1)
        else:
            even = pltpu.roll(even, HEAD_DIM, axis=1)
        ch = g * (HPG // 2) + pair
        o_ref[:, ch * LANES:(ch + 1) * LANES] = jnp.where(lane < HEAD_DIM, even, odd).astype(o_ref.dtype)


def _staged_tile(state, stage, q, k, v, bias, refs, tq, first):
    s_ref, p_ref, m_ref, acc_ref = refs
    kt = k.shape[0]
    reps = kt // LANES
    s_ref[stage, :, 0:kt] = _dot_nt(q, k)
    for h in range(HPG):
        rows = slice(h * tq, (h + 1) * tq)

        def scores():
            s = s_ref[stage, rows, 0:kt]
            return s if bias is None else s + bias

        m_tile = jnp.max(scores(), axis=-1, keepdims=True)
        if first:
            m_new = jnp.broadcast_to(m_tile, (tq, LANES))
        else:
            m_old = m_ref[state, rows, :]
            m_new = jnp.maximum(m_old, m_tile)
            acc_ref[state, rows, :] = jnp.exp(m_old - m_new) * acc_ref[state, rows, :]
        m_ref[state, rows, :] = m_new
        p_ref[stage, rows, 0:kt] = jnp.exp(scores() - jnp.concatenate([m_new] * reps, axis=1)).astype(BF)
    pv = _dot(p_ref[stage, :, 0:kt], v)
    if first:
        acc_ref[state] = pv
    else:
        acc_ref[state] = acc_ref[state] + pv


def _attn_prompt_kernel(qs_ref, qrs_ref, gt_ref, kc_ref, vc_ref, ka_ref, sva_ref, win_ref, wva_ref, mt_ref, o_ref,
                        imp_ref, s_ref, p_ref, m_ref, acc_ref, oc_ref, *, n_cb, n_sb):
    tq = qs_ref.shape[0]
    t_all = ka_ref.shape[0]
    qb = pl.program_id(1)
    t0 = qb * tq
    tpos_col = t0 + lax.broadcasted_iota(jnp.int32, (tq, 1), 0)
    tpos_row = t0 + lax.broadcasted_iota(jnp.int32, (1, LANES), 1)
    groups = range(N_KV_HEADS)
    k_lanes, v_lanes = slice(0, LANES), slice(LANES, 2 * LANES)

    psums = [_cmp_branch_staged(g, _stack_heads(qs_ref, g, tq), kc_ref[...], vc_ref[...], tpos_col, n_cb,
                                s_ref, p_ref, oc_ref, tq) for g in groups]
    blk_bias = _select_blocks(psums, mt_ref, imp_ref, tpos_row, n_sb, tq)
    qr_blk = [_stack_heads(qrs_ref, g, tq, fill=blk_bias[g]) for g in groups]

    refs = (s_ref, p_ref, m_ref, acc_ref)
    g_lanes = [slice(g * LANES, (g + 1) * LANES) for g in groups]

    def slc_tile(j, bias, first):
        rows = pl.ds(pl.multiple_of(j * SLC_TILE, SLC_TILE), SLC_TILE)
        for g in groups:
            _staged_tile(g, g, qr_blk[g], ka_ref[rows, g_lanes[g]], sva_ref[rows, g_lanes[g]], bias, refs, tq, first)

    last = (t0 + tq - 1) // SLC_TILE
    causal = (last * SLC_TILE + lax.broadcasted_iota(jnp.int32, (1, SLC_TILE), 1)) <= tpos_col
    slc_tile(last, _mask_bias(causal), True)

    def slc_body(j, c):
        slc_tile(j, None, False)
        return c

    lax.fori_loop(0, last, slc_body, 0)

    wk = min(WINDOW + tq, t_all)
    start = pl.multiple_of(jnp.clip(t0 - WINDOW, 0, t_all - wk), tq)
    wbias = _mask_bias(_win_valid(start + lax.broadcasted_iota(jnp.int32, (1, wk), 1), tpos_col))
    wrows = pl.ds(start, wk)
    kw = win_ref[wrows, k_lanes]
    for g in groups:
        _staged_tile(N_KV_HEADS + g, g, _stack_heads(qrs_ref, g, tq), kw, wva_ref[wrows, g_lanes[g]], wbias, refs,
                     tq, True)

    def normalised(slot, rows):
        acc = acc_ref[slot, rows, :]
        return acc / pltpu.roll(acc, HEAD_DIM, axis=1)

    for g in groups:
        _merge_heads(lambda rows: oc_ref[g, rows, :], functools.partial(normalised, g),
                     functools.partial(normalised, N_KV_HEADS + g), gt_ref, g, o_ref, tq)


def _attn_prompt(qs, qrs, gates, kc, vc, ka_bf, slc_va, win_bf, win_va, mt, b, t, n_cb, n_sb):
    tq = Q_BLOCK
    nqb = t // tq
    d = qs.shape[1]
    stage_w = max(SLC_TILE, min(WINDOW + tq, t))
    row = lambda w: pl.BlockSpec((tq, w), lambda bi, qi: (bi * nqb + qi, 0))
    per_b = lambda a: pl.BlockSpec((None,) + a.shape[1:], lambda bi, qi: (bi, 0, 0))
    return pl.pallas_call(
        functools.partial(_attn_prompt_kernel, n_cb=n_cb, n_sb=n_sb),
        grid=(b, nqb),
        in_specs=[row(d), row(d), row(LANES), per_b(kc), per_b(vc), per_b(ka_bf), per_b(slc_va), per_b(win_bf),
                  per_b(win_va), _const_spec(mt.shape)],
        out_specs=row(d),
        out_shape=jax.ShapeDtypeStruct((b * t, d), BF),
        scratch_shapes=[pltpu.VMEM((N_KV_HEADS, mt.shape[0], LANES), F32),
                        pltpu.VMEM((N_KV_HEADS, HPG * tq, stage_w), F32),
                        pltpu.VMEM((N_KV_HEADS, HPG * tq, stage_w), BF),
                        pltpu.VMEM((2 * N_KV_HEADS, HPG * tq, LANES), F32),
                        pltpu.VMEM((2 * N_KV_HEADS, HPG * tq, LANES), F32),
                        pltpu.VMEM((N_KV_HEADS, HPG * tq, LANES), F32)],
        compiler_params=_cparams(2), name="attn_prompt",
    )(qs, qrs, gates, kc, vc, ka_bf, slc_va, win_bf, win_va, mt)


def _attn_sample_kernel(pt_ref, qs_ref, qrs_ref, gt_ref, kc_ref, vc_ref, slct_ref, win_ref, wint_ref, mt_ref, e_ref,
                        et_ref, *rest, n_cb, past):
    del pt_ref
    page_refs, o_ref, slc_ref = rest[:-2], rest[-2], rest[-1]
    for r, page in enumerate(page_refs):
        slc_ref[r * PAGE_SIZE:(r + 1) * PAGE_SIZE, :] = page[...].astype(BF)
    tq = qs_ref.shape[0]
    tpos_col = jnp.full((tq, 1), past, jnp.int32)
    w_buf = win_ref.shape[0]
    tail = slct_ref.shape[0]
    k_lanes, v_lanes = slice(0, LANES), slice(LANES, 2 * LANES)
    tail_pos = past + lax.broadcasted_iota(jnp.int32, (1, tail), 1)
    win_pos = past - w_buf + lax.broadcasted_iota(jnp.int32, (1, w_buf), 1)
    wbias = jnp.broadcast_to(_mask_bias(_win_valid(win_pos, tpos_col)), (tq, w_buf))
    wtbias = jnp.broadcast_to(_mask_bias(_win_valid(tail_pos, tpos_col)), (tq, tail))
    kw, vw = win_ref[:, k_lanes].astype(BF), win_ref[:, v_lanes].astype(BF)
    for g in range(N_KV_HEADS):
        qr = _stack_heads(qrs_ref, g, tq)
        o_c, psum = _cmp_branch(_stack_heads(qs_ref, g, tq), kc_ref[...], vc_ref[...], tpos_col, n_cb, tq)
        sel = _select_single(psum, mt_ref, past, tq)
        carry = _flash_tile(_flash_init(tq), qr, slc_ref[:, k_lanes], slc_ref[:, v_lanes],
                            _mask_bias(_dot(sel, e_ref[...]) > 0.5), tq)
        carry = _flash_tile(carry, qr, slct_ref[:, k_lanes], slct_ref[:, v_lanes],
                            _mask_bias((_dot(sel, et_ref[...]) > 0.5) & (tail_pos <= tpos_col)), tq)
        o_s = _flash_out(carry, tq)
        carry = _flash_tile(_flash_init(tq), qr, kw, vw, wbias, tq)
        carry = _flash_tile(carry, qr, wint_ref[:, k_lanes], wint_ref[:, v_lanes], wtbias, tq)
        o_w = _flash_out(carry, tq)
        _merge_heads(lambda rows: o_c[rows], lambda rows: o_s[rows], lambda rows: o_w[rows], gt_ref, g, o_ref, tq)


def _attn_sample(qs, qrs, gates, kc, vc, slc_cache, page_table, slc_tail, win_c, win_tail, mt, e_all, et, n_cb, past):
    db, n_pages = page_table.shape
    per_b = lambda a: pl.BlockSpec((None,) + a.shape[1:], lambda bi, pt: (bi, 0, 0))
    args = (qs, qrs, gates, kc, vc, slc_tail, win_c, win_tail)
    pages = [pl.BlockSpec((None, PAGE_SIZE, slc_cache.shape[2]), functools.partial(
        lambda bi, pt, r: (pt[bi, r], 0, 0), r=r)) for r in range(n_pages)]
    return pl.pallas_call(
        functools.partial(_attn_sample_kernel, n_cb=n_cb, past=past),
        grid_spec=pltpu.PrefetchScalarGridSpec(
            num_scalar_prefetch=1, grid=(db,),
            in_specs=[per_b(a) for a in args] + [_const_spec(mt.shape), _const_spec(e_all.shape),
                                                 _const_spec(et.shape)] + pages,
            out_specs=per_b(qs),
            scratch_shapes=[pltpu.VMEM((n_pages * PAGE_SIZE, slc_cache.shape[2]), BF)]),
        out_shape=jax.ShapeDtypeStruct(qs.shape, BF),
        compiler_params=_cparams(1), name="attn_sample",
    )(page_table, *args, mt, e_all, et, *([slc_cache] * n_pages))


def _block_diag(blocks):
    n, k, j = blocks.shape
    eye = jnp.eye(n, dtype=blocks.dtype)
    return (eye[:, None, :, None] * blocks[:, :, None, :]).reshape(n * k, n * j)


def _pad_rows8(rows):
    return jnp.pad(rows, ((0, 0), (SUBLANES - rows.shape[1], 0), (0, 0)))


def _rope_tables(pos):
    half = HEAD_DIM // 2
    inv_freq = ROPE_THETA ** (-jnp.arange(half, dtype=F32) / half)
    ang = pos.astype(F32)[:, None] * inv_freq[None, :]
    cos, sin = jnp.cos(ang), jnp.sin(ang)
    reps = LANES // HEAD_DIM
    return (jnp.tile(jnp.concatenate([cos, cos], axis=1), (1, reps)),
            jnp.tile(jnp.concatenate([-sin, sin], axis=1), (1, reps)))


def _block_onehot(pos):
    lane = jnp.arange(LANES, dtype=jnp.int32)[None, :] % HEAD_DIM
    return (lane == (pos[:, None] // L_SLC)).astype(F32)


def _overlap_t(n_cb, n_sb, ncp, sp):
    c0 = np.arange(ncp)[None, :] * D_CMP
    s0 = np.arange(sp)[:, None] * L_SLC
    m = (c0 < s0 + L_SLC) & (c0 + L_CMP > s0) & (np.arange(ncp)[None, :] < n_cb) & (np.arange(sp)[:, None] < n_sb)
    return jnp.asarray(m, dtype=BF)


def _expand_tiles(n_tiles, sp, tile, first_key=0):
    key = first_key + np.arange(n_tiles)[:, None, None] * tile + np.arange(tile)[None, None, :]
    return jnp.asarray(np.arange(sp)[None, :, None] == key // L_SLC, dtype=BF)


def _round_up(x, m):
    return (x + m - 1) // m * m


def kernel(x_prompt, x_sample, p_prompt, p_sample, cache_cmp_kv, cache_slc_kv, cache_win_kv, state_rg_conv,
           state_rg_h, state_ffn_conv, page_table, g_mix, g_ffn, g_ple, g_final, rg_w_in, rg_conv_w, rg_conv_b,
           rg_w_a, rg_b_a, rg_w_x, rg_b_x, rg_lambda, rg_w_out, g_kv, w_kv, cmp_pos, cmp_w1, cmp_b1, cmp_w2,
           attn_w_qg, attn_w_o, ffn_w_up, ffn_conv_w, ffn_conv_b, ffn_w_down, ple_w_in, ple_w_gate):
    b, t, d = x_prompt.shape
    db = x_sample.shape[0]
    assert x_sample.shape[1] == 1, "the sample path handles one new token per sequence"
    depth = g_mix.shape[0]
    n_a = rg_w_in.shape[0]
    d_rnn = rg_w_out.shape[1]
    n_pages = page_table.shape[1]
    past = n_pages * PAGE_SIZE
    kvc = 2 * N_KV_HEADS * HEAD_DIM
    n_q = N_KV_HEADS * HPG * HEAD_DIM

    row1 = lambda v: v.reshape(1, -1)
    pad8 = lambda w: jnp.pad(w, ((0, SUBLANES - w.shape[0]), (0, 0)))

    rg_in_bf = rg_w_in.astype(BF)
    rg_out_bf = rg_w_out.astype(BF)
    rg_gate_bf = [jnp.concatenate([_block_diag(rg_w_a[i]), _block_diag(rg_w_x[i])], axis=1).astype(BF)
                  for i in range(n_a)]
    rg_gate_b = [row1(jnp.concatenate([rg_b_a[i].reshape(-1), rg_b_x[i].reshape(-1)])) for i in range(n_a)]
    rg_c = [row1(-RG_C * jax.nn.softplus(-rg_lambda[i])) for i in range(n_a)]
    up_bf = ffn_w_up.astype(BF)
    down_bf = ffn_w_down.astype(BF)
    gate_bf = ple_w_gate.astype(BF)
    pin_bf = ple_w_in.astype(BF)
    wkv_bf = w_kv.astype(BF)
    wq_bf = attn_w_qg[:, :, :n_q].astype(BF)
    wgl_bf = jnp.pad(attn_w_qg[:, :, n_q:], ((0, 0), (0, 0), (0, LANES - 3 * N_KV_HEADS * HPG))).astype(BF)
    wo_bf = attn_w_o.astype(BF)
    gfin = row1(g_final)

    w1 = cmp_w1.reshape(2, 2, D_CMP, HEAD_DIM, -1)
    eye_g = jnp.eye(N_KV_HEADS, dtype=F32)
    w1_bd = (eye_g[None, None, None, :, None, :, None] * w1[:, :, :, None, :, None, :]).reshape(
        2, 2, D_CMP, N_KV_HEADS * HEAD_DIM, -1).astype(BF)
    pos_t = jnp.tile(jnp.transpose(cmp_pos, (1, 0, 2)), (1, 1, N_KV_HEADS)).reshape(2, 2, D_CMP, LANES)
    cmp_params = (pos_t[:, 0], pos_t[:, 1], w1_bd[:, 0], w1_bd[:, 1],
                  jnp.tile(cmp_b1, (1, N_KV_HEADS)).reshape(2, 1, -1),
                  jnp.stack([_block_diag(jnp.stack([cmp_w2[j]] * N_KV_HEADS)) for j in range(2)]).astype(BF))

    def layer_consts(i, final):
        return (row1(g_ffn[i]), up_bf[i], pad8(ffn_conv_w[i]), row1(ffn_conv_b[i]), down_bf[i], row1(g_ple[i]),
                gate_bf[i], pin_bf[i], gfin), final

    n = b * t
    x = x_prompt.reshape(n, d)
    p = p_prompt.reshape(depth, n, -1)
    pos_p = jnp.tile(jnp.arange(t, dtype=jnp.int32), b)
    cos_p, sin_p = _rope_tables(pos_p)
    rg_conv_p, rg_h_p, ffn_conv_p = [], [], []
    zeros_rg = jnp.zeros((b, SUBLANES, d_rnn), F32)
    zeros_ffn = jnp.zeros((b, SUBLANES, ffn_w_up.shape[2]), F32)
    for i in range(depth):
        if i == n_a:
            kv_p, _, win_bf, ka_bf, slc_va, win_va = _kv_proj(x, row1(g_kv), wkv_bf, cos_p, sin_p,
                                                              _block_onehot(pos_p))
            kc_p, vc_p = _compress(kv_p.reshape(b, t, -1), cmp_params, b, t)
            n_cb_p = (t - L_CMP) // D_CMP + 1
            n_sb_p = t // L_SLC
            mt_p = _overlap_t(n_cb_p, n_sb_p, t // D_CMP, _round_up(n_sb_p, LANES))
            ka_bf, slc_va, win_bf, win_va = (a.reshape(b, t, -1) for a in (ka_bf, slc_va, win_bf, win_va))
        if i < n_a:
            x, h_last, xr_last = _rg_mixer(x, row1(g_mix[i]), rg_in_bf[i], zeros_rg, zeros_rg, pad8(rg_conv_w[i]),
                                           row1(rg_conv_b[i]), rg_gate_bf[i], rg_gate_b[i], rg_c[i], rg_out_bf[i],
                                           b, t)
            rg_conv_p.append(xr_last[:, SUBLANES - 3:])
            rg_h_p.append(h_last[:, 0])
        else:
            j = i - n_a
            qs, qrs, gates = _q_proj(x, row1(g_mix[i]), wq_bf[j], wgl_bf[j], cos_p, sin_p)
            o = _attn_prompt(qs, qrs, gates, kc_p, vc_p, ka_bf, slc_va, win_bf, win_va, mt_p, b, t, n_cb_p, n_sb_p)
            x = _matmul_res(o, wo_bf[j], x)
        consts, final = layer_consts(i, i == depth - 1)
        x, f_hist = _ffn(x, p, i, zeros_ffn, *consts, b, t, final)
        ffn_conv_p.append(f_hist[:, SUBLANES - 2:])
    y_prompt = x.reshape(b, t, d)
    kv5 = kv_p.reshape(b, t, 6, N_KV_HEADS, HEAD_DIM)
    keep = min(WINDOW, t)
    cmp_p, slc_p, win_p = kv5[:, :, 0:2], kv5[:, :, 2:4], kv5[:, t - keep:, 4:6]

    x = x_sample.reshape(db, d)
    p = p_sample.reshape(depth, db, -1)
    cos_s, sin_s = _rope_tables(jnp.full((db,), past, jnp.int32))
    rg_conv_s, rg_h_s, ffn_conv_s = [], [], []
    for i in range(depth):
        if i == n_a:
            pos_s = jnp.full((db,), past, jnp.int32)
            kv_s, slc_new_bf, win_new_bf = _kv_proj(x, row1(g_kv), wkv_bf, cos_s, sin_s, _block_onehot(pos_s))[:3]
            kc_s, vc_s = _compress_paged(cache_cmp_kv.reshape(-1, PAGE_SIZE, kvc), page_table, cmp_params)
            slc_cache = cache_slc_kv.reshape(-1, PAGE_SIZE, kvc)
            n_cb_s = (past + 1 - L_CMP) // D_CMP + 1
            n_sb_s = -(-(past + 1) // L_SLC)
            sp_s = _round_up(n_sb_s, LANES)
            mt_s = _overlap_t(n_cb_s, n_sb_s, past // D_CMP, sp_s)
            e_s = _expand_tiles(1, sp_s, past)[0]
            et_s = _expand_tiles(1, sp_s, LANES, first_key=past)[0]
            tail = lambda a: jnp.pad(a.reshape(db, 1, -1), ((0, 0), (0, LANES - 1), (0, 0)))
            slc_tail, win_tail = tail(slc_new_bf), tail(win_new_bf)
            win_c = cache_win_kv.reshape(db, cache_win_kv.shape[1], kvc)
        if i < n_a:
            yg, xr = _rg_in(x, row1(g_mix[i]), rg_in_bf[i])
            hist = state_rg_conv[i]
            gated, h_new = _rg_step(xr, yg, hist[:, 0], hist[:, 1], hist[:, 2], state_rg_h[i], pad8(rg_conv_w[i]),
                                    row1(rg_conv_b[i]), rg_gate_bf[i], rg_gate_b[i], rg_c[i])
            x = _matmul_res(gated, rg_out_bf[i], x)
            rg_conv_s.append(jnp.concatenate([hist[:, 1:], xr[:, None, :]], axis=1))
            rg_h_s.append(h_new)
        else:
            j = i - n_a
            qs, qrs, gates = _q_proj(x, row1(g_mix[i]), wq_bf[j], wgl_bf[j], cos_s, sin_s)
            rep = lambda a: jnp.broadcast_to(a[:, None, :], (db, SUBLANES, a.shape[1]))
            o = _attn_sample(rep(qs), rep(qrs), rep(gates), kc_s, vc_s, slc_cache, page_table, slc_tail, win_c,
                             win_tail, mt_s, e_s, et_s, n_cb_s, past)
            x = _matmul_res(o[:, 0], wo_bf[j], x)
        consts, final = layer_consts(i, i == depth - 1)
        fh = state_ffn_conv[i]
        x, up = _ffn_step(x, p[i], fh[:, 0], fh[:, 1], *consts, final)
        ffn_conv_s.append(jnp.stack([fh[:, 1], up], axis=1))
    y_sample = x.reshape(db, 1, d)
    kv5 = kv_s.reshape(db, 1, 6, N_KV_HEADS, HEAD_DIM)
    cmp_s, slc_s, win_new = kv5[:, :, 0:2], kv5[:, :, 2:4], kv5[:, :, 4:6]
    win_full = jnp.concatenate([cache_win_kv, win_new], axis=1)
    keep = min(WINDOW, past + 1)
    win_s = win_full[:, win_full.shape[1] - keep:]

    return (y_prompt, y_sample, cmp_p, cmp_s, slc_p, slc_s, win_p, win_s,
            jnp.stack(rg_conv_p), jnp.stack(rg_conv_s), jnp.stack(rg_h_p), jnp.stack(rg_h_s),
            jnp.stack(ffn_conv_p), jnp.stack(ffn_conv_s))
```

```python
import functools

import jax
import jax.numpy as jnp
import numpy as np
from jax import lax
from jax.experimental import pallas as pl
from jax.experimental.pallas import tpu as pltpu

BF = jnp.bfloat16
F32 = jnp.float32

EPS = 1e-6
RG_C = 8.0
N_RG_BLOCKS = 16
HEAD_DIM = 64
N_KV_HEADS = 2
HPG = 8
L_CMP = 32
D_CMP = 16
L_SLC = 64
N_SEL = 16
WINDOW = 512
Q_BLOCK = 128
ROPE_THETA = 10000.0
PAGE_SIZE = 128
NEG = -1e30
LANES = 128
SUBLANES = 8
SLC_TILE = 512
WIN_TILE = 128
VMEM_LIMIT = 56 * 1024 * 1024


def _cparams(n_axes):
    return pltpu.CompilerParams(dimension_semantics=("arbitrary",) * n_axes,
                                vmem_limit_bytes=VMEM_LIMIT)


def _const_spec(shape):
    zeros = (0,) * len(shape)
    return pl.BlockSpec(shape, lambda *_: zeros, pipeline_mode=pl.Buffered(1))


def _dot(a, b):
    return jnp.dot(a, b, preferred_element_type=F32)


def _dot_nt(a, b):
    return lax.dot_general(a, b, (((1,), (1,)), ((), ())), preferred_element_type=F32)


def _rms(x, g):
    return x * lax.rsqrt(jnp.mean(x * x, axis=-1, keepdims=True) + EPS) * g


def _gelu(x):
    return jax.nn.gelu(x, approximate=True)


def _sigmoid(x):
    return jax.nn.sigmoid(x)


def _shift_rows(x, prev8, j):
    if j == 0:
        return x
    r = pltpu.roll(x, j, axis=0)
    p = pltpu.roll(prev8, j, axis=0)
    row = lax.broadcasted_iota(jnp.int32, (SUBLANES, x.shape[1]), 0)
    head = jnp.where(row < j, p, r[:SUBLANES])
    if x.shape[0] == SUBLANES:
        return head
    return jnp.concatenate([head, r[SUBLANES:]], axis=0)


def _rope(z, cos, sin_signed):
    lane = lax.broadcasted_iota(jnp.int32, z.shape, 1)
    first = (lane & (HEAD_DIM - 1)) < (HEAD_DIM // 2)
    rot = jnp.where(first, pltpu.roll(z, LANES - HEAD_DIM // 2, axis=1), pltpu.roll(z, HEAD_DIM // 2, axis=1))
    return z * cos + rot * sin_signed


def _rg_in_kernel(x_ref, g_ref, w_ref, yg_ref, xr_ref, *, chunk):
    h = _rms(x_ref[...], g_ref[...]).astype(BF)
    n = yg_ref.shape[1]
    for c0 in range(0, n, chunk):
        yg_ref[:, c0:c0 + chunk] = _gelu(_dot(h, w_ref[:, c0:c0 + chunk]))
        xr_ref[:, c0:c0 + chunk] = _dot(h, w_ref[:, n + c0:n + c0 + chunk])


def _rg_in(x, g, w_bf):
    n, d = x.shape
    dr = w_bf.shape[1] // 2
    tm = min(512, n)
    return pl.pallas_call(
        functools.partial(_rg_in_kernel, chunk=256),
        grid=(n // tm,),
        in_specs=[pl.BlockSpec((tm, d), lambda i: (i, 0)), _const_spec((1, d)), _const_spec(w_bf.shape)],
        out_specs=[pl.BlockSpec((tm, dr), lambda i: (i, 0)), pl.BlockSpec((tm, dr), lambda i: (i, 0))],
        out_shape=[jax.ShapeDtypeStruct((n, dr), F32), jax.ShapeDtypeStruct((n, dr), F32)],
        compiler_params=_cparams(1), name="rg_in",
    )(x, g, w_bf)


def _rg_gates(xc, wg_ref, bg_ref, c_ref):
    d = xc.shape[1]
    gates = _dot(xc.astype(BF), wg_ref[...]) + bg_ref[...]
    r = _sigmoid(gates[:, :d])
    i = _sigmoid(gates[:, d:])
    log_a = c_ref[...] * r
    a = jnp.exp(log_a)
    z = -jnp.tanh(log_a) * (a * a + 1.0)
    mult = jnp.where(z > 0.0, z * lax.rsqrt(z), 0.0)
    return a, mult, i * xc


def _rg_mixer_kernel(xin_ref, gm_ref, win_ref, hist_ref, h0_ref, cw_ref, cb_ref, wg_ref, bg_ref, c_ref, wout_ref,
                     out_ref, hl_ref, xl_ref, xprev, hc, *, sub):
    t = pl.program_id(1)

    @pl.when(t == 0)
    def _():
        xprev[...] = hist_ref[...]
        hc[...] = h0_ref[...]

    d = wout_ref.shape[0]
    prev = xprev[...]
    carry = hc[0:1, :]
    row = lax.broadcasted_iota(jnp.int32, (sub, d), 0)
    pieces = range(0, xin_ref.shape[0], sub)
    projs = []
    for r0 in pieces:
        hn = _rms(xin_ref[r0:r0 + sub, :], gm_ref[...]).astype(BF)
        projs.append((_dot(hn, win_ref[:, 0:d]), _dot(hn, win_ref[:, d:2 * d])))
    for (yg_pre, x), r0 in zip(projs, pieces):
        x_in = xin_ref[r0:r0 + sub, :]
        yg = _gelu(yg_pre)
        xc = cb_ref[...] + _shift_rows(x, prev, 3) * cw_ref[0:1, :]
        xc = xc + _shift_rows(x, prev, 2) * cw_ref[1:2, :]
        xc = xc + _shift_rows(x, prev, 1) * cw_ref[2:3, :]
        xc = xc + x * cw_ref[3:4, :]
        prev = x[sub - SUBLANES:, :]

        a, mult, ixc = _rg_gates(xc, wg_ref, bg_ref, c_ref)
        if r0 == 0:
            mult = jnp.where((row == 0) & (t == 0), 1.0, mult)
        u = mult * ixc

        av, bv = a, u
        dist = 1
        while dist < sub:
            if dist < SUBLANES:
                ok = row >= dist
                bv = jnp.where(ok, av * pltpu.roll(bv, dist, axis=0) + bv, bv)
                av = jnp.where(ok, av * pltpu.roll(av, dist, axis=0), av)
            else:
                bv = jnp.concatenate([bv[:dist], av[dist:] * bv[:sub - dist] + bv[dist:]], axis=0)
                av = jnp.concatenate([av[:dist], av[dist:] * av[:sub - dist]], axis=0)
            dist *= 2
        h = av * carry + bv
        carry = h[sub - 1:sub, :]
        out_ref[r0:r0 + sub, :] = x_in + _dot((yg * h).astype(BF), wout_ref[...])
    xprev[...] = prev
    hc[...] = jnp.broadcast_to(carry, (SUBLANES, d))
    hl_ref[...] = hc[...]
    xl_ref[...] = prev


def _rg_mixer(x, gm, win_bf, hist8, h08, cw8, cb, wg_bf, bg, c, wout_bf, b, t):
    n, dm = x.shape
    d = wout_bf.shape[0]
    tt = min(512, t)
    nt = t // tt
    row_spec = pl.BlockSpec((tt, dm), lambda bi, ti: (bi * nt + ti, 0))
    st_spec = pl.BlockSpec((None, SUBLANES, d), lambda bi, ti: (bi, 0, 0))
    consts = (cw8, cb, wg_bf, bg, c, wout_bf)
    return pl.pallas_call(
        functools.partial(_rg_mixer_kernel, sub=min(256, tt)),
        grid=(b, nt),
        in_specs=[row_spec, _const_spec(gm.shape), _const_spec(win_bf.shape), st_spec, st_spec]
        + [_const_spec(a.shape) for a in consts],
        out_specs=[row_spec, st_spec, st_spec],
        out_shape=[jax.ShapeDtypeStruct((n, dm), F32), jax.ShapeDtypeStruct((b, SUBLANES, d), F32),
                   jax.ShapeDtypeStruct((b, SUBLANES, d), F32)],
        scratch_shapes=[pltpu.VMEM((SUBLANES, d), F32), pltpu.VMEM((SUBLANES, d), F32)],
        compiler_params=_cparams(2), name="rg_mixer",
    )(x, gm, win_bf, hist8, h08, *consts)


def _rg_step_kernel(xr_ref, yg_ref, h0_ref, h1_ref, h2_ref, hs_ref, cw_ref, cb_ref, wg_ref, bg_ref, c_ref,
                    out_ref, hn_ref):
    x = xr_ref[...]
    xc = cb_ref[...] + h0_ref[...] * cw_ref[0:1, :]
    xc = xc + h1_ref[...] * cw_ref[1:2, :]
    xc = xc + h2_ref[...] * cw_ref[2:3, :]
    xc = xc + x * cw_ref[3:4, :]
    a, mult, ixc = _rg_gates(xc, wg_ref, bg_ref, c_ref)
    h = mult * ixc + a * hs_ref[...]
    hn_ref[...] = h
    out_ref[...] = (yg_ref[...] * h).astype(BF)


def _rg_step(xr, yg, h0, h1, h2, hs, cw8, cb, wg_bf, bg, c):
    n, d = xr.shape
    args = (xr, yg, h0, h1, h2, hs, cw8, cb, wg_bf, bg, c)
    return pl.pallas_call(
        _rg_step_kernel,
        grid=(1,),
        in_specs=[_const_spec(a.shape) for a in args],
        out_specs=[pl.BlockSpec((n, d), lambda i: (0, 0)), pl.BlockSpec((n, d), lambda i: (0, 0))],
        out_shape=[jax.ShapeDtypeStruct((n, d), BF), jax.ShapeDtypeStruct((n, d), F32)],
        compiler_params=_cparams(1), name="rg_step",
    )(*args)


def _matmul_res_kernel(a_ref, w_ref, x_ref, o_ref):
    o_ref[...] = x_ref[...] + _dot(a_ref[...], w_ref[...])


def _matmul_res(a_bf, w_bf, x):
    n, k = a_bf.shape
    d = w_bf.shape[1]
    tm = min(512, n)
    return pl.pallas_call(
        _matmul_res_kernel,
        grid=(n // tm,),
        in_specs=[pl.BlockSpec((tm, k), lambda i: (i, 0)), _const_spec(w_bf.shape),
                  pl.BlockSpec((tm, d), lambda i: (i, 0))],
        out_specs=pl.BlockSpec((tm, d), lambda i: (i, 0)),
        out_shape=jax.ShapeDtypeStruct((n, d), F32),
        compiler_params=_cparams(1), name="matmul_res",
    )(a_bf, w_bf, x)


def _ffn_tail(x, acc, p_ref, gp_ref, wgate_ref, wpin_ref, gfin_ref, final_norm):
    x1 = x + acc
    hp = _rms(x1, gp_ref[...]).astype(BF)
    sig = _sigmoid(_dot(hp, wgate_ref[...]))
    pe = _dot(p_ref[...].astype(BF), wpin_ref[...])
    x2 = x1 + pe * sig
    if final_norm:
        x2 = _rms(x2, gfin_ref[...])
    return x2


def _ffn_kernel(x_ref, p_ref, hist_ref, gf_ref, wup_ref, cw_ref, cb_ref, wdn_ref, gp_ref, wgate_ref, wpin_ref,
                gfin_ref, o_ref, ho_ref, uprev, *, chunk, final_norm):
    t = pl.program_id(1)

    @pl.when(t == 0)
    def _():
        uprev[...] = hist_ref[...]

    x = x_ref[...]
    tm = x.shape[0]
    dff = wdn_ref.shape[0]
    h = _rms(x, gf_ref[...]).astype(BF)
    acc = jnp.zeros(x.shape, F32)
    for c0 in range(0, dff, chunk):
        halves = []
        for off in (c0, dff + c0):
            up = _dot(h, wup_ref[:, off:off + chunk])
            prev = uprev[:, off:off + chunk]
            uc = cb_ref[:, off:off + chunk] + _shift_rows(up, prev, 2) * cw_ref[0:1, off:off + chunk]
            uc = uc + _shift_rows(up, prev, 1) * cw_ref[1:2, off:off + chunk]
            uc = uc + up * cw_ref[2:3, off:off + chunk]
            uprev[:, off:off + chunk] = up[tm - SUBLANES:, :]
            halves.append(uc)
        act = (_gelu(halves[0]) * halves[1]).astype(BF)
        acc = acc + _dot(act, wdn_ref[c0:c0 + chunk, :])
    ho_ref[...] = uprev[...]
    o_ref[...] = _ffn_tail(x, acc, p_ref, gp_ref, wgate_ref, wpin_ref, gfin_ref, final_norm)


def _ffn(x, p_all, layer, hist8, gf, wup_bf, cw8, cb, wdn_bf, gp, wgate_bf, wpin_bf, gfin, b, t, final_norm):
    n, d = x.shape
    dp = p_all.shape[2]
    d2 = wup_bf.shape[1]
    tm = min(1024, t)
    nt = t // tm
    row = lambda w: pl.BlockSpec((tm, w), lambda bi, ti: (bi * nt + ti, 0))
    p_spec = pl.BlockSpec((None, tm, dp), lambda bi, ti: (layer, bi * nt + ti, 0))
    st_spec = pl.BlockSpec((None, SUBLANES, d2), lambda bi, ti: (bi, 0, 0))
    consts = (gf, wup_bf, cw8, cb, wdn_bf, gp, wgate_bf, wpin_bf, gfin)
    return pl.pallas_call(
        functools.partial(_ffn_kernel, chunk=512, final_norm=final_norm),
        grid=(b, nt),
        in_specs=[row(d), p_spec, st_spec] + [_const_spec(a.shape) for a in consts],
        out_specs=[row(d), st_spec],
        out_shape=[jax.ShapeDtypeStruct((n, d), F32), jax.ShapeDtypeStruct((b, SUBLANES, d2), F32)],
        scratch_shapes=[pltpu.VMEM((SUBLANES, d2), F32)],
        compiler_params=_cparams(2), name="ffn",
    )(x, p_all, hist8, *consts)


def _ffn_step_kernel(x_ref, p_ref, h0_ref, h1_ref, gf_ref, wup_ref, cw_ref, cb_ref, wdn_ref, gp_ref, wgate_ref,
                     wpin_ref, gfin_ref, o_ref, up_ref, *, chunk, final_norm):
    x = x_ref[...]
    dff = wdn_ref.shape[0]
    h = _rms(x, gf_ref[...]).astype(BF)
    acc = jnp.zeros(x.shape, F32)
    for c0 in range(0, dff, chunk):
        halves = []
        for off in (c0, dff + c0):
            up = _dot(h, wup_ref[:, off:off + chunk])
            up_ref[:, off:off + chunk] = up
            uc = cb_ref[:, off:off + chunk] + h0_ref[:, off:off + chunk] * cw_ref[0:1, off:off + chunk]
            uc = uc + h1_ref[:, off:off + chunk] * cw_ref[1:2, off:off + chunk]
            uc = uc + up * cw_ref[2:3, off:off + chunk]
            halves.append(uc)
        act = (_gelu(halves[0]) * halves[1]).astype(BF)
        acc = acc + _dot(act, wdn_ref[c0:c0 + chunk, :])
    o_ref[...] = _ffn_tail(x, acc, p_ref, gp_ref, wgate_ref, wpin_ref, gfin_ref, final_norm)


def _ffn_step(x, p, h0, h1, gf, wup_bf, cw8, cb, wdn_bf, gp, wgate_bf, wpin_bf, gfin, final_norm):
    n, d = x.shape
    d2 = wup_bf.shape[1]
    args = (x, p, h0, h1, gf, wup_bf, cw8, cb, wdn_bf, gp, wgate_bf, wpin_bf, gfin)
    return pl.pallas_call(
        functools.partial(_ffn_step_kernel, chunk=512, final_norm=final_norm),
        grid=(1,),
        in_specs=[_const_spec(a.shape) for a in args],
        out_specs=[pl.BlockSpec((n, d), lambda i: (0, 0)), pl.BlockSpec((n, d2), lambda i: (0, 0))],
        out_shape=[jax.ShapeDtypeStruct((n, d), F32), jax.ShapeDtypeStruct((n, d2), F32)],
        compiler_params=_cparams(1), name="ffn_step",
    )(*args)


def _kv_kernel(x_ref, g_ref, w_ref, cos_ref, sin_ref, oh_ref, kv_ref, slc_ref, win_ref, ka_ref, sva_ref, wva_ref):
    h = _rms(x_ref[...], g_ref[...]).astype(BF)
    kv = _dot(h, w_ref[...])
    cos, sin = cos_ref[...], sin_ref[...]
    w2 = 2 * LANES
    k_slc = _rope(kv[:, w2:w2 + LANES], cos, sin)
    k_win = _rope(kv[:, 2 * w2:2 * w2 + LANES], cos, sin)
    kv_ref[:, 0:w2] = kv[:, 0:w2]
    kv_ref[:, w2:w2 + LANES] = k_slc
    kv_ref[:, w2 + LANES:2 * w2] = kv[:, w2 + LANES:2 * w2]
    kv_ref[:, 2 * w2:2 * w2 + LANES] = k_win
    kv_ref[:, 2 * w2 + LANES:3 * w2] = kv[:, 2 * w2 + LANES:3 * w2]
    slc_ref[:, 0:LANES] = k_slc.astype(BF)
    slc_ref[:, LANES:w2] = kv[:, w2 + LANES:2 * w2].astype(BF)
    win_ref[:, 0:LANES] = k_win.astype(BF)
    win_ref[:, LANES:w2] = kv[:, 2 * w2 + LANES:3 * w2].astype(BF)
    low = lax.broadcasted_iota(jnp.int32, k_slc.shape, 1) < HEAD_DIM
    oh = oh_ref[...]
    ka_ref[:, 0:LANES] = jnp.where(low, k_slc, oh).astype(BF)
    ka_ref[:, LANES:w2] = jnp.where(low, oh, k_slc).astype(BF)
    for va_ref, v in ((sva_ref, kv[:, w2 + LANES:2 * w2]), (wva_ref, kv[:, 2 * w2 + LANES:3 * w2])):
        va_ref[:, 0:LANES] = jnp.where(low, v, 1.0).astype(BF)
        va_ref[:, LANES:w2] = jnp.where(low, 1.0, v).astype(BF)


def _kv_proj(x, g, w_bf, cos, sin, onehot):
    n, d = x.shape
    nk = w_bf.shape[1]
    tm = min(512, n)
    row = lambda w: pl.BlockSpec((tm, w), lambda i: (i, 0))
    return pl.pallas_call(
        _kv_kernel,
        grid=(n // tm,),
        in_specs=[row(d), _const_spec((1, d)), _const_spec(w_bf.shape), row(LANES), row(LANES), row(LANES)],
        out_specs=[row(nk)] + [row(2 * LANES)] * 5,
        out_shape=[jax.ShapeDtypeStruct((n, nk), F32)] + [jax.ShapeDtypeStruct((n, 2 * LANES), BF)] * 5,
        compiler_params=_cparams(1), name="kv_proj",
    )(x, g, w_bf, cos, sin, onehot)


def _q_kernel(x_ref, g_ref, wq_ref, wg_ref, cos_ref, sin_ref, qs_ref, qrs_ref, gt_ref):
    h = _rms(x_ref[...], g_ref[...]).astype(BF)
    cos, sin = cos_ref[...], sin_ref[...]
    scale = HEAD_DIM ** -0.5
    for c0 in range(0, wq_ref.shape[1], LANES):
        q = _dot(h, wq_ref[:, c0:c0 + LANES])
        qs_ref[:, c0:c0 + LANES] = (q * scale).astype(BF)
        qrs_ref[:, c0:c0 + LANES] = (_rope(q, cos, sin) * scale).astype(BF)
    gt_ref[...] = _sigmoid(_dot(h, wg_ref[...]))


def _q_proj(x, g, wq_bf, wg_bf, cos, sin):
    n, d = x.shape
    dq = wq_bf.shape[1]
    tm = min(512, n)
    row = lambda w: pl.BlockSpec((tm, w), lambda i: (i, 0))
    return pl.pallas_call(
        _q_kernel,
        grid=(n // tm,),
        in_specs=[row(d), _const_spec((1, d)), _const_spec(wq_bf.shape), _const_spec(wg_bf.shape),
                  row(LANES), row(LANES)],
        out_specs=[row(dq), row(dq), row(LANES)],
        out_shape=[jax.ShapeDtypeStruct((n, dq), BF), jax.ShapeDtypeStruct((n, dq), BF),
                   jax.ShapeDtypeStruct((n, LANES), F32)],
        compiler_params=_cparams(1), name="q_proj",
    )(x, g, wq_bf, wg_bf, cos, sin)


def _compress_kernel(xk_ref, xv_ref, pa_ref, pb_ref, w1a_ref, w1b_ref, b1_ref, w2_ref, kc_ref, vc_ref):
    nhb = kc_ref.shape[0]
    for j, x_ref, o_ref in ((0, xk_ref, kc_ref), (1, xv_ref, vc_ref)):
        acc_a = jnp.zeros((nhb, 2 * LANES), F32)
        acc_b = jnp.zeros((nhb, 2 * LANES), F32)
        for l in range(D_CMP):
            xl = x_ref[pl.ds(l, nhb, stride=D_CMP), :]
            acc_a = acc_a + _dot((xl + pa_ref[j, l:l + 1, :]).astype(BF), w1a_ref[j, l])
            acc_b = acc_b + _dot((xl + pb_ref[j, l:l + 1, :]).astype(BF), w1b_ref[j, l])
        hid = _gelu(acc_a + pltpu.roll(acc_b, nhb - 1, axis=0) + b1_ref[j])
        o_ref[...] = _dot(hid.astype(BF), w2_ref[j]).astype(BF)


def _compress(rows, params, b, t):
    nhb = t // D_CMP
    out = pl.BlockSpec((None, nhb, LANES), lambda bi: (bi, 0, 0))
    return pl.pallas_call(
        _compress_kernel,
        grid=(b,),
        in_specs=[pl.BlockSpec((None, t, LANES), lambda bi: (bi, 0, 0)),
                  pl.BlockSpec((None, t, LANES), lambda bi: (bi, 0, 1))] + [_const_spec(a.shape) for a in params],
        out_specs=[out, out],
        out_shape=[jax.ShapeDtypeStruct((b, nhb, LANES), BF)] * 2,
        compiler_params=_cparams(1), name="compress",
    )(rows, rows, *params)


def _compress_paged_kernel(pt_ref, *refs, per_step):
    del pt_ref
    page_refs = refs[:2 * per_step]
    params = refs[2 * per_step:-4]
    kc_ref, vc_ref, xk_scr, xv_scr = refs[-4:]
    step = pl.program_id(1)
    for r in range(per_step):
        rows = pl.ds(pl.multiple_of((step * per_step + r) * PAGE_SIZE, PAGE_SIZE), PAGE_SIZE)
        xk_scr[rows, :] = page_refs[2 * r][...]
        xv_scr[rows, :] = page_refs[2 * r + 1][...]

    @pl.when(step == pl.num_programs(1) - 1)
    def _():
        _compress_kernel(xk_scr, xv_scr, *params, kc_ref, vc_ref)


def _compress_paged(cache, page_table, params):
    db, n_pages = page_table.shape
    t = n_pages * PAGE_SIZE
    nhb = t // D_CMP
    per_step = min(16, n_pages)
    in_specs = []
    for r in range(per_step):
        for half in range(2):
            in_specs.append(pl.BlockSpec((None, PAGE_SIZE, LANES), functools.partial(
                lambda bi, i, pt, r, half: (pt[bi, i * per_step + r], 0, half), r=r, half=half)))
    in_specs += [_const_spec(a.shape) for a in params]
    out = pl.BlockSpec((None, nhb, LANES), lambda bi, i, pt: (bi, 0, 0))
    return pl.pallas_call(
        functools.partial(_compress_paged_kernel, per_step=per_step),
        grid_spec=pltpu.PrefetchScalarGridSpec(
            num_scalar_prefetch=1, grid=(db, n_pages // per_step), in_specs=in_specs, out_specs=[out, out],
            scratch_shapes=[pltpu.VMEM((t, LANES), F32), pltpu.VMEM((t, LANES), F32)]),
        out_shape=[jax.ShapeDtypeStruct((db, nhb, LANES), BF)] * 2,
        compiler_params=_cparams(2), name="compress_paged",
    )(page_table, *([cache] * (2 * per_step)), *params)


def _stack_heads(q_ref, g, tq, fill=0.0):
    lane = lax.broadcasted_iota(jnp.int32, (tq, LANES), 1)
    keep = (lane >= HEAD_DIM * g) & (lane < HEAD_DIM * (g + 1))
    parts = []
    for h in range(HPG):
        hh = g * HPG + h
        ch = hh // 2
        c = q_ref[:, ch * LANES:(ch + 1) * LANES].astype(F32)
        if hh % 2 != g:
            c = pltpu.roll(c, HEAD_DIM, axis=1)
        parts.append(jnp.where(keep, c, fill))
    return jnp.concatenate(parts, axis=0).astype(BF)


def _cmp_branch(q, kc, vc, tpos_col, n_cb, tq):
    ncp = kc.shape[0]
    cidx = lax.broadcasted_iota(jnp.int32, (tq, ncp), 1)
    valid = ((cidx * D_CMP + (L_CMP - 1)) <= tpos_col) & (cidx < n_cb)
    s = _dot_nt(q, kc).reshape(HPG, tq, ncp) + jnp.where(valid, 0.0, NEG)[None]
    e = jnp.exp(s - jnp.max(s, axis=-1, keepdims=True))
    any_valid = (tpos_col >= L_CMP - 1) & (n_cb > 0)
    inv = jnp.where(any_valid[None], 1.0 / jnp.sum(e, axis=-1, keepdims=True), 0.0)
    p = e * inv
    o = _dot(p.reshape(HPG * tq, ncp).astype(BF), vc)
    return o, jnp.sum(p, axis=0)


def _cmp_branch_staged(g, q, kc, vc, tpos_col, n_cb, s_ref, p_ref, oc_ref, tq):
    ncp = kc.shape[0]
    cidx = lax.broadcasted_iota(jnp.int32, (tq, ncp), 1)
    bias = _mask_bias(((cidx * D_CMP + (L_CMP - 1)) <= tpos_col) & (cidx < n_cb))
    any_valid = (tpos_col >= L_CMP - 1) & (n_cb > 0)
    s_ref[g, :, 0:ncp] = _dot_nt(q, kc)
    psum = jnp.zeros((tq, ncp), F32)
    for h in range(HPG):
        rows = slice(h * tq, (h + 1) * tq)
        m = jnp.max(s_ref[g, rows, 0:ncp] + bias, axis=-1, keepdims=True)
        e = jnp.exp(s_ref[g, rows, 0:ncp] + bias - m)
        p = e * jnp.where(any_valid, 1.0 / jnp.sum(e, axis=-1, keepdims=True), 0.0)
        psum = psum + p
        p_ref[g, rows, 0:ncp] = p.astype(BF)
    oc_ref[g] = _dot(p_ref[g, :, 0:ncp], vc)
    return psum


def _importance(psum, mt_ref, tpos_row):
    sp = mt_ref.shape[0]
    mt = mt_ref[...]
    p_hi = psum.astype(BF)
    r1 = psum - p_hi.astype(F32)
    p_mid = r1.astype(BF)
    p_lo = (r1 - p_mid.astype(F32)).astype(BF)
    imp = _dot_nt(mt, p_hi) + _dot_nt(mt, p_mid) + _dot_nt(mt, p_lo)
    s_idx = lax.broadcasted_iota(jnp.int32, (sp, LANES), 0)
    cur = jnp.right_shift(tpos_row, L_SLC.bit_length() - 1)
    forced = (s_idx == 0) | (s_idx == cur) | (s_idx == cur - 1)
    imp = jnp.where(forced, jnp.inf, imp)
    return jnp.where(s_idx * L_SLC <= tpos_row, imp, -jnp.inf)


def _select_blocks(psums, mt_ref, imp_ref, tpos_row, n_sb, tq):
    sp = mt_ref.shape[0]
    assert n_sb <= HEAD_DIM and sp == LANES, "one-hot block lanes hold at most 64 selection blocks"
    nr = _round_up(n_sb, SUBLANES)
    imps = []
    for g, psum in enumerate(psums):
        imp = _importance(psum, mt_ref, tpos_row)
        imp_ref[g] = imp
        imps.append(imp[:nr])
    s_idx = lax.broadcasted_iota(jnp.int32, (nr, LANES), 0)

    def body(i, cnts):
        tie = jnp.where(s_idx > i, 1.0, 0.0)
        out = []
        for g, imp in enumerate(imps):
            row = imp_ref[g, pl.ds(i, 1), :]
            out.append(cnts[g] + jnp.where(row > imp, 1.0, jnp.where(row == imp, tie, 0.0)))
        return tuple(out)

    cnts = lax.fori_loop(0, n_sb, body, tuple(jnp.zeros((nr, LANES), F32) for _ in psums))
    starts_before = s_idx * L_SLC <= tpos_row
    biases = []
    for g, cnt in enumerate(cnts):
        bias_t = jnp.where((cnt < float(N_SEL)) & starts_before, 0.0, NEG)
        if nr < sp:
            bias_t = jnp.concatenate([bias_t, jnp.full((sp - nr, LANES), NEG, F32)], axis=0)
        bias = bias_t.T[:tq]
        biases.append(pltpu.roll(bias, HEAD_DIM, axis=1) if g == 0 else bias)
    return biases


def _select_ids(psum, mt_ref, tpos):
    sp = mt_ref.shape[0]
    prow = jnp.broadcast_to(psum[0:1], (LANES, psum.shape[1]))
    imp = _importance(prow, mt_ref, jnp.full((1, LANES), tpos, jnp.int32))
    a = jnp.broadcast_to(imp[:, 0:1], (sp, sp))
    b = a.T
    r_i = lax.broadcasted_iota(jnp.int32, (sp, sp), 0)
    c_i = lax.broadcasted_iota(jnp.int32, (sp, sp), 1)
    beats = jnp.where(a > b, 1.0, jnp.where(a == b, jnp.where(r_i < c_i, 1.0, 0.0), 0.0))
    rank = jnp.sum(beats, axis=0, keepdims=True)
    slot = lax.broadcasted_iota(jnp.int32, (N_SEL, sp), 0).astype(F32)
    blk = lax.broadcasted_iota(jnp.int32, (N_SEL, sp), 1).astype(F32)
    ids = jnp.sum(jnp.where(rank == slot, blk, 0.0), axis=1, keepdims=True)
    return jnp.broadcast_to(ids, (N_SEL, LANES)).astype(jnp.int32)


def _flash_init(tq):
    return (jnp.full((HPG, tq, 1), NEG, F32), jnp.zeros((HPG, tq, 1), F32), jnp.zeros((HPG * tq, LANES), F32))


def _flash_tile(carry, q, k, v, bias, tq):
    m, l, acc = carry
    kt = k.shape[0]
    s = _dot_nt(q, k).reshape(HPG, tq, kt)
    if bias is not None:
        s = s + bias[None]
    m_new = jnp.maximum(m, jnp.max(s, axis=-1, keepdims=True))
    alpha = jnp.exp(m - m_new)
    p = jnp.exp(s - m_new)
    l = alpha * l + jnp.sum(p, axis=-1, keepdims=True)
    acc = alpha.reshape(HPG * tq, 1) * acc + _dot(p.reshape(HPG * tq, kt).astype(BF), v)
    return m_new, l, acc


def _flash_out(carry, tq):
    _, l, acc = carry
    return acc / l.reshape(HPG * tq, 1)


def _attend_once(q, k, v, bias, tq):
    kt = k.shape[0]
    s = _dot_nt(q, k).reshape(HPG, tq, kt) + bias[None]
    p = jnp.exp(s - jnp.max(s, axis=-1, keepdims=True))
    l = jnp.sum(p, axis=-1, keepdims=True)
    return _dot(p.reshape(HPG * tq, kt).astype(BF), v) / l.reshape(HPG * tq, 1)


def _mask_bias(valid):
    return jnp.where(valid, 0.0, NEG)


def _win_valid(kpos_row, tpos_col):
    dist = tpos_col - kpos_row
    return (dist >= 0) & (dist < WINDOW)


def _merge_heads(o_c, o_s, o_w, gt_ref, g, o_ref, tq):
    lane = lax.broadcasted_iota(jnp.int32, (tq, LANES), 1)
    for pair in range(HPG // 2):
        both = []
        for h in (2 * pair, 2 * pair + 1):
            col = (g * HPG + h) * 3
            rows = slice(h * tq, (h + 1) * tq)
            both.append(o_c(rows) * gt_ref[:, col:col + 1] + o_s(rows) * gt_ref[:, col + 1:col + 2]
                        + o_w(rows) * gt_ref[:, col + 2:col + 3])
        even, odd = both
        if g == 0:
            odd = pltpu.roll(odd, HEAD_DIM, axis=1)
        else:
            even = pltpu.roll(even, HEAD_DIM, axis=1)
        ch = g * (HPG // 2) + pair
        o_ref[:, ch * LANES:(ch + 1) * LANES] = jnp.where(lane < HEAD_DIM, even, odd).astype(o_ref.dtype)


def _staged_tile(state, stage, q, k, v, bias, refs, tq, first):
    s_ref, p_ref, m_ref, acc_ref = refs
    kt = k.shape[0]
    reps = kt // LANES
    s_ref[stage, :, 0:kt] = _dot_nt(q, k)
    for h in range(HPG):
        rows = slice(h * tq, (h + 1) * tq)

        def scores():
            s = s_ref[stage, rows, 0:kt]
            return s if bias is None else s + bias

        m_tile = jnp.max(scores(), axis=-1, keepdims=True)
        if first:
            m_new = jnp.broadcast_to(m_tile, (tq, LANES))
        else:
            m_old = m_ref[state, rows, :]
            m_new = jnp.maximum(m_old, m_tile)
            acc_ref[state, rows, :] = jnp.exp(m_old - m_new) * acc_ref[state, rows, :]
        m_ref[state, rows, :] = m_new
        p_ref[stage, rows, 0:kt] = jnp.exp(scores() - jnp.concatenate([m_new] * reps, axis=1)).astype(BF)
    pv = _dot(p_ref[stage, :, 0:kt], v)
    if first:
        acc_ref[state] = pv
    else:
        acc_ref[state] = acc_ref[state] + pv


def _attn_prompt_kernel(qs_ref, qrs_ref, gt_ref, kc_ref, vc_ref, ka_ref, sva_ref, win_ref, wva_ref, mt_ref, o_ref,
                        imp_ref, s_ref, p_ref, m_ref, acc_ref, oc_ref, *, n_cb, n_sb):
    tq = qs_ref.shape[0]
    t_all = ka_ref.shape[0]
    qb = pl.program_id(1)
    t0 = qb * tq
    tpos_col = t0 + lax.broadcasted_iota(jnp.int32, (tq, 1), 0)
    tpos_row = t0 + lax.broadcasted_iota(jnp.int32, (1, LANES), 1)
    groups = range(N_KV_HEADS)
    k_lanes, v_lanes = slice(0, LANES), slice(LANES, 2 * LANES)

    psums = [_cmp_branch_staged(g, _stack_heads(qs_ref, g, tq), kc_ref[...], vc_ref[...], tpos_col, n_cb,
                                s_ref, p_ref, oc_ref, tq) for g in groups]
    blk_bias = _select_blocks(psums, mt_ref, imp_ref, tpos_row, n_sb, tq)
    qr_blk = [_stack_heads(qrs_ref, g, tq, fill=blk_bias[g]) for g in groups]

    refs = (s_ref, p_ref, m_ref, acc_ref)
    g_lanes = [slice(g * LANES, (g + 1) * LANES) for g in groups]

    def slc_tile(j, bias, first):
        rows = pl.ds(pl.multiple_of(j * SLC_TILE, SLC_TILE), SLC_TILE)
        for g in groups:
            _staged_tile(g, g, qr_blk[g], ka_ref[rows, g_lanes[g]], sva_ref[rows, g_lanes[g]], bias, refs, tq, first)

    last = (t0 + tq - 1) // SLC_TILE
    causal = (last * SLC_TILE + lax.broadcasted_iota(jnp.int32, (1, SLC_TILE), 1)) <= tpos_col
    slc_tile(last, _mask_bias(causal), True)

    def slc_body(j, c):
        slc_tile(j, None, False)
        return c

    lax.fori_loop(0, last, slc_body, 0)

    wk = min(WINDOW + tq, t_all)
    start = pl.multiple_of(jnp.clip(t0 - WINDOW, 0, t_all - wk), tq)
    wbias = _mask_bias(_win_valid(start + lax.broadcasted_iota(jnp.int32, (1, wk), 1), tpos_col))
    wrows = pl.ds(start, wk)
    kw = win_ref[wrows, k_lanes]
    for g in groups:
        _staged_tile(N_KV_HEADS + g, g, _stack_heads(qrs_ref, g, tq), kw, wva_ref[wrows, g_lanes[g]], wbias, refs,
                     tq, True)

    def normalised(slot, rows):
        acc = acc_ref[slot, rows, :]
        return acc / pltpu.roll(acc, HEAD_DIM, axis=1)

    for g in groups:
        _merge_heads(lambda rows: oc_ref[g, rows, :], functools.partial(normalised, g),
                     functools.partial(normalised, N_KV_HEADS + g), gt_ref, g, o_ref, tq)


def _attn_prompt(qs, qrs, gates, kc, vc, ka_bf, slc_va, win_bf, win_va, mt, b, t, n_cb, n_sb):
    tq = Q_BLOCK
    nqb = t // tq
    d = qs.shape[1]
    stage_w = max(SLC_TILE, min(WINDOW + tq, t))
    row = lambda w: pl.BlockSpec((tq, w), lambda bi, qi: (bi * nqb + qi, 0))
    per_b = lambda a: pl.BlockSpec((None,) + a.shape[1:], lambda bi, qi: (bi, 0, 0))
    return pl.pallas_call(
        functools.partial(_attn_prompt_kernel, n_cb=n_cb, n_sb=n_sb),
        grid=(b, nqb),
        in_specs=[row(d), row(d), row(LANES), per_b(kc), per_b(vc), per_b(ka_bf), per_b(slc_va), per_b(win_bf),
                  per_b(win_va), _const_spec(mt.shape)],
        out_specs=row(d),
        out_shape=jax.ShapeDtypeStruct((b * t, d), BF),
        scratch_shapes=[pltpu.VMEM((N_KV_HEADS, mt.shape[0], LANES), F32),
                        pltpu.VMEM((N_KV_HEADS, HPG * tq, stage_w), F32),
                        pltpu.VMEM((N_KV_HEADS, HPG * tq, stage_w), BF),
                        pltpu.VMEM((2 * N_KV_HEADS, HPG * tq, LANES), F32),
                        pltpu.VMEM((2 * N_KV_HEADS, HPG * tq, LANES), F32),
                        pltpu.VMEM((N_KV_HEADS, HPG * tq, LANES), F32)],
        compiler_params=_cparams(2), name="attn_prompt",
    )(qs, qrs, gates, kc, vc, ka_bf, slc_va, win_bf, win_va, mt)


def _sample_select_kernel(qs_ref, kc_ref, vc_ref, mt_ref, oc_ref, ids_ref, *, n_cb, past):
    tq = qs_ref.shape[0]
    tpos_col = jnp.full((tq, 1), past, jnp.int32)
    for g in range(N_KV_HEADS):
        o_c, psum = _cmp_branch(_stack_heads(qs_ref, g, tq), kc_ref[...], vc_ref[...], tpos_col, n_cb, tq)
        oc_ref[g] = o_c
        ids_ref[g] = _select_ids(psum, mt_ref, past)


def _sample_select(qs, kc, vc, mt, n_cb, past):
    db, tq, _ = qs.shape
    per_b = lambda a: pl.BlockSpec((None,) + a.shape[1:], lambda bi: (bi,) + (0,) * (a.ndim - 1))
    oc = jax.ShapeDtypeStruct((db, N_KV_HEADS, HPG * tq, LANES), F32)
    ids = jax.ShapeDtypeStruct((db, N_KV_HEADS, N_SEL, LANES), jnp.int32)
    return pl.pallas_call(
        functools.partial(_sample_select_kernel, n_cb=n_cb, past=past),
        grid=(db,),
        in_specs=[per_b(qs), per_b(kc), per_b(vc), _const_spec(mt.shape)],
        out_specs=[per_b(oc), per_b(ids)],
        out_shape=[oc, ids],
        compiler_params=_cparams(1), name="sample_select",
    )(qs, kc, vc, mt)


def _sample_attend_kernel(pt_ref, ids_ref, qrs_ref, gt_ref, oc_ref, slct_ref, win_ref, wint_ref, *rest, past, n_cached):
    del pt_ref
    blk_refs, o_ref, k_scr, v_scr = rest[:-3], rest[-3], rest[-2], rest[-1]
    b = pl.program_id(0)
    tq = qrs_ref.shape[0]
    tpos_col = jnp.full((tq, 1), past, jnp.int32)
    w_buf = win_ref.shape[0]
    tail = slct_ref.shape[0]
    k_lanes, v_lanes = slice(0, LANES), slice(LANES, 2 * LANES)
    tail_pos = past + lax.broadcasted_iota(jnp.int32, (1, tail), 1)
    win_pos = past - w_buf + lax.broadcasted_iota(jnp.int32, (1, w_buf), 1)
    wbias = jnp.broadcast_to(_mask_bias(_win_valid(win_pos, tpos_col)), (tq, w_buf))
    wtbias = jnp.broadcast_to(_mask_bias(_win_valid(tail_pos, tpos_col)), (tq, tail))
    tbias = jnp.broadcast_to(_mask_bias(tail_pos <= tpos_col), (tq, tail))
    kw, vw = win_ref[:, k_lanes].astype(BF), win_ref[:, v_lanes].astype(BF)
    lane = lax.broadcasted_iota(jnp.int32, (1, LANES), 1)
    for g in range(N_KV_HEADS):
        for r in range(N_SEL):
            blk = blk_refs[g * N_SEL + r]
            k_scr[g, r * L_SLC:(r + 1) * L_SLC, :] = blk[:, k_lanes].astype(BF)
            v_scr[g, r * L_SLC:(r + 1) * L_SLC, :] = blk[:, v_lanes].astype(BF)
        chunks = []
        for c in range(N_SEL // 2):
            lo = jnp.where(ids_ref[b, g * N_SEL + 2 * c] >= n_cached, NEG, 0.0)
            hi = jnp.where(ids_ref[b, g * N_SEL + 2 * c + 1] >= n_cached, NEG, 0.0)
            chunks.append(jnp.where(lane < L_SLC, lo, hi))
        sbias = jnp.broadcast_to(jnp.concatenate(chunks, axis=1), (tq, N_SEL * L_SLC))
        qr = _stack_heads(qrs_ref, g, tq)
        carry = _flash_tile(_flash_init(tq), qr, k_scr[g], v_scr[g], sbias, tq)
        carry = _flash_tile(carry, qr, slct_ref[:, k_lanes], slct_ref[:, v_lanes], tbias, tq)
        o_s = _flash_out(carry, tq)
        carry = _flash_tile(_flash_init(tq), qr, kw, vw, wbias, tq)
        carry = _flash_tile(carry, qr, wint_ref[:, k_lanes], wint_ref[:, v_lanes], wtbias, tq)
        o_w = _flash_out(carry, tq)
        _merge_heads(lambda rows: oc_ref[g, rows, :], lambda rows: o_s[rows], lambda rows: o_w[rows], gt_ref, g,
                     o_ref, tq)


def _sample_attend(qrs, gates, oc, slc_cache, page_table, ids, slc_tail, win_c, win_tail, past):
    db = qrs.shape[0]
    per_page = PAGE_SIZE // L_SLC
    n_cached = past // L_SLC
    blocks = slc_cache.reshape(-1, L_SLC, slc_cache.shape[2])
    per_b = lambda a: pl.BlockSpec((None,) + a.shape[1:], lambda bi, pt, sel: (bi,) + (0,) * (a.ndim - 1))

    def blk_map(bi, pt, sel, g, r):
        s = jnp.clip(sel[bi, g * N_SEL + r], 0, n_cached - 1)
        return pt[bi, s // per_page] * per_page + s % per_page, 0, 0

    blk_specs = [pl.BlockSpec((None, L_SLC, blocks.shape[2]), functools.partial(blk_map, g=g, r=r))
                 for g in range(N_KV_HEADS) for r in range(N_SEL)]
    args = (qrs, gates, oc, slc_tail, win_c, win_tail)
    return pl.pallas_call(
        functools.partial(_sample_attend_kernel, past=past, n_cached=n_cached),
        grid_spec=pltpu.PrefetchScalarGridSpec(
            num_scalar_prefetch=2, grid=(db,),
            in_specs=[per_b(a) for a in args] + blk_specs,
            out_specs=per_b(qrs),
            scratch_shapes=[pltpu.VMEM((N_KV_HEADS, N_SEL * L_SLC, LANES), BF)] * 2),
        out_shape=jax.ShapeDtypeStruct(qrs.shape, BF),
        compiler_params=_cparams(1), name="sample_attend",
    )(page_table, ids, *args, *([blocks] * len(blk_specs)))


def _block_diag(blocks):
    n, k, j = blocks.shape
    eye = jnp.eye(n, dtype=blocks.dtype)
    return (eye[:, None, :, None] * blocks[:, :, None, :]).reshape(n * k, n * j)


def _pad_rows8(rows):
    return jnp.pad(rows, ((0, 0), (SUBLANES - rows.shape[1], 0), (0, 0)))


def _rope_tables(pos):
    half = HEAD_DIM // 2
    inv_freq = ROPE_THETA ** (-jnp.arange(half, dtype=F32) / half)
    ang = pos.astype(F32)[:, None] * inv_freq[None, :]
    cos, sin = jnp.cos(ang), jnp.sin(ang)
    reps = LANES // HEAD_DIM
    return (jnp.tile(jnp.concatenate([cos, cos], axis=1), (1, reps)),
            jnp.tile(jnp.concatenate([-sin, sin], axis=1), (1, reps)))


def _block_onehot(pos):
    lane = jnp.arange(LANES, dtype=jnp.int32)[None, :] % HEAD_DIM
    return (lane == (pos[:, None] // L_SLC)).astype(F32)


def _overlap_t(n_cb, n_sb, ncp, sp):
    c0 = np.arange(ncp)[None, :] * D_CMP
    s0 = np.arange(sp)[:, None] * L_SLC
    m = (c0 < s0 + L_SLC) & (c0 + L_CMP > s0) & (np.arange(ncp)[None, :] < n_cb) & (np.arange(sp)[:, None] < n_sb)
    return jnp.asarray(m, dtype=BF)


def _round_up(x, m):
    return (x + m - 1) // m * m


def kernel(x_prompt, x_sample, p_prompt, p_sample, cache_cmp_kv, cache_slc_kv, cache_win_kv, state_rg_conv,
           state_rg_h, state_ffn_conv, page_table, g_mix, g_ffn, g_ple, g_final, rg_w_in, rg_conv_w, rg_conv_b,
           rg_w_a, rg_b_a, rg_w_x, rg_b_x, rg_lambda, rg_w_out, g_kv, w_kv, cmp_pos, cmp_w1, cmp_b1, cmp_w2,
           attn_w_qg, attn_w_o, ffn_w_up, ffn_conv_w, ffn_conv_b, ffn_w_down, ple_w_in, ple_w_gate):
    b, t, d = x_prompt.shape
    db = x_sample.shape[0]
    assert x_sample.shape[1] == 1, "the sample path handles one new token per sequence"
    depth = g_mix.shape[0]
    n_a = rg_w_in.shape[0]
    d_rnn = rg_w_out.shape[1]
    n_pages = page_table.shape[1]
    past = n_pages * PAGE_SIZE
    kvc = 2 * N_KV_HEADS * HEAD_DIM
    n_q = N_KV_HEADS * HPG * HEAD_DIM

    row1 = lambda v: v.reshape(1, -1)
    pad8 = lambda w: jnp.pad(w, ((0, SUBLANES - w.shape[0]), (0, 0)))

    rg_in_bf = rg_w_in.astype(BF)
    rg_out_bf = rg_w_out.astype(BF)
    rg_gate_bf = [jnp.concatenate([_block_diag(rg_w_a[i]), _block_diag(rg_w_x[i])], axis=1).astype(BF)
                  for i in range(n_a)]
    rg_gate_b = [row1(jnp.concatenate([rg_b_a[i].reshape(-1), rg_b_x[i].reshape(-1)])) for i in range(n_a)]
    rg_c = [row1(-RG_C * jax.nn.softplus(-rg_lambda[i])) for i in range(n_a)]
    up_bf = ffn_w_up.astype(BF)
    down_bf = ffn_w_down.astype(BF)
    gate_bf = ple_w_gate.astype(BF)
    pin_bf = ple_w_in.astype(BF)
    wkv_bf = w_kv.astype(BF)
    wq_bf = attn_w_qg[:, :, :n_q].astype(BF)
    wgl_bf = jnp.pad(attn_w_qg[:, :, n_q:], ((0, 0), (0, 0), (0, LANES - 3 * N_KV_HEADS * HPG))).astype(BF)
    wo_bf = attn_w_o.astype(BF)
    gfin = row1(g_final)

    w1 = cmp_w1.reshape(2, 2, D_CMP, HEAD_DIM, -1)
    eye_g = jnp.eye(N_KV_HEADS, dtype=F32)
    w1_bd = (eye_g[None, None, None, :, None, :, None] * w1[:, :, :, None, :, None, :]).reshape(
        2, 2, D_CMP, N_KV_HEADS * HEAD_DIM, -1).astype(BF)
    pos_t = jnp.tile(jnp.transpose(cmp_pos, (1, 0, 2)), (1, 1, N_KV_HEADS)).reshape(2, 2, D_CMP, LANES)
    cmp_params = (pos_t[:, 0], pos_t[:, 1], w1_bd[:, 0], w1_bd[:, 1],
                  jnp.tile(cmp_b1, (1, N_KV_HEADS)).reshape(2, 1, -1),
                  jnp.stack([_block_diag(jnp.stack([cmp_w2[j]] * N_KV_HEADS)) for j in range(2)]).astype(BF))

    def layer_consts(i, final):
        return (row1(g_ffn[i]), up_bf[i], pad8(ffn_conv_w[i]), row1(ffn_conv_b[i]), down_bf[i], row1(g_ple[i]),
                gate_bf[i], pin_bf[i], gfin), final

    n = b * t
    x = x_prompt.reshape(n, d)
    p = p_prompt.reshape(depth, n, -1)
    pos_p = jnp.tile(jnp.arange(t, dtype=jnp.int32), b)
    cos_p, sin_p = _rope_tables(pos_p)
    rg_conv_p, rg_h_p, ffn_conv_p = [], [], []
    zeros_rg = jnp.zeros((b, SUBLANES, d_rnn), F32)
    zeros_ffn = jnp.zeros((b, SUBLANES, ffn_w_up.shape[2]), F32)
    for i in range(depth):
        if i == n_a:
            kv_p, _, win_bf, ka_bf, slc_va, win_va = _kv_proj(x, row1(g_kv), wkv_bf, cos_p, sin_p,
                                                              _block_onehot(pos_p))
            kc_p, vc_p = _compress(kv_p.reshape(b, t, -1), cmp_params, b, t)
            n_cb_p = (t - L_CMP) // D_CMP + 1
            n_sb_p = t // L_SLC
            mt_p = _overlap_t(n_cb_p, n_sb_p, t // D_CMP, _round_up(n_sb_p, LANES))
            ka_bf, slc_va, win_bf, win_va = (a.reshape(b, t, -1) for a in (ka_bf, slc_va, win_bf, win_va))
        if i < n_a:
            x, h_last, xr_last = _rg_mixer(x, row1(g_mix[i]), rg_in_bf[i], zeros_rg, zeros_rg, pad8(rg_conv_w[i]),
                                           row1(rg_conv_b[i]), rg_gate_bf[i], rg_gate_b[i], rg_c[i], rg_out_bf[i],
                                           b, t)
            rg_conv_p.append(xr_last[:, SUBLANES - 3:])
            rg_h_p.append(h_last[:, 0])
        else:
            j = i - n_a
            qs, qrs, gates = _q_proj(x, row1(g_mix[i]), wq_bf[j], wgl_bf[j], cos_p, sin_p)
            o = _attn_prompt(qs, qrs, gates, kc_p, vc_p, ka_bf, slc_va, win_bf, win_va, mt_p, b, t, n_cb_p, n_sb_p)
            x = _matmul_res(o, wo_bf[j], x)
        consts, final = layer_consts(i, i == depth - 1)
        x, f_hist = _ffn(x, p, i, zeros_ffn, *consts, b, t, final)
        ffn_conv_p.append(f_hist[:, SUBLANES - 2:])
    y_prompt = x.reshape(b, t, d)
    kv5 = kv_p.reshape(b, t, 6, N_KV_HEADS, HEAD_DIM)
    keep = min(WINDOW, t)
    cmp_p, slc_p, win_p = kv5[:, :, 0:2], kv5[:, :, 2:4], kv5[:, t - keep:, 4:6]

    x = x_sample.reshape(db, d)
    p = p_sample.reshape(depth, db, -1)
    cos_s, sin_s = _rope_tables(jnp.full((db,), past, jnp.int32))
    rg_conv_s, rg_h_s, ffn_conv_s = [], [], []
    for i in range(depth):
        if i == n_a:
            pos_s = jnp.full((db,), past, jnp.int32)
            kv_s, slc_new_bf, win_new_bf = _kv_proj(x, row1(g_kv), wkv_bf, cos_s, sin_s, _block_onehot(pos_s))[:3]
            kc_s, vc_s = _compress_paged(cache_cmp_kv.reshape(-1, PAGE_SIZE, kvc), page_table, cmp_params)
            slc_cache = cache_slc_kv.reshape(-1, PAGE_SIZE, kvc)
            n_cb_s = (past + 1 - L_CMP) // D_CMP + 1
            n_sb_s = -(-(past + 1) // L_SLC)
            sp_s = _round_up(n_sb_s, LANES)
            mt_s = _overlap_t(n_cb_s, n_sb_s, past // D_CMP, sp_s)
            tail = lambda a: jnp.pad(a.reshape(db, 1, -1), ((0, 0), (0, LANES - 1), (0, 0)))
            slc_tail, win_tail = tail(slc_new_bf), tail(win_new_bf)
            win_c = cache_win_kv.reshape(db, cache_win_kv.shape[1], kvc)
        if i < n_a:
            yg, xr = _rg_in(x, row1(g_mix[i]), rg_in_bf[i])
            hist = state_rg_conv[i]
            gated, h_new = _rg_step(xr, yg, hist[:, 0], hist[:, 1], hist[:, 2], state_rg_h[i], pad8(rg_conv_w[i]),
                                    row1(rg_conv_b[i]), rg_gate_bf[i], rg_gate_b[i], rg_c[i])
            x = _matmul_res(gated, rg_out_bf[i], x)
            rg_conv_s.append(jnp.concatenate([hist[:, 1:], xr[:, None, :]], axis=1))
            rg_h_s.append(h_new)
        else:
            j = i - n_a
            qs, qrs, gates = _q_proj(x, row1(g_mix[i]), wq_bf[j], wgl_bf[j], cos_s, sin_s)
            rep = lambda a: jnp.broadcast_to(a[:, None, :], (db, SUBLANES, a.shape[1]))
            oc, ids = _sample_select(rep(qs), kc_s, vc_s, mt_s, n_cb_s, past)
            o = _sample_attend(rep(qrs), rep(gates), oc, slc_cache, page_table, ids[:, :, :, 0].reshape(db, -1),
                               slc_tail, win_c,
                               win_tail, past)
            x = _matmul_res(o[:, 0], wo_bf[j], x)
        consts, final = layer_consts(i, i == depth - 1)
        fh = state_ffn_conv[i]
        x, up = _ffn_step(x, p[i], fh[:, 0], fh[:, 1], *consts, final)
        ffn_conv_s.append(jnp.stack([fh[:, 1], up], axis=1))
    y_sample = x.reshape(db, 1, d)
    kv5 = kv_s.reshape(db, 1, 6, N_KV_HEADS, HEAD_DIM)
    cmp_s, slc_s, win_new = kv5[:, :, 0:2], kv5[:, :, 2:4], kv5[:, :, 4:6]
    win_full = jnp.concatenate([cache_win_kv, win_new], axis=1)
    keep = min(WINDOW, past + 1)
    win_s = win_full[:, win_full.shape[1] - keep:]

    return (y_prompt, y_sample, cmp_p, cmp_s, slc_p, slc_s, win_p, win_s,
            jnp.stack(rg_conv_p), jnp.stack(rg_conv_s), jnp.stack(rg_h_p), jnp.stack(rg_h_s),
            jnp.stack(ffn_conv_p), jnp.stack(ffn_conv_s))
```

```python
import functools

import jax
import jax.numpy as jnp
import numpy as np
from jax import lax
from jax.experimental import pallas as pl
from jax.experimental.pallas import tpu as pltpu

BF = jnp.bfloat16
F32 = jnp.float32

EPS = 1e-6
RG_C = 8.0
N_RG_BLOCKS = 16
HEAD_DIM = 64
N_KV_HEADS = 2
HPG = 8
L_CMP = 32
D_CMP = 16
L_SLC = 64
N_SEL = 16
WINDOW = 512
Q_BLOCK = 128
ROPE_THETA = 10000.0
PAGE_SIZE = 128
NEG = -1e30
LANES = 128
SUBLANES = 8
SLC_TILE = 512
WIN_TILE = 128
VMEM_LIMIT = 56 * 1024 * 1024


def _cparams(n_axes):
    return pltpu.CompilerParams(dimension_semantics=("arbitrary",) * n_axes,
                                vmem_limit_bytes=VMEM_LIMIT)


def _const_spec(shape):
    zeros = (0,) * len(shape)
    return pl.BlockSpec(shape, lambda *_: zeros, pipeline_mode=pl.Buffered(1))


def _dot(a, b):
    return jnp.dot(a, b, preferred_element_type=F32)


def _dot_nt(a, b):
    return lax.dot_general(a, b, (((1,), (1,)), ((), ())), preferred_element_type=F32)


def _rms(x, g):
    return x * lax.rsqrt(jnp.mean(x * x, axis=-1, keepdims=True) + EPS) * g


def _gelu(x):
    return jax.nn.gelu(x, approximate=True)


def _sigmoid(x):
    return jax.nn.sigmoid(x)


def _shift_rows(x, prev8, j):
    if j == 0:
        return x
    r = pltpu.roll(x, j, axis=0)
    p = pltpu.roll(prev8, j, axis=0)
    row = lax.broadcasted_iota(jnp.int32, (SUBLANES, x.shape[1]), 0)
    head = jnp.where(row < j, p, r[:SUBLANES])
    if x.shape[0] == SUBLANES:
        return head
    return jnp.concatenate([head, r[SUBLANES:]], axis=0)


def _rope(z, cos, sin_signed):
    lane = lax.broadcasted_iota(jnp.int32, z.shape, 1)
    first = (lane & (HEAD_DIM - 1)) < (HEAD_DIM // 2)
    rot = jnp.where(first, pltpu.roll(z, LANES - HEAD_DIM // 2, axis=1), pltpu.roll(z, HEAD_DIM // 2, axis=1))
    return z * cos + rot * sin_signed


def _rg_in_kernel(x_ref, g_ref, w_ref, yg_ref, xr_ref, *, chunk):
    h = _rms(x_ref[...], g_ref[...]).astype(BF)
    n = yg_ref.shape[1]
    for c0 in range(0, n, chunk):
        yg_ref[:, c0:c0 + chunk] = _gelu(_dot(h, w_ref[:, c0:c0 + chunk]))
        xr_ref[:, c0:c0 + chunk] = _dot(h, w_ref[:, n + c0:n + c0 + chunk])


def _rg_in(x, g, w_bf):
    n, d = x.shape
    dr = w_bf.shape[1] // 2
    tm = min(512, n)
    return pl.pallas_call(
        functools.partial(_rg_in_kernel, chunk=256),
        grid=(n // tm,),
        in_specs=[pl.BlockSpec((tm, d), lambda i: (i, 0)), _const_spec((1, d)), _const_spec(w_bf.shape)],
        out_specs=[pl.BlockSpec((tm, dr), lambda i: (i, 0)), pl.BlockSpec((tm, dr), lambda i: (i, 0))],
        out_shape=[jax.ShapeDtypeStruct((n, dr), F32), jax.ShapeDtypeStruct((n, dr), F32)],
        compiler_params=_cparams(1), name="rg_in",
    )(x, g, w_bf)


def _rg_gates(xc, wg_ref, bg_ref, c_ref):
    d = xc.shape[1]
    gates = _dot(xc.astype(BF), wg_ref[...]) + bg_ref[...]
    r = _sigmoid(gates[:, :d])
    i = _sigmoid(gates[:, d:])
    log_a = c_ref[...] * r
    a = jnp.exp(log_a)
    z = -jnp.tanh(log_a) * (a * a + 1.0)
    mult = jnp.where(z > 0.0, z * lax.rsqrt(z), 0.0)
    return a, mult, i * xc


def _rg_mixer_kernel(xin_ref, gm_ref, win_ref, hist_ref, h0_ref, cw_ref, cb_ref, wg_ref, bg_ref, c_ref, wout_ref,
                     out_ref, hl_ref, xl_ref, xprev, hc, *, sub):
    t = pl.program_id(1)

    @pl.when(t == 0)
    def _():
        xprev[...] = hist_ref[...]
        hc[...] = h0_ref[...]

    d = wout_ref.shape[0]
    prev = xprev[...]
    carry = hc[0:1, :]
    row = lax.broadcasted_iota(jnp.int32, (sub, d), 0)
    pieces = range(0, xin_ref.shape[0], sub)
    projs = []
    for r0 in pieces:
        hn = _rms(xin_ref[r0:r0 + sub, :], gm_ref[...]).astype(BF)
        projs.append((_dot(hn, win_ref[:, 0:d]), _dot(hn, win_ref[:, d:2 * d])))
    for (yg_pre, x), r0 in zip(projs, pieces):
        x_in = xin_ref[r0:r0 + sub, :]
        yg = _gelu(yg_pre)
        xc = cb_ref[...] + _shift_rows(x, prev, 3) * cw_ref[0:1, :]
        xc = xc + _shift_rows(x, prev, 2) * cw_ref[1:2, :]
        xc = xc + _shift_rows(x, prev, 1) * cw_ref[2:3, :]
        xc = xc + x * cw_ref[3:4, :]
        prev = x[sub - SUBLANES:, :]

        a, mult, ixc = _rg_gates(xc, wg_ref, bg_ref, c_ref)
        if r0 == 0:
            mult = jnp.where((row == 0) & (t == 0), 1.0, mult)
        u = mult * ixc

        av, bv = a, u
        dist = 1
        while dist < sub:
            if dist < SUBLANES:
                ok = row >= dist
                bv = jnp.where(ok, av * pltpu.roll(bv, dist, axis=0) + bv, bv)
                av = jnp.where(ok, av * pltpu.roll(av, dist, axis=0), av)
            else:
                bv = jnp.concatenate([bv[:dist], av[dist:] * bv[:sub - dist] + bv[dist:]], axis=0)
                av = jnp.concatenate([av[:dist], av[dist:] * av[:sub - dist]], axis=0)
            dist *= 2
        h = av * carry + bv
        carry = h[sub - 1:sub, :]
        out_ref[r0:r0 + sub, :] = x_in + _dot((yg * h).astype(BF), wout_ref[...])
    xprev[...] = prev
    hc[...] = jnp.broadcast_to(carry, (SUBLANES, d))
    hl_ref[...] = hc[...]
    xl_ref[...] = prev


def _rg_mixer(x, gm, win_bf, hist8, h08, cw8, cb, wg_bf, bg, c, wout_bf, b, t):
    n, dm = x.shape
    d = wout_bf.shape[0]
    tt = min(512, t)
    nt = t // tt
    row_spec = pl.BlockSpec((tt, dm), lambda bi, ti: (bi * nt + ti, 0))
    st_spec = pl.BlockSpec((None, SUBLANES, d), lambda bi, ti: (bi, 0, 0))
    consts = (cw8, cb, wg_bf, bg, c, wout_bf)
    return pl.pallas_call(
        functools.partial(_rg_mixer_kernel, sub=min(256, tt)),
        grid=(b, nt),
        in_specs=[row_spec, _const_spec(gm.shape), _const_spec(win_bf.shape), st_spec, st_spec]
        + [_const_spec(a.shape) for a in consts],
        out_specs=[row_spec, st_spec, st_spec],
        out_shape=[jax.ShapeDtypeStruct((n, dm), F32), jax.ShapeDtypeStruct((b, SUBLANES, d), F32),
                   jax.ShapeDtypeStruct((b, SUBLANES, d), F32)],
        scratch_shapes=[pltpu.VMEM((SUBLANES, d), F32), pltpu.VMEM((SUBLANES, d), F32)],
        compiler_params=_cparams(2), name="rg_mixer",
    )(x, gm, win_bf, hist8, h08, *consts)


def _rg_step_kernel(xr_ref, yg_ref, h0_ref, h1_ref, h2_ref, hs_ref, cw_ref, cb_ref, wg_ref, bg_ref, c_ref,
                    out_ref, hn_ref):
    x = xr_ref[...]
    xc = cb_ref[...] + h0_ref[...] * cw_ref[0:1, :]
    xc = xc + h1_ref[...] * cw_ref[1:2, :]
    xc = xc + h2_ref[...] * cw_ref[2:3, :]
    xc = xc + x * cw_ref[3:4, :]
    a, mult, ixc = _rg_gates(xc, wg_ref, bg_ref, c_ref)
    h = mult * ixc + a * hs_ref[...]
    hn_ref[...] = h
    out_ref[...] = (yg_ref[...] * h).astype(BF)


def _rg_step(xr, yg, h0, h1, h2, hs, cw8, cb, wg_bf, bg, c):
    n, d = xr.shape
    args = (xr, yg, h0, h1, h2, hs, cw8, cb, wg_bf, bg, c)
    return pl.pallas_call(
        _rg_step_kernel,
        grid=(1,),
        in_specs=[_const_spec(a.shape) for a in args],
        out_specs=[pl.BlockSpec((n, d), lambda i: (0, 0)), pl.BlockSpec((n, d), lambda i: (0, 0))],
        out_shape=[jax.ShapeDtypeStruct((n, d), BF), jax.ShapeDtypeStruct((n, d), F32)],
        compiler_params=_cparams(1), name="rg_step",
    )(*args)


def _matmul_res_kernel(a_ref, w_ref, x_ref, o_ref):
    o_ref[...] = x_ref[...] + _dot(a_ref[...], w_ref[...])


def _matmul_res(a_bf, w_bf, x):
    n, k = a_bf.shape
    d = w_bf.shape[1]
    tm = min(512, n)
    return pl.pallas_call(
        _matmul_res_kernel,
        grid=(n // tm,),
        in_specs=[pl.BlockSpec((tm, k), lambda i: (i, 0)), _const_spec(w_bf.shape),
                  pl.BlockSpec((tm, d), lambda i: (i, 0))],
        out_specs=pl.BlockSpec((tm, d), lambda i: (i, 0)),
        out_shape=jax.ShapeDtypeStruct((n, d), F32),
        compiler_params=_cparams(1), name="matmul_res",
    )(a_bf, w_bf, x)


def _ffn_tail(x, acc, p_ref, gp_ref, wgate_ref, wpin_ref, gfin_ref, final_norm):
    x1 = x + acc
    hp = _rms(x1, gp_ref[...]).astype(BF)
    sig = _sigmoid(_dot(hp, wgate_ref[...]))
    pe = _dot(p_ref[...].astype(BF), wpin_ref[...])
    x2 = x1 + pe * sig
    if final_norm:
        x2 = _rms(x2, gfin_ref[...])
    return x2


def _ffn_kernel(x_ref, p_ref, hist_ref, gf_ref, wup_ref, cw_ref, cb_ref, wdn_ref, gp_ref, wgate_ref, wpin_ref,
                gfin_ref, o_ref, ho_ref, uprev, *, chunk, final_norm):
    t = pl.program_id(1)

    @pl.when(t == 0)
    def _():
        uprev[...] = hist_ref[...]

    x = x_ref[...]
    tm = x.shape[0]
    dff = wdn_ref.shape[0]
    h = _rms(x, gf_ref[...]).astype(BF)
    acc = jnp.zeros(x.shape, F32)
    for c0 in range(0, dff, chunk):
        halves = []
        for off in (c0, dff + c0):
            up = _dot(h, wup_ref[:, off:off + chunk])
            prev = uprev[:, off:off + chunk]
            uc = cb_ref[:, off:off + chunk] + _shift_rows(up, prev, 2) * cw_ref[0:1, off:off + chunk]
            uc = uc + _shift_rows(up, prev, 1) * cw_ref[1:2, off:off + chunk]
            uc = uc + up * cw_ref[2:3, off:off + chunk]
            uprev[:, off:off + chunk] = up[tm - SUBLANES:, :]
            halves.append(uc)
        act = (_gelu(halves[0]) * halves[1]).astype(BF)
        acc = acc + _dot(act, wdn_ref[c0:c0 + chunk, :])
    ho_ref[...] = uprev[...]
    o_ref[...] = _ffn_tail(x, acc, p_ref, gp_ref, wgate_ref, wpin_ref, gfin_ref, final_norm)


def _ffn(x, p_all, layer, hist8, gf, wup_bf, cw8, cb, wdn_bf, gp, wgate_bf, wpin_bf, gfin, b, t, final_norm):
    n, d = x.shape
    dp = p_all.shape[2]
    d2 = wup_bf.shape[1]
    tm = min(1024, t)
    nt = t // tm
    row = lambda w: pl.BlockSpec((tm, w), lambda bi, ti: (bi * nt + ti, 0))
    p_spec = pl.BlockSpec((None, tm, dp), lambda bi, ti: (layer, bi * nt + ti, 0))
    st_spec = pl.BlockSpec((None, SUBLANES, d2), lambda bi, ti: (bi, 0, 0))
    consts = (gf, wup_bf, cw8, cb, wdn_bf, gp, wgate_bf, wpin_bf, gfin)
    return pl.pallas_call(
        functools.partial(_ffn_kernel, chunk=512, final_norm=final_norm),
        grid=(b, nt),
        in_specs=[row(d), p_spec, st_spec] + [_const_spec(a.shape) for a in consts],
        out_specs=[row(d), st_spec],
        out_shape=[jax.ShapeDtypeStruct((n, d), F32), jax.ShapeDtypeStruct((b, SUBLANES, d2), F32)],
        scratch_shapes=[pltpu.VMEM((SUBLANES, d2), F32)],
        compiler_params=_cparams(2), name="ffn",
    )(x, p_all, hist8, *consts)


def _ffn_step_kernel(x_ref, p_ref, h0_ref, h1_ref, gf_ref, wup_ref, cw_ref, cb_ref, wdn_ref, gp_ref, wgate_ref,
                     wpin_ref, gfin_ref, o_ref, up_ref, *, chunk, final_norm):
    x = x_ref[...]
    dff = wdn_ref.shape[0]
    h = _rms(x, gf_ref[...]).astype(BF)
    acc = jnp.zeros(x.shape, F32)
    for c0 in range(0, dff, chunk):
        halves = []
        for off in (c0, dff + c0):
            up = _dot(h, wup_ref[:, off:off + chunk])
            up_ref[:, off:off + chunk] = up
            uc = cb_ref[:, off:off + chunk] + h0_ref[:, off:off + chunk] * cw_ref[0:1, off:off + chunk]
            uc = uc + h1_ref[:, off:off + chunk] * cw_ref[1:2, off:off + chunk]
            uc = uc + up * cw_ref[2:3, off:off + chunk]
            halves.append(uc)
        act = (_gelu(halves[0]) * halves[1]).astype(BF)
        acc = acc + _dot(act, wdn_ref[c0:c0 + chunk, :])
    o_ref[...] = _ffn_tail(x, acc, p_ref, gp_ref, wgate_ref, wpin_ref, gfin_ref, final_norm)


def _ffn_step(x, p, h0, h1, gf, wup_bf, cw8, cb, wdn_bf, gp, wgate_bf, wpin_bf, gfin, final_norm):
    n, d = x.shape
    d2 = wup_bf.shape[1]
    args = (x, p, h0, h1, gf, wup_bf, cw8, cb, wdn_bf, gp, wgate_bf, wpin_bf, gfin)
    return pl.pallas_call(
        functools.partial(_ffn_step_kernel, chunk=512, final_norm=final_norm),
        grid=(1,),
        in_specs=[_const_spec(a.shape) for a in args],
        out_specs=[pl.BlockSpec((n, d), lambda i: (0, 0)), pl.BlockSpec((n, d2), lambda i: (0, 0))],
        out_shape=[jax.ShapeDtypeStruct((n, d), F32), jax.ShapeDtypeStruct((n, d2), F32)],
        compiler_params=_cparams(1), name="ffn_step",
    )(*args)


def _kv_kernel(x_ref, g_ref, w_ref, cos_ref, sin_ref, oh_ref, kv_ref, slc_ref, win_ref, ka_ref, sva_ref, wva_ref):
    h = _rms(x_ref[...], g_ref[...]).astype(BF)
    kv = _dot(h, w_ref[...])
    cos, sin = cos_ref[...], sin_ref[...]
    w2 = 2 * LANES
    k_slc = _rope(kv[:, w2:w2 + LANES], cos, sin)
    k_win = _rope(kv[:, 2 * w2:2 * w2 + LANES], cos, sin)
    kv_ref[:, 0:w2] = kv[:, 0:w2]
    kv_ref[:, w2:w2 + LANES] = k_slc
    kv_ref[:, w2 + LANES:2 * w2] = kv[:, w2 + LANES:2 * w2]
    kv_ref[:, 2 * w2:2 * w2 + LANES] = k_win
    kv_ref[:, 2 * w2 + LANES:3 * w2] = kv[:, 2 * w2 + LANES:3 * w2]
    slc_ref[:, 0:LANES] = k_slc.astype(BF)
    slc_ref[:, LANES:w2] = kv[:, w2 + LANES:2 * w2].astype(BF)
    win_ref[:, 0:LANES] = k_win.astype(BF)
    win_ref[:, LANES:w2] = kv[:, 2 * w2 + LANES:3 * w2].astype(BF)
    low = lax.broadcasted_iota(jnp.int32, k_slc.shape, 1) < HEAD_DIM
    oh = oh_ref[...]
    ka_ref[:, 0:LANES] = jnp.where(low, k_slc, oh).astype(BF)
    ka_ref[:, LANES:w2] = jnp.where(low, oh, k_slc).astype(BF)
    for va_ref, v in ((sva_ref, kv[:, w2 + LANES:2 * w2]), (wva_ref, kv[:, 2 * w2 + LANES:3 * w2])):
        va_ref[:, 0:LANES] = jnp.where(low, v, 1.0).astype(BF)
        va_ref[:, LANES:w2] = jnp.where(low, 1.0, v).astype(BF)


def _kv_proj(x, g, w_bf, cos, sin, onehot):
    n, d = x.shape
    nk = w_bf.shape[1]
    tm = min(512, n)
    row = lambda w: pl.BlockSpec((tm, w), lambda i: (i, 0))
    return pl.pallas_call(
        _kv_kernel,
        grid=(n // tm,),
        in_specs=[row(d), _const_spec((1, d)), _const_spec(w_bf.shape), row(LANES), row(LANES), row(LANES)],
        out_specs=[row(nk)] + [row(2 * LANES)] * 5,
        out_shape=[jax.ShapeDtypeStruct((n, nk), F32)] + [jax.ShapeDtypeStruct((n, 2 * LANES), BF)] * 5,
        compiler_params=_cparams(1), name="kv_proj",
    )(x, g, w_bf, cos, sin, onehot)


def _q_kernel(x_ref, g_ref, wq_ref, wg_ref, cos_ref, sin_ref, qs_ref, qrs_ref, gt_ref):
    h = _rms(x_ref[...], g_ref[...]).astype(BF)
    cos, sin = cos_ref[...], sin_ref[...]
    scale = HEAD_DIM ** -0.5
    for c0 in range(0, wq_ref.shape[1], LANES):
        q = _dot(h, wq_ref[:, c0:c0 + LANES])
        qs_ref[:, c0:c0 + LANES] = (q * scale).astype(BF)
        qrs_ref[:, c0:c0 + LANES] = (_rope(q, cos, sin) * scale).astype(BF)
    gt_ref[...] = _sigmoid(_dot(h, wg_ref[...]))


def _q_proj(x, g, wq_bf, wg_bf, cos, sin):
    n, d = x.shape
    dq = wq_bf.shape[1]
    tm = min(512, n)
    row = lambda w: pl.BlockSpec((tm, w), lambda i: (i, 0))
    return pl.pallas_call(
        _q_kernel,
        grid=(n // tm,),
        in_specs=[row(d), _const_spec((1, d)), _const_spec(wq_bf.shape), _const_spec(wg_bf.shape),
                  row(LANES), row(LANES)],
        out_specs=[row(dq), row(dq), row(LANES)],
        out_shape=[jax.ShapeDtypeStruct((n, dq), BF), jax.ShapeDtypeStruct((n, dq), BF),
                   jax.ShapeDtypeStruct((n, LANES), F32)],
        compiler_params=_cparams(1), name="q_proj",
    )(x, g, wq_bf, wg_bf, cos, sin)


def _compress_kernel(xk_ref, xv_ref, pa_ref, pb_ref, w1a_ref, w1b_ref, b1_ref, w2_ref, kc_ref, vc_ref):
    nhb = kc_ref.shape[0]
    for j, x_ref, o_ref in ((0, xk_ref, kc_ref), (1, xv_ref, vc_ref)):
        acc_a = jnp.zeros((nhb, 2 * LANES), F32)
        acc_b = jnp.zeros((nhb, 2 * LANES), F32)
        for l in range(D_CMP):
            xl = x_ref[pl.ds(l, nhb, stride=D_CMP), :]
            acc_a = acc_a + _dot((xl + pa_ref[j, l:l + 1, :]).astype(BF), w1a_ref[j, l])
            acc_b = acc_b + _dot((xl + pb_ref[j, l:l + 1, :]).astype(BF), w1b_ref[j, l])
        hid = _gelu(acc_a + pltpu.roll(acc_b, nhb - 1, axis=0) + b1_ref[j])
        o_ref[...] = _dot(hid.astype(BF), w2_ref[j]).astype(BF)


def _compress(rows, params, b, t):
    nhb = t // D_CMP
    out = pl.BlockSpec((None, nhb, LANES), lambda bi: (bi, 0, 0))
    return pl.pallas_call(
        _compress_kernel,
        grid=(b,),
        in_specs=[pl.BlockSpec((None, t, LANES), lambda bi: (bi, 0, 0)),
                  pl.BlockSpec((None, t, LANES), lambda bi: (bi, 0, 1))] + [_const_spec(a.shape) for a in params],
        out_specs=[out, out],
        out_shape=[jax.ShapeDtypeStruct((b, nhb, LANES), BF)] * 2,
        compiler_params=_cparams(1), name="compress",
    )(rows, rows, *params)


def _compress_paged_kernel(pt_ref, *refs, per_step):
    del pt_ref
    page_refs = refs[:2 * per_step]
    params = refs[2 * per_step:-4]
    kc_ref, vc_ref, xk_scr, xv_scr = refs[-4:]
    step = pl.program_id(1)
    for r in range(per_step):
        rows = pl.ds(pl.multiple_of((step * per_step + r) * PAGE_SIZE, PAGE_SIZE), PAGE_SIZE)
        xk_scr[rows, :] = page_refs[2 * r][...]
        xv_scr[rows, :] = page_refs[2 * r + 1][...]

    @pl.when(step == pl.num_programs(1) - 1)
    def _():
        _compress_kernel(xk_scr, xv_scr, *params, kc_ref, vc_ref)


def _compress_paged(cache, page_table, params):
    db, n_pages = page_table.shape
    t = n_pages * PAGE_SIZE
    nhb = t // D_CMP
    per_step = min(16, n_pages)
    in_specs = []
    for r in range(per_step):
        for half in range(2):
            in_specs.append(pl.BlockSpec((None, PAGE_SIZE, LANES), functools.partial(
                lambda bi, i, pt, r, half: (pt[bi, i * per_step + r], 0, half), r=r, half=half)))
    in_specs += [_const_spec(a.shape) for a in params]
    out = pl.BlockSpec((None, nhb, LANES), lambda bi, i, pt: (bi, 0, 0))
    return pl.pallas_call(
        functools.partial(_compress_paged_kernel, per_step=per_step),
        grid_spec=pltpu.PrefetchScalarGridSpec(
            num_scalar_prefetch=1, grid=(db, n_pages // per_step), in_specs=in_specs, out_specs=[out, out],
            scratch_shapes=[pltpu.VMEM((t, LANES), F32), pltpu.VMEM((t, LANES), F32)]),
        out_shape=[jax.ShapeDtypeStruct((db, nhb, LANES), BF)] * 2,
        compiler_params=_cparams(2), name="compress_paged",
    )(page_table, *([cache] * (2 * per_step)), *params)


def _stack_heads(q_ref, g, tq, fill=0.0):
    lane = lax.broadcasted_iota(jnp.int32, (tq, LANES), 1)
    keep = (lane >= HEAD_DIM * g) & (lane < HEAD_DIM * (g + 1))
    parts = []
    for h in range(HPG):
        hh = g * HPG + h
        ch = hh // 2
        c = q_ref[:, ch * LANES:(ch + 1) * LANES].astype(F32)
        if hh % 2 != g:
            c = pltpu.roll(c, HEAD_DIM, axis=1)
        parts.append(jnp.where(keep, c, fill))
    return jnp.concatenate(parts, axis=0).astype(BF)


def _cmp_branch(q, kc, vc, tpos_col, n_cb, tq):
    ncp = kc.shape[0]
    cidx = lax.broadcasted_iota(jnp.int32, (tq, ncp), 1)
    valid = ((cidx * D_CMP + (L_CMP - 1)) <= tpos_col) & (cidx < n_cb)
    s = _dot_nt(q, kc).reshape(HPG, tq, ncp) + jnp.where(valid, 0.0, NEG)[None]
    e = jnp.exp(s - jnp.max(s, axis=-1, keepdims=True))
    any_valid = (tpos_col >= L_CMP - 1) & (n_cb > 0)
    inv = jnp.where(any_valid[None], 1.0 / jnp.sum(e, axis=-1, keepdims=True), 0.0)
    p = e * inv
    o = _dot(p.reshape(HPG * tq, ncp).astype(BF), vc)
    return o, jnp.sum(p, axis=0)


def _cmp_branch_staged(g, q, kc, vc, tpos_col, n_cb, s_ref, p_ref, oc_ref, tq):
    ncp = kc.shape[0]
    cidx = lax.broadcasted_iota(jnp.int32, (tq, ncp), 1)
    bias = _mask_bias(((cidx * D_CMP + (L_CMP - 1)) <= tpos_col) & (cidx < n_cb))
    any_valid = (tpos_col >= L_CMP - 1) & (n_cb > 0)
    s_ref[g, :, 0:ncp] = _dot_nt(q, kc)
    psum = jnp.zeros((tq, ncp), F32)
    for h in range(HPG):
        rows = slice(h * tq, (h + 1) * tq)
        m = jnp.max(s_ref[g, rows, 0:ncp] + bias, axis=-1, keepdims=True)
        e = jnp.exp(s_ref[g, rows, 0:ncp] + bias - m)
        p = e * jnp.where(any_valid, 1.0 / jnp.sum(e, axis=-1, keepdims=True), 0.0)
        psum = psum + p
        p_ref[g, rows, 0:ncp] = p.astype(BF)
    oc_ref[g] = _dot(p_ref[g, :, 0:ncp], vc)
    return psum


def _importance(psum, mt_ref, tpos_row):
    sp = mt_ref.shape[0]
    mt = mt_ref[...]
    p_hi = psum.astype(BF)
    r1 = psum - p_hi.astype(F32)
    p_mid = r1.astype(BF)
    p_lo = (r1 - p_mid.astype(F32)).astype(BF)
    imp = _dot_nt(mt, p_hi) + _dot_nt(mt, p_mid) + _dot_nt(mt, p_lo)
    s_idx = lax.broadcasted_iota(jnp.int32, (sp, LANES), 0)
    cur = jnp.right_shift(tpos_row, L_SLC.bit_length() - 1)
    forced = (s_idx == 0) | (s_idx == cur) | (s_idx == cur - 1)
    imp = jnp.where(forced, jnp.inf, imp)
    return jnp.where(s_idx * L_SLC <= tpos_row, imp, -jnp.inf)


def _select_blocks(psums, mt_ref, imp_ref, tpos_row, n_sb, tq):
    sp = mt_ref.shape[0]
    assert n_sb <= HEAD_DIM and sp == LANES, "one-hot block lanes hold at most 64 selection blocks"
    nr = _round_up(n_sb, SUBLANES)
    imps = []
    for g, psum in enumerate(psums):
        imp = _importance(psum, mt_ref, tpos_row)
        imp_ref[g] = imp
        imps.append(imp[:nr])
    s_idx = lax.broadcasted_iota(jnp.int32, (nr, LANES), 0)

    def body(i, cnts):
        tie = jnp.where(s_idx > i, 1.0, 0.0)
        out = []
        for g, imp in enumerate(imps):
            row = imp_ref[g, pl.ds(i, 1), :]
            out.append(cnts[g] + jnp.where(row > imp, 1.0, jnp.where(row == imp, tie, 0.0)))
        return tuple(out)

    cnts = lax.fori_loop(0, n_sb, body, tuple(jnp.zeros((nr, LANES), F32) for _ in psums))
    starts_before = s_idx * L_SLC <= tpos_row
    biases = []
    for g, cnt in enumerate(cnts):
        bias_t = jnp.where((cnt < float(N_SEL)) & starts_before, 0.0, NEG)
        if nr < sp:
            bias_t = jnp.concatenate([bias_t, jnp.full((sp - nr, LANES), NEG, F32)], axis=0)
        bias = bias_t.T[:tq]
        biases.append(pltpu.roll(bias, HEAD_DIM, axis=1) if g == 0 else bias)
    return biases


def _select_ids(psum, mt_ref, tpos):
    sp = mt_ref.shape[0]
    prow = jnp.broadcast_to(psum[0:1], (LANES, psum.shape[1]))
    imp = _importance(prow, mt_ref, jnp.full((1, LANES), tpos, jnp.int32))
    a = jnp.broadcast_to(imp[:, 0:1], (sp, sp))
    b = a.T
    r_i = lax.broadcasted_iota(jnp.int32, (sp, sp), 0)
    c_i = lax.broadcasted_iota(jnp.int32, (sp, sp), 1)
    beats = jnp.where(a > b, 1.0, jnp.where(a == b, jnp.where(r_i < c_i, 1.0, 0.0), 0.0))
    rank = jnp.sum(beats, axis=0, keepdims=True)
    slot = lax.broadcasted_iota(jnp.int32, (N_SEL, sp), 0).astype(F32)
    blk = lax.broadcasted_iota(jnp.int32, (N_SEL, sp), 1).astype(F32)
    ids = jnp.sum(jnp.where(rank == slot, blk, 0.0), axis=1, keepdims=True)
    return jnp.broadcast_to(ids, (N_SEL, LANES)).astype(jnp.int32)


def _flash_init(tq):
    return (jnp.full((HPG, tq, 1), NEG, F32), jnp.zeros((HPG, tq, 1), F32), jnp.zeros((HPG * tq, LANES), F32))


def _flash_tile(carry, q, k, v, bias, tq):
    m, l, acc = carry
    kt = k.shape[0]
    s = _dot_nt(q, k).reshape(HPG, tq, kt)
    if bias is not None:
        s = s + bias[None]
    m_new = jnp.maximum(m, jnp.max(s, axis=-1, keepdims=True))
    alpha = jnp.exp(m - m_new)
    p = jnp.exp(s - m_new)
    l = alpha * l + jnp.sum(p, axis=-1, keepdims=True)
    acc = alpha.reshape(HPG * tq, 1) * acc + _dot(p.reshape(HPG * tq, kt).astype(BF), v)
    return m_new, l, acc


def _flash_out(carry, tq):
    _, l, acc = carry
    return acc / l.reshape(HPG * tq, 1)


def _attend_once(q, k, v, bias, tq):
    kt = k.shape[0]
    s = _dot_nt(q, k).reshape(HPG, tq, kt) + bias[None]
    p = jnp.exp(s - jnp.max(s, axis=-1, keepdims=True))
    l = jnp.sum(p, axis=-1, keepdims=True)
    return _dot(p.reshape(HPG * tq, kt).astype(BF), v) / l.reshape(HPG * tq, 1)


def _mask_bias(valid):
    return jnp.where(valid, 0.0, NEG)


def _win_valid(kpos_row, tpos_col):
    dist = tpos_col - kpos_row
    return (dist >= 0) & (dist < WINDOW)


def _merge_heads(o_c, o_s, o_w, gt_ref, g, o_ref, tq):
    lane = lax.broadcasted_iota(jnp.int32, (tq, LANES), 1)
    for pair in range(HPG // 2):
        both = []
        for h in (2 * pair, 2 * pair + 1):
            col = (g * HPG + h) * 3
            rows = slice(h * tq, (h + 1) * tq)
            both.append(o_c(rows) * gt_ref[:, col:col + 1] + o_s(rows) * gt_ref[:, col + 1:col + 2]
                        + o_w(rows) * gt_ref[:, col + 2:col + 3])
        even, odd = both
        if g == 0:
            odd = pltpu.roll(odd, HEAD_DIM, axis=1)
        else:
            even = pltpu.roll(even, HEAD_DIM, axis=1)
        ch = g * (HPG // 2) + pair
        o_ref[:, ch * LANES:(ch + 1) * LANES] = jnp.where(lane < HEAD_DIM, even, odd).astype(o_ref.dtype)


def _staged_tile(state, stage, q, k, v, bias, refs, tq, first):
    s_ref, p_ref, m_ref, acc_ref = refs
    kt = k.shape[0]
    reps = kt // LANES
    s_ref[stage, :, 0:kt] = _dot_nt(q, k)
    for h in range(HPG):
        rows = slice(h * tq, (h + 1) * tq)

        def scores():
            s = s_ref[stage, rows, 0:kt]
            return s if bias is None else s + bias

        m_tile = jnp.max(scores(), axis=-1, keepdims=True)
        if first:
            m_new = jnp.broadcast_to(m_tile, (tq, LANES))
        else:
            m_old = m_ref[state, rows, :]
            m_new = jnp.maximum(m_old, m_tile)
            acc_ref[state, rows, :] = jnp.exp(m_old - m_new) * acc_ref[state, rows, :]
        m_ref[state, rows, :] = m_new
        p_ref[stage, rows, 0:kt] = jnp.exp(scores() - jnp.concatenate([m_new] * reps, axis=1)).astype(BF)
    pv = _dot(p_ref[stage, :, 0:kt], v)
    if first:
        acc_ref[state] = pv
    else:
        acc_ref[state] = acc_ref[state] + pv


def _attn_prompt_kernel(qs_ref, qrs_ref, gt_ref, kc_ref, vc_ref, ka_ref, sva_ref, win_ref, wva_ref, mt_ref, o_ref,
                        imp_ref, s_ref, p_ref, m_ref, acc_ref, oc_ref, *, n_cb, n_sb):
    tq = qs_ref.shape[0]
    t_all = ka_ref.shape[0]
    qb = pl.program_id(1)
    t0 = qb * tq
    tpos_col = t0 + lax.broadcasted_iota(jnp.int32, (tq, 1), 0)
    tpos_row = t0 + lax.broadcasted_iota(jnp.int32, (1, LANES), 1)
    groups = range(N_KV_HEADS)
    k_lanes, v_lanes = slice(0, LANES), slice(LANES, 2 * LANES)

    psums = [_cmp_branch_staged(g, _stack_heads(qs_ref, g, tq), kc_ref[...], vc_ref[...], tpos_col, n_cb,
                                s_ref, p_ref, oc_ref, tq) for g in groups]
    blk_bias = _select_blocks(psums, mt_ref, imp_ref, tpos_row, n_sb, tq)
    qr_blk = [_stack_heads(qrs_ref, g, tq, fill=blk_bias[g]) for g in groups]

    refs = (s_ref, p_ref, m_ref, acc_ref)
    g_lanes = [slice(g * LANES, (g + 1) * LANES) for g in groups]

    def slc_tile(j, bias, first):
        rows = pl.ds(pl.multiple_of(j * SLC_TILE, SLC_TILE), SLC_TILE)
        for g in groups:
            _staged_tile(g, g, qr_blk[g], ka_ref[rows, g_lanes[g]], sva_ref[rows, g_lanes[g]], bias, refs, tq, first)

    last = (t0 + tq - 1) // SLC_TILE
    causal = (last * SLC_TILE + lax.broadcasted_iota(jnp.int32, (1, SLC_TILE), 1)) <= tpos_col
    slc_tile(last, _mask_bias(causal), True)

    def slc_body(j, c):
        slc_tile(j, None, False)
        return c

    lax.fori_loop(0, last, slc_body, 0)

    wk = min(WINDOW + tq, t_all)
    start = pl.multiple_of(jnp.clip(t0 - WINDOW, 0, t_all - wk), tq)
    wbias = _mask_bias(_win_valid(start + lax.broadcasted_iota(jnp.int32, (1, wk), 1), tpos_col))
    wrows = pl.ds(start, wk)
    kw = win_ref[wrows, k_lanes]
    for g in groups:
        _staged_tile(N_KV_HEADS + g, g, _stack_heads(qrs_ref, g, tq), kw, wva_ref[wrows, g_lanes[g]], wbias, refs,
                     tq, True)

    def normalised(slot, rows):
        acc = acc_ref[slot, rows, :]
        return acc / pltpu.roll(acc, HEAD_DIM, axis=1)

    for g in groups:
        _merge_heads(lambda rows: oc_ref[g, rows, :], functools.partial(normalised, g),
                     functools.partial(normalised, N_KV_HEADS + g), gt_ref, g, o_ref, tq)


def _attn_prompt(qs, qrs, gates, kc, vc, ka_bf, slc_va, win_bf, win_va, mt, b, t, n_cb, n_sb):
    tq = Q_BLOCK
    nqb = t // tq
    d = qs.shape[1]
    stage_w = max(SLC_TILE, min(WINDOW + tq, t))
    row = lambda w: pl.BlockSpec((tq, w), lambda bi, qi: (bi * nqb + qi, 0))
    per_b = lambda a: pl.BlockSpec((None,) + a.shape[1:], lambda bi, qi: (bi, 0, 0))
    return pl.pallas_call(
        functools.partial(_attn_prompt_kernel, n_cb=n_cb, n_sb=n_sb),
        grid=(b, nqb),
        in_specs=[row(d), row(d), row(LANES), per_b(kc), per_b(vc), per_b(ka_bf), per_b(slc_va), per_b(win_bf),
                  per_b(win_va), _const_spec(mt.shape)],
        out_specs=row(d),
        out_shape=jax.ShapeDtypeStruct((b * t, d), BF),
        scratch_shapes=[pltpu.VMEM((N_KV_HEADS, mt.shape[0], LANES), F32),
                        pltpu.VMEM((N_KV_HEADS, HPG * tq, stage_w), F32),
                        pltpu.VMEM((N_KV_HEADS, HPG * tq, stage_w), BF),
                        pltpu.VMEM((2 * N_KV_HEADS, HPG * tq, LANES), F32),
                        pltpu.VMEM((2 * N_KV_HEADS, HPG * tq, LANES), F32),
                        pltpu.VMEM((N_KV_HEADS, HPG * tq, LANES), F32)],
        compiler_params=_cparams(2), name="attn_prompt",
    )(qs, qrs, gates, kc, vc, ka_bf, slc_va, win_bf, win_va, mt)


def _sample_select_kernel(qs_ref, kc_ref, vc_ref, mt_ref, oc_ref, ids_ref, *, n_cb, past):
    tq = qs_ref.shape[0]
    tpos_col = jnp.full((tq, 1), past, jnp.int32)
    for g in range(N_KV_HEADS):
        o_c, psum = _cmp_branch(_stack_heads(qs_ref, g, tq), kc_ref[...], vc_ref[...], tpos_col, n_cb, tq)
        oc_ref[g] = o_c
        ids_ref[g] = _select_ids(psum, mt_ref, past)


def _sample_select(qs, kc, vc, mt, n_cb, past):
    db, tq, _ = qs.shape
    per_b = lambda a: pl.BlockSpec((None,) + a.shape[1:], lambda bi: (bi,) + (0,) * (a.ndim - 1))
    oc = jax.ShapeDtypeStruct((db, N_KV_HEADS, HPG * tq, LANES), F32)
    ids = jax.ShapeDtypeStruct((db, N_KV_HEADS, N_SEL, LANES), jnp.int32)
    return pl.pallas_call(
        functools.partial(_sample_select_kernel, n_cb=n_cb, past=past),
        grid=(db,),
        in_specs=[per_b(qs), per_b(kc), per_b(vc), _const_spec(mt.shape)],
        out_specs=[per_b(oc), per_b(ids)],
        out_shape=[oc, ids],
        compiler_params=_cparams(1), name="sample_select",
    )(qs, kc, vc, mt)


def _sample_attend_kernel(pt_ref, ids_ref, qrs_ref, gt_ref, oc_ref, slct_ref, win_ref, wint_ref, *rest, past, n_cached):
    del pt_ref
    blk_refs, o_ref, k_scr, v_scr = rest[:-3], rest[-3], rest[-2], rest[-1]
    b = pl.program_id(0)
    tq = qrs_ref.shape[0]
    tpos_col = jnp.full((tq, 1), past, jnp.int32)
    w_buf = win_ref.shape[0]
    tail = slct_ref.shape[0]
    k_lanes, v_lanes = slice(0, LANES), slice(LANES, 2 * LANES)
    tail_pos = past + lax.broadcasted_iota(jnp.int32, (1, tail), 1)
    win_pos = past - w_buf + lax.broadcasted_iota(jnp.int32, (1, w_buf), 1)
    wbias = jnp.broadcast_to(_mask_bias(_win_valid(win_pos, tpos_col)), (tq, w_buf))
    wtbias = jnp.broadcast_to(_mask_bias(_win_valid(tail_pos, tpos_col)), (tq, tail))
    tbias = jnp.broadcast_to(_mask_bias(tail_pos <= tpos_col), (tq, tail))
    kw, vw = win_ref[:, k_lanes].astype(BF), win_ref[:, v_lanes].astype(BF)
    lane = lax.broadcasted_iota(jnp.int32, (1, LANES), 1)
    for g in range(N_KV_HEADS):
        for r in range(N_SEL):
            blk = blk_refs[g * N_SEL + r]
            k_scr[g, r * L_SLC:(r + 1) * L_SLC, :] = blk[:, k_lanes].astype(BF)
            v_scr[g, r * L_SLC:(r + 1) * L_SLC, :] = blk[:, v_lanes].astype(BF)
        chunks = []
        for c in range(N_SEL // 2):
            lo = jnp.where(ids_ref[b, g * N_SEL + 2 * c] >= n_cached, NEG, 0.0)
            hi = jnp.where(ids_ref[b, g * N_SEL + 2 * c + 1] >= n_cached, NEG, 0.0)
            chunks.append(jnp.where(lane < L_SLC, lo, hi))
        sbias = jnp.broadcast_to(jnp.concatenate(chunks, axis=1), (tq, N_SEL * L_SLC))
        qr = _stack_heads(qrs_ref, g, tq)
        carry = _flash_tile(_flash_init(tq), qr, k_scr[g], v_scr[g], sbias, tq)
        carry = _flash_tile(carry, qr, slct_ref[:, k_lanes], slct_ref[:, v_lanes], tbias, tq)
        o_s = _flash_out(carry, tq)
        carry = _flash_tile(_flash_init(tq), qr, kw, vw, wbias, tq)
        carry = _flash_tile(carry, qr, wint_ref[:, k_lanes], wint_ref[:, v_lanes], wtbias, tq)
        o_w = _flash_out(carry, tq)
        _merge_heads(lambda rows: oc_ref[g, rows, :], lambda rows: o_s[rows], lambda rows: o_w[rows], gt_ref, g,
                     o_ref, tq)


def _sample_attend(qrs, gates, oc, slc_cache, page_table, ids, slc_tail, win_c, win_tail, past):
    db = qrs.shape[0]
    per_page = PAGE_SIZE // L_SLC
    n_cached = past // L_SLC
    blocks = slc_cache
    per_b = lambda a: pl.BlockSpec((None,) + a.shape[1:], lambda bi, pt, sel: (bi,) + (0,) * (a.ndim - 1))

    def blk_map(bi, pt, sel, g, r):
        s = jnp.clip(sel[bi, g * N_SEL + r], 0, n_cached - 1)
        return pt[bi, s // per_page], s % per_page, 0

    blk_specs = [pl.BlockSpec((None, L_SLC, blocks.shape[2]), functools.partial(blk_map, g=g, r=r))
                 for g in range(N_KV_HEADS) for r in range(N_SEL)]
    args = (qrs, gates, oc, slc_tail, win_c, win_tail)
    return pl.pallas_call(
        functools.partial(_sample_attend_kernel, past=past, n_cached=n_cached),
        grid_spec=pltpu.PrefetchScalarGridSpec(
            num_scalar_prefetch=2, grid=(db,),
            in_specs=[per_b(a) for a in args] + blk_specs,
            out_specs=per_b(qrs),
            scratch_shapes=[pltpu.VMEM((N_KV_HEADS, N_SEL * L_SLC, LANES), BF)] * 2),
        out_shape=jax.ShapeDtypeStruct(qrs.shape, BF),
        compiler_params=_cparams(1), name="sample_attend",
    )(page_table, ids, *args, *([blocks] * len(blk_specs)))


def _block_diag(blocks):
    n, k, j = blocks.shape
    eye = jnp.eye(n, dtype=blocks.dtype)
    return (eye[:, None, :, None] * blocks[:, :, None, :]).reshape(n * k, n * j)


def _pad_rows8(rows):
    return jnp.pad(rows, ((0, 0), (SUBLANES - rows.shape[1], 0), (0, 0)))


def _rope_tables(pos):
    half = HEAD_DIM // 2
    inv_freq = ROPE_THETA ** (-jnp.arange(half, dtype=F32) / half)
    ang = pos.astype(F32)[:, None] * inv_freq[None, :]
    cos, sin = jnp.cos(ang), jnp.sin(ang)
    reps = LANES // HEAD_DIM
    return (jnp.tile(jnp.concatenate([cos, cos], axis=1), (1, reps)),
            jnp.tile(jnp.concatenate([-sin, sin], axis=1), (1, reps)))


def _block_onehot(pos):
    lane = jnp.arange(LANES, dtype=jnp.int32)[None, :] % HEAD_DIM
    return (lane == (pos[:, None] // L_SLC)).astype(F32)


def _overlap_t(n_cb, n_sb, ncp, sp):
    c0 = np.arange(ncp)[None, :] * D_CMP
    s0 = np.arange(sp)[:, None] * L_SLC
    m = (c0 < s0 + L_SLC) & (c0 + L_CMP > s0) & (np.arange(ncp)[None, :] < n_cb) & (np.arange(sp)[:, None] < n_sb)
    return jnp.asarray(m, dtype=BF)


def _round_up(x, m):
    return (x + m - 1) // m * m


def kernel(x_prompt, x_sample, p_prompt, p_sample, cache_cmp_kv, cache_slc_kv, cache_win_kv, state_rg_conv,
           state_rg_h, state_ffn_conv, page_table, g_mix, g_ffn, g_ple, g_final, rg_w_in, rg_conv_w, rg_conv_b,
           rg_w_a, rg_b_a, rg_w_x, rg_b_x, rg_lambda, rg_w_out, g_kv, w_kv, cmp_pos, cmp_w1, cmp_b1, cmp_w2,
           attn_w_qg, attn_w_o, ffn_w_up, ffn_conv_w, ffn_conv_b, ffn_w_down, ple_w_in, ple_w_gate):
    b, t, d = x_prompt.shape
    db = x_sample.shape[0]
    assert x_sample.shape[1] == 1, "the sample path handles one new token per sequence"
    depth = g_mix.shape[0]
    n_a = rg_w_in.shape[0]
    d_rnn = rg_w_out.shape[1]
    n_pages = page_table.shape[1]
    past = n_pages * PAGE_SIZE
    kvc = 2 * N_KV_HEADS * HEAD_DIM
    n_q = N_KV_HEADS * HPG * HEAD_DIM

    row1 = lambda v: v.reshape(1, -1)
    pad8 = lambda w: jnp.pad(w, ((0, SUBLANES - w.shape[0]), (0, 0)))

    rg_in_bf = rg_w_in.astype(BF)
    rg_out_bf = rg_w_out.astype(BF)
    rg_gate_bf = [jnp.concatenate([_block_diag(rg_w_a[i]), _block_diag(rg_w_x[i])], axis=1).astype(BF)
                  for i in range(n_a)]
    rg_gate_b = [row1(jnp.concatenate([rg_b_a[i].reshape(-1), rg_b_x[i].reshape(-1)])) for i in range(n_a)]
    rg_c = [row1(-RG_C * jax.nn.softplus(-rg_lambda[i])) for i in range(n_a)]
    up_bf = ffn_w_up.astype(BF)
    down_bf = ffn_w_down.astype(BF)
    gate_bf = ple_w_gate.astype(BF)
    pin_bf = ple_w_in.astype(BF)
    wkv_bf = w_kv.astype(BF)
    wq_bf = attn_w_qg[:, :, :n_q].astype(BF)
    wgl_bf = jnp.pad(attn_w_qg[:, :, n_q:], ((0, 0), (0, 0), (0, LANES - 3 * N_KV_HEADS * HPG))).astype(BF)
    wo_bf = attn_w_o.astype(BF)
    gfin = row1(g_final)

    w1 = cmp_w1.reshape(2, 2, D_CMP, HEAD_DIM, -1)
    eye_g = jnp.eye(N_KV_HEADS, dtype=F32)
    w1_bd = (eye_g[None, None, None, :, None, :, None] * w1[:, :, :, None, :, None, :]).reshape(
        2, 2, D_CMP, N_KV_HEADS * HEAD_DIM, -1).astype(BF)
    pos_t = jnp.tile(jnp.transpose(cmp_pos, (1, 0, 2)), (1, 1, N_KV_HEADS)).reshape(2, 2, D_CMP, LANES)
    cmp_params = (pos_t[:, 0], pos_t[:, 1], w1_bd[:, 0], w1_bd[:, 1],
                  jnp.tile(cmp_b1, (1, N_KV_HEADS)).reshape(2, 1, -1),
                  jnp.stack([_block_diag(jnp.stack([cmp_w2[j]] * N_KV_HEADS)) for j in range(2)]).astype(BF))

    def layer_consts(i, final):
        return (row1(g_ffn[i]), up_bf[i], pad8(ffn_conv_w[i]), row1(ffn_conv_b[i]), down_bf[i], row1(g_ple[i]),
                gate_bf[i], pin_bf[i], gfin), final

    n = b * t
    x = x_prompt.reshape(n, d)
    p = p_prompt.reshape(depth, n, -1)
    pos_p = jnp.tile(jnp.arange(t, dtype=jnp.int32), b)
    cos_p, sin_p = _rope_tables(pos_p)
    rg_conv_p, rg_h_p, ffn_conv_p = [], [], []
    zeros_rg = jnp.zeros((b, SUBLANES, d_rnn), F32)
    zeros_ffn = jnp.zeros((b, SUBLANES, ffn_w_up.shape[2]), F32)
    for i in range(depth):
        if i == n_a:
            kv_p, _, win_bf, ka_bf, slc_va, win_va = _kv_proj(x, row1(g_kv), wkv_bf, cos_p, sin_p,
                                                              _block_onehot(pos_p))
            kc_p, vc_p = _compress(kv_p.reshape(b, t, -1), cmp_params, b, t)
            n_cb_p = (t - L_CMP) // D_CMP + 1
            n_sb_p = t // L_SLC
            mt_p = _overlap_t(n_cb_p, n_sb_p, t // D_CMP, _round_up(n_sb_p, LANES))
            ka_bf, slc_va, win_bf, win_va = (a.reshape(b, t, -1) for a in (ka_bf, slc_va, win_bf, win_va))
        if i < n_a:
            x, h_last, xr_last = _rg_mixer(x, row1(g_mix[i]), rg_in_bf[i], zeros_rg, zeros_rg, pad8(rg_conv_w[i]),
                                           row1(rg_conv_b[i]), rg_gate_bf[i], rg_gate_b[i], rg_c[i], rg_out_bf[i],
                                           b, t)
            rg_conv_p.append(xr_last[:, SUBLANES - 3:])
            rg_h_p.append(h_last[:, 0])
        else:
            j = i - n_a
            qs, qrs, gates = _q_proj(x, row1(g_mix[i]), wq_bf[j], wgl_bf[j], cos_p, sin_p)
            o = _attn_prompt(qs, qrs, gates, kc_p, vc_p, ka_bf, slc_va, win_bf, win_va, mt_p, b, t, n_cb_p, n_sb_p)
            x = _matmul_res(o, wo_bf[j], x)
        consts, final = layer_consts(i, i == depth - 1)
        x, f_hist = _ffn(x, p, i, zeros_ffn, *consts, b, t, final)
        ffn_conv_p.append(f_hist[:, SUBLANES - 2:])
    y_prompt = x.reshape(b, t, d)
    kv5 = kv_p.reshape(b, t, 6, N_KV_HEADS, HEAD_DIM)
    keep = min(WINDOW, t)
    cmp_p, slc_p, win_p = kv5[:, :, 0:2], kv5[:, :, 2:4], kv5[:, t - keep:, 4:6]

    x = x_sample.reshape(db, d)
    p = p_sample.reshape(depth, db, -1)
    cos_s, sin_s = _rope_tables(jnp.full((db,), past, jnp.int32))
    rg_conv_s, rg_h_s, ffn_conv_s = [], [], []
    for i in range(depth):
        if i == n_a:
            pos_s = jnp.full((db,), past, jnp.int32)
            kv_s, slc_new_bf, win_new_bf = _kv_proj(x, row1(g_kv), wkv_bf, cos_s, sin_s, _block_onehot(pos_s))[:3]
            kc_s, vc_s = _compress_paged(cache_cmp_kv.reshape(-1, PAGE_SIZE, kvc), page_table, cmp_params)
            slc_cache = cache_slc_kv.reshape(-1, PAGE_SIZE, kvc)
            n_cb_s = (past + 1 - L_CMP) // D_CMP + 1
            n_sb_s = -(-(past + 1) // L_SLC)
            sp_s = _round_up(n_sb_s, LANES)
            mt_s = _overlap_t(n_cb_s, n_sb_s, past // D_CMP, sp_s)
            tail = lambda a: jnp.pad(a.reshape(db, 1, -1), ((0, 0), (0, LANES - 1), (0, 0)))
            slc_tail, win_tail = tail(slc_new_bf), tail(win_new_bf)
            win_c = cache_win_kv.reshape(db, cache_win_kv.shape[1], kvc)
        if i < n_a:
            yg, xr = _rg_in(x, row1(g_mix[i]), rg_in_bf[i])
            hist = state_rg_conv[i]
            gated, h_new = _rg_step(xr, yg, hist[:, 0], hist[:, 1], hist[:, 2], state_rg_h[i], pad8(rg_conv_w[i]),
                                    row1(rg_conv_b[i]), rg_gate_bf[i], rg_gate_b[i], rg_c[i])
            x = _matmul_res(gated, rg_out_bf[i], x)
            rg_conv_s.append(jnp.concatenate([hist[:, 1:], xr[:, None, :]], axis=1))
            rg_h_s.append(h_new)
        else:
            j = i - n_a
            qs, qrs, gates = _q_proj(x, row1(g_mix[i]), wq_bf[j], wgl_bf[j], cos_s, sin_s)
            rep = lambda a: jnp.broadcast_to(a[:, None, :], (db, SUBLANES, a.shape[1]))
            oc, ids = _sample_select(rep(qs), kc_s, vc_s, mt_s, n_cb_s, past)
            o = _sample_attend(rep(qrs), rep(gates), oc, slc_cache, page_table, ids[:, :, :, 0].reshape(db, -1),
                               slc_tail, win_c,
                               win_tail, past)
            x = _matmul_res(o[:, 0], wo_bf[j], x)
        consts, final = layer_consts(i, i == depth - 1)
        fh = state_ffn_conv[i]
        x, up = _ffn_step(x, p[i], fh[:, 0], fh[:, 1], *consts, final)
        ffn_conv_s.append(jnp.stack([fh[:, 1], up], axis=1))
    y_sample = x.reshape(db, 1, d)
    kv5 = kv_s.reshape(db, 1, 6, N_KV_HEADS, HEAD_DIM)
    cmp_s, slc_s, win_new = kv5[:, :, 0:2], kv5[:, :, 2:4], kv5[:, :, 4:6]
    win_full = jnp.concatenate([cache_win_kv, win_new], axis=1)
    keep = min(WINDOW, past + 1)
    win_s = win_full[:, win_full.shape[1] - keep:]

    return (y_prompt, y_sample, cmp_p, cmp_s, slc_p, slc_s, win_p, win_s,
            jnp.stack(rg_conv_p), jnp.stack(rg_conv_s), jnp.stack(rg_h_p), jnp.stack(rg_h_s),
            jnp.stack(ffn_conv_p), jnp.stack(ffn_conv_s))
```

```python
import functools

import jax
import jax.numpy as jnp
import numpy as np
from jax import lax
from jax.experimental import pallas as pl
from jax.experimental.pallas import tpu as pltpu

BF = jnp.bfloat16
F32 = jnp.float32

EPS = 1e-6
RG_C = 8.0
N_RG_BLOCKS = 16
HEAD_DIM = 64
N_KV_HEADS = 2
HPG = 8
L_CMP = 32
D_CMP = 16
L_SLC = 64
N_SEL = 16
WINDOW = 512
Q_BLOCK = 128
ROPE_THETA = 10000.0
PAGE_SIZE = 128
NEG = -1e30
LANES = 128
SUBLANES = 8
SLC_TILE = 512
WIN_TILE = 128
VMEM_LIMIT = 56 * 1024 * 1024


def _cparams(n_axes):
    return pltpu.CompilerParams(dimension_semantics=("arbitrary",) * n_axes,
                                vmem_limit_bytes=VMEM_LIMIT)


def _const_spec(shape):
    zeros = (0,) * len(shape)
    return pl.BlockSpec(shape, lambda *_: zeros, pipeline_mode=pl.Buffered(1))


def _dot(a, b):
    return jnp.dot(a, b, preferred_element_type=F32)


def _dot_nt(a, b):
    return lax.dot_general(a, b, (((1,), (1,)), ((), ())), preferred_element_type=F32)


def _rms(x, g):
    return x * lax.rsqrt(jnp.mean(x * x, axis=-1, keepdims=True) + EPS) * g


def _gelu(x):
    return jax.nn.gelu(x, approximate=True)


def _sigmoid(x):
    return jax.nn.sigmoid(x)


def _shift_rows(x, prev8, j):
    if j == 0:
        return x
    r = pltpu.roll(x, j, axis=0)
    p = pltpu.roll(prev8, j, axis=0)
    row = lax.broadcasted_iota(jnp.int32, (SUBLANES, x.shape[1]), 0)
    head = jnp.where(row < j, p, r[:SUBLANES])
    if x.shape[0] == SUBLANES:
        return head
    return jnp.concatenate([head, r[SUBLANES:]], axis=0)


def _rope(z, cos, sin_signed):
    lane = lax.broadcasted_iota(jnp.int32, z.shape, 1)
    first = (lane & (HEAD_DIM - 1)) < (HEAD_DIM // 2)
    rot = jnp.where(first, pltpu.roll(z, LANES - HEAD_DIM // 2, axis=1), pltpu.roll(z, HEAD_DIM // 2, axis=1))
    return z * cos + rot * sin_signed


def _rg_in_kernel(x_ref, g_ref, w_ref, yg_ref, xr_ref, *, chunk):
    h = _rms(x_ref[...], g_ref[...]).astype(BF)
    n = yg_ref.shape[1]
    for c0 in range(0, n, chunk):
        yg_ref[:, c0:c0 + chunk] = _gelu(_dot(h, w_ref[:, c0:c0 + chunk]))
        xr_ref[:, c0:c0 + chunk] = _dot(h, w_ref[:, n + c0:n + c0 + chunk])


def _rg_in(x, g, w_bf):
    n, d = x.shape
    dr = w_bf.shape[1] // 2
    tm = min(512, n)
    return pl.pallas_call(
        functools.partial(_rg_in_kernel, chunk=256),
        grid=(n // tm,),
        in_specs=[pl.BlockSpec((tm, d), lambda i: (i, 0)), _const_spec((1, d)), _const_spec(w_bf.shape)],
        out_specs=[pl.BlockSpec((tm, dr), lambda i: (i, 0)), pl.BlockSpec((tm, dr), lambda i: (i, 0))],
        out_shape=[jax.ShapeDtypeStruct((n, dr), F32), jax.ShapeDtypeStruct((n, dr), F32)],
        compiler_params=_cparams(1), name="rg_in",
    )(x, g, w_bf)


def _rg_gates(xc, wg_ref, bg_ref, c_ref):
    d = xc.shape[1]
    gates = _dot(xc.astype(BF), wg_ref[...]) + bg_ref[...]
    r = _sigmoid(gates[:, :d])
    i = _sigmoid(gates[:, d:])
    log_a = c_ref[...] * r
    a = jnp.exp(log_a)
    z = -jnp.tanh(log_a) * (a * a + 1.0)
    mult = jnp.where(z > 0.0, z * lax.rsqrt(z), 0.0)
    return a, mult, i * xc


def _rg_mixer_kernel(xin_ref, gm_ref, win_ref, hist_ref, h0_ref, cw_ref, cb_ref, wg_ref, bg_ref, c_ref, wout_ref,
                     out_ref, hl_ref, xl_ref, xprev, hc, *, sub):
    t = pl.program_id(1)

    @pl.when(t == 0)
    def _():
        xprev[...] = hist_ref[...]
        hc[...] = h0_ref[...]

    d = wout_ref.shape[0]
    prev = xprev[...]
    carry = hc[0:1, :]
    row = lax.broadcasted_iota(jnp.int32, (sub, d), 0)
    pieces = range(0, xin_ref.shape[0], sub)
    projs = []
    for r0 in pieces:
        hn = _rms(xin_ref[r0:r0 + sub, :], gm_ref[...]).astype(BF)
        projs.append((_dot(hn, win_ref[:, 0:d]), _dot(hn, win_ref[:, d:2 * d])))
    for (yg_pre, x), r0 in zip(projs, pieces):
        x_in = xin_ref[r0:r0 + sub, :]
        yg = _gelu(yg_pre)
        xc = cb_ref[...] + _shift_rows(x, prev, 3) * cw_ref[0:1, :]
        xc = xc + _shift_rows(x, prev, 2) * cw_ref[1:2, :]
        xc = xc + _shift_rows(x, prev, 1) * cw_ref[2:3, :]
        xc = xc + x * cw_ref[3:4, :]
        prev = x[sub - SUBLANES:, :]

        a, mult, ixc = _rg_gates(xc, wg_ref, bg_ref, c_ref)
        if r0 == 0:
            mult = jnp.where((row == 0) & (t == 0), 1.0, mult)
        u = mult * ixc

        av, bv = a, u
        dist = 1
        while dist < sub:
            if dist < SUBLANES:
                ok = row >= dist
                bv = jnp.where(ok, av * pltpu.roll(bv, dist, axis=0) + bv, bv)
                av = jnp.where(ok, av * pltpu.roll(av, dist, axis=0), av)
            else:
                bv = jnp.concatenate([bv[:dist], av[dist:] * bv[:sub - dist] + bv[dist:]], axis=0)
                av = jnp.concatenate([av[:dist], av[dist:] * av[:sub - dist]], axis=0)
            dist *= 2
        h = av * carry + bv
        carry = h[sub - 1:sub, :]
        out_ref[r0:r0 + sub, :] = x_in + _dot((yg * h).astype(BF), wout_ref[...])
    xprev[...] = prev
    hc[...] = jnp.broadcast_to(carry, (SUBLANES, d))
    hl_ref[...] = hc[...]
    xl_ref[...] = prev


def _rg_mixer(x, gm, win_bf, hist8, h08, cw8, cb, wg_bf, bg, c, wout_bf, b, t):
    n, dm = x.shape
    d = wout_bf.shape[0]
    tt = min(512, t)
    nt = t // tt
    row_spec = pl.BlockSpec((tt, dm), lambda bi, ti: (bi * nt + ti, 0))
    st_spec = pl.BlockSpec((None, SUBLANES, d), lambda bi, ti: (bi, 0, 0))
    consts = (cw8, cb, wg_bf, bg, c, wout_bf)
    return pl.pallas_call(
        functools.partial(_rg_mixer_kernel, sub=min(256, tt)),
        grid=(b, nt),
        in_specs=[row_spec, _const_spec(gm.shape), _const_spec(win_bf.shape), st_spec, st_spec]
        + [_const_spec(a.shape) for a in consts],
        out_specs=[row_spec, st_spec, st_spec],
        out_shape=[jax.ShapeDtypeStruct((n, dm), F32), jax.ShapeDtypeStruct((b, SUBLANES, d), F32),
                   jax.ShapeDtypeStruct((b, SUBLANES, d), F32)],
        scratch_shapes=[pltpu.VMEM((SUBLANES, d), F32), pltpu.VMEM((SUBLANES, d), F32)],
        compiler_params=_cparams(2), name="rg_mixer",
    )(x, gm, win_bf, hist8, h08, *consts)


def _rg_step_kernel(xr_ref, yg_ref, h0_ref, h1_ref, h2_ref, hs_ref, cw_ref, cb_ref, wg_ref, bg_ref, c_ref,
                    out_ref, hn_ref):
    x = xr_ref[...]
    xc = cb_ref[...] + h0_ref[...] * cw_ref[0:1, :]
    xc = xc + h1_ref[...] * cw_ref[1:2, :]
    xc = xc + h2_ref[...] * cw_ref[2:3, :]
    xc = xc + x * cw_ref[3:4, :]
    a, mult, ixc = _rg_gates(xc, wg_ref, bg_ref, c_ref)
    h = mult * ixc + a * hs_ref[...]
    hn_ref[...] = h
    out_ref[...] = (yg_ref[...] * h).astype(BF)


def _rg_step(xr, yg, h0, h1, h2, hs, cw8, cb, wg_bf, bg, c):
    n, d = xr.shape
    args = (xr, yg, h0, h1, h2, hs, cw8, cb, wg_bf, bg, c)
    return pl.pallas_call(
        _rg_step_kernel,
        grid=(1,),
        in_specs=[_const_spec(a.shape) for a in args],
        out_specs=[pl.BlockSpec((n, d), lambda i: (0, 0)), pl.BlockSpec((n, d), lambda i: (0, 0))],
        out_shape=[jax.ShapeDtypeStruct((n, d), BF), jax.ShapeDtypeStruct((n, d), F32)],
        compiler_params=_cparams(1), name="rg_step",
    )(*args)


def _matmul_res_kernel(a_ref, w_ref, x_ref, o_ref):
    o_ref[...] = x_ref[...] + _dot(a_ref[...], w_ref[...])


def _matmul_res(a_bf, w_bf, x):
    n, k = a_bf.shape
    d = w_bf.shape[1]
    tm = min(512, n)
    return pl.pallas_call(
        _matmul_res_kernel,
        grid=(n // tm,),
        in_specs=[pl.BlockSpec((tm, k), lambda i: (i, 0)), _const_spec(w_bf.shape),
                  pl.BlockSpec((tm, d), lambda i: (i, 0))],
        out_specs=pl.BlockSpec((tm, d), lambda i: (i, 0)),
        out_shape=jax.ShapeDtypeStruct((n, d), F32),
        compiler_params=_cparams(1), name="matmul_res",
    )(a_bf, w_bf, x)


def _ffn_tail(x, acc, p_ref, gp_ref, wgate_ref, wpin_ref, gfin_ref, final_norm):
    x1 = x + acc
    hp = _rms(x1, gp_ref[...]).astype(BF)
    sig = _sigmoid(_dot(hp, wgate_ref[...]))
    pe = _dot(p_ref[...].astype(BF), wpin_ref[...])
    x2 = x1 + pe * sig
    if final_norm:
        x2 = _rms(x2, gfin_ref[...])
    return x2


def _ffn_kernel(x_ref, p_ref, hist_ref, gf_ref, wup_ref, cw_ref, cb_ref, wdn_ref, gp_ref, wgate_ref, wpin_ref,
                gfin_ref, o_ref, ho_ref, uprev, *, chunk, final_norm):
    t = pl.program_id(1)

    @pl.when(t == 0)
    def _():
        uprev[...] = hist_ref[...]

    x = x_ref[...]
    tm = x.shape[0]
    dff = wdn_ref.shape[0]
    h = _rms(x, gf_ref[...]).astype(BF)
    acc = jnp.zeros(x.shape, F32)
    for c0 in range(0, dff, chunk):
        halves = []
        for off in (c0, dff + c0):
            up = _dot(h, wup_ref[:, off:off + chunk])
            prev = uprev[:, off:off + chunk]
            uc = cb_ref[:, off:off + chunk] + _shift_rows(up, prev, 2) * cw_ref[0:1, off:off + chunk]
            uc = uc + _shift_rows(up, prev, 1) * cw_ref[1:2, off:off + chunk]
            uc = uc + up * cw_ref[2:3, off:off + chunk]
            uprev[:, off:off + chunk] = up[tm - SUBLANES:, :]
            halves.append(uc)
        act = (_gelu(halves[0]) * halves[1]).astype(BF)
        acc = acc + _dot(act, wdn_ref[c0:c0 + chunk, :])
    ho_ref[...] = uprev[...]
    o_ref[...] = _ffn_tail(x, acc, p_ref, gp_ref, wgate_ref, wpin_ref, gfin_ref, final_norm)


def _ffn(x, p_all, layer, hist8, gf, wup_bf, cw8, cb, wdn_bf, gp, wgate_bf, wpin_bf, gfin, b, t, final_norm):
    n, d = x.shape
    dp = p_all.shape[2]
    d2 = wup_bf.shape[1]
    tm = min(1024, t)
    nt = t // tm
    row = lambda w: pl.BlockSpec((tm, w), lambda bi, ti: (bi * nt + ti, 0))
    p_spec = pl.BlockSpec((None, tm, dp), lambda bi, ti: (layer, bi * nt + ti, 0))
    st_spec = pl.BlockSpec((None, SUBLANES, d2), lambda bi, ti: (bi, 0, 0))
    consts = (gf, wup_bf, cw8, cb, wdn_bf, gp, wgate_bf, wpin_bf, gfin)
    return pl.pallas_call(
        functools.partial(_ffn_kernel, chunk=512, final_norm=final_norm),
        grid=(b, nt),
        in_specs=[row(d), p_spec, st_spec] + [_const_spec(a.shape) for a in consts],
        out_specs=[row(d), st_spec],
        out_shape=[jax.ShapeDtypeStruct((n, d), F32), jax.ShapeDtypeStruct((b, SUBLANES, d2), F32)],
        scratch_shapes=[pltpu.VMEM((SUBLANES, d2), F32)],
        compiler_params=_cparams(2), name="ffn",
    )(x, p_all, hist8, *consts)


def _ffn_step_kernel(x_ref, p_ref, h0_ref, h1_ref, gf_ref, wup_ref, cw_ref, cb_ref, wdn_ref, gp_ref, wgate_ref,
                     wpin_ref, gfin_ref, o_ref, up_ref, *, chunk, final_norm):
    x = x_ref[...]
    dff = wdn_ref.shape[0]
    h = _rms(x, gf_ref[...]).astype(BF)
    acc = jnp.zeros(x.shape, F32)
    for c0 in range(0, dff, chunk):
        halves = []
        for off in (c0, dff + c0):
            up = _dot(h, wup_ref[:, off:off + chunk])
            up_ref[:, off:off + chunk] = up
            uc = cb_ref[:, off:off + chunk] + h0_ref[:, off:off + chunk] * cw_ref[0:1, off:off + chunk]
            uc = uc + h1_ref[:, off:off + chunk] * cw_ref[1:2, off:off + chunk]
            uc = uc + up * cw_ref[2:3, off:off + chunk]
            halves.append(uc)
        act = (_gelu(halves[0]) * halves[1]).astype(BF)
        acc = acc + _dot(act, wdn_ref[c0:c0 + chunk, :])
    o_ref[...] = _ffn_tail(x, acc, p_ref, gp_ref, wgate_ref, wpin_ref, gfin_ref, final_norm)


def _ffn_step(x, p, h0, h1, gf, wup_bf, cw8, cb, wdn_bf, gp, wgate_bf, wpin_bf, gfin, final_norm):
    n, d = x.shape
    d2 = wup_bf.shape[1]
    args = (x, p, h0, h1, gf, wup_bf, cw8, cb, wdn_bf, gp, wgate_bf, wpin_bf, gfin)
    return pl.pallas_call(
        functools.partial(_ffn_step_kernel, chunk=512, final_norm=final_norm),
        grid=(1,),
        in_specs=[_const_spec(a.shape) for a in args],
        out_specs=[pl.BlockSpec((n, d), lambda i: (0, 0)), pl.BlockSpec((n, d2), lambda i: (0, 0))],
        out_shape=[jax.ShapeDtypeStruct((n, d), F32), jax.ShapeDtypeStruct((n, d2), F32)],
        compiler_params=_cparams(1), name="ffn_step",
    )(*args)


def _kv_kernel(x_ref, g_ref, w_ref, cos_ref, sin_ref, oh_ref, kv_ref, slc_ref, win_ref, ka_ref, sva_ref, wva_ref):
    h = _rms(x_ref[...], g_ref[...]).astype(BF)
    kv = _dot(h, w_ref[...])
    cos, sin = cos_ref[...], sin_ref[...]
    w2 = 2 * LANES
    k_slc = _rope(kv[:, w2:w2 + LANES], cos, sin)
    k_win = _rope(kv[:, 2 * w2:2 * w2 + LANES], cos, sin)
    kv_ref[:, 0:w2] = kv[:, 0:w2]
    kv_ref[:, w2:w2 + LANES] = k_slc
    kv_ref[:, w2 + LANES:2 * w2] = kv[:, w2 + LANES:2 * w2]
    kv_ref[:, 2 * w2:2 * w2 + LANES] = k_win
    kv_ref[:, 2 * w2 + LANES:3 * w2] = kv[:, 2 * w2 + LANES:3 * w2]
    slc_ref[:, 0:LANES] = k_slc.astype(BF)
    slc_ref[:, LANES:w2] = kv[:, w2 + LANES:2 * w2].astype(BF)
    win_ref[:, 0:LANES] = k_win.astype(BF)
    win_ref[:, LANES:w2] = kv[:, 2 * w2 + LANES:3 * w2].astype(BF)
    low = lax.broadcasted_iota(jnp.int32, k_slc.shape, 1) < HEAD_DIM
    oh = oh_ref[...]
    ka_ref[:, 0:LANES] = jnp.where(low, k_slc, oh).astype(BF)
    ka_ref[:, LANES:w2] = jnp.where(low, oh, k_slc).astype(BF)
    for va_ref, v in ((sva_ref, kv[:, w2 + LANES:2 * w2]), (wva_ref, kv[:, 2 * w2 + LANES:3 * w2])):
        va_ref[:, 0:LANES] = jnp.where(low, v, 1.0).astype(BF)
        va_ref[:, LANES:w2] = jnp.where(low, 1.0, v).astype(BF)


def _kv_proj(x, g, w_bf, cos, sin, onehot):
    n, d = x.shape
    nk = w_bf.shape[1]
    tm = min(512, n)
    row = lambda w: pl.BlockSpec((tm, w), lambda i: (i, 0))
    return pl.pallas_call(
        _kv_kernel,
        grid=(n // tm,),
        in_specs=[row(d), _const_spec((1, d)), _const_spec(w_bf.shape), row(LANES), row(LANES), row(LANES)],
        out_specs=[row(nk)] + [row(2 * LANES)] * 5,
        out_shape=[jax.ShapeDtypeStruct((n, nk), F32)] + [jax.ShapeDtypeStruct((n, 2 * LANES), BF)] * 5,
        compiler_params=_cparams(1), name="kv_proj",
    )(x, g, w_bf, cos, sin, onehot)


def _q_kernel(x_ref, g_ref, wq_ref, wg_ref, cos_ref, sin_ref, qs_ref, qrs_ref, gt_ref):
    h = _rms(x_ref[...], g_ref[...]).astype(BF)
    cos, sin = cos_ref[...], sin_ref[...]
    scale = HEAD_DIM ** -0.5
    mxu_cols = 2 * LANES
    for c0 in range(0, wq_ref.shape[1], mxu_cols):
        q2 = _dot(h, wq_ref[:, c0:c0 + mxu_cols])
        for c1 in range(0, mxu_cols, LANES):
            q = q2[:, c1:c1 + LANES]
            qs_ref[:, c0 + c1:c0 + c1 + LANES] = (q * scale).astype(BF)
            qrs_ref[:, c0 + c1:c0 + c1 + LANES] = (_rope(q, cos, sin) * scale).astype(BF)
    gt_ref[...] = _sigmoid(_dot(h, wg_ref[...]))


def _q_proj(x, g, wq_bf, wg_bf, cos, sin):
    n, d = x.shape
    dq = wq_bf.shape[1]
    tm = min(512, n)
    row = lambda w: pl.BlockSpec((tm, w), lambda i: (i, 0))
    return pl.pallas_call(
        _q_kernel,
        grid=(n // tm,),
        in_specs=[row(d), _const_spec((1, d)), _const_spec(wq_bf.shape), _const_spec(wg_bf.shape),
                  row(LANES), row(LANES)],
        out_specs=[row(dq), row(dq), row(LANES)],
        out_shape=[jax.ShapeDtypeStruct((n, dq), BF), jax.ShapeDtypeStruct((n, dq), BF),
                   jax.ShapeDtypeStruct((n, LANES), F32)],
        compiler_params=_cparams(1), name="q_proj",
    )(x, g, wq_bf, wg_bf, cos, sin)


def _compress_kernel(xk_ref, xv_ref, pa_ref, pb_ref, w1a_ref, w1b_ref, b1_ref, w2_ref, kc_ref, vc_ref):
    nhb = kc_ref.shape[0]
    for j, x_ref, o_ref in ((0, xk_ref, kc_ref), (1, xv_ref, vc_ref)):
        rows = [x_ref[pl.ds(l, nhb, stride=D_CMP), :] for l in range(D_CMP)]
        lhs_a = jnp.concatenate([(x + pa_ref[j, l:l + 1, :]).astype(BF) for l, x in enumerate(rows)], axis=1)
        lhs_b = jnp.concatenate([(x + pb_ref[j, l:l + 1, :]).astype(BF) for l, x in enumerate(rows)], axis=1)
        hid = _gelu(_dot(lhs_a, w1a_ref[j]) + pltpu.roll(_dot(lhs_b, w1b_ref[j]), nhb - 1, axis=0) + b1_ref[j])
        o_ref[...] = _dot(hid.astype(BF), w2_ref[j]).astype(BF)


def _compress(rows, params, b, t):
    nhb = t // D_CMP
    out = pl.BlockSpec((None, nhb, LANES), lambda bi: (bi, 0, 0))
    return pl.pallas_call(
        _compress_kernel,
        grid=(b,),
        in_specs=[pl.BlockSpec((None, t, LANES), lambda bi: (bi, 0, 0)),
                  pl.BlockSpec((None, t, LANES), lambda bi: (bi, 0, 1))] + [_const_spec(a.shape) for a in params],
        out_specs=[out, out],
        out_shape=[jax.ShapeDtypeStruct((b, nhb, LANES), BF)] * 2,
        compiler_params=_cparams(1), name="compress",
    )(rows, rows, *params)


def _compress_paged_kernel(pt_ref, *refs, per_step):
    del pt_ref
    page_refs = refs[:2 * per_step]
    params = refs[2 * per_step:-4]
    kc_ref, vc_ref, xk_scr, xv_scr = refs[-4:]
    step = pl.program_id(1)
    for r in range(per_step):
        rows = pl.ds(pl.multiple_of((step * per_step + r) * PAGE_SIZE, PAGE_SIZE), PAGE_SIZE)
        xk_scr[rows, :] = page_refs[2 * r][...]
        xv_scr[rows, :] = page_refs[2 * r + 1][...]

    @pl.when(step == pl.num_programs(1) - 1)
    def _():
        _compress_kernel(xk_scr, xv_scr, *params, kc_ref, vc_ref)


def _compress_paged(cache, page_table, params):
    db, n_pages = page_table.shape
    t = n_pages * PAGE_SIZE
    nhb = t // D_CMP
    per_step = min(16, n_pages)
    in_specs = []
    for r in range(per_step):
        for half in range(2):
            in_specs.append(pl.BlockSpec((None, PAGE_SIZE, LANES), functools.partial(
                lambda bi, i, pt, r, half: (pt[bi, i * per_step + r], 0, half), r=r, half=half)))
    in_specs += [_const_spec(a.shape) for a in params]
    out = pl.BlockSpec((None, nhb, LANES), lambda bi, i, pt: (bi, 0, 0))
    return pl.pallas_call(
        functools.partial(_compress_paged_kernel, per_step=per_step),
        grid_spec=pltpu.PrefetchScalarGridSpec(
            num_scalar_prefetch=1, grid=(db, n_pages // per_step), in_specs=in_specs, out_specs=[out, out],
            scratch_shapes=[pltpu.VMEM((t, LANES), F32), pltpu.VMEM((t, LANES), F32)]),
        out_shape=[jax.ShapeDtypeStruct((db, nhb, LANES), BF)] * 2,
        compiler_params=_cparams(2), name="compress_paged",
    )(page_table, *([cache] * (2 * per_step)), *params)


def _stack_heads(q_ref, g, tq, fill=0.0):
    lane = lax.broadcasted_iota(jnp.int32, (tq, LANES), 1)
    keep = (lane >= HEAD_DIM * g) & (lane < HEAD_DIM * (g + 1))
    parts = []
    for h in range(HPG):
        hh = g * HPG + h
        ch = hh // 2
        c = q_ref[:, ch * LANES:(ch + 1) * LANES].astype(F32)
        if hh % 2 != g:
            c = pltpu.roll(c, HEAD_DIM, axis=1)
        parts.append(jnp.where(keep, c, fill))
    return jnp.concatenate(parts, axis=0).astype(BF)


def _cmp_branch(q, kc, vc, tpos_col, n_cb, tq):
    ncp = kc.shape[0]
    cidx = lax.broadcasted_iota(jnp.int32, (tq, ncp), 1)
    valid = ((cidx * D_CMP + (L_CMP - 1)) <= tpos_col) & (cidx < n_cb)
    s = _dot_nt(q, kc).reshape(HPG, tq, ncp) + jnp.where(valid, 0.0, NEG)[None]
    e = jnp.exp(s - jnp.max(s, axis=-1, keepdims=True))
    any_valid = (tpos_col >= L_CMP - 1) & (n_cb > 0)
    inv = jnp.where(any_valid[None], 1.0 / jnp.sum(e, axis=-1, keepdims=True), 0.0)
    p = e * inv
    o = _dot(p.reshape(HPG * tq, ncp).astype(BF), vc)
    return o, jnp.sum(p, axis=0)


def _cmp_branch_staged(g, q, kc, vc, tpos_col, n_cb, s_ref, p_ref, oc_ref, tq):
    ncp = kc.shape[0]
    cidx = lax.broadcasted_iota(jnp.int32, (tq, ncp), 1)
    bias = _mask_bias(((cidx * D_CMP + (L_CMP - 1)) <= tpos_col) & (cidx < n_cb))
    any_valid = (tpos_col >= L_CMP - 1) & (n_cb > 0)
    s_ref[g, :, 0:ncp] = _dot_nt(q, kc)
    psum = jnp.zeros((tq, ncp), F32)
    for h in range(HPG):
        rows = slice(h * tq, (h + 1) * tq)
        m = jnp.max(s_ref[g, rows, 0:ncp] + bias, axis=-1, keepdims=True)
        e = jnp.exp(s_ref[g, rows, 0:ncp] + bias - m)
        p = e * jnp.where(any_valid, 1.0 / jnp.sum(e, axis=-1, keepdims=True), 0.0)
        psum = psum + p
        p_ref[g, rows, 0:ncp] = p.astype(BF)
    oc_ref[g] = _dot(p_ref[g, :, 0:ncp], vc)
    return psum


def _importance(psum, mt_ref, tpos_row):
    sp = mt_ref.shape[0]
    mt = mt_ref[...]
    p_hi = psum.astype(BF)
    r1 = psum - p_hi.astype(F32)
    p_mid = r1.astype(BF)
    p_lo = (r1 - p_mid.astype(F32)).astype(BF)
    imp = _dot_nt(mt, p_hi) + _dot_nt(mt, p_mid) + _dot_nt(mt, p_lo)
    s_idx = lax.broadcasted_iota(jnp.int32, (sp, LANES), 0)
    cur = jnp.right_shift(tpos_row, L_SLC.bit_length() - 1)
    forced = (s_idx == 0) | (s_idx == cur) | (s_idx == cur - 1)
    imp = jnp.where(forced, jnp.inf, imp)
    return jnp.where(s_idx * L_SLC <= tpos_row, imp, -jnp.inf)


def _select_blocks(psums, mt_ref, imp_ref, tpos_row, n_sb, tq):
    sp = mt_ref.shape[0]
    assert n_sb <= HEAD_DIM and sp == LANES, "one-hot block lanes hold at most 64 selection blocks"
    nr = _round_up(n_sb, SUBLANES)
    imps = []
    for g, psum in enumerate(psums):
        imp = _importance(psum, mt_ref, tpos_row)
        imp_ref[g] = imp
        imps.append(imp[:nr])
    s_idx = lax.broadcasted_iota(jnp.int32, (nr, LANES), 0)

    def body(i, cnts):
        tie = jnp.where(s_idx > i, 1.0, 0.0)
        out = []
        for g, imp in enumerate(imps):
            row = imp_ref[g, pl.ds(i, 1), :]
            out.append(cnts[g] + jnp.where(row > imp, 1.0, jnp.where(row == imp, tie, 0.0)))
        return tuple(out)

    cnts = lax.fori_loop(0, n_sb, body, tuple(jnp.zeros((nr, LANES), F32) for _ in psums))
    starts_before = s_idx * L_SLC <= tpos_row
    biases = []
    for g, cnt in enumerate(cnts):
        bias_t = jnp.where((cnt < float(N_SEL)) & starts_before, 0.0, NEG)
        if nr < sp:
            bias_t = jnp.concatenate([bias_t, jnp.full((sp - nr, LANES), NEG, F32)], axis=0)
        bias = bias_t.T[:tq]
        biases.append(pltpu.roll(bias, HEAD_DIM, axis=1) if g == 0 else bias)
    return biases


def _select_ids(psum, mt_ref, tpos):
    sp = mt_ref.shape[0]
    prow = jnp.broadcast_to(psum[0:1], (LANES, psum.shape[1]))
    imp = _importance(prow, mt_ref, jnp.full((1, LANES), tpos, jnp.int32))
    a = jnp.broadcast_to(imp[:, 0:1], (sp, sp))
    b = a.T
    r_i = lax.broadcasted_iota(jnp.int32, (sp, sp), 0)
    c_i = lax.broadcasted_iota(jnp.int32, (sp, sp), 1)
    beats = jnp.where(a > b, 1.0, jnp.where(a == b, jnp.where(r_i < c_i, 1.0, 0.0), 0.0))
    rank = jnp.sum(beats, axis=0, keepdims=True)
    slot = lax.broadcasted_iota(jnp.int32, (N_SEL, sp), 0).astype(F32)
    blk = lax.broadcasted_iota(jnp.int32, (N_SEL, sp), 1).astype(F32)
    ids = jnp.sum(jnp.where(rank == slot, blk, 0.0), axis=1, keepdims=True)
    return jnp.broadcast_to(ids, (N_SEL, LANES)).astype(jnp.int32)


def _flash_init(tq):
    return (jnp.full((HPG, tq, 1), NEG, F32), jnp.zeros((HPG, tq, 1), F32), jnp.zeros((HPG * tq, LANES), F32))


def _flash_tile(carry, q, k, v, bias, tq):
    m, l, acc = carry
    kt = k.shape[0]
    s = _dot_nt(q, k).reshape(HPG, tq, kt)
    if bias is not None:
        s = s + bias[None]
    m_new = jnp.maximum(m, jnp.max(s, axis=-1, keepdims=True))
    alpha = jnp.exp(m - m_new)
    p = jnp.exp(s - m_new)
    l = alpha * l + jnp.sum(p, axis=-1, keepdims=True)
    acc = alpha.reshape(HPG * tq, 1) * acc + _dot(p.reshape(HPG * tq, kt).astype(BF), v)
    return m_new, l, acc


def _flash_out(carry, tq):
    _, l, acc = carry
    return acc / l.reshape(HPG * tq, 1)


def _attend_once(q, k, v, bias, tq):
    kt = k.shape[0]
    s = _dot_nt(q, k).reshape(HPG, tq, kt) + bias[None]
    p = jnp.exp(s - jnp.max(s, axis=-1, keepdims=True))
    l = jnp.sum(p, axis=-1, keepdims=True)
    return _dot(p.reshape(HPG * tq, kt).astype(BF), v) / l.reshape(HPG * tq, 1)


def _mask_bias(valid):
    return jnp.where(valid, 0.0, NEG)


def _win_valid(kpos_row, tpos_col):
    dist = tpos_col - kpos_row
    return (dist >= 0) & (dist < WINDOW)


def _merge_heads(o_c, o_s, o_w, gt_ref, g, o_ref, tq):
    lane = lax.broadcasted_iota(jnp.int32, (tq, LANES), 1)
    for pair in range(HPG // 2):
        both = []
        for h in (2 * pair, 2 * pair + 1):
            col = (g * HPG + h) * 3
            rows = slice(h * tq, (h + 1) * tq)
            both.append(o_c(rows) * gt_ref[:, col:col + 1] + o_s(rows) * gt_ref[:, col + 1:col + 2]
                        + o_w(rows) * gt_ref[:, col + 2:col + 3])
        even, odd = both
        if g == 0:
            odd = pltpu.roll(odd, HEAD_DIM, axis=1)
        else:
            even = pltpu.roll(even, HEAD_DIM, axis=1)
        ch = g * (HPG // 2) + pair
        o_ref[:, ch * LANES:(ch + 1) * LANES] = jnp.where(lane < HEAD_DIM, even, odd).astype(o_ref.dtype)


def _staged_tile(state, stage, q, k, v, bias, refs, tq, first):
    s_ref, p_ref, m_ref, acc_ref = refs
    kt = k.shape[0]
    reps = kt // LANES
    s_ref[stage, :, 0:kt] = _dot_nt(q, k)
    for h in range(HPG):
        rows = slice(h * tq, (h + 1) * tq)

        def scores():
            s = s_ref[stage, rows, 0:kt]
            return s if bias is None else s + bias

        m_tile = jnp.max(scores(), axis=-1, keepdims=True)
        if first:
            m_new = jnp.broadcast_to(m_tile, (tq, LANES))
        else:
            m_old = m_ref[state, rows, :]
            m_new = jnp.maximum(m_old, m_tile)
            acc_ref[state, rows, :] = jnp.exp(m_old - m_new) * acc_ref[state, rows, :]
        m_ref[state, rows, :] = m_new
        p_ref[stage, rows, 0:kt] = jnp.exp(scores() - jnp.concatenate([m_new] * reps, axis=1)).astype(BF)
    pv = _dot(p_ref[stage, :, 0:kt], v)
    if first:
        acc_ref[state] = pv
    else:
        acc_ref[state] = acc_ref[state] + pv


def _attn_prompt_kernel(qs_ref, qrs_ref, gt_ref, kc_ref, vc_ref, ka_ref, sva_ref, win_ref, wva_ref, mt_ref, o_ref,
                        imp_ref, s_ref, p_ref, m_ref, acc_ref, oc_ref, *, n_cb, n_sb):
    tq = qs_ref.shape[0]
    t_all = ka_ref.shape[0]
    qb = pl.program_id(1)
    t0 = qb * tq
    tpos_col = t0 + lax.broadcasted_iota(jnp.int32, (tq, 1), 0)
    tpos_row = t0 + lax.broadcasted_iota(jnp.int32, (1, LANES), 1)
    groups = range(N_KV_HEADS)
    k_lanes, v_lanes = slice(0, LANES), slice(LANES, 2 * LANES)

    psums = [_cmp_branch_staged(g, _stack_heads(qs_ref, g, tq), kc_ref[...], vc_ref[...], tpos_col, n_cb,
                                s_ref, p_ref, oc_ref, tq) for g in groups]
    blk_bias = _select_blocks(psums, mt_ref, imp_ref, tpos_row, n_sb, tq)
    qr_blk = [_stack_heads(qrs_ref, g, tq, fill=blk_bias[g]) for g in groups]

    refs = (s_ref, p_ref, m_ref, acc_ref)
    g_lanes = [slice(g * LANES, (g + 1) * LANES) for g in groups]

    def slc_tile(j, bias, first):
        rows = pl.ds(pl.multiple_of(j * SLC_TILE, SLC_TILE), SLC_TILE)
        for g in groups:
            _staged_tile(g, g, qr_blk[g], ka_ref[rows, g_lanes[g]], sva_ref[rows, g_lanes[g]], bias, refs, tq, first)

    last = (t0 + tq - 1) // SLC_TILE
    causal = (last * SLC_TILE + lax.broadcasted_iota(jnp.int32, (1, SLC_TILE), 1)) <= tpos_col
    slc_tile(last, _mask_bias(causal), True)

    def slc_body(j, c):
        slc_tile(j, None, False)
        return c

    lax.fori_loop(0, last, slc_body, 0)

    wk = min(WINDOW + tq, t_all)
    start = pl.multiple_of(jnp.clip(t0 - WINDOW, 0, t_all - wk), tq)
    wbias = _mask_bias(_win_valid(start + lax.broadcasted_iota(jnp.int32, (1, wk), 1), tpos_col))
    wrows = pl.ds(start, wk)
    kw = win_ref[wrows, k_lanes]
    for g in groups:
        _staged_tile(N_KV_HEADS + g, g, _stack_heads(qrs_ref, g, tq), kw, wva_ref[wrows, g_lanes[g]], wbias, refs,
                     tq, True)

    def normalised(slot, rows):
        acc = acc_ref[slot, rows, :]
        return acc / pltpu.roll(acc, HEAD_DIM, axis=1)

    for g in groups:
        _merge_heads(lambda rows: oc_ref[g, rows, :], functools.partial(normalised, g),
                     functools.partial(normalised, N_KV_HEADS + g), gt_ref, g, o_ref, tq)


def _attn_prompt(qs, qrs, gates, kc, vc, ka_bf, slc_va, win_bf, win_va, mt, b, t, n_cb, n_sb):
    tq = Q_BLOCK
    nqb = t // tq
    d = qs.shape[1]
    stage_w = max(SLC_TILE, min(WINDOW + tq, t))
    row = lambda w: pl.BlockSpec((tq, w), lambda bi, qi: (bi * nqb + qi, 0))
    per_b = lambda a: pl.BlockSpec((None,) + a.shape[1:], lambda bi, qi: (bi, 0, 0))
    return pl.pallas_call(
        functools.partial(_attn_prompt_kernel, n_cb=n_cb, n_sb=n_sb),
        grid=(b, nqb),
        in_specs=[row(d), row(d), row(LANES), per_b(kc), per_b(vc), per_b(ka_bf), per_b(slc_va), per_b(win_bf),
                  per_b(win_va), _const_spec(mt.shape)],
        out_specs=row(d),
        out_shape=jax.ShapeDtypeStruct((b * t, d), BF),
        scratch_shapes=[pltpu.VMEM((N_KV_HEADS, mt.shape[0], LANES), F32),
                        pltpu.VMEM((N_KV_HEADS, HPG * tq, stage_w), F32),
                        pltpu.VMEM((N_KV_HEADS, HPG * tq, stage_w), BF),
                        pltpu.VMEM((2 * N_KV_HEADS, HPG * tq, LANES), F32),
                        pltpu.VMEM((2 * N_KV_HEADS, HPG * tq, LANES), F32),
                        pltpu.VMEM((N_KV_HEADS, HPG * tq, LANES), F32)],
        compiler_params=_cparams(2), name="attn_prompt",
    )(qs, qrs, gates, kc, vc, ka_bf, slc_va, win_bf, win_va, mt)


def _sample_select_kernel(qs_ref, kc_ref, vc_ref, mt_ref, oc_ref, ids_ref, *, n_cb, past):
    tq = qs_ref.shape[0]
    tpos_col = jnp.full((tq, 1), past, jnp.int32)
    for g in range(N_KV_HEADS):
        o_c, psum = _cmp_branch(_stack_heads(qs_ref, g, tq), kc_ref[...], vc_ref[...], tpos_col, n_cb, tq)
        oc_ref[g] = o_c
        ids_ref[g] = _select_ids(psum, mt_ref, past)


def _sample_select(qs, kc, vc, mt, n_cb, past):
    db, tq, _ = qs.shape
    per_b = lambda a: pl.BlockSpec((None,) + a.shape[1:], lambda bi: (bi,) + (0,) * (a.ndim - 1))
    oc = jax.ShapeDtypeStruct((db, N_KV_HEADS, HPG * tq, LANES), F32)
    ids = jax.ShapeDtypeStruct((db, N_KV_HEADS, N_SEL, LANES), jnp.int32)
    return pl.pallas_call(
        functools.partial(_sample_select_kernel, n_cb=n_cb, past=past),
        grid=(db,),
        in_specs=[per_b(qs), per_b(kc), per_b(vc), _const_spec(mt.shape)],
        out_specs=[per_b(oc), per_b(ids)],
        out_shape=[oc, ids],
        compiler_params=_cparams(1), name="sample_select",
    )(qs, kc, vc, mt)


def _sample_attend_kernel(pt_ref, ids_ref, qrs_ref, gt_ref, oc_ref, slct_ref, win_ref, wint_ref, *rest, past, n_cached):
    del pt_ref
    blk_refs, o_ref, k_scr, v_scr = rest[:-3], rest[-3], rest[-2], rest[-1]
    b = pl.program_id(0)
    tq = qrs_ref.shape[0]
    tpos_col = jnp.full((tq, 1), past, jnp.int32)
    w_buf = win_ref.shape[0]
    tail = slct_ref.shape[0]
    k_lanes, v_lanes = slice(0, LANES), slice(LANES, 2 * LANES)
    tail_pos = past + lax.broadcasted_iota(jnp.int32, (1, tail), 1)
    win_pos = past - w_buf + lax.broadcasted_iota(jnp.int32, (1, w_buf), 1)
    wbias = jnp.broadcast_to(_mask_bias(_win_valid(win_pos, tpos_col)), (tq, w_buf))
    wtbias = jnp.broadcast_to(_mask_bias(_win_valid(tail_pos, tpos_col)), (tq, tail))
    tbias = jnp.broadcast_to(_mask_bias(tail_pos <= tpos_col), (tq, tail))
    kw, vw = win_ref[:, k_lanes].astype(BF), win_ref[:, v_lanes].astype(BF)
    lane = lax.broadcasted_iota(jnp.int32, (1, LANES), 1)
    for g in range(N_KV_HEADS):
        for r in range(N_SEL):
            blk = blk_refs[g * N_SEL + r]
            k_scr[g, r * L_SLC:(r + 1) * L_SLC, :] = blk[:, k_lanes].astype(BF)
            v_scr[g, r * L_SLC:(r + 1) * L_SLC, :] = blk[:, v_lanes].astype(BF)
        chunks = []
        for c in range(N_SEL // 2):
            lo = jnp.where(ids_ref[b, g * N_SEL + 2 * c] >= n_cached, NEG, 0.0)
            hi = jnp.where(ids_ref[b, g * N_SEL + 2 * c + 1] >= n_cached, NEG, 0.0)
            chunks.append(jnp.where(lane < L_SLC, lo, hi))
        sbias = jnp.broadcast_to(jnp.concatenate(chunks, axis=1), (tq, N_SEL * L_SLC))
        qr = _stack_heads(qrs_ref, g, tq)
        carry = _flash_tile(_flash_init(tq), qr, k_scr[g], v_scr[g], sbias, tq)
        carry = _flash_tile(carry, qr, slct_ref[:, k_lanes], slct_ref[:, v_lanes], tbias, tq)
        o_s = _flash_out(carry, tq)
        carry = _flash_tile(_flash_init(tq), qr, kw, vw, wbias, tq)
        carry = _flash_tile(carry, qr, wint_ref[:, k_lanes], wint_ref[:, v_lanes], wtbias, tq)
        o_w = _flash_out(carry, tq)
        _merge_heads(lambda rows: oc_ref[g, rows, :], lambda rows: o_s[rows], lambda rows: o_w[rows], gt_ref, g,
                     o_ref, tq)


def _sample_attend(qrs, gates, oc, slc_cache, page_table, ids, slc_tail, win_c, win_tail, past):
    db = qrs.shape[0]
    per_page = PAGE_SIZE // L_SLC
    n_cached = past // L_SLC
    blocks = slc_cache
    per_b = lambda a: pl.BlockSpec((None,) + a.shape[1:], lambda bi, pt, sel: (bi,) + (0,) * (a.ndim - 1))

    def blk_map(bi, pt, sel, g, r):
        s = jnp.clip(sel[bi, g * N_SEL + r], 0, n_cached - 1)
        return pt[bi, s // per_page], s % per_page, 0

    blk_specs = [pl.BlockSpec((None, L_SLC, blocks.shape[2]), functools.partial(blk_map, g=g, r=r))
                 for g in range(N_KV_HEADS) for r in range(N_SEL)]
    args = (qrs, gates, oc, slc_tail, win_c, win_tail)
    return pl.pallas_call(
        functools.partial(_sample_attend_kernel, past=past, n_cached=n_cached),
        grid_spec=pltpu.PrefetchScalarGridSpec(
            num_scalar_prefetch=2, grid=(db,),
            in_specs=[per_b(a) for a in args] + blk_specs,
            out_specs=per_b(qrs),
            scratch_shapes=[pltpu.VMEM((N_KV_HEADS, N_SEL * L_SLC, LANES), BF)] * 2),
        out_shape=jax.ShapeDtypeStruct(qrs.shape, BF),
        compiler_params=_cparams(1), name="sample_attend",
    )(page_table, ids, *args, *([blocks] * len(blk_specs)))


def _block_diag(blocks):
    n, k, j = blocks.shape
    eye = jnp.eye(n, dtype=blocks.dtype)
    return (eye[:, None, :, None] * blocks[:, :, None, :]).reshape(n * k, n * j)


def _pad_rows8(rows):
    return jnp.pad(rows, ((0, 0), (SUBLANES - rows.shape[1], 0), (0, 0)))


def _rope_tables(pos):
    half = HEAD_DIM // 2
    inv_freq = ROPE_THETA ** (-jnp.arange(half, dtype=F32) / half)
    ang = pos.astype(F32)[:, None] * inv_freq[None, :]
    cos, sin = jnp.cos(ang), jnp.sin(ang)
    reps = LANES // HEAD_DIM
    return (jnp.tile(jnp.concatenate([cos, cos], axis=1), (1, reps)),
            jnp.tile(jnp.concatenate([-sin, sin], axis=1), (1, reps)))


def _block_onehot(pos):
    lane = jnp.arange(LANES, dtype=jnp.int32)[None, :] % HEAD_DIM
    return (lane == (pos[:, None] // L_SLC)).astype(F32)


def _overlap_t(n_cb, n_sb, ncp, sp):
    c0 = np.arange(ncp)[None, :] * D_CMP
    s0 = np.arange(sp)[:, None] * L_SLC
    m = (c0 < s0 + L_SLC) & (c0 + L_CMP > s0) & (np.arange(ncp)[None, :] < n_cb) & (np.arange(sp)[:, None] < n_sb)
    return jnp.asarray(m, dtype=BF)


def _round_up(x, m):
    return (x + m - 1) // m * m


def kernel(x_prompt, x_sample, p_prompt, p_sample, cache_cmp_kv, cache_slc_kv, cache_win_kv, state_rg_conv,
           state_rg_h, state_ffn_conv, page_table, g_mix, g_ffn, g_ple, g_final, rg_w_in, rg_conv_w, rg_conv_b,
           rg_w_a, rg_b_a, rg_w_x, rg_b_x, rg_lambda, rg_w_out, g_kv, w_kv, cmp_pos, cmp_w1, cmp_b1, cmp_w2,
           attn_w_qg, attn_w_o, ffn_w_up, ffn_conv_w, ffn_conv_b, ffn_w_down, ple_w_in, ple_w_gate):
    b, t, d = x_prompt.shape
    db = x_sample.shape[0]
    assert x_sample.shape[1] == 1, "the sample path handles one new token per sequence"
    depth = g_mix.shape[0]
    n_a = rg_w_in.shape[0]
    d_rnn = rg_w_out.shape[1]
    n_pages = page_table.shape[1]
    past = n_pages * PAGE_SIZE
    kvc = 2 * N_KV_HEADS * HEAD_DIM
    n_q = N_KV_HEADS * HPG * HEAD_DIM

    row1 = lambda v: v.reshape(1, -1)
    pad8 = lambda w: jnp.pad(w, ((0, SUBLANES - w.shape[0]), (0, 0)))

    rg_in_bf = rg_w_in.astype(BF)
    rg_out_bf = rg_w_out.astype(BF)
    rg_gate_bf = [jnp.concatenate([_block_diag(rg_w_a[i]), _block_diag(rg_w_x[i])], axis=1).astype(BF)
                  for i in range(n_a)]
    rg_gate_b = [row1(jnp.concatenate([rg_b_a[i].reshape(-1), rg_b_x[i].reshape(-1)])) for i in range(n_a)]
    rg_c = [row1(-RG_C * jax.nn.softplus(-rg_lambda[i])) for i in range(n_a)]
    up_bf = ffn_w_up.astype(BF)
    down_bf = ffn_w_down.astype(BF)
    gate_bf = ple_w_gate.astype(BF)
    pin_bf = ple_w_in.astype(BF)
    wkv_bf = w_kv.astype(BF)
    wq_bf = attn_w_qg[:, :, :n_q].astype(BF)
    wgl_bf = jnp.pad(attn_w_qg[:, :, n_q:], ((0, 0), (0, 0), (0, LANES - 3 * N_KV_HEADS * HPG))).astype(BF)
    wo_bf = attn_w_o.astype(BF)
    gfin = row1(g_final)

    w1 = cmp_w1.reshape(2, 2, D_CMP, HEAD_DIM, -1)
    eye_g = jnp.eye(N_KV_HEADS, dtype=F32)
    w1_bd = (eye_g[None, None, None, :, None, :, None] * w1[:, :, :, None, :, None, :]).reshape(
        2, 2, D_CMP, N_KV_HEADS * HEAD_DIM, -1).astype(BF)
    pos_t = jnp.tile(jnp.transpose(cmp_pos, (1, 0, 2)), (1, 1, N_KV_HEADS)).reshape(2, 2, D_CMP, LANES)
    w1_rows = D_CMP * N_KV_HEADS * HEAD_DIM
    cmp_params = (pos_t[:, 0], pos_t[:, 1], w1_bd[:, 0].reshape(2, w1_rows, -1), w1_bd[:, 1].reshape(2, w1_rows, -1),
                  jnp.tile(cmp_b1, (1, N_KV_HEADS)).reshape(2, 1, -1),
                  jnp.stack([_block_diag(jnp.stack([cmp_w2[j]] * N_KV_HEADS)) for j in range(2)]).astype(BF))

    def layer_consts(i, final):
        return (row1(g_ffn[i]), up_bf[i], pad8(ffn_conv_w[i]), row1(ffn_conv_b[i]), down_bf[i], row1(g_ple[i]),
                gate_bf[i], pin_bf[i], gfin), final

    n = b * t
    x = x_prompt.reshape(n, d)
    p = p_prompt.reshape(depth, n, -1)
    pos_p = jnp.tile(jnp.arange(t, dtype=jnp.int32), b)
    cos_p, sin_p = _rope_tables(pos_p)
    rg_conv_p, rg_h_p, ffn_conv_p = [], [], []
    zeros_rg = jnp.zeros((b, SUBLANES, d_rnn), F32)
    zeros_ffn = jnp.zeros((b, SUBLANES, ffn_w_up.shape[2]), F32)
    for i in range(depth):
        if i == n_a:
            kv_p, _, win_bf, ka_bf, slc_va, win_va = _kv_proj(x, row1(g_kv), wkv_bf, cos_p, sin_p,
                                                              _block_onehot(pos_p))
            kc_p, vc_p = _compress(kv_p.reshape(b, t, -1), cmp_params, b, t)
            n_cb_p = (t - L_CMP) // D_CMP + 1
            n_sb_p = t // L_SLC
            mt_p = _overlap_t(n_cb_p, n_sb_p, t // D_CMP, _round_up(n_sb_p, LANES))
            ka_bf, slc_va, win_bf, win_va = (a.reshape(b, t, -1) for a in (ka_bf, slc_va, win_bf, win_va))
        if i < n_a:
            x, h_last, xr_last = _rg_mixer(x, row1(g_mix[i]), rg_in_bf[i], zeros_rg, zeros_rg, pad8(rg_conv_w[i]),
                                           row1(rg_conv_b[i]), rg_gate_bf[i], rg_gate_b[i], rg_c[i], rg_out_bf[i],
                                           b, t)
            rg_conv_p.append(xr_last[:, SUBLANES - 3:])
            rg_h_p.append(h_last[:, 0])
        else:
            j = i - n_a
            qs, qrs, gates = _q_proj(x, row1(g_mix[i]), wq_bf[j], wgl_bf[j], cos_p, sin_p)
            o = _attn_prompt(qs, qrs, gates, kc_p, vc_p, ka_bf, slc_va, win_bf, win_va, mt_p, b, t, n_cb_p, n_sb_p)
            x = _matmul_res(o, wo_bf[j], x)
        consts, final = layer_consts(i, i == depth - 1)
        x, f_hist = _ffn(x, p, i, zeros_ffn, *consts, b, t, final)
        ffn_conv_p.append(f_hist[:, SUBLANES - 2:])
    y_prompt = x.reshape(b, t, d)
    kv5 = kv_p.reshape(b, t, 6, N_KV_HEADS, HEAD_DIM)
    keep = min(WINDOW, t)
    cmp_p, slc_p, win_p = kv5[:, :, 0:2], kv5[:, :, 2:4], kv5[:, t - keep:, 4:6]

    x = x_sample.reshape(db, d)
    p = p_sample.reshape(depth, db, -1)
    cos_s, sin_s = _rope_tables(jnp.full((db,), past, jnp.int32))
    rg_conv_s, rg_h_s, ffn_conv_s = [], [], []
    for i in range(depth):
        if i == n_a:
            pos_s = jnp.full((db,), past, jnp.int32)
            kv_s, slc_new_bf, win_new_bf = _kv_proj(x, row1(g_kv), wkv_bf, cos_s, sin_s, _block_onehot(pos_s))[:3]
            kc_s, vc_s = _compress_paged(cache_cmp_kv.reshape(-1, PAGE_SIZE, kvc), page_table, cmp_params)
            slc_cache = cache_slc_kv.reshape(-1, PAGE_SIZE, kvc)
            n_cb_s = (past + 1 - L_CMP) // D_CMP + 1
            n_sb_s = -(-(past + 1) // L_SLC)
            sp_s = _round_up(n_sb_s, LANES)
            mt_s = _overlap_t(n_cb_s, n_sb_s, past // D_CMP, sp_s)
            tail = lambda a: jnp.pad(a.reshape(db, 1, -1), ((0, 0), (0, LANES - 1), (0, 0)))
            slc_tail, win_tail = tail(slc_new_bf), tail(win_new_bf)
            win_c = cache_win_kv.reshape(db, cache_win_kv.shape[1], kvc)
        if i < n_a:
            yg, xr = _rg_in(x, row1(g_mix[i]), rg_in_bf[i])
            hist = state_rg_conv[i]
            gated, h_new = _rg_step(xr, yg, hist[:, 0], hist[:, 1], hist[:, 2], state_rg_h[i], pad8(rg_conv_w[i]),
                                    row1(rg_conv_b[i]), rg_gate_bf[i], rg_gate_b[i], rg_c[i])
            x = _matmul_res(gated, rg_out_bf[i], x)
            rg_conv_s.append(jnp.concatenate([hist[:, 1:], xr[:, None, :]], axis=1))
            rg_h_s.append(h_new)
        else:
            j = i - n_a
            qs, qrs, gates = _q_proj(x, row1(g_mix[i]), wq_bf[j], wgl_bf[j], cos_s, sin_s)
            rep = lambda a: jnp.broadcast_to(a[:, None, :], (db, SUBLANES, a.shape[1]))
            oc, ids = _sample_select(rep(qs), kc_s, vc_s, mt_s, n_cb_s, past)
            o = _sample_attend(rep(qrs), rep(gates), oc, slc_cache, page_table, ids[:, :, :, 0].reshape(db, -1),
                               slc_tail, win_c,
                               win_tail, past)
            x = _matmul_res(o[:, 0], wo_bf[j], x)
        consts, final = layer_consts(i, i == depth - 1)
        fh = state_ffn_conv[i]
        x, up = _ffn_step(x, p[i], fh[:, 0], fh[:, 1], *consts, final)
        ffn_conv_s.append(jnp.stack([fh[:, 1], up], axis=1))
    y_sample = x.reshape(db, 1, d)
    kv5 = kv_s.reshape(db, 1, 6, N_KV_HEADS, HEAD_DIM)
    cmp_s, slc_s, win_new = kv5[:, :, 0:2], kv5[:, :, 2:4], kv5[:, :, 4:6]
    win_full = jnp.concatenate([cache_win_kv, win_new], axis=1)
    keep = min(WINDOW, past + 1)
    win_s = win_full[:, win_full.shape[1] - keep:]

    return (y_prompt, y_sample, cmp_p, cmp_s, slc_p, slc_s, win_p, win_s,
            jnp.stack(rg_conv_p), jnp.stack(rg_conv_s), jnp.stack(rg_h_p), jnp.stack(rg_h_s),
            jnp.stack(ffn_conv_p), jnp.stack(ffn_conv_s))
```

```python
import functools

import jax
import jax.numpy as jnp
import numpy as np
from jax import lax
from jax.experimental import pallas as pl
from jax.experimental.pallas import tpu as pltpu

BF = jnp.bfloat16
F32 = jnp.float32

EPS = 1e-6
RG_C = 8.0
N_RG_BLOCKS = 16
HEAD_DIM = 64
N_KV_HEADS = 2
HPG = 8
L_CMP = 32
D_CMP = 16
L_SLC = 64
N_SEL = 16
WINDOW = 512
Q_BLOCK = 128
ROPE_THETA = 10000.0
PAGE_SIZE = 128
NEG = -1e30
LANES = 128
SUBLANES = 8
SLC_TILE = 512
WIN_TILE = 128
VMEM_LIMIT = 56 * 1024 * 1024


def _cparams(n_axes):
    return pltpu.CompilerParams(dimension_semantics=("arbitrary",) * n_axes,
                                vmem_limit_bytes=VMEM_LIMIT)


def _const_spec(shape):
    zeros = (0,) * len(shape)
    return pl.BlockSpec(shape, lambda *_: zeros, pipeline_mode=pl.Buffered(1))


def _dot(a, b):
    return jnp.dot(a, b, preferred_element_type=F32)


def _dot_nt(a, b):
    return lax.dot_general(a, b, (((1,), (1,)), ((), ())), preferred_element_type=F32)


def _rms(x, g):
    return x * lax.rsqrt(jnp.mean(x * x, axis=-1, keepdims=True) + EPS) * g


def _gelu(x):
    return jax.nn.gelu(x, approximate=True)


def _sigmoid(x):
    return jax.nn.sigmoid(x)


def _shift_rows(x, prev8, j):
    if j == 0:
        return x
    r = pltpu.roll(x, j, axis=0)
    p = pltpu.roll(prev8, j, axis=0)
    row = lax.broadcasted_iota(jnp.int32, (SUBLANES, x.shape[1]), 0)
    head = jnp.where(row < j, p, r[:SUBLANES])
    if x.shape[0] == SUBLANES:
        return head
    return jnp.concatenate([head, r[SUBLANES:]], axis=0)


def _rope(z, cos, sin_signed):
    lane = lax.broadcasted_iota(jnp.int32, z.shape, 1)
    first = (lane & (HEAD_DIM - 1)) < (HEAD_DIM // 2)
    rot = jnp.where(first, pltpu.roll(z, LANES - HEAD_DIM // 2, axis=1), pltpu.roll(z, HEAD_DIM // 2, axis=1))
    return z * cos + rot * sin_signed


def _rg_in_kernel(x_ref, g_ref, w_ref, yg_ref, xr_ref, *, chunk):
    h = _rms(x_ref[...], g_ref[...]).astype(BF)
    n = yg_ref.shape[1]
    for c0 in range(0, n, chunk):
        yg_ref[:, c0:c0 + chunk] = _gelu(_dot(h, w_ref[:, c0:c0 + chunk]))
        xr_ref[:, c0:c0 + chunk] = _dot(h, w_ref[:, n + c0:n + c0 + chunk])


def _rg_in(x, g, w_bf):
    n, d = x.shape
    dr = w_bf.shape[1] // 2
    tm = min(512, n)
    return pl.pallas_call(
        functools.partial(_rg_in_kernel, chunk=256),
        grid=(n // tm,),
        in_specs=[pl.BlockSpec((tm, d), lambda i: (i, 0)), _const_spec((1, d)), _const_spec(w_bf.shape)],
        out_specs=[pl.BlockSpec((tm, dr), lambda i: (i, 0)), pl.BlockSpec((tm, dr), lambda i: (i, 0))],
        out_shape=[jax.ShapeDtypeStruct((n, dr), F32), jax.ShapeDtypeStruct((n, dr), F32)],
        compiler_params=_cparams(1), name="rg_in",
    )(x, g, w_bf)


def _rg_gates(xc, wg_ref, bg_ref, c_ref):
    d = xc.shape[1]
    gates = _dot(xc.astype(BF), wg_ref[...]) + bg_ref[...]
    r = _sigmoid(gates[:, :d])
    i = _sigmoid(gates[:, d:])
    log_a = c_ref[...] * r
    a = jnp.exp(log_a)
    z = -jnp.tanh(log_a) * (a * a + 1.0)
    mult = jnp.where(z > 0.0, z * lax.rsqrt(z), 0.0)
    return a, mult, i * xc


def _rg_mixer_kernel(xin_ref, gm_ref, win_ref, hist_ref, h0_ref, cw_ref, cb_ref, wg_ref, bg_ref, c_ref, wout_ref,
                     out_ref, hl_ref, xl_ref, xprev, hc, *, sub):
    t = pl.program_id(1)

    @pl.when(t == 0)
    def _():
        xprev[...] = hist_ref[...]
        hc[...] = h0_ref[...]

    d = wout_ref.shape[0]
    prev = xprev[...]
    carry = hc[0:1, :]
    row = lax.broadcasted_iota(jnp.int32, (sub, d), 0)
    pieces = range(0, xin_ref.shape[0], sub)
    projs = []
    for r0 in pieces:
        hn = _rms(xin_ref[r0:r0 + sub, :], gm_ref[...]).astype(BF)
        projs.append((_dot(hn, win_ref[:, 0:d]), _dot(hn, win_ref[:, d:2 * d])))
    for (yg_pre, x), r0 in zip(projs, pieces):
        x_in = xin_ref[r0:r0 + sub, :]
        yg = _gelu(yg_pre)
        xc = cb_ref[...] + _shift_rows(x, prev, 3) * cw_ref[0:1, :]
        xc = xc + _shift_rows(x, prev, 2) * cw_ref[1:2, :]
        xc = xc + _shift_rows(x, prev, 1) * cw_ref[2:3, :]
        xc = xc + x * cw_ref[3:4, :]
        prev = x[sub - SUBLANES:, :]

        a, mult, ixc = _rg_gates(xc, wg_ref, bg_ref, c_ref)
        if r0 == 0:
            mult = jnp.where((row == 0) & (t == 0), 1.0, mult)
        u = mult * ixc

        av, bv = a, u
        dist = 1
        while dist < sub:
            if dist < SUBLANES:
                ok = row >= dist
                bv = jnp.where(ok, av * pltpu.roll(bv, dist, axis=0) + bv, bv)
                av = jnp.where(ok, av * pltpu.roll(av, dist, axis=0), av)
            else:
                bv = jnp.concatenate([bv[:dist], av[dist:] * bv[:sub - dist] + bv[dist:]], axis=0)
                av = jnp.concatenate([av[:dist], av[dist:] * av[:sub - dist]], axis=0)
            dist *= 2
        h = av * carry + bv
        carry = h[sub - 1:sub, :]
        out_ref[r0:r0 + sub, :] = x_in + _dot((yg * h).astype(BF), wout_ref[...])
    xprev[...] = prev
    hc[...] = jnp.broadcast_to(carry, (SUBLANES, d))
    hl_ref[...] = hc[...]
    xl_ref[...] = prev


def _rg_mixer(x, gm, win_bf, hist8, h08, cw8, cb, wg_bf, bg, c, wout_bf, b, t):
    n, dm = x.shape
    d = wout_bf.shape[0]
    tt = min(512, t)
    nt = t // tt
    row_spec = pl.BlockSpec((tt, dm), lambda bi, ti: (bi * nt + ti, 0))
    st_spec = pl.BlockSpec((None, SUBLANES, d), lambda bi, ti: (bi, 0, 0))
    consts = (cw8, cb, wg_bf, bg, c, wout_bf)
    return pl.pallas_call(
        functools.partial(_rg_mixer_kernel, sub=min(256, tt)),
        grid=(b, nt),
        in_specs=[row_spec, _const_spec(gm.shape), _const_spec(win_bf.shape), st_spec, st_spec]
        + [_const_spec(a.shape) for a in consts],
        out_specs=[row_spec, st_spec, st_spec],
        out_shape=[jax.ShapeDtypeStruct((n, dm), F32), jax.ShapeDtypeStruct((b, SUBLANES, d), F32),
                   jax.ShapeDtypeStruct((b, SUBLANES, d), F32)],
        scratch_shapes=[pltpu.VMEM((SUBLANES, d), F32), pltpu.VMEM((SUBLANES, d), F32)],
        compiler_params=_cparams(2), name="rg_mixer",
    )(x, gm, win_bf, hist8, h08, *consts)


def _rg_step_kernel(xr_ref, yg_ref, h0_ref, h1_ref, h2_ref, hs_ref, cw_ref, cb_ref, wg_ref, bg_ref, c_ref,
                    out_ref, hn_ref):
    x = xr_ref[...]
    xc = cb_ref[...] + h0_ref[...] * cw_ref[0:1, :]
    xc = xc + h1_ref[...] * cw_ref[1:2, :]
    xc = xc + h2_ref[...] * cw_ref[2:3, :]
    xc = xc + x * cw_ref[3:4, :]
    a, mult, ixc = _rg_gates(xc, wg_ref, bg_ref, c_ref)
    h = mult * ixc + a * hs_ref[...]
    hn_ref[...] = h
    out_ref[...] = (yg_ref[...] * h).astype(BF)


def _rg_step(xr, yg, h0, h1, h2, hs, cw8, cb, wg_bf, bg, c):
    n, d = xr.shape
    args = (xr, yg, h0, h1, h2, hs, cw8, cb, wg_bf, bg, c)
    return pl.pallas_call(
        _rg_step_kernel,
        grid=(1,),
        in_specs=[_const_spec(a.shape) for a in args],
        out_specs=[pl.BlockSpec((n, d), lambda i: (0, 0)), pl.BlockSpec((n, d), lambda i: (0, 0))],
        out_shape=[jax.ShapeDtypeStruct((n, d), BF), jax.ShapeDtypeStruct((n, d), F32)],
        compiler_params=_cparams(1), name="rg_step",
    )(*args)


def _matmul_res_kernel(a_ref, w_ref, x_ref, o_ref):
    o_ref[...] = x_ref[...] + _dot(a_ref[...], w_ref[...])


def _matmul_res(a_bf, w_bf, x):
    n, k = a_bf.shape
    d = w_bf.shape[1]
    tm = min(512, n)
    return pl.pallas_call(
        _matmul_res_kernel,
        grid=(n // tm,),
        in_specs=[pl.BlockSpec((tm, k), lambda i: (i, 0)), _const_spec(w_bf.shape),
                  pl.BlockSpec((tm, d), lambda i: (i, 0))],
        out_specs=pl.BlockSpec((tm, d), lambda i: (i, 0)),
        out_shape=jax.ShapeDtypeStruct((n, d), F32),
        compiler_params=_cparams(1), name="matmul_res",
    )(a_bf, w_bf, x)


def _ffn_tail(x, acc, p_ref, gp_ref, wgate_ref, wpin_ref, gfin_ref, final_norm):
    x1 = x + acc
    hp = _rms(x1, gp_ref[...]).astype(BF)
    sig = _sigmoid(_dot(hp, wgate_ref[...]))
    pe = _dot(p_ref[...].astype(BF), wpin_ref[...])
    x2 = x1 + pe * sig
    if final_norm:
        x2 = _rms(x2, gfin_ref[...])
    return x2


def _ffn_kernel(x_ref, p_ref, hist_ref, gf_ref, wup_ref, cw_ref, cb_ref, wdn_ref, gp_ref, wgate_ref, wpin_ref,
                gfin_ref, o_ref, ho_ref, uprev, *, chunk, final_norm):
    t = pl.program_id(1)

    @pl.when(t == 0)
    def _():
        uprev[...] = hist_ref[...]

    x = x_ref[...]
    tm = x.shape[0]
    dff = wdn_ref.shape[0]
    h = _rms(x, gf_ref[...]).astype(BF)
    acc = jnp.zeros(x.shape, F32)
    for c0 in range(0, dff, chunk):
        halves = []
        for off in (c0, dff + c0):
            up = _dot(h, wup_ref[:, off:off + chunk])
            prev = uprev[:, off:off + chunk]
            uc = cb_ref[:, off:off + chunk] + _shift_rows(up, prev, 2) * cw_ref[0:1, off:off + chunk]
            uc = uc + _shift_rows(up, prev, 1) * cw_ref[1:2, off:off + chunk]
            uc = uc + up * cw_ref[2:3, off:off + chunk]
            uprev[:, off:off + chunk] = up[tm - SUBLANES:, :]
            halves.append(uc)
        act = (_gelu(halves[0]) * halves[1]).astype(BF)
        acc = acc + _dot(act, wdn_ref[c0:c0 + chunk, :])
    ho_ref[...] = uprev[...]
    o_ref[...] = _ffn_tail(x, acc, p_ref, gp_ref, wgate_ref, wpin_ref, gfin_ref, final_norm)


def _ffn(x, p_all, layer, hist8, gf, wup_bf, cw8, cb, wdn_bf, gp, wgate_bf, wpin_bf, gfin, b, t, final_norm):
    n, d = x.shape
    dp = p_all.shape[2]
    d2 = wup_bf.shape[1]
    tm = min(1024, t)
    nt = t // tm
    row = lambda w: pl.BlockSpec((tm, w), lambda bi, ti: (bi * nt + ti, 0))
    p_spec = pl.BlockSpec((None, tm, dp), lambda bi, ti: (layer, bi * nt + ti, 0))
    st_spec = pl.BlockSpec((None, SUBLANES, d2), lambda bi, ti: (bi, 0, 0))
    consts = (gf, wup_bf, cw8, cb, wdn_bf, gp, wgate_bf, wpin_bf, gfin)
    return pl.pallas_call(
        functools.partial(_ffn_kernel, chunk=512, final_norm=final_norm),
        grid=(b, nt),
        in_specs=[row(d), p_spec, st_spec] + [_const_spec(a.shape) for a in consts],
        out_specs=[row(d), st_spec],
        out_shape=[jax.ShapeDtypeStruct((n, d), F32), jax.ShapeDtypeStruct((b, SUBLANES, d2), F32)],
        scratch_shapes=[pltpu.VMEM((SUBLANES, d2), F32)],
        compiler_params=_cparams(2), name="ffn",
    )(x, p_all, hist8, *consts)


def _ffn_step_kernel(x_ref, p_ref, h0_ref, h1_ref, gf_ref, wup_ref, cw_ref, cb_ref, wdn_ref, gp_ref, wgate_ref,
                     wpin_ref, gfin_ref, o_ref, up_ref, *, chunk, final_norm):
    x = x_ref[...]
    dff = wdn_ref.shape[0]
    h = _rms(x, gf_ref[...]).astype(BF)
    acc = jnp.zeros(x.shape, F32)
    for c0 in range(0, dff, chunk):
        halves = []
        for off in (c0, dff + c0):
            up = _dot(h, wup_ref[:, off:off + chunk])
            up_ref[:, off:off + chunk] = up
            uc = cb_ref[:, off:off + chunk] + h0_ref[:, off:off + chunk] * cw_ref[0:1, off:off + chunk]
            uc = uc + h1_ref[:, off:off + chunk] * cw_ref[1:2, off:off + chunk]
            uc = uc + up * cw_ref[2:3, off:off + chunk]
            halves.append(uc)
        act = (_gelu(halves[0]) * halves[1]).astype(BF)
        acc = acc + _dot(act, wdn_ref[c0:c0 + chunk, :])
    o_ref[...] = _ffn_tail(x, acc, p_ref, gp_ref, wgate_ref, wpin_ref, gfin_ref, final_norm)


def _ffn_step(x, p, h0, h1, gf, wup_bf, cw8, cb, wdn_bf, gp, wgate_bf, wpin_bf, gfin, final_norm):
    n, d = x.shape
    d2 = wup_bf.shape[1]
    args = (x, p, h0, h1, gf, wup_bf, cw8, cb, wdn_bf, gp, wgate_bf, wpin_bf, gfin)
    return pl.pallas_call(
        functools.partial(_ffn_step_kernel, chunk=512, final_norm=final_norm),
        grid=(1,),
        in_specs=[_const_spec(a.shape) for a in args],
        out_specs=[pl.BlockSpec((n, d), lambda i: (0, 0)), pl.BlockSpec((n, d2), lambda i: (0, 0))],
        out_shape=[jax.ShapeDtypeStruct((n, d), F32), jax.ShapeDtypeStruct((n, d2), F32)],
        compiler_params=_cparams(1), name="ffn_step",
    )(*args)


def _kv_kernel(x_ref, g_ref, w_ref, cos_ref, sin_ref, oh_ref, kv_ref, slc_ref, win_ref, ka_ref, sva_ref, wva_ref):
    h = _rms(x_ref[...], g_ref[...]).astype(BF)
    kv = _dot(h, w_ref[...])
    cos, sin = cos_ref[...], sin_ref[...]
    w2 = 2 * LANES
    k_slc = _rope(kv[:, w2:w2 + LANES], cos, sin)
    k_win = _rope(kv[:, 2 * w2:2 * w2 + LANES], cos, sin)
    kv_ref[:, 0:w2] = kv[:, 0:w2]
    kv_ref[:, w2:w2 + LANES] = k_slc
    kv_ref[:, w2 + LANES:2 * w2] = kv[:, w2 + LANES:2 * w2]
    kv_ref[:, 2 * w2:2 * w2 + LANES] = k_win
    kv_ref[:, 2 * w2 + LANES:3 * w2] = kv[:, 2 * w2 + LANES:3 * w2]
    slc_ref[:, 0:LANES] = k_slc.astype(BF)
    slc_ref[:, LANES:w2] = kv[:, w2 + LANES:2 * w2].astype(BF)
    win_ref[:, 0:LANES] = k_win.astype(BF)
    win_ref[:, LANES:w2] = kv[:, 2 * w2 + LANES:3 * w2].astype(BF)
    low = lax.broadcasted_iota(jnp.int32, k_slc.shape, 1) < HEAD_DIM
    oh = oh_ref[...]
    ka_ref[:, 0:LANES] = jnp.where(low, k_slc, oh).astype(BF)
    ka_ref[:, LANES:w2] = jnp.where(low, oh, k_slc).astype(BF)
    for va_ref, v in ((sva_ref, kv[:, w2 + LANES:2 * w2]), (wva_ref, kv[:, 2 * w2 + LANES:3 * w2])):
        va_ref[:, 0:LANES] = jnp.where(low, v, 1.0).astype(BF)
        va_ref[:, LANES:w2] = jnp.where(low, 1.0, v).astype(BF)


def _kv_proj(x, g, w_bf, cos, sin, onehot):
    n, d = x.shape
    nk = w_bf.shape[1]
    tm = min(512, n)
    row = lambda w: pl.BlockSpec((tm, w), lambda i: (i, 0))
    return pl.pallas_call(
        _kv_kernel,
        grid=(n // tm,),
        in_specs=[row(d), _const_spec((1, d)), _const_spec(w_bf.shape), row(LANES), row(LANES), row(LANES)],
        out_specs=[row(nk)] + [row(2 * LANES)] * 5,
        out_shape=[jax.ShapeDtypeStruct((n, nk), F32)] + [jax.ShapeDtypeStruct((n, 2 * LANES), BF)] * 5,
        compiler_params=_cparams(1), name="kv_proj",
    )(x, g, w_bf, cos, sin, onehot)


def _q_kernel(x_ref, g_ref, wq_ref, wg_ref, cos_ref, sin_ref, qs_ref, qrs_ref, gt_ref):
    h = _rms(x_ref[...], g_ref[...]).astype(BF)
    cos, sin = cos_ref[...], sin_ref[...]
    scale = HEAD_DIM ** -0.5
    mxu_cols = 2 * LANES
    for c0 in range(0, wq_ref.shape[1], mxu_cols):
        q2 = _dot(h, wq_ref[:, c0:c0 + mxu_cols])
        for c1 in range(0, mxu_cols, LANES):
            q = q2[:, c1:c1 + LANES]
            qs_ref[:, c0 + c1:c0 + c1 + LANES] = (q * scale).astype(BF)
            qrs_ref[:, c0 + c1:c0 + c1 + LANES] = (_rope(q, cos, sin) * scale).astype(BF)
    gt_ref[...] = _sigmoid(_dot(h, wg_ref[...]))


def _q_proj(x, g, wq_bf, wg_bf, cos, sin):
    n, d = x.shape
    dq = wq_bf.shape[1]
    tm = min(512, n)
    row = lambda w: pl.BlockSpec((tm, w), lambda i: (i, 0))
    return pl.pallas_call(
        _q_kernel,
        grid=(n // tm,),
        in_specs=[row(d), _const_spec((1, d)), _const_spec(wq_bf.shape), _const_spec(wg_bf.shape),
                  row(LANES), row(LANES)],
        out_specs=[row(dq), row(dq), row(LANES)],
        out_shape=[jax.ShapeDtypeStruct((n, dq), BF), jax.ShapeDtypeStruct((n, dq), BF),
                   jax.ShapeDtypeStruct((n, LANES), F32)],
        compiler_params=_cparams(1), name="q_proj",
    )(x, g, wq_bf, wg_bf, cos, sin)


def _compress_kernel(xk_ref, xv_ref, pa_ref, pb_ref, w1a_ref, w1b_ref, b1_ref, w2_ref, kc_ref, vc_ref):
    nhb = kc_ref.shape[0]
    for j, x_ref, o_ref in ((0, xk_ref, kc_ref), (1, xv_ref, vc_ref)):
        rows = [x_ref[pl.ds(l, nhb, stride=D_CMP), :] for l in range(D_CMP)]
        lhs_a = jnp.concatenate([(x + pa_ref[j, l:l + 1, :]).astype(BF) for l, x in enumerate(rows)], axis=1)
        lhs_b = jnp.concatenate([(x + pb_ref[j, l:l + 1, :]).astype(BF) for l, x in enumerate(rows)], axis=1)
        hid = _gelu(_dot(lhs_a, w1a_ref[j]) + pltpu.roll(_dot(lhs_b, w1b_ref[j]), nhb - 1, axis=0) + b1_ref[j])
        o_ref[...] = _dot(hid.astype(BF), w2_ref[j]).astype(BF)


def _compress(rows, params, b, t):
    nhb = t // D_CMP
    out = pl.BlockSpec((None, nhb, LANES), lambda bi: (bi, 0, 0))
    return pl.pallas_call(
        _compress_kernel,
        grid=(b,),
        in_specs=[pl.BlockSpec((None, t, LANES), lambda bi: (bi, 0, 0)),
                  pl.BlockSpec((None, t, LANES), lambda bi: (bi, 0, 1))] + [_const_spec(a.shape) for a in params],
        out_specs=[out, out],
        out_shape=[jax.ShapeDtypeStruct((b, nhb, LANES), BF)] * 2,
        compiler_params=_cparams(1), name="compress",
    )(rows, rows, *params)


def _compress_paged_kernel(pt_ref, *refs, per_step):
    del pt_ref
    page_refs = refs[:2 * per_step]
    params = refs[2 * per_step:-4]
    kc_ref, vc_ref, xk_scr, xv_scr = refs[-4:]
    step = pl.program_id(1)
    for r in range(per_step):
        rows = pl.ds(pl.multiple_of((step * per_step + r) * PAGE_SIZE, PAGE_SIZE), PAGE_SIZE)
        xk_scr[rows, :] = page_refs[2 * r][...]
        xv_scr[rows, :] = page_refs[2 * r + 1][...]

    @pl.when(step == pl.num_programs(1) - 1)
    def _():
        _compress_kernel(xk_scr, xv_scr, *params, kc_ref, vc_ref)


def _compress_paged(cache, page_table, params):
    db, n_pages = page_table.shape
    t = n_pages * PAGE_SIZE
    nhb = t // D_CMP
    per_step = min(16, n_pages)
    in_specs = []
    for r in range(per_step):
        for half in range(2):
            in_specs.append(pl.BlockSpec((None, PAGE_SIZE, LANES), functools.partial(
                lambda bi, i, pt, r, half: (pt[bi, i * per_step + r], 0, half), r=r, half=half)))
    in_specs += [_const_spec(a.shape) for a in params]
    out = pl.BlockSpec((None, nhb, LANES), lambda bi, i, pt: (bi, 0, 0))
    return pl.pallas_call(
        functools.partial(_compress_paged_kernel, per_step=per_step),
        grid_spec=pltpu.PrefetchScalarGridSpec(
            num_scalar_prefetch=1, grid=(db, n_pages // per_step), in_specs=in_specs, out_specs=[out, out],
            scratch_shapes=[pltpu.VMEM((t, LANES), F32), pltpu.VMEM((t, LANES), F32)]),
        out_shape=[jax.ShapeDtypeStruct((db, nhb, LANES), BF)] * 2,
        compiler_params=_cparams(2), name="compress_paged",
    )(page_table, *([cache] * (2 * per_step)), *params)


def _stack_heads(q_ref, g, tq, fill=0.0):
    lane = lax.broadcasted_iota(jnp.int32, (tq, LANES), 1)
    keep = (lane >= HEAD_DIM * g) & (lane < HEAD_DIM * (g + 1))
    parts = []
    for h in range(HPG):
        hh = g * HPG + h
        ch = hh // 2
        c = q_ref[:, ch * LANES:(ch + 1) * LANES].astype(F32)
        if hh % 2 != g:
            c = pltpu.roll(c, HEAD_DIM, axis=1)
        parts.append(jnp.where(keep, c, fill))
    return jnp.concatenate(parts, axis=0).astype(BF)


def _cmp_branch(q, kc, vc, tpos_col, n_cb, tq):
    ncp = kc.shape[0]
    cidx = lax.broadcasted_iota(jnp.int32, (tq, ncp), 1)
    valid = ((cidx * D_CMP + (L_CMP - 1)) <= tpos_col) & (cidx < n_cb)
    s = _dot_nt(q, kc).reshape(HPG, tq, ncp) + jnp.where(valid, 0.0, NEG)[None]
    e = jnp.exp(s - jnp.max(s, axis=-1, keepdims=True))
    any_valid = (tpos_col >= L_CMP - 1) & (n_cb > 0)
    inv = jnp.where(any_valid[None], 1.0 / jnp.sum(e, axis=-1, keepdims=True), 0.0)
    p = e * inv
    o = _dot(p.reshape(HPG * tq, ncp).astype(BF), vc)
    return o, jnp.sum(p, axis=0)


def _cmp_branch_staged(g, q, kc, vc, tpos_col, n_cb, s_ref, p_ref, oc_ref, tq):
    ncp = kc.shape[0]
    cidx = lax.broadcasted_iota(jnp.int32, (tq, ncp), 1)
    bias = _mask_bias(((cidx * D_CMP + (L_CMP - 1)) <= tpos_col) & (cidx < n_cb))
    any_valid = (tpos_col >= L_CMP - 1) & (n_cb > 0)
    s_ref[g, :, 0:ncp] = _dot_nt(q, kc)
    psum = jnp.zeros((tq, ncp), F32)
    for h in range(HPG):
        rows = slice(h * tq, (h + 1) * tq)
        m = jnp.max(s_ref[g, rows, 0:ncp] + bias, axis=-1, keepdims=True)
        e = jnp.exp(s_ref[g, rows, 0:ncp] + bias - m)
        p = e * jnp.where(any_valid, 1.0 / jnp.sum(e, axis=-1, keepdims=True), 0.0)
        psum = psum + p
        p_ref[g, rows, 0:ncp] = p.astype(BF)
    oc_ref[g] = _dot(p_ref[g, :, 0:ncp], vc)
    return psum


def _importance(psum, mt_ref, tpos_row):
    sp = mt_ref.shape[0]
    mt = mt_ref[...]
    p_hi = psum.astype(BF)
    r1 = psum - p_hi.astype(F32)
    p_mid = r1.astype(BF)
    p_lo = (r1 - p_mid.astype(F32)).astype(BF)
    imp = _dot_nt(mt, p_hi) + _dot_nt(mt, p_mid) + _dot_nt(mt, p_lo)
    s_idx = lax.broadcasted_iota(jnp.int32, (sp, LANES), 0)
    cur = jnp.right_shift(tpos_row, L_SLC.bit_length() - 1)
    forced = (s_idx == 0) | (s_idx == cur) | (s_idx == cur - 1)
    imp = jnp.where(forced, jnp.inf, imp)
    return jnp.where(s_idx * L_SLC <= tpos_row, imp, -jnp.inf)


def _select_blocks(psums, mt_ref, imp_ref, tpos_row, n_sb, n_live, tq):
    sp = mt_ref.shape[0]
    assert n_sb <= HEAD_DIM and sp == LANES, "one-hot block lanes hold at most 64 selection blocks"
    nr = _round_up(n_sb, SUBLANES)
    imps = []
    for g, psum in enumerate(psums):
        imp = _importance(psum, mt_ref, tpos_row)
        imp_ref[g] = imp
        imps.append(imp[:nr])
    s_idx = lax.broadcasted_iota(jnp.int32, (nr, LANES), 0)

    def body(i, cnts):
        tie = jnp.where(s_idx > i, 1.0, 0.0)
        out = []
        for g, imp in enumerate(imps):
            row = imp_ref[g, pl.ds(i, 1), :]
            out.append(cnts[g] + jnp.where(row > imp, 1.0, jnp.where(row == imp, tie, 0.0)))
        return tuple(out)

    cnts = lax.fori_loop(0, jnp.minimum(n_sb, n_live), body, tuple(jnp.zeros((nr, LANES), F32) for _ in psums))
    starts_before = s_idx * L_SLC <= tpos_row
    biases = []
    for g, cnt in enumerate(cnts):
        bias_t = jnp.where((cnt < float(N_SEL)) & starts_before, 0.0, NEG)
        if nr < sp:
            bias_t = jnp.concatenate([bias_t, jnp.full((sp - nr, LANES), NEG, F32)], axis=0)
        bias = bias_t.T[:tq]
        biases.append(pltpu.roll(bias, HEAD_DIM, axis=1) if g == 0 else bias)
    return biases


def _select_ids(psum, mt_ref, tpos):
    sp = mt_ref.shape[0]
    prow = jnp.broadcast_to(psum[0:1], (LANES, psum.shape[1]))
    imp = _importance(prow, mt_ref, jnp.full((1, LANES), tpos, jnp.int32))
    a = jnp.broadcast_to(imp[:, 0:1], (sp, sp))
    b = a.T
    r_i = lax.broadcasted_iota(jnp.int32, (sp, sp), 0)
    c_i = lax.broadcasted_iota(jnp.int32, (sp, sp), 1)
    beats = jnp.where(a > b, 1.0, jnp.where(a == b, jnp.where(r_i < c_i, 1.0, 0.0), 0.0))
    rank = jnp.sum(beats, axis=0, keepdims=True)
    slot = lax.broadcasted_iota(jnp.int32, (N_SEL, sp), 0).astype(F32)
    blk = lax.broadcasted_iota(jnp.int32, (N_SEL, sp), 1).astype(F32)
    ids = jnp.sum(jnp.where(rank == slot, blk, 0.0), axis=1, keepdims=True)
    return jnp.broadcast_to(ids, (N_SEL, LANES)).astype(jnp.int32)


def _flash_init(tq):
    return (jnp.full((HPG, tq, 1), NEG, F32), jnp.zeros((HPG, tq, 1), F32), jnp.zeros((HPG * tq, LANES), F32))


def _flash_tile(carry, q, k, v, bias, tq):
    m, l, acc = carry
    kt = k.shape[0]
    s = _dot_nt(q, k).reshape(HPG, tq, kt)
    if bias is not None:
        s = s + bias[None]
    m_new = jnp.maximum(m, jnp.max(s, axis=-1, keepdims=True))
    alpha = jnp.exp(m - m_new)
    p = jnp.exp(s - m_new)
    l = alpha * l + jnp.sum(p, axis=-1, keepdims=True)
    acc = alpha.reshape(HPG * tq, 1) * acc + _dot(p.reshape(HPG * tq, kt).astype(BF), v)
    return m_new, l, acc


def _flash_out(carry, tq):
    _, l, acc = carry
    return acc / l.reshape(HPG * tq, 1)


def _attend_once(q, k, v, bias, tq):
    kt = k.shape[0]
    s = _dot_nt(q, k).reshape(HPG, tq, kt) + bias[None]
    p = jnp.exp(s - jnp.max(s, axis=-1, keepdims=True))
    l = jnp.sum(p, axis=-1, keepdims=True)
    return _dot(p.reshape(HPG * tq, kt).astype(BF), v) / l.reshape(HPG * tq, 1)


def _mask_bias(valid):
    return jnp.where(valid, 0.0, NEG)


def _win_valid(kpos_row, tpos_col):
    dist = tpos_col - kpos_row
    return (dist >= 0) & (dist < WINDOW)


def _merge_heads(o_c, o_s, o_w, gt_ref, g, o_ref, tq):
    lane = lax.broadcasted_iota(jnp.int32, (tq, LANES), 1)
    for pair in range(HPG // 2):
        both = []
        for h in (2 * pair, 2 * pair + 1):
            col = (g * HPG + h) * 3
            rows = slice(h * tq, (h + 1) * tq)
            both.append(o_c(rows) * gt_ref[:, col:col + 1] + o_s(rows) * gt_ref[:, col + 1:col + 2]
                        + o_w(rows) * gt_ref[:, col + 2:col + 3])
        even, odd = both
        if g == 0:
            odd = pltpu.roll(odd, HEAD_DIM, axis=1)
        else:
            even = pltpu.roll(even, HEAD_DIM, axis=1)
        ch = g * (HPG // 2) + pair
        o_ref[:, ch * LANES:(ch + 1) * LANES] = jnp.where(lane < HEAD_DIM, even, odd).astype(o_ref.dtype)


def _staged_tile(state, stage, q, k, v, bias, refs, tq, first):
    s_ref, p_ref, m_ref, acc_ref = refs
    kt = k.shape[0]
    reps = kt // LANES
    s_ref[stage, :, 0:kt] = _dot_nt(q, k)
    for h in range(HPG):
        rows = slice(h * tq, (h + 1) * tq)

        def scores():
            s = s_ref[stage, rows, 0:kt]
            return s if bias is None else s + bias

        m_tile = jnp.max(scores(), axis=-1, keepdims=True)
        if first:
            m_new = jnp.broadcast_to(m_tile, (tq, LANES))
        else:
            m_old = m_ref[state, rows, :]
            m_new = jnp.maximum(m_old, m_tile)
            acc_ref[state, rows, :] = jnp.exp(m_old - m_new) * acc_ref[state, rows, :]
        m_ref[state, rows, :] = m_new
        p_ref[stage, rows, 0:kt] = jnp.exp(scores() - jnp.concatenate([m_new] * reps, axis=1)).astype(BF)
    pv = _dot(p_ref[stage, :, 0:kt], v)
    if first:
        acc_ref[state] = pv
    else:
        acc_ref[state] = acc_ref[state] + pv


def _attn_prompt_kernel(qs_ref, qrs_ref, gt_ref, kc_ref, vc_ref, ka_ref, sva_ref, win_ref, wva_ref, mt_ref, o_ref,
                        imp_ref, s_ref, p_ref, m_ref, acc_ref, oc_ref, *, n_cb, n_sb):
    tq = qs_ref.shape[0]
    t_all = ka_ref.shape[0]
    qb = pl.program_id(1)
    t0 = qb * tq
    tpos_col = t0 + lax.broadcasted_iota(jnp.int32, (tq, 1), 0)
    tpos_row = t0 + lax.broadcasted_iota(jnp.int32, (1, LANES), 1)
    groups = range(N_KV_HEADS)
    k_lanes, v_lanes = slice(0, LANES), slice(LANES, 2 * LANES)

    psums = [_cmp_branch_staged(g, _stack_heads(qs_ref, g, tq), kc_ref[...], vc_ref[...], tpos_col, n_cb,
                                s_ref, p_ref, oc_ref, tq) for g in groups]
    blk_bias = _select_blocks(psums, mt_ref, imp_ref, tpos_row, n_sb, (t0 + tq - 1) // L_SLC + 1, tq)
    qr_blk = [_stack_heads(qrs_ref, g, tq, fill=blk_bias[g]) for g in groups]

    refs = (s_ref, p_ref, m_ref, acc_ref)
    g_lanes = [slice(g * LANES, (g + 1) * LANES) for g in groups]

    def slc_tile(j, bias, first):
        rows = pl.ds(pl.multiple_of(j * SLC_TILE, SLC_TILE), SLC_TILE)
        for g in groups:
            _staged_tile(g, g, qr_blk[g], ka_ref[rows, g_lanes[g]], sva_ref[rows, g_lanes[g]], bias, refs, tq, first)

    last = (t0 + tq - 1) // SLC_TILE
    causal = (last * SLC_TILE + lax.broadcasted_iota(jnp.int32, (1, SLC_TILE), 1)) <= tpos_col
    slc_tile(last, _mask_bias(causal), True)

    def slc_body(j, c):
        slc_tile(j, None, False)
        return c

    lax.fori_loop(0, last, slc_body, 0)

    wk = min(WINDOW + tq, t_all)
    start = pl.multiple_of(jnp.clip(t0 - WINDOW, 0, t_all - wk), tq)
    wbias = _mask_bias(_win_valid(start + lax.broadcasted_iota(jnp.int32, (1, wk), 1), tpos_col))
    wrows = pl.ds(start, wk)
    kw = win_ref[wrows, k_lanes]
    for g in groups:
        _staged_tile(N_KV_HEADS + g, g, _stack_heads(qrs_ref, g, tq), kw, wva_ref[wrows, g_lanes[g]], wbias, refs,
                     tq, True)

    def normalised(slot, rows):
        acc = acc_ref[slot, rows, :]
        return acc / pltpu.roll(acc, HEAD_DIM, axis=1)

    for g in groups:
        _merge_heads(lambda rows: oc_ref[g, rows, :], functools.partial(normalised, g),
                     functools.partial(normalised, N_KV_HEADS + g), gt_ref, g, o_ref, tq)


def _attn_prompt(qs, qrs, gates, kc, vc, ka_bf, slc_va, win_bf, win_va, mt, b, t, n_cb, n_sb):
    tq = Q_BLOCK
    nqb = t // tq
    d = qs.shape[1]
    stage_w = max(SLC_TILE, min(WINDOW + tq, t))
    row = lambda w: pl.BlockSpec((tq, w), lambda bi, qi: (bi * nqb + qi, 0))
    per_b = lambda a: pl.BlockSpec((None,) + a.shape[1:], lambda bi, qi: (bi, 0, 0))
    return pl.pallas_call(
        functools.partial(_attn_prompt_kernel, n_cb=n_cb, n_sb=n_sb),
        grid=(b, nqb),
        in_specs=[row(d), row(d), row(LANES), per_b(kc), per_b(vc), per_b(ka_bf), per_b(slc_va), per_b(win_bf),
                  per_b(win_va), _const_spec(mt.shape)],
        out_specs=row(d),
        out_shape=jax.ShapeDtypeStruct((b * t, d), BF),
        scratch_shapes=[pltpu.VMEM((N_KV_HEADS, mt.shape[0], LANES), F32),
                        pltpu.VMEM((N_KV_HEADS, HPG * tq, stage_w), F32),
                        pltpu.VMEM((N_KV_HEADS, HPG * tq, stage_w), BF),
                        pltpu.VMEM((2 * N_KV_HEADS, HPG * tq, LANES), F32),
                        pltpu.VMEM((2 * N_KV_HEADS, HPG * tq, LANES), F32),
                        pltpu.VMEM((N_KV_HEADS, HPG * tq, LANES), F32)],
        compiler_params=_cparams(2), name="attn_prompt",
    )(qs, qrs, gates, kc, vc, ka_bf, slc_va, win_bf, win_va, mt)


def _sample_select_kernel(qs_ref, kc_ref, vc_ref, mt_ref, oc_ref, ids_ref, *, n_cb, past):
    tq = qs_ref.shape[0]
    tpos_col = jnp.full((tq, 1), past, jnp.int32)
    for g in range(N_KV_HEADS):
        o_c, psum = _cmp_branch(_stack_heads(qs_ref, g, tq), kc_ref[...], vc_ref[...], tpos_col, n_cb, tq)
        oc_ref[g] = o_c
        ids_ref[g] = _select_ids(psum, mt_ref, past)


def _sample_select(qs, kc, vc, mt, n_cb, past):
    db, tq, _ = qs.shape
    per_b = lambda a: pl.BlockSpec((None,) + a.shape[1:], lambda bi: (bi,) + (0,) * (a.ndim - 1))
    oc = jax.ShapeDtypeStruct((db, N_KV_HEADS, HPG * tq, LANES), F32)
    ids = jax.ShapeDtypeStruct((db, N_KV_HEADS, N_SEL, LANES), jnp.int32)
    return pl.pallas_call(
        functools.partial(_sample_select_kernel, n_cb=n_cb, past=past),
        grid=(db,),
        in_specs=[per_b(qs), per_b(kc), per_b(vc), _const_spec(mt.shape)],
        out_specs=[per_b(oc), per_b(ids)],
        out_shape=[oc, ids],
        compiler_params=_cparams(1), name="sample_select",
    )(qs, kc, vc, mt)


def _sample_attend_kernel(pt_ref, ids_ref, qrs_ref, gt_ref, oc_ref, slct_ref, win_ref, wint_ref, *rest, past, n_cached):
    del pt_ref
    blk_refs, o_ref, k_scr, v_scr = rest[:-3], rest[-3], rest[-2], rest[-1]
    b = pl.program_id(0)
    tq = qrs_ref.shape[0]
    tpos_col = jnp.full((tq, 1), past, jnp.int32)
    w_buf = win_ref.shape[0]
    tail = slct_ref.shape[0]
    k_lanes, v_lanes = slice(0, LANES), slice(LANES, 2 * LANES)
    tail_pos = past + lax.broadcasted_iota(jnp.int32, (1, tail), 1)
    win_pos = past - w_buf + lax.broadcasted_iota(jnp.int32, (1, w_buf), 1)
    wbias = jnp.broadcast_to(_mask_bias(_win_valid(win_pos, tpos_col)), (tq, w_buf))
    wtbias = jnp.broadcast_to(_mask_bias(_win_valid(tail_pos, tpos_col)), (tq, tail))
    tbias = jnp.broadcast_to(_mask_bias(tail_pos <= tpos_col), (tq, tail))
    kw, vw = win_ref[:, k_lanes].astype(BF), win_ref[:, v_lanes].astype(BF)
    lane = lax.broadcasted_iota(jnp.int32, (1, LANES), 1)
    for g in range(N_KV_HEADS):
        for r in range(N_SEL):
            blk = blk_refs[g * N_SEL + r]
            k_scr[g, r * L_SLC:(r + 1) * L_SLC, :] = blk[:, k_lanes].astype(BF)
            v_scr[g, r * L_SLC:(r + 1) * L_SLC, :] = blk[:, v_lanes].astype(BF)
        chunks = []
        for c in range(N_SEL // 2):
            lo = jnp.where(ids_ref[b, g * N_SEL + 2 * c] >= n_cached, NEG, 0.0)
            hi = jnp.where(ids_ref[b, g * N_SEL + 2 * c + 1] >= n_cached, NEG, 0.0)
            chunks.append(jnp.where(lane < L_SLC, lo, hi))
        sbias = jnp.broadcast_to(jnp.concatenate(chunks, axis=1), (tq, N_SEL * L_SLC))
        qr = _stack_heads(qrs_ref, g, tq)
        carry = _flash_tile(_flash_init(tq), qr, k_scr[g], v_scr[g], sbias, tq)
        carry = _flash_tile(carry, qr, slct_ref[:, k_lanes], slct_ref[:, v_lanes], tbias, tq)
        o_s = _flash_out(carry, tq)
        carry = _flash_tile(_flash_init(tq), qr, kw, vw, wbias, tq)
        carry = _flash_tile(carry, qr, wint_ref[:, k_lanes], wint_ref[:, v_lanes], wtbias, tq)
        o_w = _flash_out(carry, tq)
        _merge_heads(lambda rows: oc_ref[g, rows, :], lambda rows: o_s[rows], lambda rows: o_w[rows], gt_ref, g,
                     o_ref, tq)


def _sample_attend(qrs, gates, oc, slc_cache, page_table, ids, slc_tail, win_c, win_tail, past):
    db = qrs.shape[0]
    per_page = PAGE_SIZE // L_SLC
    n_cached = past // L_SLC
    blocks = slc_cache
    per_b = lambda a: pl.BlockSpec((None,) + a.shape[1:], lambda bi, pt, sel: (bi,) + (0,) * (a.ndim - 1))

    def blk_map(bi, pt, sel, g, r):
        s = jnp.clip(sel[bi, g * N_SEL + r], 0, n_cached - 1)
        return pt[bi, s // per_page], s % per_page, 0

    blk_specs = [pl.BlockSpec((None, L_SLC, blocks.shape[2]), functools.partial(blk_map, g=g, r=r))
                 for g in range(N_KV_HEADS) for r in range(N_SEL)]
    args = (qrs, gates, oc, slc_tail, win_c, win_tail)
    return pl.pallas_call(
        functools.partial(_sample_attend_kernel, past=past, n_cached=n_cached),
        grid_spec=pltpu.PrefetchScalarGridSpec(
            num_scalar_prefetch=2, grid=(db,),
            in_specs=[per_b(a) for a in args] + blk_specs,
            out_specs=per_b(qrs),
            scratch_shapes=[pltpu.VMEM((N_KV_HEADS, N_SEL * L_SLC, LANES), BF)] * 2),
        out_shape=jax.ShapeDtypeStruct(qrs.shape, BF),
        compiler_params=_cparams(1), name="sample_attend",
    )(page_table, ids, *args, *([blocks] * len(blk_specs)))


def _block_diag(blocks):
    n, k, j = blocks.shape
    eye = jnp.eye(n, dtype=blocks.dtype)
    return (eye[:, None, :, None] * blocks[:, :, None, :]).reshape(n * k, n * j)


def _pad_rows8(rows):
    return jnp.pad(rows, ((0, 0), (SUBLANES - rows.shape[1], 0), (0, 0)))


def _rope_tables(pos):
    half = HEAD_DIM // 2
    inv_freq = ROPE_THETA ** (-jnp.arange(half, dtype=F32) / half)
    ang = pos.astype(F32)[:, None] * inv_freq[None, :]
    cos, sin = jnp.cos(ang), jnp.sin(ang)
    reps = LANES // HEAD_DIM
    return (jnp.tile(jnp.concatenate([cos, cos], axis=1), (1, reps)),
            jnp.tile(jnp.concatenate([-sin, sin], axis=1), (1, reps)))


def _block_onehot(pos):
    lane = jnp.arange(LANES, dtype=jnp.int32)[None, :] % HEAD_DIM
    return (lane == (pos[:, None] // L_SLC)).astype(F32)


def _overlap_t(n_cb, n_sb, ncp, sp):
    c0 = np.arange(ncp)[None, :] * D_CMP
    s0 = np.arange(sp)[:, None] * L_SLC
    m = (c0 < s0 + L_SLC) & (c0 + L_CMP > s0) & (np.arange(ncp)[None, :] < n_cb) & (np.arange(sp)[:, None] < n_sb)
    return jnp.asarray(m, dtype=BF)


def _round_up(x, m):
    return (x + m - 1) // m * m


def kernel(x_prompt, x_sample, p_prompt, p_sample, cache_cmp_kv, cache_slc_kv, cache_win_kv, state_rg_conv,
           state_rg_h, state_ffn_conv, page_table, g_mix, g_ffn, g_ple, g_final, rg_w_in, rg_conv_w, rg_conv_b,
           rg_w_a, rg_b_a, rg_w_x, rg_b_x, rg_lambda, rg_w_out, g_kv, w_kv, cmp_pos, cmp_w1, cmp_b1, cmp_w2,
           attn_w_qg, attn_w_o, ffn_w_up, ffn_conv_w, ffn_conv_b, ffn_w_down, ple_w_in, ple_w_gate):
    b, t, d = x_prompt.shape
    db = x_sample.shape[0]
    assert x_sample.shape[1] == 1, "the sample path handles one new token per sequence"
    depth = g_mix.shape[0]
    n_a = rg_w_in.shape[0]
    d_rnn = rg_w_out.shape[1]
    n_pages = page_table.shape[1]
    past = n_pages * PAGE_SIZE
    kvc = 2 * N_KV_HEADS * HEAD_DIM
    n_q = N_KV_HEADS * HPG * HEAD_DIM

    row1 = lambda v: v.reshape(1, -1)
    pad8 = lambda w: jnp.pad(w, ((0, SUBLANES - w.shape[0]), (0, 0)))

    rg_in_bf = rg_w_in.astype(BF)
    rg_out_bf = rg_w_out.astype(BF)
    rg_gate_bf = [jnp.concatenate([_block_diag(rg_w_a[i]), _block_diag(rg_w_x[i])], axis=1).astype(BF)
                  for i in range(n_a)]
    rg_gate_b = [row1(jnp.concatenate([rg_b_a[i].reshape(-1), rg_b_x[i].reshape(-1)])) for i in range(n_a)]
    rg_c = [row1(-RG_C * jax.nn.softplus(-rg_lambda[i])) for i in range(n_a)]
    up_bf = ffn_w_up.astype(BF)
    down_bf = ffn_w_down.astype(BF)
    gate_bf = ple_w_gate.astype(BF)
    pin_bf = ple_w_in.astype(BF)
    wkv_bf = w_kv.astype(BF)
    wq_bf = attn_w_qg[:, :, :n_q].astype(BF)
    wgl_bf = jnp.pad(attn_w_qg[:, :, n_q:], ((0, 0), (0, 0), (0, LANES - 3 * N_KV_HEADS * HPG))).astype(BF)
    wo_bf = attn_w_o.astype(BF)
    gfin = row1(g_final)

    w1 = cmp_w1.reshape(2, 2, D_CMP, HEAD_DIM, -1)
    eye_g = jnp.eye(N_KV_HEADS, dtype=F32)
    w1_bd = (eye_g[None, None, None, :, None, :, None] * w1[:, :, :, None, :, None, :]).reshape(
        2, 2, D_CMP, N_KV_HEADS * HEAD_DIM, -1).astype(BF)
    pos_t = jnp.tile(jnp.transpose(cmp_pos, (1, 0, 2)), (1, 1, N_KV_HEADS)).reshape(2, 2, D_CMP, LANES)
    w1_rows = D_CMP * N_KV_HEADS * HEAD_DIM
    cmp_params = (pos_t[:, 0], pos_t[:, 1], w1_bd[:, 0].reshape(2, w1_rows, -1), w1_bd[:, 1].reshape(2, w1_rows, -1),
                  jnp.tile(cmp_b1, (1, N_KV_HEADS)).reshape(2, 1, -1),
                  jnp.stack([_block_diag(jnp.stack([cmp_w2[j]] * N_KV_HEADS)) for j in range(2)]).astype(BF))

    def layer_consts(i, final):
        return (row1(g_ffn[i]), up_bf[i], pad8(ffn_conv_w[i]), row1(ffn_conv_b[i]), down_bf[i], row1(g_ple[i]),
                gate_bf[i], pin_bf[i], gfin), final

    n = b * t
    x = x_prompt.reshape(n, d)
    p = p_prompt.reshape(depth, n, -1)
    pos_p = jnp.tile(jnp.arange(t, dtype=jnp.int32), b)
    cos_p, sin_p = _rope_tables(pos_p)
    rg_conv_p, rg_h_p, ffn_conv_p = [], [], []
    zeros_rg = jnp.zeros((b, SUBLANES, d_rnn), F32)
    zeros_ffn = jnp.zeros((b, SUBLANES, ffn_w_up.shape[2]), F32)
    for i in range(depth):
        if i == n_a:
            kv_p, _, win_bf, ka_bf, slc_va, win_va = _kv_proj(x, row1(g_kv), wkv_bf, cos_p, sin_p,
                                                              _block_onehot(pos_p))
            kc_p, vc_p = _compress(kv_p.reshape(b, t, -1), cmp_params, b, t)
            n_cb_p = (t - L_CMP) // D_CMP + 1
            n_sb_p = t // L_SLC
            mt_p = _overlap_t(n_cb_p, n_sb_p, t // D_CMP, _round_up(n_sb_p, LANES))
            ka_bf, slc_va, win_bf, win_va = (a.reshape(b, t, -1) for a in (ka_bf, slc_va, win_bf, win_va))
        if i < n_a:
            x, h_last, xr_last = _rg_mixer(x, row1(g_mix[i]), rg_in_bf[i], zeros_rg, zeros_rg, pad8(rg_conv_w[i]),
                                           row1(rg_conv_b[i]), rg_gate_bf[i], rg_gate_b[i], rg_c[i], rg_out_bf[i],
                                           b, t)
            rg_conv_p.append(xr_last[:, SUBLANES - 3:])
            rg_h_p.append(h_last[:, 0])
        else:
            j = i - n_a
            qs, qrs, gates = _q_proj(x, row1(g_mix[i]), wq_bf[j], wgl_bf[j], cos_p, sin_p)
            o = _attn_prompt(qs, qrs, gates, kc_p, vc_p, ka_bf, slc_va, win_bf, win_va, mt_p, b, t, n_cb_p, n_sb_p)
            x = _matmul_res(o, wo_bf[j], x)
        consts, final = layer_consts(i, i == depth - 1)
        x, f_hist = _ffn(x, p, i, zeros_ffn, *consts, b, t, final)
        ffn_conv_p.append(f_hist[:, SUBLANES - 2:])
    y_prompt = x.reshape(b, t, d)
    kv5 = kv_p.reshape(b, t, 6, N_KV_HEADS, HEAD_DIM)
    keep = min(WINDOW, t)
    cmp_p, slc_p, win_p = kv5[:, :, 0:2], kv5[:, :, 2:4], kv5[:, t - keep:, 4:6]

    x = x_sample.reshape(db, d)
    p = p_sample.reshape(depth, db, -1)
    cos_s, sin_s = _rope_tables(jnp.full((db,), past, jnp.int32))
    rg_conv_s, rg_h_s, ffn_conv_s = [], [], []
    for i in range(depth):
        if i == n_a:
            pos_s = jnp.full((db,), past, jnp.int32)
            kv_s, slc_new_bf, win_new_bf = _kv_proj(x, row1(g_kv), wkv_bf, cos_s, sin_s, _block_onehot(pos_s))[:3]
            kc_s, vc_s = _compress_paged(cache_cmp_kv.reshape(-1, PAGE_SIZE, kvc), page_table, cmp_params)
            slc_cache = cache_slc_kv.reshape(-1, PAGE_SIZE, kvc)
            n_cb_s = (past + 1 - L_CMP) // D_CMP + 1
            n_sb_s = -(-(past + 1) // L_SLC)
            sp_s = _round_up(n_sb_s, LANES)
            mt_s = _overlap_t(n_cb_s, n_sb_s, past // D_CMP, sp_s)
            tail = lambda a: jnp.pad(a.reshape(db, 1, -1), ((0, 0), (0, LANES - 1), (0, 0)))
            slc_tail, win_tail = tail(slc_new_bf), tail(win_new_bf)
            win_c = cache_win_kv.reshape(db, cache_win_kv.shape[1], kvc)
        if i < n_a:
            yg, xr = _rg_in(x, row1(g_mix[i]), rg_in_bf[i])
            hist = state_rg_conv[i]
            gated, h_new = _rg_step(xr, yg, hist[:, 0], hist[:, 1], hist[:, 2], state_rg_h[i], pad8(rg_conv_w[i]),
                                    row1(rg_conv_b[i]), rg_gate_bf[i], rg_gate_b[i], rg_c[i])
            x = _matmul_res(gated, rg_out_bf[i], x)
            rg_conv_s.append(jnp.concatenate([hist[:, 1:], xr[:, None, :]], axis=1))
            rg_h_s.append(h_new)
        else:
            j = i - n_a
            qs, qrs, gates = _q_proj(x, row1(g_mix[i]), wq_bf[j], wgl_bf[j], cos_s, sin_s)
            rep = lambda a: jnp.broadcast_to(a[:, None, :], (db, SUBLANES, a.shape[1]))
            oc, ids = _sample_select(rep(qs), kc_s, vc_s, mt_s, n_cb_s, past)
            o = _sample_attend(rep(qrs), rep(gates), oc, slc_cache, page_table, ids[:, :, :, 0].reshape(db, -1),
                               slc_tail, win_c,
                               win_tail, past)
            x = _matmul_res(o[:, 0], wo_bf[j], x)
        consts, final = layer_consts(i, i == depth - 1)
        fh = state_ffn_conv[i]
        x, up = _ffn_step(x, p[i], fh[:, 0], fh[:, 1], *consts, final)
        ffn_conv_s.append(jnp.stack([fh[:, 1], up], axis=1))
    y_sample = x.reshape(db, 1, d)
    kv5 = kv_s.reshape(db, 1, 6, N_KV_HEADS, HEAD_DIM)
    cmp_s, slc_s, win_new = kv5[:, :, 0:2], kv5[:, :, 2:4], kv5[:, :, 4:6]
    win_full = jnp.concatenate([cache_win_kv, win_new], axis=1)
    keep = min(WINDOW, past + 1)
    win_s = win_full[:, win_full.shape[1] - keep:]

    return (y_prompt, y_sample, cmp_p, cmp_s, slc_p, slc_s, win_p, win_s,
            jnp.stack(rg_conv_p), jnp.stack(rg_conv_s), jnp.stack(rg_h_p), jnp.stack(rg_h_s),
            jnp.stack(ffn_conv_p), jnp.stack(ffn_conv_s))
```

```python
import functools

import jax
import jax.numpy as jnp
import numpy as np
from jax import lax
from jax.experimental import pallas as pl
from jax.experimental.pallas import tpu as pltpu

BF = jnp.bfloat16
F32 = jnp.float32

EPS = 1e-6
RG_C = 8.0
N_RG_BLOCKS = 16
HEAD_DIM = 64
N_KV_HEADS = 2
HPG = 8
L_CMP = 32
D_CMP = 16
L_SLC = 64
N_SEL = 16
WINDOW = 512
Q_BLOCK = 128
ROPE_THETA = 10000.0
PAGE_SIZE = 128
NEG = -1e30
LANES = 128
SUBLANES = 8
SLC_TILE = 512
VMEM_LIMIT = 56 * 1024 * 1024


def _cparams(n_axes):
    return pltpu.CompilerParams(dimension_semantics=("arbitrary",) * n_axes,
                                vmem_limit_bytes=VMEM_LIMIT)


def _const_spec(shape):
    zeros = (0,) * len(shape)
    return pl.BlockSpec(shape, lambda *_: zeros, pipeline_mode=pl.Buffered(1))


def _dot(a, b):
    return jnp.dot(a, b, preferred_element_type=F32)


def _dot_nt(a, b):
    return lax.dot_general(a, b, (((1,), (1,)), ((), ())), preferred_element_type=F32)


def _rms(x, g):
    return x * lax.rsqrt(jnp.mean(x * x, axis=-1, keepdims=True) + EPS) * g


def _gelu(x):
    return jax.nn.gelu(x, approximate=True)


def _sigmoid(x):
    return jax.nn.sigmoid(x)


def _shift_rows(x, prev8, j):
    if j == 0:
        return x
    r = pltpu.roll(x, j, axis=0)
    p = pltpu.roll(prev8, j, axis=0)
    row = lax.broadcasted_iota(jnp.int32, (SUBLANES, x.shape[1]), 0)
    head = jnp.where(row < j, p, r[:SUBLANES])
    if x.shape[0] == SUBLANES:
        return head
    return jnp.concatenate([head, r[SUBLANES:]], axis=0)


def _rope(z, cos, sin_signed):
    lane = lax.broadcasted_iota(jnp.int32, z.shape, 1)
    first = (lane & (HEAD_DIM - 1)) < (HEAD_DIM // 2)
    rot = jnp.where(first, pltpu.roll(z, LANES - HEAD_DIM // 2, axis=1), pltpu.roll(z, HEAD_DIM // 2, axis=1))
    return z * cos + rot * sin_signed


def _rg_in_kernel(x_ref, g_ref, w_ref, yg_ref, xr_ref, *, chunk):
    h = _rms(x_ref[...], g_ref[...]).astype(BF)
    n = yg_ref.shape[1]
    for c0 in range(0, n, chunk):
        yg_ref[:, c0:c0 + chunk] = _gelu(_dot(h, w_ref[:, c0:c0 + chunk]))
        xr_ref[:, c0:c0 + chunk] = _dot(h, w_ref[:, n + c0:n + c0 + chunk])


def _rg_in(x, g, w_bf):
    n, d = x.shape
    dr = w_bf.shape[1] // 2
    tm = min(512, n)
    return pl.pallas_call(
        functools.partial(_rg_in_kernel, chunk=256),
        grid=(n // tm,),
        in_specs=[pl.BlockSpec((tm, d), lambda i: (i, 0)), _const_spec((1, d)), _const_spec(w_bf.shape)],
        out_specs=[pl.BlockSpec((tm, dr), lambda i: (i, 0)), pl.BlockSpec((tm, dr), lambda i: (i, 0))],
        out_shape=[jax.ShapeDtypeStruct((n, dr), F32), jax.ShapeDtypeStruct((n, dr), F32)],
        compiler_params=_cparams(1), name="rg_in",
    )(x, g, w_bf)


def _rg_gates(xc, wg_ref, bg_ref, c_ref):
    d = xc.shape[1]
    gates = _dot(xc.astype(BF), wg_ref[...]) + bg_ref[...]
    r = _sigmoid(gates[:, :d])
    i = _sigmoid(gates[:, d:])
    log_a = c_ref[...] * r
    a = jnp.exp(log_a)
    z = -jnp.tanh(log_a) * (a * a + 1.0)
    mult = jnp.where(z > 0.0, z * lax.rsqrt(z), 0.0)
    return a, mult, i * xc


def _rg_mixer_kernel(xin_ref, gm_ref, win_ref, hist_ref, h0_ref, cw_ref, cb_ref, wg_ref, bg_ref, c_ref, wout_ref,
                     out_ref, hl_ref, xl_ref, xprev, hc, *, sub):
    t = pl.program_id(1)

    @pl.when(t == 0)
    def _():
        xprev[...] = hist_ref[...]
        hc[...] = h0_ref[...]

    d = wout_ref.shape[0]
    prev = xprev[...]
    carry = hc[0:1, :]
    row = lax.broadcasted_iota(jnp.int32, (sub, d), 0)
    pieces = range(0, xin_ref.shape[0], sub)
    projs = []
    for r0 in pieces:
        hn = _rms(xin_ref[r0:r0 + sub, :], gm_ref[...]).astype(BF)
        projs.append((_dot(hn, win_ref[:, 0:d]), _dot(hn, win_ref[:, d:2 * d])))
    for (yg_pre, x), r0 in zip(projs, pieces):
        x_in = xin_ref[r0:r0 + sub, :]
        yg = _gelu(yg_pre)
        xc = cb_ref[...] + _shift_rows(x, prev, 3) * cw_ref[0:1, :]
        xc = xc + _shift_rows(x, prev, 2) * cw_ref[1:2, :]
        xc = xc + _shift_rows(x, prev, 1) * cw_ref[2:3, :]
        xc = xc + x * cw_ref[3:4, :]
        prev = x[sub - SUBLANES:, :]

        a, mult, ixc = _rg_gates(xc, wg_ref, bg_ref, c_ref)
        if r0 == 0:
            mult = jnp.where((row == 0) & (t == 0), 1.0, mult)
        u = mult * ixc

        av, bv = a, u
        dist = 1
        while dist < sub:
            if dist < SUBLANES:
                ok = row >= dist
                bv = jnp.where(ok, av * pltpu.roll(bv, dist, axis=0) + bv, bv)
                av = jnp.where(ok, av * pltpu.roll(av, dist, axis=0), av)
            else:
                bv = jnp.concatenate([bv[:dist], av[dist:] * bv[:sub - dist] + bv[dist:]], axis=0)
                av = jnp.concatenate([av[:dist], av[dist:] * av[:sub - dist]], axis=0)
            dist *= 2
        h = av * carry + bv
        carry = h[sub - 1:sub, :]
        out_ref[r0:r0 + sub, :] = x_in + _dot((yg * h).astype(BF), wout_ref[...])
    xprev[...] = prev
    hc[...] = jnp.broadcast_to(carry, (SUBLANES, d))
    hl_ref[...] = hc[...]
    xl_ref[...] = prev


def _rg_mixer(x, gm, win_bf, hist8, h08, cw8, cb, wg_bf, bg, c, wout_bf, b, t):
    n, dm = x.shape
    d = wout_bf.shape[0]
    tt = min(512, t)
    nt = t // tt
    row_spec = pl.BlockSpec((tt, dm), lambda bi, ti: (bi * nt + ti, 0))
    st_spec = pl.BlockSpec((None, SUBLANES, d), lambda bi, ti: (bi, 0, 0))
    consts = (cw8, cb, wg_bf, bg, c, wout_bf)
    return pl.pallas_call(
        functools.partial(_rg_mixer_kernel, sub=min(256, tt)),
        grid=(b, nt),
        in_specs=[row_spec, _const_spec(gm.shape), _const_spec(win_bf.shape), st_spec, st_spec]
        + [_const_spec(a.shape) for a in consts],
        out_specs=[row_spec, st_spec, st_spec],
        out_shape=[jax.ShapeDtypeStruct((n, dm), F32), jax.ShapeDtypeStruct((b, SUBLANES, d), F32),
                   jax.ShapeDtypeStruct((b, SUBLANES, d), F32)],
        scratch_shapes=[pltpu.VMEM((SUBLANES, d), F32), pltpu.VMEM((SUBLANES, d), F32)],
        compiler_params=_cparams(2), name="rg_mixer",
    )(x, gm, win_bf, hist8, h08, *consts)


def _rg_step_kernel(xr_ref, yg_ref, h0_ref, h1_ref, h2_ref, hs_ref, cw_ref, cb_ref, wg_ref, bg_ref, c_ref,
                    out_ref, hn_ref):
    x = xr_ref[...]
    xc = cb_ref[...] + h0_ref[...] * cw_ref[0:1, :]
    xc = xc + h1_ref[...] * cw_ref[1:2, :]
    xc = xc + h2_ref[...] * cw_ref[2:3, :]
    xc = xc + x * cw_ref[3:4, :]
    a, mult, ixc = _rg_gates(xc, wg_ref, bg_ref, c_ref)
    h = mult * ixc + a * hs_ref[...]
    hn_ref[...] = h
    out_ref[...] = (yg_ref[...] * h).astype(BF)


def _rg_step(xr, yg, h0, h1, h2, hs, cw8, cb, wg_bf, bg, c):
    n, d = xr.shape
    args = (xr, yg, h0, h1, h2, hs, cw8, cb, wg_bf, bg, c)
    return pl.pallas_call(
        _rg_step_kernel,
        grid=(1,),
        in_specs=[_const_spec(a.shape) for a in args],
        out_specs=[pl.BlockSpec((n, d), lambda i: (0, 0)), pl.BlockSpec((n, d), lambda i: (0, 0))],
        out_shape=[jax.ShapeDtypeStruct((n, d), BF), jax.ShapeDtypeStruct((n, d), F32)],
        compiler_params=_cparams(1), name="rg_step",
    )(*args)


def _matmul_res_kernel(a_ref, w_ref, x_ref, o_ref):
    o_ref[...] = x_ref[...] + _dot(a_ref[...], w_ref[...])


def _matmul_res(a_bf, w_bf, x):
    n, k = a_bf.shape
    d = w_bf.shape[1]
    tm = min(512, n)
    return pl.pallas_call(
        _matmul_res_kernel,
        grid=(n // tm,),
        in_specs=[pl.BlockSpec((tm, k), lambda i: (i, 0)), _const_spec(w_bf.shape),
                  pl.BlockSpec((tm, d), lambda i: (i, 0))],
        out_specs=pl.BlockSpec((tm, d), lambda i: (i, 0)),
        out_shape=jax.ShapeDtypeStruct((n, d), F32),
        compiler_params=_cparams(1), name="matmul_res",
    )(a_bf, w_bf, x)


def _ffn_tail(x, acc, p_ref, gp_ref, wgate_ref, wpin_ref, gfin_ref, final_norm):
    x1 = x + acc
    hp = _rms(x1, gp_ref[...]).astype(BF)
    sig = _sigmoid(_dot(hp, wgate_ref[...]))
    pe = _dot(p_ref[...].astype(BF), wpin_ref[...])
    x2 = x1 + pe * sig
    if final_norm:
        x2 = _rms(x2, gfin_ref[...])
    return x2


def _ffn_kernel(x_ref, p_ref, hist_ref, gf_ref, wup_ref, cw_ref, cb_ref, wdn_ref, gp_ref, wgate_ref, wpin_ref,
                gfin_ref, o_ref, ho_ref, uprev, *, chunk, final_norm):
    t = pl.program_id(1)

    @pl.when(t == 0)
    def _():
        uprev[...] = hist_ref[...]

    x = x_ref[...]
    tm = x.shape[0]
    dff = wdn_ref.shape[0]
    h = _rms(x, gf_ref[...]).astype(BF)
    acc = jnp.zeros(x.shape, F32)
    for c0 in range(0, dff, chunk):
        halves = []
        for off in (c0, dff + c0):
            up = _dot(h, wup_ref[:, off:off + chunk])
            prev = uprev[:, off:off + chunk]
            uc = cb_ref[:, off:off + chunk] + _shift_rows(up, prev, 2) * cw_ref[0:1, off:off + chunk]
            uc = uc + _shift_rows(up, prev, 1) * cw_ref[1:2, off:off + chunk]
            uc = uc + up * cw_ref[2:3, off:off + chunk]
            uprev[:, off:off + chunk] = up[tm - SUBLANES:, :]
            halves.append(uc)
        act = (_gelu(halves[0]) * halves[1]).astype(BF)
        acc = acc + _dot(act, wdn_ref[c0:c0 + chunk, :])
    ho_ref[...] = uprev[...]
    o_ref[...] = _ffn_tail(x, acc, p_ref, gp_ref, wgate_ref, wpin_ref, gfin_ref, final_norm)


def _ffn(x, p_all, layer, hist8, gf, wup_bf, cw8, cb, wdn_bf, gp, wgate_bf, wpin_bf, gfin, b, t, final_norm):
    n, d = x.shape
    dp = p_all.shape[2]
    d2 = wup_bf.shape[1]
    tm = min(1024, t)
    nt = t // tm
    row = lambda w: pl.BlockSpec((tm, w), lambda bi, ti: (bi * nt + ti, 0))
    p_spec = pl.BlockSpec((None, tm, dp), lambda bi, ti: (layer, bi * nt + ti, 0))
    st_spec = pl.BlockSpec((None, SUBLANES, d2), lambda bi, ti: (bi, 0, 0))
    consts = (gf, wup_bf, cw8, cb, wdn_bf, gp, wgate_bf, wpin_bf, gfin)
    return pl.pallas_call(
        functools.partial(_ffn_kernel, chunk=512, final_norm=final_norm),
        grid=(b, nt),
        in_specs=[row(d), p_spec, st_spec] + [_const_spec(a.shape) for a in consts],
        out_specs=[row(d), st_spec],
        out_shape=[jax.ShapeDtypeStruct((n, d), F32), jax.ShapeDtypeStruct((b, SUBLANES, d2), F32)],
        scratch_shapes=[pltpu.VMEM((SUBLANES, d2), F32)],
        compiler_params=_cparams(2), name="ffn",
    )(x, p_all, hist8, *consts)


def _ffn_step_kernel(x_ref, p_ref, h0_ref, h1_ref, gf_ref, wup_ref, cw_ref, cb_ref, wdn_ref, gp_ref, wgate_ref,
                     wpin_ref, gfin_ref, o_ref, up_ref, *, chunk, final_norm):
    x = x_ref[...]
    dff = wdn_ref.shape[0]
    h = _rms(x, gf_ref[...]).astype(BF)
    acc = jnp.zeros(x.shape, F32)
    for c0 in range(0, dff, chunk):
        halves = []
        for off in (c0, dff + c0):
            up = _dot(h, wup_ref[:, off:off + chunk])
            up_ref[:, off:off + chunk] = up
            uc = cb_ref[:, off:off + chunk] + h0_ref[:, off:off + chunk] * cw_ref[0:1, off:off + chunk]
            uc = uc + h1_ref[:, off:off + chunk] * cw_ref[1:2, off:off + chunk]
            uc = uc + up * cw_ref[2:3, off:off + chunk]
            halves.append(uc)
        act = (_gelu(halves[0]) * halves[1]).astype(BF)
        acc = acc + _dot(act, wdn_ref[c0:c0 + chunk, :])
    o_ref[...] = _ffn_tail(x, acc, p_ref, gp_ref, wgate_ref, wpin_ref, gfin_ref, final_norm)


def _ffn_step(x, p, h0, h1, gf, wup_bf, cw8, cb, wdn_bf, gp, wgate_bf, wpin_bf, gfin, final_norm):
    n, d = x.shape
    d2 = wup_bf.shape[1]
    args = (x, p, h0, h1, gf, wup_bf, cw8, cb, wdn_bf, gp, wgate_bf, wpin_bf, gfin)
    return pl.pallas_call(
        functools.partial(_ffn_step_kernel, chunk=512, final_norm=final_norm),
        grid=(1,),
        in_specs=[_const_spec(a.shape) for a in args],
        out_specs=[pl.BlockSpec((n, d), lambda i: (0, 0)), pl.BlockSpec((n, d2), lambda i: (0, 0))],
        out_shape=[jax.ShapeDtypeStruct((n, d), F32), jax.ShapeDtypeStruct((n, d2), F32)],
        compiler_params=_cparams(1), name="ffn_step",
    )(*args)


def _kv_kernel(x_ref, g_ref, w_ref, cos_ref, sin_ref, oh_ref, kv_ref, slc_ref, win_ref, ka_ref, sva_ref, wva_ref):
    h = _rms(x_ref[...], g_ref[...]).astype(BF)
    kv = _dot(h, w_ref[...])
    cos, sin = cos_ref[...], sin_ref[...]
    w2 = 2 * LANES
    k_slc = _rope(kv[:, w2:w2 + LANES], cos, sin)
    k_win = _rope(kv[:, 2 * w2:2 * w2 + LANES], cos, sin)
    kv_ref[:, 0:w2] = kv[:, 0:w2]
    kv_ref[:, w2:w2 + LANES] = k_slc
    kv_ref[:, w2 + LANES:2 * w2] = kv[:, w2 + LANES:2 * w2]
    kv_ref[:, 2 * w2:2 * w2 + LANES] = k_win
    kv_ref[:, 2 * w2 + LANES:3 * w2] = kv[:, 2 * w2 + LANES:3 * w2]
    slc_ref[:, 0:LANES] = k_slc.astype(BF)
    slc_ref[:, LANES:w2] = kv[:, w2 + LANES:2 * w2].astype(BF)
    win_ref[:, 0:LANES] = k_win.astype(BF)
    win_ref[:, LANES:w2] = kv[:, 2 * w2 + LANES:3 * w2].astype(BF)
    low = lax.broadcasted_iota(jnp.int32, k_slc.shape, 1) < HEAD_DIM
    oh = oh_ref[...]
    ka_ref[:, 0:LANES] = jnp.where(low, k_slc, oh).astype(BF)
    ka_ref[:, LANES:w2] = jnp.where(low, oh, k_slc).astype(BF)
    for va_ref, v in ((sva_ref, kv[:, w2 + LANES:2 * w2]), (wva_ref, kv[:, 2 * w2 + LANES:3 * w2])):
        va_ref[:, 0:LANES] = jnp.where(low, v, 1.0).astype(BF)
        va_ref[:, LANES:w2] = jnp.where(low, 1.0, v).astype(BF)


def _kv_proj(x, g, w_bf, cos, sin, onehot):
    n, d = x.shape
    nk = w_bf.shape[1]
    tm = min(512, n)
    row = lambda w: pl.BlockSpec((tm, w), lambda i: (i, 0))
    return pl.pallas_call(
        _kv_kernel,
        grid=(n // tm,),
        in_specs=[row(d), _const_spec((1, d)), _const_spec(w_bf.shape), row(LANES), row(LANES), row(LANES)],
        out_specs=[row(nk)] + [row(2 * LANES)] * 5,
        out_shape=[jax.ShapeDtypeStruct((n, nk), F32)] + [jax.ShapeDtypeStruct((n, 2 * LANES), BF)] * 5,
        compiler_params=_cparams(1), name="kv_proj",
    )(x, g, w_bf, cos, sin, onehot)


def _q_kernel(x_ref, g_ref, wq_ref, wg_ref, cos_ref, sin_ref, qs_ref, qrs_ref, gt_ref):
    h = _rms(x_ref[...], g_ref[...]).astype(BF)
    cos, sin = cos_ref[...], sin_ref[...]
    scale = HEAD_DIM ** -0.5
    mxu_cols = 2 * LANES
    for c0 in range(0, wq_ref.shape[1], mxu_cols):
        q2 = _dot(h, wq_ref[:, c0:c0 + mxu_cols])
        for c1 in range(0, mxu_cols, LANES):
            q = q2[:, c1:c1 + LANES]
            qs_ref[:, c0 + c1:c0 + c1 + LANES] = (q * scale).astype(BF)
            qrs_ref[:, c0 + c1:c0 + c1 + LANES] = (_rope(q, cos, sin) * scale).astype(BF)
    gt_ref[...] = _sigmoid(_dot(h, wg_ref[...]))


def _q_proj(x, g, wq_bf, wg_bf, cos, sin):
    n, d = x.shape
    dq = wq_bf.shape[1]
    tm = min(512, n)
    row = lambda w: pl.BlockSpec((tm, w), lambda i: (i, 0))
    return pl.pallas_call(
        _q_kernel,
        grid=(n // tm,),
        in_specs=[row(d), _const_spec((1, d)), _const_spec(wq_bf.shape), _const_spec(wg_bf.shape),
                  row(LANES), row(LANES)],
        out_specs=[row(dq), row(dq), row(LANES)],
        out_shape=[jax.ShapeDtypeStruct((n, dq), BF), jax.ShapeDtypeStruct((n, dq), BF),
                   jax.ShapeDtypeStruct((n, LANES), F32)],
        compiler_params=_cparams(1), name="q_proj",
    )(x, g, wq_bf, wg_bf, cos, sin)


def _compress_kernel(xk_ref, xv_ref, pa_ref, pb_ref, w1a_ref, w1b_ref, b1_ref, w2_ref, kc_ref, vc_ref):
    nhb = kc_ref.shape[0]
    for j, x_ref, o_ref in ((0, xk_ref, kc_ref), (1, xv_ref, vc_ref)):
        rows = [x_ref[pl.ds(l, nhb, stride=D_CMP), :] for l in range(D_CMP)]
        lhs_a = jnp.concatenate([(x + pa_ref[j, l:l + 1, :]).astype(BF) for l, x in enumerate(rows)], axis=1)
        lhs_b = jnp.concatenate([(x + pb_ref[j, l:l + 1, :]).astype(BF) for l, x in enumerate(rows)], axis=1)
        hid = _gelu(_dot(lhs_a, w1a_ref[j]) + pltpu.roll(_dot(lhs_b, w1b_ref[j]), nhb - 1, axis=0) + b1_ref[j])
        o_ref[...] = _dot(hid.astype(BF), w2_ref[j]).astype(BF)


def _compress(rows, params, b, t):
    nhb = t // D_CMP
    out = pl.BlockSpec((None, nhb, LANES), lambda bi: (bi, 0, 0))
    return pl.pallas_call(
        _compress_kernel,
        grid=(b,),
        in_specs=[pl.BlockSpec((None, t, LANES), lambda bi: (bi, 0, 0)),
                  pl.BlockSpec((None, t, LANES), lambda bi: (bi, 0, 1))] + [_const_spec(a.shape) for a in params],
        out_specs=[out, out],
        out_shape=[jax.ShapeDtypeStruct((b, nhb, LANES), BF)] * 2,
        compiler_params=_cparams(1), name="compress",
    )(rows, rows, *params)


def _compress_paged_kernel(pt_ref, *refs, per_step):
    del pt_ref
    page_refs = refs[:2 * per_step]
    params = refs[2 * per_step:-4]
    kc_ref, vc_ref, xk_scr, xv_scr = refs[-4:]
    step = pl.program_id(1)
    for r in range(per_step):
        rows = pl.ds(pl.multiple_of((step * per_step + r) * PAGE_SIZE, PAGE_SIZE), PAGE_SIZE)
        xk_scr[rows, :] = page_refs[2 * r][...]
        xv_scr[rows, :] = page_refs[2 * r + 1][...]

    @pl.when(step == pl.num_programs(1) - 1)
    def _():
        _compress_kernel(xk_scr, xv_scr, *params, kc_ref, vc_ref)


def _compress_paged(cache, page_table, params):
    db, n_pages = page_table.shape
    t = n_pages * PAGE_SIZE
    nhb = t // D_CMP
    per_step = min(16, n_pages)
    in_specs = []
    for r in range(per_step):
        for half in range(2):
            in_specs.append(pl.BlockSpec((None, PAGE_SIZE, LANES), functools.partial(
                lambda bi, i, pt, r, half: (pt[bi, i * per_step + r], 0, half), r=r, half=half)))
    in_specs += [_const_spec(a.shape) for a in params]
    out = pl.BlockSpec((None, nhb, LANES), lambda bi, i, pt: (bi, 0, 0))
    return pl.pallas_call(
        functools.partial(_compress_paged_kernel, per_step=per_step),
        grid_spec=pltpu.PrefetchScalarGridSpec(
            num_scalar_prefetch=1, grid=(db, n_pages // per_step), in_specs=in_specs, out_specs=[out, out],
            scratch_shapes=[pltpu.VMEM((t, LANES), F32), pltpu.VMEM((t, LANES), F32)]),
        out_shape=[jax.ShapeDtypeStruct((db, nhb, LANES), BF)] * 2,
        compiler_params=_cparams(2), name="compress_paged",
    )(page_table, *([cache] * (2 * per_step)), *params)


def _stack_heads(q_ref, g, tq, fill=0.0):
    lane = lax.broadcasted_iota(jnp.int32, (tq, LANES), 1)
    keep = (lane >= HEAD_DIM * g) & (lane < HEAD_DIM * (g + 1))
    parts = []
    for h in range(HPG):
        hh = g * HPG + h
        ch = hh // 2
        c = q_ref[:, ch * LANES:(ch + 1) * LANES].astype(F32)
        if hh % 2 != g:
            c = pltpu.roll(c, HEAD_DIM, axis=1)
        parts.append(jnp.where(keep, c, fill))
    return jnp.concatenate(parts, axis=0).astype(BF)


def _cmp_branch(q, kc, vc, tpos_col, n_cb, tq):
    ncp = kc.shape[0]
    cidx = lax.broadcasted_iota(jnp.int32, (tq, ncp), 1)
    valid = ((cidx * D_CMP + (L_CMP - 1)) <= tpos_col) & (cidx < n_cb)
    s = _dot_nt(q, kc).reshape(HPG, tq, ncp) + jnp.where(valid, 0.0, NEG)[None]
    e = jnp.exp(s - jnp.max(s, axis=-1, keepdims=True))
    any_valid = (tpos_col >= L_CMP - 1) & (n_cb > 0)
    inv = jnp.where(any_valid[None], 1.0 / jnp.sum(e, axis=-1, keepdims=True), 0.0)
    p = e * inv
    o = _dot(p.reshape(HPG * tq, ncp).astype(BF), vc)
    return o, jnp.sum(p, axis=0)


def _cmp_branch_staged(g, q, kc, vc, tpos_col, n_cb, s_ref, p_ref, oc_ref, tq):
    ncp = kc.shape[0]
    cidx = lax.broadcasted_iota(jnp.int32, (tq, ncp), 1)
    bias = _mask_bias(((cidx * D_CMP + (L_CMP - 1)) <= tpos_col) & (cidx < n_cb))
    any_valid = (tpos_col >= L_CMP - 1) & (n_cb > 0)
    s_ref[g, :, 0:ncp] = _dot_nt(q, kc)
    psum = jnp.zeros((tq, ncp), F32)
    for h in range(HPG):
        rows = slice(h * tq, (h + 1) * tq)
        m = jnp.max(s_ref[g, rows, 0:ncp] + bias, axis=-1, keepdims=True)
        e = jnp.exp(s_ref[g, rows, 0:ncp] + bias - m)
        p = e * jnp.where(any_valid, 1.0 / jnp.sum(e, axis=-1, keepdims=True), 0.0)
        psum = psum + p
        p_ref[g, rows, 0:ncp] = p.astype(BF)
    oc_ref[g] = _dot(p_ref[g, :, 0:ncp], vc)
    return psum


def _importance(psum, mt_ref, tpos_row):
    sp = mt_ref.shape[0]
    mt = mt_ref[...]
    p_hi = psum.astype(BF)
    r1 = psum - p_hi.astype(F32)
    p_mid = r1.astype(BF)
    p_lo = (r1 - p_mid.astype(F32)).astype(BF)
    imp = _dot_nt(mt, p_hi) + _dot_nt(mt, p_mid) + _dot_nt(mt, p_lo)
    s_idx = lax.broadcasted_iota(jnp.int32, (sp, LANES), 0)
    cur = jnp.right_shift(tpos_row, L_SLC.bit_length() - 1)
    forced = (s_idx == 0) | (s_idx == cur) | (s_idx == cur - 1)
    imp = jnp.where(forced, jnp.inf, imp)
    return jnp.where(s_idx * L_SLC <= tpos_row, imp, -jnp.inf)


def _select_blocks(psums, mt_ref, imp_ref, tpos_row, n_sb, n_live, tq):
    sp = mt_ref.shape[0]
    assert n_sb <= HEAD_DIM and sp == LANES, "one-hot block lanes hold at most 64 selection blocks"
    nr = _round_up(n_sb, SUBLANES)
    imps = []
    for g, psum in enumerate(psums):
        imp = _importance(psum, mt_ref, tpos_row)
        imp_ref[g] = imp
        imps.append(imp[:nr])
    s_idx = lax.broadcasted_iota(jnp.int32, (nr, LANES), 0)

    def body(i, cnts):
        tie = jnp.where(s_idx > i, 1.0, 0.0)
        out = []
        for g, imp in enumerate(imps):
            row = imp_ref[g, pl.ds(i, 1), :]
            out.append(cnts[g] + jnp.where(row > imp, 1.0, jnp.where(row == imp, tie, 0.0)))
        return tuple(out)

    cnts = lax.fori_loop(0, jnp.minimum(n_sb, n_live), body, tuple(jnp.zeros((nr, LANES), F32) for _ in psums))
    starts_before = s_idx * L_SLC <= tpos_row
    biases = []
    for g, cnt in enumerate(cnts):
        bias_t = jnp.where((cnt < float(N_SEL)) & starts_before, 0.0, NEG)
        if nr < sp:
            bias_t = jnp.concatenate([bias_t, jnp.full((sp - nr, LANES), NEG, F32)], axis=0)
        bias = bias_t.T[:tq]
        biases.append(pltpu.roll(bias, HEAD_DIM, axis=1) if g == 0 else bias)
    return biases


def _select_ids(psum, mt_ref, tpos):
    sp = mt_ref.shape[0]
    prow = jnp.broadcast_to(psum[0:1], (LANES, psum.shape[1]))
    imp = _importance(prow, mt_ref, jnp.full((1, LANES), tpos, jnp.int32))
    a = jnp.broadcast_to(imp[:, 0:1], (sp, sp))
    b = a.T
    r_i = lax.broadcasted_iota(jnp.int32, (sp, sp), 0)
    c_i = lax.broadcasted_iota(jnp.int32, (sp, sp), 1)
    beats = jnp.where(a > b, 1.0, jnp.where(a == b, jnp.where(r_i < c_i, 1.0, 0.0), 0.0))
    rank = jnp.sum(beats, axis=0, keepdims=True)
    slot = lax.broadcasted_iota(jnp.int32, (N_SEL, sp), 0).astype(F32)
    blk = lax.broadcasted_iota(jnp.int32, (N_SEL, sp), 1).astype(F32)
    ids = jnp.sum(jnp.where(rank == slot, blk, 0.0), axis=1, keepdims=True)
    return jnp.broadcast_to(ids, (N_SEL, LANES)).astype(jnp.int32)


def _flash_init(tq):
    return (jnp.full((HPG, tq, 1), NEG, F32), jnp.zeros((HPG, tq, 1), F32), jnp.zeros((HPG * tq, LANES), F32))


def _flash_tile(carry, q, k, v, bias, tq):
    m, l, acc = carry
    kt = k.shape[0]
    s = _dot_nt(q, k).reshape(HPG, tq, kt)
    if bias is not None:
        s = s + bias[None]
    m_new = jnp.maximum(m, jnp.max(s, axis=-1, keepdims=True))
    alpha = jnp.exp(m - m_new)
    p = jnp.exp(s - m_new)
    l = alpha * l + jnp.sum(p, axis=-1, keepdims=True)
    acc = alpha.reshape(HPG * tq, 1) * acc + _dot(p.reshape(HPG * tq, kt).astype(BF), v)
    return m_new, l, acc


def _flash_out(carry, tq):
    _, l, acc = carry
    return acc / l.reshape(HPG * tq, 1)


def _mask_bias(valid):
    return jnp.where(valid, 0.0, NEG)


def _win_valid(kpos_row, tpos_col):
    dist = tpos_col - kpos_row
    return (dist >= 0) & (dist < WINDOW)


def _merge_heads(o_c, o_s, o_w, gt_ref, g, o_ref, tq):
    lane = lax.broadcasted_iota(jnp.int32, (tq, LANES), 1)
    for pair in range(HPG // 2):
        both = []
        for h in (2 * pair, 2 * pair + 1):
            col = (g * HPG + h) * 3
            rows = slice(h * tq, (h + 1) * tq)
            both.append(o_c(rows) * gt_ref[:, col:col + 1] + o_s(rows) * gt_ref[:, col + 1:col + 2]
                        + o_w(rows) * gt_ref[:, col + 2:col + 3])
        even, odd = both
        if g == 0:
            odd = pltpu.roll(odd, HEAD_DIM, axis=1)
        else:
            even = pltpu.roll(even, HEAD_DIM, axis=1)
        ch = g * (HPG // 2) + pair
        o_ref[:, ch * LANES:(ch + 1) * LANES] = jnp.where(lane < HEAD_DIM, even, odd).astype(o_ref.dtype)


def _staged_tile(state, stage, q, k, v, bias, refs, tq, first):
    s_ref, p_ref, m_ref, acc_ref = refs
    kt = k.shape[0]
    reps = kt // LANES
    s_ref[stage, :, 0:kt] = _dot_nt(q, k)
    for h in range(HPG):
        rows = slice(h * tq, (h + 1) * tq)

        def scores():
            s = s_ref[stage, rows, 0:kt]
            return s if bias is None else s + bias

        m_tile = jnp.max(scores(), axis=-1, keepdims=True)
        if first:
            m_new = jnp.broadcast_to(m_tile, (tq, LANES))
        else:
            m_old = m_ref[state, rows, :]
            m_new = jnp.maximum(m_old, m_tile)
            acc_ref[state, rows, :] = jnp.exp(m_old - m_new) * acc_ref[state, rows, :]
        m_ref[state, rows, :] = m_new
        p_ref[stage, rows, 0:kt] = jnp.exp(scores() - jnp.concatenate([m_new] * reps, axis=1)).astype(BF)
    pv = _dot(p_ref[stage, :, 0:kt], v)
    if first:
        acc_ref[state] = pv
    else:
        acc_ref[state] = acc_ref[state] + pv


def _attn_prompt_kernel(qs_ref, qrs_ref, gt_ref, kc_ref, vc_ref, ka_ref, sva_ref, win_ref, wva_ref, mt_ref, o_ref,
                        imp_ref, s_ref, p_ref, m_ref, acc_ref, oc_ref, *, n_cb, n_sb):
    tq = qs_ref.shape[0]
    t_all = ka_ref.shape[0]
    qb = pl.program_id(1)
    t0 = qb * tq
    tpos_col = t0 + lax.broadcasted_iota(jnp.int32, (tq, 1), 0)
    tpos_row = t0 + lax.broadcasted_iota(jnp.int32, (1, LANES), 1)
    groups = range(N_KV_HEADS)
    k_lanes, v_lanes = slice(0, LANES), slice(LANES, 2 * LANES)

    psums = [_cmp_branch_staged(g, _stack_heads(qs_ref, g, tq), kc_ref[...], vc_ref[...], tpos_col, n_cb,
                                s_ref, p_ref, oc_ref, tq) for g in groups]
    blk_bias = _select_blocks(psums, mt_ref, imp_ref, tpos_row, n_sb, (t0 + tq - 1) // L_SLC + 1, tq)
    qr_blk = [_stack_heads(qrs_ref, g, tq, fill=blk_bias[g]) for g in groups]

    refs = (s_ref, p_ref, m_ref, acc_ref)
    g_lanes = [slice(g * LANES, (g + 1) * LANES) for g in groups]

    def slc_tile(j, bias, first):
        rows = pl.ds(pl.multiple_of(j * SLC_TILE, SLC_TILE), SLC_TILE)
        for g in groups:
            _staged_tile(g, g, qr_blk[g], ka_ref[rows, g_lanes[g]], sva_ref[rows, g_lanes[g]], bias, refs, tq, first)

    last = (t0 + tq - 1) // SLC_TILE
    causal = (last * SLC_TILE + lax.broadcasted_iota(jnp.int32, (1, SLC_TILE), 1)) <= tpos_col
    slc_tile(last, _mask_bias(causal), True)

    def slc_body(j, c):
        slc_tile(j, None, False)
        return c

    lax.fori_loop(0, last, slc_body, 0)

    wk = min(WINDOW + tq, t_all)
    start = pl.multiple_of(jnp.clip(t0 - WINDOW, 0, t_all - wk), tq)
    wbias = _mask_bias(_win_valid(start + lax.broadcasted_iota(jnp.int32, (1, wk), 1), tpos_col))
    wrows = pl.ds(start, wk)
    kw = win_ref[wrows, k_lanes]
    for g in groups:
        _staged_tile(N_KV_HEADS + g, g, _stack_heads(qrs_ref, g, tq), kw, wva_ref[wrows, g_lanes[g]], wbias, refs,
                     tq, True)

    def normalised(slot, rows):
        acc = acc_ref[slot, rows, :]
        return acc / pltpu.roll(acc, HEAD_DIM, axis=1)

    for g in groups:
        _merge_heads(lambda rows: oc_ref[g, rows, :], functools.partial(normalised, g),
                     functools.partial(normalised, N_KV_HEADS + g), gt_ref, g, o_ref, tq)


def _attn_prompt(qs, qrs, gates, kc, vc, ka_bf, slc_va, win_bf, win_va, mt, b, t, n_cb, n_sb):
    tq = Q_BLOCK
    nqb = t // tq
    d = qs.shape[1]
    stage_w = max(SLC_TILE, min(WINDOW + tq, t))
    row = lambda w: pl.BlockSpec((tq, w), lambda bi, qi: (bi * nqb + qi, 0))
    per_b = lambda a: pl.BlockSpec((None,) + a.shape[1:], lambda bi, qi: (bi, 0, 0))
    return pl.pallas_call(
        functools.partial(_attn_prompt_kernel, n_cb=n_cb, n_sb=n_sb),
        grid=(b, nqb),
        in_specs=[row(d), row(d), row(LANES), per_b(kc), per_b(vc), per_b(ka_bf), per_b(slc_va), per_b(win_bf),
                  per_b(win_va), _const_spec(mt.shape)],
        out_specs=row(d),
        out_shape=jax.ShapeDtypeStruct((b * t, d), BF),
        scratch_shapes=[pltpu.VMEM((N_KV_HEADS, mt.shape[0], LANES), F32),
                        pltpu.VMEM((N_KV_HEADS, HPG * tq, stage_w), F32),
                        pltpu.VMEM((N_KV_HEADS, HPG * tq, stage_w), BF),
                        pltpu.VMEM((2 * N_KV_HEADS, HPG * tq, LANES), F32),
                        pltpu.VMEM((2 * N_KV_HEADS, HPG * tq, LANES), F32),
                        pltpu.VMEM((N_KV_HEADS, HPG * tq, LANES), F32)],
        compiler_params=_cparams(2), name="attn_prompt",
    )(qs, qrs, gates, kc, vc, ka_bf, slc_va, win_bf, win_va, mt)


def _sample_select_kernel(qs_ref, kc_ref, vc_ref, mt_ref, oc_ref, ids_ref, *, n_cb, past):
    tq = qs_ref.shape[0]
    tpos_col = jnp.full((tq, 1), past, jnp.int32)
    for g in range(N_KV_HEADS):
        o_c, psum = _cmp_branch(_stack_heads(qs_ref, g, tq), kc_ref[...], vc_ref[...], tpos_col, n_cb, tq)
        oc_ref[g] = o_c
        ids_ref[g] = _select_ids(psum, mt_ref, past)


def _sample_select(qs, kc, vc, mt, n_cb, past):
    db, tq, _ = qs.shape
    per_b = lambda a: pl.BlockSpec((None,) + a.shape[1:], lambda bi: (bi,) + (0,) * (a.ndim - 1))
    oc = jax.ShapeDtypeStruct((db, N_KV_HEADS, HPG * tq, LANES), F32)
    ids = jax.ShapeDtypeStruct((db, N_KV_HEADS, N_SEL, LANES), jnp.int32)
    return pl.pallas_call(
        functools.partial(_sample_select_kernel, n_cb=n_cb, past=past),
        grid=(db,),
        in_specs=[per_b(qs), per_b(kc), per_b(vc), _const_spec(mt.shape)],
        out_specs=[per_b(oc), per_b(ids)],
        out_shape=[oc, ids],
        compiler_params=_cparams(1), name="sample_select",
    )(qs, kc, vc, mt)


def _sample_attend_kernel(pt_ref, ids_ref, qrs_ref, gt_ref, oc_ref, slct_ref, win_ref, wint_ref, *rest, past, n_cached):
    del pt_ref
    blk_refs, o_ref, k_scr, v_scr = rest[:-3], rest[-3], rest[-2], rest[-1]
    b = pl.program_id(0)
    tq = qrs_ref.shape[0]
    tpos_col = jnp.full((tq, 1), past, jnp.int32)
    w_buf = win_ref.shape[0]
    tail = slct_ref.shape[0]
    k_lanes, v_lanes = slice(0, LANES), slice(LANES, 2 * LANES)
    tail_pos = past + lax.broadcasted_iota(jnp.int32, (1, tail), 1)
    win_pos = past - w_buf + lax.broadcasted_iota(jnp.int32, (1, w_buf), 1)
    wbias = jnp.broadcast_to(_mask_bias(_win_valid(win_pos, tpos_col)), (tq, w_buf))
    wtbias = jnp.broadcast_to(_mask_bias(_win_valid(tail_pos, tpos_col)), (tq, tail))
    tbias = jnp.broadcast_to(_mask_bias(tail_pos <= tpos_col), (tq, tail))
    kw, vw = win_ref[:, k_lanes].astype(BF), win_ref[:, v_lanes].astype(BF)
    lane = lax.broadcasted_iota(jnp.int32, (1, LANES), 1)
    for g in range(N_KV_HEADS):
        for r in range(N_SEL):
            blk = blk_refs[g * N_SEL + r]
            k_scr[g, r * L_SLC:(r + 1) * L_SLC, :] = blk[:, k_lanes].astype(BF)
            v_scr[g, r * L_SLC:(r + 1) * L_SLC, :] = blk[:, v_lanes].astype(BF)
        chunks = []
        for c in range(N_SEL // 2):
            lo = jnp.where(ids_ref[b, g * N_SEL + 2 * c] >= n_cached, NEG, 0.0)
            hi = jnp.where(ids_ref[b, g * N_SEL + 2 * c + 1] >= n_cached, NEG, 0.0)
            chunks.append(jnp.where(lane < L_SLC, lo, hi))
        sbias = jnp.broadcast_to(jnp.concatenate(chunks, axis=1), (tq, N_SEL * L_SLC))
        qr = _stack_heads(qrs_ref, g, tq)
        carry = _flash_tile(_flash_init(tq), qr, k_scr[g], v_scr[g], sbias, tq)
        carry = _flash_tile(carry, qr, slct_ref[:, k_lanes], slct_ref[:, v_lanes], tbias, tq)
        o_s = _flash_out(carry, tq)
        carry = _flash_tile(_flash_init(tq), qr, kw, vw, wbias, tq)
        carry = _flash_tile(carry, qr, wint_ref[:, k_lanes], wint_ref[:, v_lanes], wtbias, tq)
        o_w = _flash_out(carry, tq)
        _merge_heads(lambda rows: oc_ref[g, rows, :], lambda rows: o_s[rows], lambda rows: o_w[rows], gt_ref, g,
                     o_ref, tq)


def _sample_attend(qrs, gates, oc, slc_cache, page_table, ids, slc_tail, win_c, win_tail, past):
    db = qrs.shape[0]
    per_page = PAGE_SIZE // L_SLC
    n_cached = past // L_SLC
    blocks = slc_cache
    per_b = lambda a: pl.BlockSpec((None,) + a.shape[1:], lambda bi, pt, sel: (bi,) + (0,) * (a.ndim - 1))

    def blk_map(bi, pt, sel, g, r):
        s = jnp.clip(sel[bi, g * N_SEL + r], 0, n_cached - 1)
        return pt[bi, s // per_page], s % per_page, 0

    blk_specs = [pl.BlockSpec((None, L_SLC, blocks.shape[2]), functools.partial(blk_map, g=g, r=r))
                 for g in range(N_KV_HEADS) for r in range(N_SEL)]
    args = (qrs, gates, oc, slc_tail, win_c, win_tail)
    return pl.pallas_call(
        functools.partial(_sample_attend_kernel, past=past, n_cached=n_cached),
        grid_spec=pltpu.PrefetchScalarGridSpec(
            num_scalar_prefetch=2, grid=(db,),
            in_specs=[per_b(a) for a in args] + blk_specs,
            out_specs=per_b(qrs),
            scratch_shapes=[pltpu.VMEM((N_KV_HEADS, N_SEL * L_SLC, LANES), BF)] * 2),
        out_shape=jax.ShapeDtypeStruct(qrs.shape, BF),
        compiler_params=_cparams(1), name="sample_attend",
    )(page_table, ids, *args, *([blocks] * len(blk_specs)))


def _block_diag(blocks):
    n, k, j = blocks.shape
    eye = jnp.eye(n, dtype=blocks.dtype)
    return (eye[:, None, :, None] * blocks[:, :, None, :]).reshape(n * k, n * j)


def _rope_tables(pos):
    half = HEAD_DIM // 2
    inv_freq = ROPE_THETA ** (-jnp.arange(half, dtype=F32) / half)
    ang = pos.astype(F32)[:, None] * inv_freq[None, :]
    cos, sin = jnp.cos(ang), jnp.sin(ang)
    reps = LANES // HEAD_DIM
    return (jnp.tile(jnp.concatenate([cos, cos], axis=1), (1, reps)),
            jnp.tile(jnp.concatenate([-sin, sin], axis=1), (1, reps)))


def _block_onehot(pos):
    lane = jnp.arange(LANES, dtype=jnp.int32)[None, :] % HEAD_DIM
    return (lane == (pos[:, None] // L_SLC)).astype(F32)


def _overlap_t(n_cb, n_sb, ncp, sp):
    c0 = np.arange(ncp)[None, :] * D_CMP
    s0 = np.arange(sp)[:, None] * L_SLC
    m = (c0 < s0 + L_SLC) & (c0 + L_CMP > s0) & (np.arange(ncp)[None, :] < n_cb) & (np.arange(sp)[:, None] < n_sb)
    return jnp.asarray(m, dtype=BF)


def _round_up(x, m):
    return (x + m - 1) // m * m


def kernel(x_prompt, x_sample, p_prompt, p_sample, cache_cmp_kv, cache_slc_kv, cache_win_kv, state_rg_conv,
           state_rg_h, state_ffn_conv, page_table, g_mix, g_ffn, g_ple, g_final, rg_w_in, rg_conv_w, rg_conv_b,
           rg_w_a, rg_b_a, rg_w_x, rg_b_x, rg_lambda, rg_w_out, g_kv, w_kv, cmp_pos, cmp_w1, cmp_b1, cmp_w2,
           attn_w_qg, attn_w_o, ffn_w_up, ffn_conv_w, ffn_conv_b, ffn_w_down, ple_w_in, ple_w_gate):
    b, t, d = x_prompt.shape
    db = x_sample.shape[0]
    assert x_sample.shape[1] == 1, "the sample path handles one new token per sequence"
    depth = g_mix.shape[0]
    n_a = rg_w_in.shape[0]
    d_rnn = rg_w_out.shape[1]
    n_pages = page_table.shape[1]
    past = n_pages * PAGE_SIZE
    kvc = 2 * N_KV_HEADS * HEAD_DIM
    n_q = N_KV_HEADS * HPG * HEAD_DIM

    row1 = lambda v: v.reshape(1, -1)
    pad8 = lambda w: jnp.pad(w, ((0, SUBLANES - w.shape[0]), (0, 0)))

    rg_in_bf = rg_w_in.astype(BF)
    rg_out_bf = rg_w_out.astype(BF)
    rg_gate_bf = [jnp.concatenate([_block_diag(rg_w_a[i]), _block_diag(rg_w_x[i])], axis=1).astype(BF)
                  for i in range(n_a)]
    rg_gate_b = [row1(jnp.concatenate([rg_b_a[i].reshape(-1), rg_b_x[i].reshape(-1)])) for i in range(n_a)]
    rg_c = [row1(-RG_C * jax.nn.softplus(-rg_lambda[i])) for i in range(n_a)]
    up_bf = ffn_w_up.astype(BF)
    down_bf = ffn_w_down.astype(BF)
    gate_bf = ple_w_gate.astype(BF)
    pin_bf = ple_w_in.astype(BF)
    wkv_bf = w_kv.astype(BF)
    wq_bf = attn_w_qg[:, :, :n_q].astype(BF)
    wgl_bf = jnp.pad(attn_w_qg[:, :, n_q:], ((0, 0), (0, 0), (0, LANES - 3 * N_KV_HEADS * HPG))).astype(BF)
    wo_bf = attn_w_o.astype(BF)
    gfin = row1(g_final)

    w1 = cmp_w1.reshape(2, 2, D_CMP, HEAD_DIM, -1)
    eye_g = jnp.eye(N_KV_HEADS, dtype=F32)
    w1_bd = (eye_g[None, None, None, :, None, :, None] * w1[:, :, :, None, :, None, :]).reshape(
        2, 2, D_CMP, N_KV_HEADS * HEAD_DIM, -1).astype(BF)
    pos_t = jnp.tile(jnp.transpose(cmp_pos, (1, 0, 2)), (1, 1, N_KV_HEADS)).reshape(2, 2, D_CMP, LANES)
    w1_rows = D_CMP * N_KV_HEADS * HEAD_DIM
    cmp_params = (pos_t[:, 0], pos_t[:, 1], w1_bd[:, 0].reshape(2, w1_rows, -1), w1_bd[:, 1].reshape(2, w1_rows, -1),
                  jnp.tile(cmp_b1, (1, N_KV_HEADS)).reshape(2, 1, -1),
                  jnp.stack([_block_diag(jnp.stack([cmp_w2[j]] * N_KV_HEADS)) for j in range(2)]).astype(BF))

    def layer_consts(i, final):
        return (row1(g_ffn[i]), up_bf[i], pad8(ffn_conv_w[i]), row1(ffn_conv_b[i]), down_bf[i], row1(g_ple[i]),
                gate_bf[i], pin_bf[i], gfin), final

    n = b * t
    x = x_prompt.reshape(n, d)
    p = p_prompt.reshape(depth, n, -1)
    pos_p = jnp.tile(jnp.arange(t, dtype=jnp.int32), b)
    cos_p, sin_p = _rope_tables(pos_p)
    rg_conv_p, rg_h_p, ffn_conv_p = [], [], []
    zeros_rg = jnp.zeros((b, SUBLANES, d_rnn), F32)
    zeros_ffn = jnp.zeros((b, SUBLANES, ffn_w_up.shape[2]), F32)
    for i in range(depth):
        if i == n_a:
            kv_p, _, win_bf, ka_bf, slc_va, win_va = _kv_proj(x, row1(g_kv), wkv_bf, cos_p, sin_p,
                                                              _block_onehot(pos_p))
            kc_p, vc_p = _compress(kv_p.reshape(b, t, -1), cmp_params, b, t)
            n_cb_p = (t - L_CMP) // D_CMP + 1
            n_sb_p = t // L_SLC
            mt_p = _overlap_t(n_cb_p, n_sb_p, t // D_CMP, _round_up(n_sb_p, LANES))
            ka_bf, slc_va, win_bf, win_va = (a.reshape(b, t, -1) for a in (ka_bf, slc_va, win_bf, win_va))
        if i < n_a:
            x, h_last, xr_last = _rg_mixer(x, row1(g_mix[i]), rg_in_bf[i], zeros_rg, zeros_rg, pad8(rg_conv_w[i]),
                                           row1(rg_conv_b[i]), rg_gate_bf[i], rg_gate_b[i], rg_c[i], rg_out_bf[i],
                                           b, t)
            rg_conv_p.append(xr_last[:, SUBLANES - 3:])
            rg_h_p.append(h_last[:, 0])
        else:
            j = i - n_a
            qs, qrs, gates = _q_proj(x, row1(g_mix[i]), wq_bf[j], wgl_bf[j], cos_p, sin_p)
            o = _attn_prompt(qs, qrs, gates, kc_p, vc_p, ka_bf, slc_va, win_bf, win_va, mt_p, b, t, n_cb_p, n_sb_p)
            x = _matmul_res(o, wo_bf[j], x)
        consts, final = layer_consts(i, i == depth - 1)
        x, f_hist = _ffn(x, p, i, zeros_ffn, *consts, b, t, final)
        ffn_conv_p.append(f_hist[:, SUBLANES - 2:])
    y_prompt = x.reshape(b, t, d)
    kv5 = kv_p.reshape(b, t, 6, N_KV_HEADS, HEAD_DIM)
    keep = min(WINDOW, t)
    cmp_p, slc_p, win_p = kv5[:, :, 0:2], kv5[:, :, 2:4], kv5[:, t - keep:, 4:6]

    x = x_sample.reshape(db, d)
    p = p_sample.reshape(depth, db, -1)
    cos_s, sin_s = _rope_tables(jnp.full((db,), past, jnp.int32))
    rg_conv_s, rg_h_s, ffn_conv_s = [], [], []
    for i in range(depth):
        if i == n_a:
            pos_s = jnp.full((db,), past, jnp.int32)
            kv_s, slc_new_bf, win_new_bf = _kv_proj(x, row1(g_kv), wkv_bf, cos_s, sin_s, _block_onehot(pos_s))[:3]
            kc_s, vc_s = _compress_paged(cache_cmp_kv.reshape(-1, PAGE_SIZE, kvc), page_table, cmp_params)
            slc_cache = cache_slc_kv.reshape(-1, PAGE_SIZE, kvc)
            n_cb_s = (past + 1 - L_CMP) // D_CMP + 1
            n_sb_s = -(-(past + 1) // L_SLC)
            sp_s = _round_up(n_sb_s, LANES)
            mt_s = _overlap_t(n_cb_s, n_sb_s, past // D_CMP, sp_s)
            tail = lambda a: jnp.pad(a.reshape(db, 1, -1), ((0, 0), (0, LANES - 1), (0, 0)))
            slc_tail, win_tail = tail(slc_new_bf), tail(win_new_bf)
            win_c = cache_win_kv.reshape(db, cache_win_kv.shape[1], kvc)
        if i < n_a:
            yg, xr = _rg_in(x, row1(g_mix[i]), rg_in_bf[i])
            hist = state_rg_conv[i]
            gated, h_new = _rg_step(xr, yg, hist[:, 0], hist[:, 1], hist[:, 2], state_rg_h[i], pad8(rg_conv_w[i]),
                                    row1(rg_conv_b[i]), rg_gate_bf[i], rg_gate_b[i], rg_c[i])
            x = _matmul_res(gated, rg_out_bf[i], x)
            rg_conv_s.append(jnp.concatenate([hist[:, 1:], xr[:, None, :]], axis=1))
            rg_h_s.append(h_new)
        else:
            j = i - n_a
            qs, qrs, gates = _q_proj(x, row1(g_mix[i]), wq_bf[j], wgl_bf[j], cos_s, sin_s)
            rep = lambda a: jnp.broadcast_to(a[:, None, :], (db, SUBLANES, a.shape[1]))
            oc, ids = _sample_select(rep(qs), kc_s, vc_s, mt_s, n_cb_s, past)
            o = _sample_attend(rep(qrs), rep(gates), oc, slc_cache, page_table, ids[:, :, :, 0].reshape(db, -1),
                               slc_tail, win_c,
                               win_tail, past)
            x = _matmul_res(o[:, 0], wo_bf[j], x)
        consts, final = layer_consts(i, i == depth - 1)
        fh = state_ffn_conv[i]
        x, up = _ffn_step(x, p[i], fh[:, 0], fh[:, 1], *consts, final)
        ffn_conv_s.append(jnp.stack([fh[:, 1], up], axis=1))
    y_sample = x.reshape(db, 1, d)
    kv5 = kv_s.reshape(db, 1, 6, N_KV_HEADS, HEAD_DIM)
    cmp_s, slc_s, win_new = kv5[:, :, 0:2], kv5[:, :, 2:4], kv5[:, :, 4:6]
    win_full = jnp.concatenate([cache_win_kv, win_new], axis=1)
    keep = min(WINDOW, past + 1)
    win_s = win_full[:, win_full.shape[1] - keep:]

    return (y_prompt, y_sample, cmp_p, cmp_s, slc_p, slc_s, win_p, win_s,
            jnp.stack(rg_conv_p), jnp.stack(rg_conv_s), jnp.stack(rg_h_p), jnp.stack(rg_h_s),
            jnp.stack(ffn_conv_p), jnp.stack(ffn_conv_s))
```
